```python
import math
import jax, jax.numpy as jnp
from jax import lax
import numpy as np

D_MODEL = 1024
BATCH = 4
SEQ = 4096
DEPTH = 2

D_MIX = D_MODEL
GROUP = D_MIX // 4
HEAD_DIM = 64
N_HEADS_GROUP = GROUP // HEAD_DIM
CHUNK = 64
CONV_K = 4
LRU_BLOCKS = 4
LRU_BLOCK = GROUP // LRU_BLOCKS
LRU_C = 8.0
RWKV_W_LORA = 64
RWKV_A_LORA = 64
RWKV_G_LORA = 128
RWKV_V_LORA = 32
RWKV_GN_EPS = 64e-5
MEM_LEN = 256
XATTN_HEADS = 4
XATTN_HEAD_DIM = D_MODEL // XATTN_HEADS
D_FF = 2816
N_EXPERTS = 8
TOP_K = 2
D_FF_EXPERT = 3584
EPS = 1e-6

HG_COLS = 4 * GROUP
GDN_COLS = 4 * GROUP + 2 * N_HEADS_GROUP
LRU_COLS = 2 * GROUP
RWKV_COLS = 3 * GROUP + RWKV_W_LORA + RWKV_A_LORA + RWKV_G_LORA
N_IN = HG_COLS + GDN_COLS + LRU_COLS + RWKV_COLS

kernel_name = "hybrid_parallel_heads_trunk"


def rmsnorm(x, g):
    xf = x.astype(jnp.float32)
    y = xf * lax.rsqrt(jnp.mean(xf * xf, axis=-1, keepdims=True) + EPS)
    return (y * g.astype(jnp.float32)).astype(x.dtype)


def l2norm(x):
    xf = x.astype(jnp.float32)
    return xf * lax.rsqrt(jnp.sum(xf * xf, axis=-1, keepdims=True) + EPS)


def split_cols(t, sizes):
    return jnp.split(t, [int(o) for o in np.cumsum(sizes)[:-1]], axis=-1)


def causal_dwconv(x, w, b=None):
    y = lax.conv_general_dilated(x, w[:, None, :], window_strides=(1,), padding=[(w.shape[0] - 1, 0)],
                                 dimension_numbers=("NWC", "WIO", "NWC"), feature_group_count=x.shape[-1])
    return y if b is None else y + b


def to_heads(t):
    return t.reshape(t.shape[0], t.shape[1], -1, HEAD_DIM).transpose(0, 2, 1, 3)


def chunk_gla(q, k, v, log_f):
    B, H, S, dk = q.shape
    dv = v.shape[-1]
    n = S // CHUNK

    def chunks(t):
        return jnp.moveaxis(t.reshape(B, H, n, CHUNK, t.shape[-1]), 2, 0)

    qc, kc, vc = chunks(q), chunks(k), chunks(v)
    bc = jnp.cumsum(chunks(log_f), axis=3)
    causal = jnp.tril(jnp.ones((CHUNK, CHUNK), bool))

    def step(state, inp):
        qi, ki, vi, bi = inp
        o_inter = jnp.einsum("bhtd,bhde->bhte", qi * jnp.exp(bi), state)
        rel = jnp.exp(jnp.where(causal[:, :, None], bi[:, :, :, None, :] - bi[:, :, None, :, :], -jnp.inf))
        scores = jnp.einsum("bhtd,bhsd,bhtsd->bhts", qi, ki, rel)
        o = o_inter + jnp.einsum("bhts,bhse->bhte", scores, vi)
        b_last = bi[:, :, -1:, :]
        state = jnp.exp(b_last[:, :, 0, :, None]) * state + jnp.einsum("bhsd,bhse->bhde", ki * jnp.exp(b_last - bi), vi)
        return state, o

    state0 = jnp.zeros((B, H, dk, dv), jnp.float32)
    _, o = lax.scan(step, state0, (qc, kc, vc, bc))
    return jnp.moveaxis(o, 0, 2).reshape(B, H, S, dv)


def hgrn2_mixer(q, f_logit, i, g, lb, norm_g):
    B, S, _ = q.shape
    f32 = jnp.float32
    fl = f_logit.astype(f32)
    lb = lb.astype(f32)
    log_f = jnp.logaddexp(jnp.log(lb), jnp.log1p(-lb) + jax.nn.log_sigmoid(fl))
    k = (1.0 - lb) * jax.nn.sigmoid(-fl)
    o = chunk_gla(to_heads(q.astype(f32)), to_heads(k), to_heads(i.astype(f32)), to_heads(log_f))
    o = rmsnorm(o.transpose(0, 2, 1, 3), norm_g.reshape(N_HEADS_GROUP, HEAD_DIM))
    return (o.reshape(B, S, GROUP) * jax.nn.silu(g.astype(f32))).astype(q.dtype)


def chunk_gated_delta(q, k, v, g, beta):
    B, H, S, dk = q.shape
    dv = v.shape[-1]
    n = S // CHUNK
    f32 = jnp.float32

    def chunks(t):
        return t.reshape(B, H, n, CHUNK, *t.shape[3:])

    qc, kc, vc, bc = chunks(q), chunks(k), chunks(v), chunks(beta)
    gc = jnp.cumsum(chunks(g), axis=-1)
    causal = jnp.tril(jnp.ones((CHUNK, CHUNK), bool))
    strict = jnp.tril(jnp.ones((CHUNK, CHUNK), f32), -1)
    rel = jnp.exp(jnp.where(causal, gc[..., :, None] - gc[..., None, :], -jnp.inf))
    kb = kc * bc[..., None]
    lower = jnp.einsum("bhntd,bhnsd->bhnts", kb, kc) * rel * strict
    rhs = jnp.concatenate([vc * bc[..., None], kb * jnp.exp(gc)[..., None]], axis=-1)
    uw = lax.linalg.triangular_solve(jnp.eye(CHUNK, dtype=f32) + lower, rhs,
                                     left_side=True, lower=True, unit_diagonal=True)
    u, w = uw[..., :dv], uw[..., dv:]
    scores = jnp.einsum("bhntd,bhnsd->bhnts", qc, kc) * rel

    def step(state, inp):
        qi, ki, ui, wi, gi, si = inp
        v_new = ui - jnp.einsum("bhtd,bhde->bhte", wi, state)
        o = (jnp.einsum("bhtd,bhde->bhte", qi * jnp.exp(gi)[..., None], state)
             + jnp.einsum("bhts,bhse->bhte", si, v_new))
        g_last = gi[..., -1:]
        state = (jnp.exp(g_last)[..., None] * state
                 + jnp.einsum("bhsd,bhse->bhde", ki * jnp.exp(g_last - gi)[..., None], v_new))
        return state, o

    xs = tuple(jnp.moveaxis(t, 2, 0) for t in (qc, kc, u, w, gc, scores))
    _, o = lax.scan(step, jnp.zeros((B, H, dk, dv), f32), xs)
    return jnp.moveaxis(o, 0, 2).reshape(B, H, S, dv)


def gated_deltanet_mixer(qkv, z, alpha, beta_logit, conv_w, A_log, dt_bias, norm_g):
    B, S, _ = qkv.shape
    f32 = jnp.float32
    qkv = jax.nn.silu(causal_dwconv(qkv, conv_w)).astype(f32)
    q, k, v = jnp.split(qkv, 3, axis=-1)
    q = l2norm(to_heads(q)) * HEAD_DIM ** -0.5
    k = l2norm(to_heads(k))
    v = to_heads(v)
    g = -jnp.exp(A_log.astype(f32)) * jax.nn.softplus(alpha.astype(f32) + dt_bias.astype(f32))
    beta = jax.nn.sigmoid(beta_logit.astype(f32))
    o = chunk_gated_delta(q, k, v, g.transpose(0, 2, 1), beta.transpose(0, 2, 1))
    o = rmsnorm(o.transpose(0, 2, 1, 3), norm_g)
    return (o.reshape(B, S, GROUP) * jax.nn.silu(z.astype(f32))).astype(z.dtype)


def rglru_mixer(xb, gate, conv_w, conv_b, wa, ba, wx, bx, lam):
    B, S, _ = xb.shape
    f32 = jnp.float32
    xc = causal_dwconv(xb, conv_w, conv_b).astype(f32)
    xblk = xc.reshape(B, S, LRU_BLOCKS, LRU_BLOCK)
    r = jax.nn.sigmoid(jnp.einsum("bsni,nij->bsnj", xblk, wa.astype(f32)).reshape(B, S, GROUP) + ba)
    i = jax.nn.sigmoid(jnp.einsum("bsni,nij->bsnj", xblk, wx.astype(f32)).reshape(B, S, GROUP) + bx)
    log_a = -LRU_C * r * jax.nn.softplus(-lam.astype(f32))
    a = jnp.exp(log_a)
    u = jnp.sqrt(-jnp.expm1(2.0 * log_a)) * (i * xc)

    def combine(left, right):
        a_l, h_l = left
        a_r, h_r = right
        return a_l * a_r, a_r * h_l + h_r

    _, h = lax.associative_scan(combine, (a, u), axis=1)
    return (h * jax.nn.gelu(gate.astype(f32))).astype(xb.dtype)


def rwkv7_scan(r, w, k, v, kk, a):
    B, S, H, d = r.shape

    def step(state, inp):
        r_t, w_t, k_t, v_t, kk_t, a_t = inp
        sa = jnp.einsum("bhvk,bhk->bhv", state, kk_t)
        state = (state * w_t[:, :, None, :] - sa[..., None] * (kk_t * a_t)[:, :, None, :]
                 + v_t[..., None] * k_t[:, :, None, :])
        return state, jnp.einsum("bhvk,bhk->bhv", state, r_t)

    xs = tuple(jnp.moveaxis(t, 1, 0) for t in (r, w, k, v, kk, a))
    _, y = lax.scan(step, jnp.zeros((B, H, d, d), jnp.float32), xs)
    return jnp.moveaxis(y, 0, 1)


def rwkv7_mixer(cols, mu, w0, w2, a0, a2, g2, k_k, k_a, r_k, ln_w, ln_b, v_first, v_mix):
    B, S, _ = cols.shape
    f32 = jnp.float32
    c = cols.astype(f32)
    prev = jnp.pad(c, ((0, 0), (1, 0), (0, 0)))[:, :-1]
    c = c + mu * (prev - c)
    r, k, v, w_lo, a_lo, g_lo = split_cols(c, (GROUP, GROUP, GROUP, RWKV_W_LORA, RWKV_A_LORA, RWKV_G_LORA))
    w = jnp.exp(-math.exp(-0.5) * jax.nn.sigmoid(w0 + jnp.tanh(w_lo) @ w2))
    a = jax.nn.sigmoid(a0 + a_lo @ a2)
    g = jax.nn.sigmoid(g_lo) @ g2
    if v_mix is None:
        v_first = v
    else:
        v0, v1, v2 = v_mix
        v = v + (v_first - v) * jax.nn.sigmoid(v0 + (v @ v1) @ v2)
    heads = lambda t: t.reshape(B, S, N_HEADS_GROUP, HEAD_DIM)
    kk = l2norm(heads(k * k_k))
    k = k * (1.0 + (a - 1.0) * k_a)
    y = rwkv7_scan(heads(r), heads(w), heads(k), heads(v), kk, heads(a))
    mean = jnp.mean(y, axis=-1, keepdims=True)
    var = jnp.mean(jnp.square(y - mean), axis=-1, keepdims=True)
    y = ((y - mean) * lax.rsqrt(var + RWKV_GN_EPS)).reshape(B, S, GROUP) * ln_w + ln_b
    r_k_h = r_k.astype(f32).reshape(N_HEADS_GROUP, HEAD_DIM)
    bonus = jnp.sum(heads(r) * heads(k) * r_k_h, axis=-1, keepdims=True) * heads(v)
    y = (y + bonus.reshape(B, S, GROUP)) * g
    return y.astype(cols.dtype), v_first


def memory_cross_attention(h, mem_n, wq, wk, wv, wo):
    B, S, _ = h.shape
    M = mem_n.shape[1]
    q = (h @ wq).reshape(B, S, XATTN_HEADS, XATTN_HEAD_DIM)
    k = (mem_n @ wk).reshape(B, M, XATTN_HEADS, XATTN_HEAD_DIM)
    v = (mem_n @ wv).reshape(B, M, XATTN_HEADS, XATTN_HEAD_DIM)
    s = jnp.einsum("bshd,bmhd->bhsm", q, k).astype(jnp.float32) * XATTN_HEAD_DIM ** -0.5
    p = jax.nn.softmax(s, axis=-1).astype(h.dtype)
    o = jnp.einsum("bhsm,bmhd->bshd", p, v).reshape(B, S, D_MODEL)
    return o @ wo


def swiglu(h, wg, wu, wd):
    return (jax.nn.silu(h @ wg) * (h @ wu)) @ wd


def moe_swiglu(h, router, wg, wu, wd):
    logits = (h @ router).astype(jnp.float32)
    top_v, top_i = lax.top_k(logits, TOP_K)
    gates = jax.nn.softmax(top_v, axis=-1)
    combine = jnp.sum(jax.nn.one_hot(top_i, N_EXPERTS, dtype=jnp.float32) * gates[..., None], axis=-2)
    out = jnp.zeros_like(h)
    for e in range(N_EXPERTS):
        out = out + combine[..., e:e + 1].astype(h.dtype) * swiglu(h, wg[e], wu[e], wd[e])
    return out


def setup_inputs(seed: int = 0) -> dict:
    key = jax.random.key(seed)
    ks = iter(jax.random.split(key, 64))
    f32 = jnp.float32

    def nrm(shape, scale):
        return scale * jax.random.normal(next(ks), shape, f32)

    def gain(shape):
        return 1.0 + 0.02 * jax.random.normal(next(ks), shape, f32)

    def unif(shape, lo, hi):
        return jax.random.uniform(next(ks), shape, f32, minval=lo, maxval=hi)

    H = N_HEADS_GROUP
    n_dense = (DEPTH + 1) // 2
    n_moe = DEPTH // 2
    n_vres = DEPTH - 1
    dt = jnp.exp(unif((DEPTH, H), math.log(1e-3), math.log(1e-1)))
    lru_a = unif((DEPTH, GROUP), 0.9, 0.999)
    return {
        "x": nrm((BATCH, SEQ, D_MODEL), 1.0),
        "mem": nrm((BATCH, MEM_LEN, D_MODEL), 1.0),
        "norm_mix": gain((DEPTH, D_MODEL)),
        "w_in": nrm((DEPTH, D_MODEL, N_IN), D_MODEL ** -0.5),
        "w_out": nrm((DEPTH, D_MIX, D_MODEL), D_MIX ** -0.5),
        "hgrn_lb_logits": nrm((DEPTH, GROUP), 0.5),
        "hgrn_norm": gain((DEPTH, GROUP)),
        "gdn_conv_w": nrm((DEPTH, CONV_K, 3 * GROUP), CONV_K ** -0.5),
        "gdn_A_log": jnp.log(unif((DEPTH, H), 1.0, 16.0)),
        "gdn_dt_bias": dt + jnp.log(-jnp.expm1(-dt)),
        "gdn_norm": gain((DEPTH, HEAD_DIM)),
        "lru_conv_w": nrm((DEPTH, CONV_K, GROUP), CONV_K ** -0.5),
        "lru_conv_b": nrm((DEPTH, GROUP), 0.02),
        "lru_wa": nrm((DEPTH, LRU_BLOCKS, LRU_BLOCK, LRU_BLOCK), LRU_BLOCK ** -0.5),
        "lru_ba": nrm((DEPTH, GROUP), 0.02),
        "lru_wx": nrm((DEPTH, LRU_BLOCKS, LRU_BLOCK, LRU_BLOCK), LRU_BLOCK ** -0.5),
        "lru_bx": nrm((DEPTH, GROUP), 0.02),
        "lru_lambda": jnp.log(lru_a) - jnp.log1p(-lru_a),
        "rwkv_mu": unif((DEPTH, RWKV_COLS), 0.0, 1.0),
        "rwkv_w0": nrm((DEPTH, GROUP), 0.5),
        "rwkv_w2": nrm((DEPTH, RWKV_W_LORA, GROUP), RWKV_W_LORA ** -0.5),
        "rwkv_a0": nrm((DEPTH, GROUP), 0.1),
        "rwkv_a2": nrm((DEPTH, RWKV_A_LORA, GROUP), RWKV_A_LORA ** -0.5),
        "rwkv_g2": nrm((DEPTH, RWKV_G_LORA, GROUP), RWKV_G_LORA ** -0.5),
        "rwkv_k_k": 0.85 + 0.05 * jax.random.normal(next(ks), (DEPTH, GROUP), f32),
        "rwkv_k_a": gain((DEPTH, GROUP)),
        "rwkv_r_k": nrm((DEPTH, GROUP), 0.1),
        "rwkv_ln_w": gain((DEPTH, GROUP)),
        "rwkv_ln_b": nrm((DEPTH, GROUP), 0.02),
        "rwkv_v0": nrm((n_vres, GROUP), 0.1),
        "rwkv_v1": nrm((n_vres, GROUP, RWKV_V_LORA), GROUP ** -0.5),
        "rwkv_v2": nrm((n_vres, RWKV_V_LORA, GROUP), RWKV_V_LORA ** -0.5),
        "mem_norm": gain((D_MODEL,)),
        "norm_xattn": gain((DEPTH, D_MODEL)),
        "xattn_wq": nrm((DEPTH, D_MODEL, D_MODEL), D_MODEL ** -0.5),
        "xattn_wk": nrm((DEPTH, D_MODEL, D_MODEL), D_MODEL ** -0.5),
        "xattn_wv": nrm((DEPTH, D_MODEL, D_MODEL), D_MODEL ** -0.5),
        "xattn_wo": nrm((DEPTH, D_MODEL, D_MODEL), D_MODEL ** -0.5),
        "norm_ffn": gain((DEPTH, D_MODEL)),
        "ffn_w_gate": nrm((n_dense, D_MODEL, D_FF), D_MODEL ** -0.5),
        "ffn_w_up": nrm((n_dense, D_MODEL, D_FF), D_MODEL ** -0.5),
        "ffn_w_down": nrm((n_dense, D_FF, D_MODEL), D_FF ** -0.5),
        "moe_router": nrm((n_moe, D_MODEL, N_EXPERTS), D_MODEL ** -0.5),
        "moe_w_gate": nrm((n_moe, N_EXPERTS, D_MODEL, D_FF_EXPERT), D_MODEL ** -0.5),
        "moe_w_up": nrm((n_moe, N_EXPERTS, D_MODEL, D_FF_EXPERT), D_MODEL ** -0.5),
        "moe_w_down": nrm((n_moe, N_EXPERTS, D_FF_EXPERT, D_MODEL), D_FF_EXPERT ** -0.5),
        "norm_final": gain((D_MODEL,)),
    }


def reference(x, mem, norm_mix, w_in, w_out, hgrn_lb_logits, hgrn_norm, gdn_conv_w, gdn_A_log, gdn_dt_bias,
              gdn_norm, lru_conv_w, lru_conv_b, lru_wa, lru_ba, lru_wx, lru_bx, lru_lambda, rwkv_mu, rwkv_w0,
              rwkv_w2, rwkv_a0, rwkv_a2, rwkv_g2, rwkv_k_k, rwkv_k_a, rwkv_r_k, rwkv_ln_w, rwkv_ln_b, rwkv_v0,
              rwkv_v1, rwkv_v2, mem_norm, norm_xattn, xattn_wq, xattn_wk, xattn_wv, xattn_wo, norm_ffn,
              ffn_w_gate, ffn_w_up, ffn_w_down, moe_router, moe_w_gate, moe_w_up, moe_w_down, norm_final):
    mem_n = rmsnorm(mem, mem_norm)
    lb = jnp.cumsum(jax.nn.softmax(hgrn_lb_logits.astype(jnp.float32), axis=0), axis=0)
    lb = lb - lb[0]
    v_first = None
    for l in range(DEPTH):
        h = rmsnorm(x, norm_mix[l])
        p = h @ w_in[l]
        p_a, p_b, p_c, p_d = split_cols(p, (HG_COLS, GDN_COLS, LRU_COLS, RWKV_COLS))
        q_a, f_a, i_a, g_a = split_cols(p_a, (GROUP, GROUP, GROUP, GROUP))
        y_a = hgrn2_mixer(q_a, f_a, i_a, g_a, lb[l], hgrn_norm[l])
        qkv_b, z_b, alpha_b, beta_b = split_cols(p_b, (3 * GROUP, GROUP, N_HEADS_GROUP, N_HEADS_GROUP))
        y_b = gated_deltanet_mixer(qkv_b, z_b, alpha_b, beta_b, gdn_conv_w[l], gdn_A_log[l], gdn_dt_bias[l],
                                   gdn_norm[l])
        x_c, gate_c = split_cols(p_c, (GROUP, GROUP))
        y_c = rglru_mixer(x_c, gate_c, lru_conv_w[l], lru_conv_b[l], lru_wa[l], lru_ba[l], lru_wx[l], lru_bx[l],
                          lru_lambda[l])
        v_mix = None if l == 0 else (rwkv_v0[l - 1], rwkv_v1[l - 1], rwkv_v2[l - 1])
        y_d, v_first = rwkv7_mixer(p_d, rwkv_mu[l], rwkv_w0[l], rwkv_w2[l], rwkv_a0[l], rwkv_a2[l], rwkv_g2[l],
                                   rwkv_k_k[l], rwkv_k_a[l], rwkv_r_k[l], rwkv_ln_w[l], rwkv_ln_b[l], v_first, v_mix)
        x = x + jnp.concatenate([y_a, y_b, y_c, y_d], axis=-1) @ w_out[l]
        x = x + memory_cross_attention(rmsnorm(x, norm_xattn[l]), mem_n, xattn_wq[l], xattn_wk[l], xattn_wv[l],
                                       xattn_wo[l])
        h = rmsnorm(x, norm_ffn[l])
        if l % 2 == 0:
            x = x + swiglu(h, ffn_w_gate[l // 2], ffn_w_up[l // 2], ffn_w_down[l // 2])
        else:
            x = x + moe_swiglu(h, moe_router[l // 2], moe_w_gate[l // 2], moe_w_up[l // 2], moe_w_down[l // 2])
    return rmsnorm(x, norm_final)
```

```python
import functools
import math

import jax
import jax.numpy as jnp
from jax import lax
from jax.experimental import pallas as pl
from jax.experimental.pallas import tpu as pltpu

F32 = jnp.float32
BF16 = jnp.bfloat16
HIGHEST = lax.Precision.HIGHEST

D_MODEL = 1024
GROUP = 256
HEAD = 64
HEADS = GROUP // HEAD
CHUNK = 64
SUB = 16
CONV_K = 4
HALO = 8
LRU_C = 8.0
RWKV_GN_EPS = 64e-5
XATTN_HEADS = 4
N_EXPERTS = 8
EPS = 1e-6
LANES = 128
VMEM_LIMIT_BYTES = 56 * 1024 * 1024


def _params(*semantics):
    return pltpu.CompilerParams(dimension_semantics=semantics, vmem_limit_bytes=VMEM_LIMIT_BYTES)


def _full(shape):
    n = len(shape)
    return pl.BlockSpec(shape, lambda *_: (0,) * n)


def _bmm(a, b, nt=False, precision=HIGHEST):
    dn = (((2,), (2 if nt else 1,)), ((0,), (0,)))
    return lax.dot_general(a, b, dn, precision=precision, preferred_element_type=F32)


def _mm(a, b, precision=None):
    return jnp.dot(a, b, precision=precision, preferred_element_type=F32)


def _softplus(x):
    return jnp.maximum(x, 0.0) + jnp.log1p(jnp.exp(-jnp.abs(x)))


def _silu(x):
    return x * jax.nn.sigmoid(x)


def _rms(x, gain):
    return x * lax.rsqrt(jnp.mean(x * x, axis=-1, keepdims=True) + EPS) * gain


def _tri_masks(n):
    t = lax.broadcasted_iota(jnp.int32, (n, n), 0)
    s = lax.broadcasted_iota(jnp.int32, (n, n), 1)
    return t, s


def _cumsum_rows(x):
    g, c, _ = x.shape
    t, s = _tri_masks(c)
    tri = jnp.broadcast_to((s <= t).astype(F32), (g, c, c))
    return _bmm(tri, x)


def _solve_unit_lower(low, rhs):
    c = low.shape[1]
    r = rhs.shape[2]
    assert c == 4 * SUB and SUB == 16
    t, s = _tri_masks(c)
    same_block = (t // SUB) == (s // SUB)
    d1 = jnp.where(same_block, low, 0.0)
    z = jnp.concatenate([rhs, jnp.where(same_block, 0.0, low)], axis=-1)
    d2 = _bmm(d1, d1)
    d4 = _bmm(d2, d2)
    d8 = _bmm(d4, d4)
    z = z + _bmm(d8, z)
    z = z + _bmm(d4, z)
    z = z + _bmm(d2, z)
    z = z - _bmm(d1, z)
    y = z[:, :, :r]
    n1 = z[:, :, r:]
    n2 = _bmm(n1, n1)
    y = y + _bmm(n2, y)
    return y - _bmm(n1, y)


def _norm_mm_kernel(x_ref, g_ref, w_ref, o_ref, hn_ref):
    @pl.when(pl.program_id(1) == 0)
    def _():
        hn_ref[...] = _rms(x_ref[...], g_ref[...]).astype(BF16)

    o_ref[...] = _mm(hn_ref[...], w_ref[...])


def _norm_matmul(x, gain, w, tm, tn):
    t, k = x.shape
    n = w.shape[1]
    return pl.pallas_call(
        _norm_mm_kernel,
        grid=(t // tm, n // tn),
        in_specs=[pl.BlockSpec((tm, k), lambda i, j: (i, 0)),
                  pl.BlockSpec((1, k), lambda i, j: (0, 0)),
                  pl.BlockSpec((k, tn), lambda i, j: (0, j))],
        out_specs=pl.BlockSpec((tm, tn), lambda i, j: (i, j)),
        out_shape=jax.ShapeDtypeStruct((t, n), F32),
        scratch_shapes=[pltpu.VMEM((tm, k), BF16)],
        compiler_params=_params("parallel", "arbitrary"),
        name="norm_matmul",
    )(x, gain.reshape(1, k), w)


def _mm_res_kernel(a_ref, w_ref, r_ref, o_ref):
    o_ref[...] = r_ref[...] + _mm(a_ref[...].astype(BF16), w_ref[...])


def _matmul_residual(a, w, res, tm):
    t, k = a.shape
    n = w.shape[1]
    return pl.pallas_call(
        _mm_res_kernel,
        grid=(t // tm,),
        in_specs=[pl.BlockSpec((tm, k), lambda i: (i, 0)),
                  pl.BlockSpec((k, n), lambda i: (0, 0)),
                  pl.BlockSpec((tm, n), lambda i: (i, 0))],
        out_specs=pl.BlockSpec((tm, n), lambda i: (i, 0)),
        out_shape=jax.ShapeDtypeStruct((t, n), F32),
        compiler_params=_params("parallel"),
        name="matmul_residual",
    )(a, w, res)


def _xattn_kernel(x_ref, g_ref, wq_ref, k_ref, v_ref, wo_ref, o_ref):
    x = x_ref[0]
    hn = _rms(x, g_ref[...]).astype(BF16)
    q = _mm(hn, wq_ref[...])
    d = q.shape[1] // XATTN_HEADS
    outs = []
    for h in range(XATTN_HEADS):
        qh = (q[:, h * d:(h + 1) * d] * d ** -0.5).astype(BF16)
        kh = k_ref[0, :, h * d:(h + 1) * d]
        vh = v_ref[0, :, h * d:(h + 1) * d]
        s = lax.dot_general(qh, kh, (((1,), (1,)), ((), ())), preferred_element_type=F32)
        s = s - jnp.max(s, axis=-1, keepdims=True)
        p = jnp.exp(s)
        p = p / jnp.sum(p, axis=-1, keepdims=True)
        outs.append(_mm(p.astype(BF16), vh))
    o = jnp.concatenate(outs, axis=-1).astype(BF16)
    o_ref[0] = x + _mm(o, wo_ref[...])


def _cross_attention(x, gain, wq, k_mem, v_mem, wo, tm):
    b, s, d = x.shape
    m = k_mem.shape[1]
    return pl.pallas_call(
        _xattn_kernel,
        grid=(b, s // tm),
        in_specs=[pl.BlockSpec((1, tm, d), lambda i, j: (i, j, 0)),
                  _full((1, d)),
                  _full((d, d)),
                  pl.BlockSpec((1, m, d), lambda i, j: (i, 0, 0)),
                  pl.BlockSpec((1, m, d), lambda i, j: (i, 0, 0)),
                  _full((d, d))],
        out_specs=pl.BlockSpec((1, tm, d), lambda i, j: (i, j, 0)),
        out_shape=jax.ShapeDtypeStruct((b, s, d), F32),
        compiler_params=_params("parallel", "parallel"),
        name="cross_attention",
    )(x, gain.reshape(1, d), wq, k_mem, v_mem, wo)


def _ffn_kernel(x_ref, g_ref, wg_ref, wu_ref, wd_ref, c_ref, o_ref, hn_ref, acc_ref):
    f = pl.program_id(1)

    @pl.when(f == 0)
    def _():
        hn_ref[...] = _rms(x_ref[...], g_ref[...]).astype(BF16)
        acc_ref[...] = x_ref[...]

    hn = hn_ref[...]
    gate = _mm(hn, wg_ref[0])
    up = _mm(hn, wu_ref[0])
    mid = _silu(gate) * up * c_ref[0]
    acc_ref[...] += _mm(mid.astype(BF16), wd_ref[0])

    @pl.when(f == pl.num_programs(1) - 1)
    def _():
        o_ref[...] = acc_ref[...]


def _ffn(x, gain, wg, wu, wd, combine, tm, tf):
    t, d = x.shape
    e, _, ff = wg.shape
    nf = ff // tf
    return pl.pallas_call(
        _ffn_kernel,
        grid=(t // tm, e * nf),
        in_specs=[pl.BlockSpec((tm, d), lambda i, f: (i, 0)),
                  _full((1, d)),
                  pl.BlockSpec((1, d, tf), lambda i, f: (f // nf, 0, f % nf)),
                  pl.BlockSpec((1, d, tf), lambda i, f: (f // nf, 0, f % nf)),
                  pl.BlockSpec((1, tf, d), lambda i, f: (f // nf, f % nf, 0)),
                  pl.BlockSpec((1, tm, 1), lambda i, f: (f // nf, i, 0))],
        out_specs=pl.BlockSpec((tm, d), lambda i, f: (i, 0)),
        out_shape=jax.ShapeDtypeStruct((t, d), F32),
        scratch_shapes=[pltpu.VMEM((tm, d), BF16), pltpu.VMEM((tm, d), F32)],
        compiler_params=_params("parallel", "arbitrary"),
        name="ffn",
    )(x, gain.reshape(1, d), wg, wu, wd, combine)


def _router_kernel(x_ref, g_ref, w_ref, o_ref):
    hn = _rms(x_ref[...], g_ref[...])
    logits = _mm(hn, w_ref[...], precision=HIGHEST)
    lane = lax.broadcasted_iota(jnp.int32, logits.shape, 1)
    neg = jnp.float32(-jnp.inf)
    logits = jnp.where(lane < N_EXPERTS, logits, neg)
    m1 = jnp.max(logits, axis=-1, keepdims=True)
    i1 = jnp.min(jnp.where(logits == m1, lane, LANES), axis=-1, keepdims=True)
    rest = jnp.where(lane == i1, neg, logits)
    m2 = jnp.max(rest, axis=-1, keepdims=True)
    i2 = jnp.min(jnp.where(rest == m2, lane, LANES), axis=-1, keepdims=True)
    e2 = jnp.exp(m2 - m1)
    g1 = 1.0 / (1.0 + e2)
    g2 = e2 / (1.0 + e2)
    o_ref[...] = jnp.where(lane == i1, g1, 0.0) + jnp.where(lane == i2, g2, 0.0)


def _router(x, gain, w_router, tm):
    t, d = x.shape
    w = jnp.zeros((d, LANES), F32).at[:, :N_EXPERTS].set(w_router)
    return pl.pallas_call(
        _router_kernel,
        grid=(t // tm,),
        in_specs=[pl.BlockSpec((tm, d), lambda i: (i, 0)), _full((1, d)), _full((d, LANES))],
        out_specs=pl.BlockSpec((tm, LANES), lambda i: (i, 0)),
        out_shape=jax.ShapeDtypeStruct((t, LANES), F32),
        compiler_params=_params("parallel"),
        name="router",
    )(x, gain.reshape(1, d), w)


def _final_norm_kernel(x_ref, g_ref, o_ref):
    o_ref[...] = _rms(x_ref[...], g_ref[...])


def _final_norm(x, gain, tm):
    t, d = x.shape
    return pl.pallas_call(
        _final_norm_kernel,
        grid=(t // tm,),
        in_specs=[pl.BlockSpec((tm, d), lambda i: (i, 0)), _full((1, d))],
        out_specs=pl.BlockSpec((tm, d), lambda i: (i, 0)),
        out_shape=jax.ShapeDtypeStruct((t, d), F32),
        compiler_params=_params("parallel"),
        name="final_norm",
    )(x, gain.reshape(1, d))


def _hm_spec():
    return pl.BlockSpec((1, HEADS, CHUNK, HEAD), lambda b, c: (b, 0, c, 0))


def _hgrn_kernel(q_ref, f_ref, i_ref, g_ref, loglb_ref, log1mlb_ref, onemlb_ref, ng_ref, o_ref, st_ref,
                 rows_ref):
    @pl.when(pl.program_id(1) == 0)
    def _():
        st_ref[...] = jnp.zeros_like(st_ref)

    q, fl, v, g = q_ref[0], f_ref[0], i_ref[0], g_ref[0]
    shape = q.shape
    log_sig = jnp.minimum(fl, 0.0) - jnp.log1p(jnp.exp(-jnp.abs(fl)))
    la = loglb_ref[...]
    lc = log1mlb_ref[...] + log_sig
    log_f = jnp.maximum(la, lc) + jnp.log1p(jnp.exp(-jnp.abs(la - lc)))
    k = onemlb_ref[...] * jax.nn.sigmoid(-fl)
    b = _cumsum_rows(log_f)

    rows_ref[0] = b
    rows_ref[1] = k
    rows_ref[2] = v
    s_sub = lax.broadcasted_iota(jnp.int32, (SUB, CHUNK), 1)
    zeros_sub = jnp.zeros((shape[0], SUB, CHUNK), F32)
    a_rows = [zeros_sub]
    for i in range(1, CHUNK // SUB):
        bref = rows_ref[0, :, i * SUB - 1:i * SUB, :]
        qd = q[:, i * SUB:(i + 1) * SUB] * jnp.exp(b[:, i * SUB:(i + 1) * SUB] - bref)
        kd = k * jnp.exp(jnp.minimum(bref - b, 0.0))
        a_i = _bmm(qd, kd, nt=True)
        a_rows.append(jnp.where(s_sub < i * SUB, a_i, 0.0))
    o = _bmm(jnp.concatenate(a_rows, axis=1), v)

    t_loc = lax.broadcasted_iota(jnp.int32, (shape[0], SUB, HEAD), 1)
    diag = []
    for i in range(CHUNK // SUB):
        sl = slice(i * SUB, (i + 1) * SUB)
        bb, qq = b[:, sl], q[:, sl]
        acc = jnp.zeros((shape[0], SUB, HEAD), F32)
        for s in range(SUB):
            row = i * SUB + s
            e = jnp.where(t_loc >= s, jnp.exp(jnp.minimum(bb - rows_ref[0, :, row:row + 1, :], 0.0)), 0.0)
            w = jnp.sum(e * qq * rows_ref[1, :, row:row + 1, :], axis=-1, keepdims=True)
            acc = acc + w * rows_ref[2, :, row:row + 1, :]
        diag.append(acc)
    o = o + jnp.concatenate(diag, axis=1)

    st = st_ref[...]
    o = o + _bmm(q * jnp.exp(b), st, nt=True)
    b_last = b[:, CHUNK - 1:CHUNK]
    st_ref[...] = st * jnp.exp(b_last) + _bmm(jnp.swapaxes(v, 1, 2), k * jnp.exp(b_last - b))

    o_ref[0] = _rms(o, ng_ref[...]) * _silu(g)


def _hgrn_mixer(q, f, i, g, lb, norm_g):
    b, h, s, d = q.shape
    par = _full((h, 1, d))
    return pl.pallas_call(
        _hgrn_kernel,
        grid=(b, s // CHUNK),
        in_specs=[_hm_spec()] * 4 + [par] * 4,
        out_specs=_hm_spec(),
        out_shape=jax.ShapeDtypeStruct((b, h, s, d), F32),
        scratch_shapes=[pltpu.VMEM((h, d, d), F32), pltpu.VMEM((3, h, CHUNK, d), F32)],
        compiler_params=_params("parallel", "arbitrary"),
        name="hgrn2",
    )(q, f, i, g, jnp.log(lb), jnp.log1p(-lb), 1.0 - lb, norm_g)


def _gdn_kernel(qkv_ref, z_ref, acol_ref, bcol_ref, arow_ref, cw_ref, alog_ref, dtb_ref, ng_ref,
                o_ref, xbuf_ref, st_ref):
    h3 = xbuf_ref.shape[0]
    h = h3 // 3

    @pl.when(pl.program_id(1) == 0)
    def _():
        st_ref[...] = jnp.zeros_like(st_ref)
        xbuf_ref[:, 0:HALO, :] = jnp.zeros((h3, HALO, HEAD), F32)

    xbuf_ref[:, HALO:HALO + CHUNK, :] = qkv_ref[0]
    conv = jnp.zeros((h3, CHUNK, HEAD), F32)
    for j in range(CONV_K):
        conv = conv + cw_ref[j] * xbuf_ref[:, HALO - (CONV_K - 1) + j:HALO - (CONV_K - 1) + j + CHUNK, :]
    xbuf_ref[:, 0:HALO, :] = xbuf_ref[:, CHUNK:CHUNK + HALO, :]
    qkv = _silu(conv)
    q, k, v = qkv[0:h], qkv[h:2 * h], qkv[2 * h:3 * h]
    q = q * lax.rsqrt(jnp.sum(q * q, axis=-1, keepdims=True) + EPS) * HEAD ** -0.5
    k = k * lax.rsqrt(jnp.sum(k * k, axis=-1, keepdims=True) + EPS)

    neg_a = -jnp.exp(alog_ref[...])
    g_col = neg_a * _softplus(acol_ref[0] + dtb_ref[...])
    g_row = neg_a * _softplus(arow_ref[0, 0] + dtb_ref[...])
    beta = jax.nn.sigmoid(bcol_ref[0])
    t_idx, s_idx = _tri_masks(CHUNK)
    causal = s_idx <= t_idx
    gc_col = jnp.sum(jnp.where(causal, g_row, 0.0), axis=-1, keepdims=True)
    gc_row = jnp.sum(jnp.where(t_idx <= s_idx, g_col, 0.0), axis=1, keepdims=True)
    rel = jnp.where(causal, jnp.exp(jnp.minimum(gc_col - gc_row, 0.0)), 0.0)

    kb = k * beta
    lower = jnp.where(s_idx < t_idx, _bmm(kb, k, nt=True) * rel, 0.0)
    uw = _solve_unit_lower(lower, jnp.concatenate([v * beta, kb * jnp.exp(gc_col)], axis=-1))
    u, w = uw[:, :, :HEAD], uw[:, :, HEAD:]
    scores = _bmm(q, k, nt=True) * rel

    st = st_ref[...]
    v_new = u - _bmm(w, st)
    o = _bmm(q * jnp.exp(gc_col), st) + _bmm(scores, v_new)
    g_last = gc_col[:, CHUNK - 1:CHUNK]
    kd = k * jnp.exp(g_last - gc_col)
    st_ref[...] = st * jnp.exp(g_last) + _bmm(jnp.swapaxes(kd, 1, 2), v_new)

    o_ref[0] = _rms(o, ng_ref[...]) * _silu(z_ref[0])


def _gdn_mixer(qkv, z, a_col, b_col, a_row, conv_w, a_log, dt_bias, norm_g):
    b, h3, s, d = qkv.shape
    h = h3 // 3
    col = pl.BlockSpec((1, h, CHUNK, 1), lambda i, c: (i, 0, c, 0))
    return pl.pallas_call(
        _gdn_kernel,
        grid=(b, s // CHUNK),
        in_specs=[pl.BlockSpec((1, h3, CHUNK, d), lambda i, c: (i, 0, c, 0)), _hm_spec(), col, col,
                  pl.BlockSpec((1, 1, h, 1, CHUNK), lambda i, c: (i, c, 0, 0, 0)),
                  _full((CONV_K, h3, 1, d)), _full((h, 1, 1)), _full((h, 1, 1)), _full((1, 1, d))],
        out_specs=_hm_spec(),
        out_shape=jax.ShapeDtypeStruct((b, h, s, d), F32),
        scratch_shapes=[pltpu.VMEM((h3, HALO + CHUNK, d), F32), pltpu.VMEM((h, d, d), F32)],
        compiler_params=_params("parallel", "arbitrary"),
        name="gated_deltanet",
    )(qkv, z, a_col, b_col, a_row, conv_w, a_log, dt_bias, norm_g)


def _lru_kernel(x_ref, gate_ref, cw_ref, cb_ref, wa_ref, ba_ref, wx_ref, bx_ref, lam_ref, o_ref,
                xbuf_ref, h_ref):
    tt = x_ref.shape[1]

    @pl.when(pl.program_id(1) == 0)
    def _():
        h_ref[...] = jnp.zeros_like(h_ref)
        xbuf_ref[0:HALO, :] = jnp.zeros((HALO, GROUP), F32)

    xbuf_ref[HALO:HALO + tt, :] = x_ref[0]
    xc = jnp.zeros((tt, GROUP), F32) + cb_ref[...]
    for j in range(CONV_K):
        xc = xc + cw_ref[j:j + 1, :] * xbuf_ref[HALO - (CONV_K - 1) + j:HALO - (CONV_K - 1) + j + tt, :]
    xbuf_ref[0:HALO, :] = xbuf_ref[tt:tt + HALO, :]

    r = jax.nn.sigmoid(_mm(xc, wa_ref[...], precision=HIGHEST) + ba_ref[...])
    i = jax.nn.sigmoid(_mm(xc, wx_ref[...], precision=HIGHEST) + bx_ref[...])
    log_a = -LRU_C * r * _softplus(-lam_ref[...])
    a = jnp.exp(log_a)
    u = jnp.sqrt(1.0 - a * a) * (i * xc)

    row = lax.broadcasted_iota(jnp.int32, (tt, GROUP), 0)
    shift = 1
    while shift < tt:
        keep = row >= shift
        a_prev = pltpu.roll(a, shift, 0)
        u_prev = pltpu.roll(u, shift, 0)
        u = jnp.where(keep, a * u_prev + u, u)
        a = jnp.where(keep, a * a_prev, a)
        shift *= 2
    hs = u + a * h_ref[...]
    h_ref[...] = hs[tt - 1:tt, :]

    gate = gate_ref[0]
    gelu = 0.5 * gate * (1.0 + jnp.tanh(math.sqrt(2.0 / math.pi) * (gate + 0.044715 * gate * gate * gate)))
    o_ref[0] = hs * gelu


def _lru_mixer(x, gate, conv_w, conv_b, wa, ba, wx, bx, lam, tt):
    b, s, c = x.shape
    tile = pl.BlockSpec((1, tt, c), lambda i, t: (i, t, 0))
    vec = _full((1, c))
    return pl.pallas_call(
        _lru_kernel,
        grid=(b, s // tt),
        in_specs=[tile, tile, _full((CONV_K, c)), vec, _full((c, c)), vec, _full((c, c)), vec, vec],
        out_specs=tile,
        out_shape=jax.ShapeDtypeStruct((b, s, c), F32),
        scratch_shapes=[pltpu.VMEM((HALO + tt, c), F32), pltpu.VMEM((1, c), F32)],
        compiler_params=_params("parallel", "arbitrary"),
        name="rg_lru",
    )(x, gate, conv_w, conv_b.reshape(1, c), wa, ba.reshape(1, c), wx, bx.reshape(1, c), lam.reshape(1, c))


def _rwkv_prep_kernel(has_vmix, *refs):
    if has_vmix:
        (p_ref, mu_ref, w0_ref, w2_ref, a0_ref, a2_ref, g2_ref, vf_ref, v0_ref, v1_ref, v2_ref,
         r_ref, lw_ref, k_ref, v_ref, a_ref, g_ref, xbuf_ref) = refs
    else:
        (p_ref, mu_ref, w0_ref, w2_ref, a0_ref, a2_ref, g2_ref,
         r_ref, lw_ref, k_ref, v_ref, a_ref, g_ref, xbuf_ref) = refs
    tt = p_ref.shape[1]

    @pl.when(pl.program_id(1) == 0)
    def _():
        xbuf_ref[0:HALO, :] = jnp.zeros((HALO, xbuf_ref.shape[1]), F32)

    c = p_ref[0]
    xbuf_ref[HALO:HALO + tt, :] = c
    prev = xbuf_ref[HALO - 1:HALO - 1 + tt, :]
    xbuf_ref[0:HALO, :] = xbuf_ref[tt:tt + HALO, :]
    c = c + mu_ref[...] * (prev - c)

    r = c[:, 0:GROUP]
    k = c[:, GROUP:2 * GROUP]
    v = c[:, 2 * GROUP:3 * GROUP]
    wa_lo = c[:, 3 * GROUP:3 * GROUP + LANES]
    g_lo = c[:, 3 * GROUP + LANES:4 * GROUP]
    lw_ref[0] = -math.exp(-0.5) * jax.nn.sigmoid(w0_ref[...] + _mm(jnp.tanh(wa_lo), w2_ref[...], precision=HIGHEST))
    a_ref[0] = jax.nn.sigmoid(a0_ref[...] + _mm(wa_lo, a2_ref[...], precision=HIGHEST))
    g_ref[0] = _mm(jax.nn.sigmoid(g_lo), g2_ref[...], precision=HIGHEST)
    if has_vmix:
        mix = jax.nn.sigmoid(v0_ref[...] + _mm(_mm(v, v1_ref[...], precision=HIGHEST), v2_ref[...], precision=HIGHEST))
        v = v + (vf_ref[0] - v) * mix
    r_ref[0] = r
    k_ref[0] = k
    v_ref[0] = v


def _rwkv_prep(p_d, mu, w0, w2, a0, a2, g2, v_first, v_mix, tt):
    b, s, n = p_d.shape
    lora = w2.shape[0]
    w2p = jnp.zeros((LANES, GROUP), F32).at[:lora].set(w2)
    a2p = jnp.zeros((LANES, GROUP), F32).at[lora:].set(a2)
    tile_in = pl.BlockSpec((1, tt, n), lambda i, t: (i, t, 0))
    tile = pl.BlockSpec((1, tt, GROUP), lambda i, t: (i, t, 0))
    vec = _full((1, GROUP))
    args = [p_d, mu.reshape(1, n), w0.reshape(1, GROUP), w2p, a0.reshape(1, GROUP), a2p, g2]
    specs = [tile_in, _full((1, n)), vec, _full((LANES, GROUP)), vec, _full((LANES, GROUP)), _full(g2.shape)]
    if v_mix is not None:
        v0, v1, v2 = v_mix
        rank = v1.shape[1]
        v1p = jnp.zeros((GROUP, LANES), F32).at[:, :rank].set(v1)
        v2p = jnp.zeros((LANES, GROUP), F32).at[:rank].set(v2)
        args += [v_first, v0.reshape(1, GROUP), v1p, v2p]
        specs += [tile, vec, _full((GROUP, LANES)), _full((LANES, GROUP))]
    out = jax.ShapeDtypeStruct((b, s, GROUP), F32)
    return pl.pallas_call(
        functools.partial(_rwkv_prep_kernel, v_mix is not None),
        grid=(b, s // tt),
        in_specs=specs,
        out_specs=[tile] * 6,
        out_shape=[out] * 6,
        scratch_shapes=[pltpu.VMEM((HALO + tt, n), F32)],
        compiler_params=_params("parallel", "arbitrary"),
        name="rwkv7_prep",
    )(*args)


def _rwkv_kernel(r_ref, lw_ref, k_ref, v_ref, a_ref, g_ref, kk_ref, ka_ref, rk_ref, lnw_ref, lnb_ref,
                 o_ref, zt_ref):
    @pl.when(pl.program_id(1) == 0)
    def _():
        zt_ref[...] = jnp.zeros_like(zt_ref)

    r, lw, k, v, a, g = r_ref[0], lw_ref[0], k_ref[0], v_ref[0], a_ref[0], g_ref[0]
    kk = k * kk_ref[...]
    kk = kk * lax.rsqrt(jnp.sum(kk * kk, axis=-1, keepdims=True) + EPS)
    k = k * (1.0 + (a - 1.0) * ka_ref[...])
    b = _cumsum_rows(lw)
    e_pos = jnp.exp(b)
    e_neg = jnp.exp(-b)
    r_t = r * e_pos
    al_t = -kk * jnp.exp(b - lw)
    be_t = kk * a * e_neg
    k_t = k * e_neg

    t_idx, s_idx = _tri_masks(CHUNK)
    strict = s_idx < t_idx
    incl = s_idx <= t_idx
    ab_k = jnp.concatenate([be_t, k_t], axis=1)
    al_all = _bmm(al_t, ab_k, nt=True)
    r_all = _bmm(r_t, ab_k, nt=True)
    l_ab = jnp.where(strict, al_all[:, :, :CHUNK], 0.0)
    l_ak = jnp.where(strict, al_all[:, :, CHUNK:], 0.0)
    m_rb = jnp.where(incl, r_all[:, :, :CHUNK], 0.0)
    m_rk = jnp.where(incl, r_all[:, :, CHUNK:], 0.0)

    sol = _solve_unit_lower(-l_ab, jnp.concatenate([al_t, _bmm(l_ak, v)], axis=-1))
    w1, u0 = sol[:, :, :HEAD], sol[:, :, HEAD:]

    zt = zt_ref[...]
    u = _bmm(w1, zt, nt=True) + u0
    y = _bmm(r_t, zt, nt=True) + _bmm(m_rb, u) + _bmm(m_rk, v)
    b_last = b[:, CHUNK - 1:CHUNK]
    e_last = jnp.exp(b_last - b)
    uv_t = jnp.swapaxes(jnp.concatenate([u, v], axis=1), 1, 2)
    zt_ref[...] = zt * jnp.exp(b_last) + _bmm(uv_t, jnp.concatenate([kk * a * e_last, k * e_last], axis=1))

    mean = jnp.mean(y, axis=-1, keepdims=True)
    var = jnp.mean(jnp.square(y - mean), axis=-1, keepdims=True)
    yn = (y - mean) * lax.rsqrt(var + RWKV_GN_EPS) * lnw_ref[...] + lnb_ref[...]
    bonus = jnp.sum(r * k * rk_ref[...], axis=-1, keepdims=True) * v
    o_ref[0] = (yn + bonus) * g


def _rwkv_mixer(r, lw, k, v, a, g, k_k, k_a, r_k, ln_w, ln_b):
    b, h, s, d = r.shape
    par = _full((h, 1, d))
    return pl.pallas_call(
        _rwkv_kernel,
        grid=(b, s // CHUNK),
        in_specs=[_hm_spec()] * 6 + [par] * 5,
        out_specs=_hm_spec(),
        out_shape=jax.ShapeDtypeStruct((b, h, s, d), F32),
        scratch_shapes=[pltpu.VMEM((h, d, d), F32)],
        compiler_params=_params("parallel", "arbitrary"),
        name="rwkv7",
    )(r, lw, k, v, a, g, k_k, k_a, r_k, ln_w, ln_b)


def _to_hm(t):
    b, s, c = t.shape
    return t.reshape(b, s, c // HEAD, HEAD).transpose(0, 2, 1, 3)


def _from_hm(t):
    b, h, s, d = t.shape
    return t.transpose(0, 2, 1, 3).reshape(b, s, h * d)


def _par_hm(p):
    return p.reshape(-1, 1, HEAD)


def _block_diag(w):
    n, a, b = w.shape
    out = jnp.zeros((n * a, n * b), w.dtype)
    for i in range(n):
        out = out.at[i * a:(i + 1) * a, i * b:(i + 1) * b].set(w[i])
    return out


def kernel(x, mem, norm_mix, w_in, w_out, hgrn_lb_logits, hgrn_norm, gdn_conv_w, gdn_A_log, gdn_dt_bias, gdn_norm, lru_conv_w, lru_conv_b, lru_wa, lru_ba, lru_wx, lru_bx, lru_lambda, rwkv_mu, rwkv_w0, rwkv_w2, rwkv_a0, rwkv_a2, rwkv_g2, rwkv_k_k, rwkv_k_a, rwkv_r_k, rwkv_ln_w, rwkv_ln_b, rwkv_v0, rwkv_v1, rwkv_v2, mem_norm, norm_xattn, xattn_wq, xattn_wk, xattn_wv, xattn_wo, norm_ffn, ffn_w_gate, ffn_w_up, ffn_w_down, moe_router, moe_w_gate, moe_w_up, moe_w_down, norm_final):
    bsz, seq, d = x.shape
    depth = w_in.shape[0]
    tok = bsz * seq
    mlen = mem.shape[1]
    g4 = 4 * GROUP

    lb = jnp.cumsum(jax.nn.softmax(hgrn_lb_logits.astype(F32), axis=0), axis=0)
    lb = lb - lb[0]

    n_small = 2 * HEADS
    off_b = g4
    off_ab = off_b + g4
    off_c = off_ab + n_small
    off_d = off_c + 2 * GROUP
    n_main = g4 + g4 + 2 * GROUP + g4

    xf = x.reshape(tok, d)
    mem_f = mem.reshape(bsz * mlen, d)
    v_first = None
    for l in range(depth):
        w = w_in[l]
        w_cat = jnp.concatenate([w[:, :off_ab], w[:, off_c:], w[:, off_ab:off_c],
                                 jnp.zeros((d, LANES - n_small), F32)], axis=1).astype(BF16)
        p = _norm_matmul(xf, norm_mix[l], w_cat, 256, n_main + LANES).reshape(bsz, seq, n_main + LANES)
        p_a = p[..., 0:g4]
        p_b = p[..., off_b:off_b + g4]
        p_c = p[..., off_ab:off_ab + 2 * GROUP]
        p_d = p[..., off_ab + 2 * GROUP:n_main]
        p_ab = p[..., n_main:n_main + n_small]

        y_a = _from_hm(_hgrn_mixer(_to_hm(p_a[..., 0:GROUP]), _to_hm(p_a[..., GROUP:2 * GROUP]),
                                   _to_hm(p_a[..., 2 * GROUP:3 * GROUP]), _to_hm(p_a[..., 3 * GROUP:]),
                                   _par_hm(lb[l]), _par_hm(hgrn_norm[l])))

        alpha = p_ab[..., :HEADS]
        beta = p_ab[..., HEADS:]
        a_col = alpha.transpose(0, 2, 1)[..., None]
        b_col = beta.transpose(0, 2, 1)[..., None]
        a_row = alpha.reshape(bsz, seq // CHUNK, CHUNK, HEADS).transpose(0, 1, 3, 2)[:, :, :, None, :]
        y_b = _from_hm(_gdn_mixer(_to_hm(p_b[..., :3 * GROUP]), _to_hm(p_b[..., 3 * GROUP:]), a_col, b_col, a_row,
                                  gdn_conv_w[l].reshape(CONV_K, 3 * HEADS, 1, HEAD),
                                  gdn_A_log[l].reshape(HEADS, 1, 1), gdn_dt_bias[l].reshape(HEADS, 1, 1),
                                  gdn_norm[l].reshape(1, 1, HEAD)))

        y_c = _lru_mixer(p_c[..., :GROUP], p_c[..., GROUP:], lru_conv_w[l], lru_conv_b[l],
                         _block_diag(lru_wa[l]), lru_ba[l], _block_diag(lru_wx[l]), lru_bx[l], lru_lambda[l], 512)

        v_mix = None if l == 0 else (rwkv_v0[l - 1], rwkv_v1[l - 1], rwkv_v2[l - 1])
        r_d, lw_d, k_d, v_d, a_d, g_d = _rwkv_prep(p_d, rwkv_mu[l], rwkv_w0[l], rwkv_w2[l], rwkv_a0[l],
                                                   rwkv_a2[l], rwkv_g2[l], v_first, v_mix, 512)
        if l == 0:
            v_first = v_d
        y_d = _from_hm(_rwkv_mixer(_to_hm(r_d), _to_hm(lw_d), _to_hm(k_d), _to_hm(v_d), _to_hm(a_d), _to_hm(g_d),
                                   _par_hm(rwkv_k_k[l]), _par_hm(rwkv_k_a[l]), _par_hm(rwkv_r_k[l]),
                                   _par_hm(rwkv_ln_w[l]), _par_hm(rwkv_ln_b[l])))

        y = jnp.concatenate([y_a, y_b, y_c, y_d], axis=-1).reshape(tok, 4 * GROUP)
        xf = _matmul_residual(y, w_out[l].astype(BF16), xf, 512)

        w_kv = jnp.concatenate([xattn_wk[l], xattn_wv[l]], axis=1).astype(BF16)
        kv = _norm_matmul(mem_f, mem_norm, w_kv, 256, 2 * d).astype(BF16).reshape(bsz, mlen, 2 * d)
        xf = _cross_attention(xf.reshape(bsz, seq, d), norm_xattn[l], xattn_wq[l].astype(BF16),
                              kv[..., :d], kv[..., d:], xattn_wo[l].astype(BF16), 512).reshape(tok, d)

        if l % 2 == 0:
            j = l // 2
            ones = jnp.ones((1, tok, 1), F32)
            xf = _ffn(xf, norm_ffn[l], ffn_w_gate[j][None].astype(BF16), ffn_w_up[j][None].astype(BF16),
                      ffn_w_down[j][None].astype(BF16), ones, 1024, 256)
        else:
            j = l // 2
            comb = _router(xf, norm_ffn[l], moe_router[j], 1024)
            comb = comb[:, :N_EXPERTS].T[..., None]
            xf = _ffn(xf, norm_ffn[l], moe_w_gate[j].astype(BF16), moe_w_up[j].astype(BF16),
                      moe_w_down[j].astype(BF16), comb, 1024, 512)
    return _final_norm(xf, norm_final, 1024).reshape(bsz, seq, d)
```

```python
import functools
import math

import jax
import jax.numpy as jnp
from jax import lax
from jax.experimental import pallas as pl
from jax.experimental.pallas import tpu as pltpu

F32 = jnp.float32
BF16 = jnp.bfloat16
HIGHEST = lax.Precision.HIGHEST

D_MODEL = 1024
GROUP = 256
HEAD = 64
HEADS = GROUP // HEAD
CHUNK = 64
SUB = 16
CONV_K = 4
HALO = 8
LRU_C = 8.0
RWKV_GN_EPS = 64e-5
XATTN_HEADS = 4
N_EXPERTS = 8
EPS = 1e-6
LANES = 128
VMEM_LIMIT_BYTES = 56 * 1024 * 1024


def _params(*semantics):
    return pltpu.CompilerParams(dimension_semantics=semantics, vmem_limit_bytes=VMEM_LIMIT_BYTES)


def _full(shape):
    n = len(shape)
    return pl.BlockSpec(shape, lambda *_: (0,) * n)


def _bmm(a, b, nt=False, exact=False):
    dn = (((2,), (2 if nt else 1,)), ((0,), (0,)))
    if exact:
        return lax.dot_general(a, b, dn, precision=HIGHEST, preferred_element_type=F32)
    return lax.dot_general(a.astype(BF16), b.astype(BF16), dn, preferred_element_type=F32)


def _mm(a, b, precision=None):
    return jnp.dot(a, b, precision=precision, preferred_element_type=F32)


def _softplus(x):
    return jnp.maximum(x, 0.0) + jnp.log1p(jnp.exp(-jnp.abs(x)))


def _silu(x):
    return x * jax.nn.sigmoid(x)


def _rms(x, gain):
    return x * lax.rsqrt(jnp.mean(x * x, axis=-1, keepdims=True) + EPS) * gain


def _tri_masks(n):
    t = lax.broadcasted_iota(jnp.int32, (n, n), 0)
    s = lax.broadcasted_iota(jnp.int32, (n, n), 1)
    return t, s


def _cumsum_rows(x):
    g, c, _ = x.shape
    t, s = _tri_masks(c)
    tri = jnp.broadcast_to((s <= t).astype(F32), (g, c, c))
    return _bmm(tri, x, exact=True)


def _solve_unit_lower(low, rhs):
    c = low.shape[1]
    r = rhs.shape[2]
    assert c == 4 * SUB and SUB == 16
    t, s = _tri_masks(c)
    same_block = (t // SUB) == (s // SUB)
    d1 = jnp.where(same_block, low, 0.0)
    z = jnp.concatenate([rhs, jnp.where(same_block, 0.0, low)], axis=-1)
    d2 = _bmm(d1, d1)
    d4 = _bmm(d2, d2)
    d8 = _bmm(d4, d4)
    z = z + _bmm(d8, z)
    z = z + _bmm(d4, z)
    z = z + _bmm(d2, z)
    z = z - _bmm(d1, z)
    y = z[:, :, :r]
    n1 = z[:, :, r:]
    n2 = _bmm(n1, n1)
    y = y + _bmm(n2, y)
    return y - _bmm(n1, y)


def _norm_mm_kernel(x_ref, g_ref, w_ref, o_ref, hn_ref):
    @pl.when(pl.program_id(1) == 0)
    def _():
        hn_ref[...] = _rms(x_ref[...], g_ref[...]).astype(BF16)

    o_ref[...] = _mm(hn_ref[...], w_ref[...])


def _norm_matmul(x, gain, w, tm, tn):
    t, k = x.shape
    n = w.shape[1]
    return pl.pallas_call(
        _norm_mm_kernel,
        grid=(t // tm, n // tn),
        in_specs=[pl.BlockSpec((tm, k), lambda i, j: (i, 0)),
                  pl.BlockSpec((1, k), lambda i, j: (0, 0)),
                  pl.BlockSpec((k, tn), lambda i, j: (0, j))],
        out_specs=pl.BlockSpec((tm, tn), lambda i, j: (i, j)),
        out_shape=jax.ShapeDtypeStruct((t, n), F32),
        scratch_shapes=[pltpu.VMEM((tm, k), BF16)],
        compiler_params=_params("parallel", "arbitrary"),
        name="norm_matmul",
    )(x, gain.reshape(1, k), w)


def _mm_res_kernel(a_ref, w_ref, r_ref, o_ref):
    o_ref[...] = r_ref[...] + _mm(a_ref[...].astype(BF16), w_ref[...])


def _matmul_residual(a, w, res, tm):
    t, k = a.shape
    n = w.shape[1]
    return pl.pallas_call(
        _mm_res_kernel,
        grid=(t // tm,),
        in_specs=[pl.BlockSpec((tm, k), lambda i: (i, 0)),
                  pl.BlockSpec((k, n), lambda i: (0, 0)),
                  pl.BlockSpec((tm, n), lambda i: (i, 0))],
        out_specs=pl.BlockSpec((tm, n), lambda i: (i, 0)),
        out_shape=jax.ShapeDtypeStruct((t, n), F32),
        compiler_params=_params("parallel"),
        name="matmul_residual",
    )(a, w, res)


def _xattn_kernel(x_ref, g_ref, wq_ref, k_ref, v_ref, wo_ref, o_ref):
    x = x_ref[0]
    hn = _rms(x, g_ref[...]).astype(BF16)
    q = _mm(hn, wq_ref[...])
    d = q.shape[1] // XATTN_HEADS
    outs = []
    for h in range(XATTN_HEADS):
        qh = (q[:, h * d:(h + 1) * d] * d ** -0.5).astype(BF16)
        kh = k_ref[0, :, h * d:(h + 1) * d]
        vh = v_ref[0, :, h * d:(h + 1) * d]
        s = lax.dot_general(qh, kh, (((1,), (1,)), ((), ())), preferred_element_type=F32)
        s = s - jnp.max(s, axis=-1, keepdims=True)
        p = jnp.exp(s)
        p = p / jnp.sum(p, axis=-1, keepdims=True)
        outs.append(_mm(p.astype(BF16), vh))
    o = jnp.concatenate(outs, axis=-1).astype(BF16)
    o_ref[0] = x + _mm(o, wo_ref[...])


def _cross_attention(x, gain, wq, k_mem, v_mem, wo, tm):
    b, s, d = x.shape
    m = k_mem.shape[1]
    return pl.pallas_call(
        _xattn_kernel,
        grid=(b, s // tm),
        in_specs=[pl.BlockSpec((1, tm, d), lambda i, j: (i, j, 0)),
                  _full((1, d)),
                  _full((d, d)),
                  pl.BlockSpec((1, m, d), lambda i, j: (i, 0, 0)),
                  pl.BlockSpec((1, m, d), lambda i, j: (i, 0, 0)),
                  _full((d, d))],
        out_specs=pl.BlockSpec((1, tm, d), lambda i, j: (i, j, 0)),
        out_shape=jax.ShapeDtypeStruct((b, s, d), F32),
        compiler_params=_params("parallel", "parallel"),
        name="cross_attention",
    )(x, gain.reshape(1, d), wq, k_mem, v_mem, wo)


def _ffn_kernel(x_ref, g_ref, wg_ref, wu_ref, wd_ref, c_ref, o_ref, hn_ref, acc_ref):
    f = pl.program_id(1)

    @pl.when(f == 0)
    def _():
        hn_ref[...] = _rms(x_ref[...], g_ref[...]).astype(BF16)
        acc_ref[...] = x_ref[...]

    hn = hn_ref[...]
    gate = _mm(hn, wg_ref[0])
    up = _mm(hn, wu_ref[0])
    mid = _silu(gate) * up * c_ref[0]
    acc_ref[...] += _mm(mid.astype(BF16), wd_ref[0])

    @pl.when(f == pl.num_programs(1) - 1)
    def _():
        o_ref[...] = acc_ref[...]


def _ffn(x, gain, wg, wu, wd, combine, tm, tf):
    t, d = x.shape
    e, _, ff = wg.shape
    nf = ff // tf
    return pl.pallas_call(
        _ffn_kernel,
        grid=(t // tm, e * nf),
        in_specs=[pl.BlockSpec((tm, d), lambda i, f: (i, 0)),
                  _full((1, d)),
                  pl.BlockSpec((1, d, tf), lambda i, f: (f // nf, 0, f % nf)),
                  pl.BlockSpec((1, d, tf), lambda i, f: (f // nf, 0, f % nf)),
                  pl.BlockSpec((1, tf, d), lambda i, f: (f // nf, f % nf, 0)),
                  pl.BlockSpec((1, tm, 1), lambda i, f: (f // nf, i, 0))],
        out_specs=pl.BlockSpec((tm, d), lambda i, f: (i, 0)),
        out_shape=jax.ShapeDtypeStruct((t, d), F32),
        scratch_shapes=[pltpu.VMEM((tm, d), BF16), pltpu.VMEM((tm, d), F32)],
        compiler_params=_params("parallel", "arbitrary"),
        name="ffn",
    )(x, gain.reshape(1, d), wg, wu, wd, combine)


def _router_kernel(x_ref, g_ref, w_ref, o_ref):
    hn = _rms(x_ref[...], g_ref[...])
    logits = _mm(hn, w_ref[...], precision=HIGHEST)
    lane = lax.broadcasted_iota(jnp.int32, logits.shape, 1)
    neg = jnp.float32(-jnp.inf)
    logits = jnp.where(lane < N_EXPERTS, logits, neg)
    m1 = jnp.max(logits, axis=-1, keepdims=True)
    i1 = jnp.min(jnp.where(logits == m1, lane, LANES), axis=-1, keepdims=True)
    rest = jnp.where(lane == i1, neg, logits)
    m2 = jnp.max(rest, axis=-1, keepdims=True)
    i2 = jnp.min(jnp.where(rest == m2, lane, LANES), axis=-1, keepdims=True)
    e2 = jnp.exp(m2 - m1)
    g1 = 1.0 / (1.0 + e2)
    g2 = e2 / (1.0 + e2)
    o_ref[...] = jnp.where(lane == i1, g1, 0.0) + jnp.where(lane == i2, g2, 0.0)


def _router(x, gain, w_router, tm):
    t, d = x.shape
    w = jnp.zeros((d, LANES), F32).at[:, :N_EXPERTS].set(w_router)
    return pl.pallas_call(
        _router_kernel,
        grid=(t // tm,),
        in_specs=[pl.BlockSpec((tm, d), lambda i: (i, 0)), _full((1, d)), _full((d, LANES))],
        out_specs=pl.BlockSpec((tm, LANES), lambda i: (i, 0)),
        out_shape=jax.ShapeDtypeStruct((t, LANES), F32),
        compiler_params=_params("parallel"),
        name="router",
    )(x, gain.reshape(1, d), w)


def _final_norm_kernel(x_ref, g_ref, o_ref):
    o_ref[...] = _rms(x_ref[...], g_ref[...])


def _final_norm(x, gain, tm):
    t, d = x.shape
    return pl.pallas_call(
        _final_norm_kernel,
        grid=(t // tm,),
        in_specs=[pl.BlockSpec((tm, d), lambda i: (i, 0)), _full((1, d))],
        out_specs=pl.BlockSpec((tm, d), lambda i: (i, 0)),
        out_shape=jax.ShapeDtypeStruct((t, d), F32),
        compiler_params=_params("parallel"),
        name="final_norm",
    )(x, gain.reshape(1, d))


def _hm_spec(h):
    return pl.BlockSpec((1, h, CHUNK, HEAD), lambda b, c: (b, 0, c, 0))


def _hgrn_kernel(q_ref, f_ref, i_ref, g_ref, loglb_ref, log1mlb_ref, onemlb_ref, ng_ref, o_ref, st_ref,
                 rows_ref):
    @pl.when(pl.program_id(1) == 0)
    def _():
        st_ref[...] = jnp.zeros_like(st_ref)

    q, fl, v, g = q_ref[0], f_ref[0], i_ref[0], g_ref[0]
    shape = q.shape
    log_sig = jnp.minimum(fl, 0.0) - jnp.log1p(jnp.exp(-jnp.abs(fl)))
    la = loglb_ref[...]
    lc = log1mlb_ref[...] + log_sig
    log_f = jnp.maximum(la, lc) + jnp.log1p(jnp.exp(-jnp.abs(la - lc)))
    k = onemlb_ref[...] * jax.nn.sigmoid(-fl)
    b = _cumsum_rows(log_f)

    rows_ref[0] = b
    rows_ref[1] = k
    rows_ref[2] = v
    s_sub = lax.broadcasted_iota(jnp.int32, (SUB, CHUNK), 1)
    zeros_sub = jnp.zeros((shape[0], SUB, CHUNK), F32)
    a_rows = [zeros_sub]
    for i in range(1, CHUNK // SUB):
        bref = rows_ref[0, :, i * SUB - 1:i * SUB, :]
        qd = q[:, i * SUB:(i + 1) * SUB] * jnp.exp(b[:, i * SUB:(i + 1) * SUB] - bref)
        kd = k * jnp.exp(jnp.minimum(bref - b, 0.0))
        a_i = _bmm(qd, kd, nt=True)
        a_rows.append(jnp.where(s_sub < i * SUB, a_i, 0.0))
    o = _bmm(jnp.concatenate(a_rows, axis=1), v)

    t_loc = lax.broadcasted_iota(jnp.int32, (shape[0], SUB, HEAD), 1)
    diag = []
    for i in range(CHUNK // SUB):
        sl = slice(i * SUB, (i + 1) * SUB)
        bb, qq = b[:, sl], q[:, sl]
        acc = jnp.zeros((shape[0], SUB, HEAD), F32)
        for s in range(SUB):
            row = i * SUB + s
            e = jnp.where(t_loc >= s, jnp.exp(jnp.minimum(bb - rows_ref[0, :, row:row + 1, :], 0.0)), 0.0)
            w = jnp.sum(e * qq * rows_ref[1, :, row:row + 1, :], axis=-1, keepdims=True)
            acc = acc + w * rows_ref[2, :, row:row + 1, :]
        diag.append(acc)
    o = o + jnp.concatenate(diag, axis=1)

    st = st_ref[...]
    o = o + _bmm(q * jnp.exp(b), st, nt=True)
    b_last = b[:, CHUNK - 1:CHUNK]
    st_ref[...] = st * jnp.exp(b_last) + _bmm(jnp.swapaxes(v, 1, 2), k * jnp.exp(b_last - b))

    o_ref[0] = _rms(o, ng_ref[...]) * _silu(g)


def _hgrn_mixer(q, f, i, g, lb, norm_g):
    b, h, s, d = q.shape
    par = _full((h, 1, d))
    return pl.pallas_call(
        _hgrn_kernel,
        grid=(b, s // CHUNK),
        in_specs=[_hm_spec(h)] * 4 + [par] * 4,
        out_specs=_hm_spec(h),
        out_shape=jax.ShapeDtypeStruct((b, h, s, d), F32),
        scratch_shapes=[pltpu.VMEM((h, d, d), F32), pltpu.VMEM((3, h, CHUNK, d), F32)],
        compiler_params=_params("parallel", "arbitrary"),
        name="hgrn2",
    )(q, f, i, g, jnp.log(lb), jnp.log1p(-lb), 1.0 - lb, norm_g)


def _gdn_kernel(qkv_ref, z_ref, acol_ref, bcol_ref, arow_ref, cw_ref, alog_ref, dtb_ref, ng_ref,
                o_ref, xbuf_ref, st_ref):
    h3 = xbuf_ref.shape[0]
    h = h3 // 3

    @pl.when(pl.program_id(1) == 0)
    def _():
        st_ref[...] = jnp.zeros_like(st_ref)
        xbuf_ref[:, 0:HALO, :] = jnp.zeros((h3, HALO, HEAD), F32)

    xbuf_ref[:, HALO:HALO + CHUNK, :] = qkv_ref[0]
    conv = jnp.zeros((h3, CHUNK, HEAD), F32)
    for j in range(CONV_K):
        conv = conv + cw_ref[j] * xbuf_ref[:, HALO - (CONV_K - 1) + j:HALO - (CONV_K - 1) + j + CHUNK, :]
    xbuf_ref[:, 0:HALO, :] = xbuf_ref[:, CHUNK:CHUNK + HALO, :]
    qkv = _silu(conv)
    q, k, v = qkv[0:h], qkv[h:2 * h], qkv[2 * h:3 * h]
    q = q * lax.rsqrt(jnp.sum(q * q, axis=-1, keepdims=True) + EPS) * HEAD ** -0.5
    k = k * lax.rsqrt(jnp.sum(k * k, axis=-1, keepdims=True) + EPS)

    neg_a = -jnp.exp(alog_ref[...])
    g_col = neg_a * _softplus(acol_ref[0] + dtb_ref[...])
    g_row = neg_a * _softplus(arow_ref[0, 0] + dtb_ref[...])
    beta = jax.nn.sigmoid(bcol_ref[0])
    t_idx, s_idx = _tri_masks(CHUNK)
    causal = s_idx <= t_idx
    gc_col = jnp.sum(jnp.where(causal, g_row, 0.0), axis=-1, keepdims=True)
    gc_row = jnp.sum(jnp.where(t_idx <= s_idx, g_col, 0.0), axis=1, keepdims=True)
    rel = jnp.where(causal, jnp.exp(jnp.minimum(gc_col - gc_row, 0.0)), 0.0)

    kb = k * beta
    lower = jnp.where(s_idx < t_idx, _bmm(kb, k, nt=True) * rel, 0.0)
    uw = _solve_unit_lower(lower, jnp.concatenate([v * beta, kb * jnp.exp(gc_col)], axis=-1))
    u, w = uw[:, :, :HEAD], uw[:, :, HEAD:]
    scores = _bmm(q, k, nt=True) * rel

    st = st_ref[...]
    v_new = u - _bmm(w, st)
    o = _bmm(q * jnp.exp(gc_col), st) + _bmm(scores, v_new)
    g_last = gc_col[:, CHUNK - 1:CHUNK]
    kd = k * jnp.exp(g_last - gc_col)
    st_ref[...] = st * jnp.exp(g_last) + _bmm(jnp.swapaxes(kd, 1, 2), v_new)

    o_ref[0] = _rms(o, ng_ref[...]) * _silu(z_ref[0])


def _gdn_mixer(qkv, z, a_col, b_col, a_row, conv_w, a_log, dt_bias, norm_g):
    b, h3, s, d = qkv.shape
    h = h3 // 3
    col = pl.BlockSpec((1, h, CHUNK, 1), lambda i, c: (i, 0, c, 0))
    return pl.pallas_call(
        _gdn_kernel,
        grid=(b, s // CHUNK),
        in_specs=[pl.BlockSpec((1, h3, CHUNK, d), lambda i, c: (i, 0, c, 0)), _hm_spec(h), col, col,
                  pl.BlockSpec((1, 1, h, 1, CHUNK), lambda i, c: (i, c, 0, 0, 0)),
                  _full((CONV_K, h3, 1, d)), _full((h, 1, 1)), _full((h, 1, 1)), _full((1, 1, d))],
        out_specs=_hm_spec(h),
        out_shape=jax.ShapeDtypeStruct((b, h, s, d), F32),
        scratch_shapes=[pltpu.VMEM((h3, HALO + CHUNK, d), F32), pltpu.VMEM((h, d, d), F32)],
        compiler_params=_params("parallel", "arbitrary"),
        name="gated_deltanet",
    )(qkv, z, a_col, b_col, a_row, conv_w, a_log, dt_bias, norm_g)


def _lru_kernel(x_ref, gate_ref, cw_ref, cb_ref, wa_ref, ba_ref, wx_ref, bx_ref, lam_ref, o_ref,
                xbuf_ref, h_ref):
    tt = x_ref.shape[1]

    @pl.when(pl.program_id(1) == 0)
    def _():
        h_ref[...] = jnp.zeros_like(h_ref)
        xbuf_ref[0:HALO, :] = jnp.zeros((HALO, GROUP), F32)

    xbuf_ref[HALO:HALO + tt, :] = x_ref[0]
    xc = jnp.zeros((tt, GROUP), F32) + cb_ref[...]
    for j in range(CONV_K):
        xc = xc + cw_ref[j:j + 1, :] * xbuf_ref[HALO - (CONV_K - 1) + j:HALO - (CONV_K - 1) + j + tt, :]
    xbuf_ref[0:HALO, :] = xbuf_ref[tt:tt + HALO, :]

    r = jax.nn.sigmoid(_mm(xc, wa_ref[...], precision=HIGHEST) + ba_ref[...])
    i = jax.nn.sigmoid(_mm(xc, wx_ref[...], precision=HIGHEST) + bx_ref[...])
    log_a = -LRU_C * r * _softplus(-lam_ref[...])
    a = jnp.exp(log_a)
    u = jnp.sqrt(1.0 - a * a) * (i * xc)

    row = lax.broadcasted_iota(jnp.int32, (tt, GROUP), 0)
    shift = 1
    while shift < tt:
        keep = row >= shift
        a_prev = pltpu.roll(a, shift, 0)
        u_prev = pltpu.roll(u, shift, 0)
        u = jnp.where(keep, a * u_prev + u, u)
        a = jnp.where(keep, a * a_prev, a)
        shift *= 2
    hs = u + a * h_ref[...]
    h_ref[...] = hs[tt - 1:tt, :]

    gate = gate_ref[0]
    gelu = 0.5 * gate * (1.0 + jnp.tanh(math.sqrt(2.0 / math.pi) * (gate + 0.044715 * gate * gate * gate)))
    o_ref[0] = hs * gelu


def _lru_mixer(x, gate, conv_w, conv_b, wa, ba, wx, bx, lam, tt):
    b, s, c = x.shape
    tile = pl.BlockSpec((1, tt, c), lambda i, t: (i, t, 0))
    vec = _full((1, c))
    return pl.pallas_call(
        _lru_kernel,
        grid=(b, s // tt),
        in_specs=[tile, tile, _full((CONV_K, c)), vec, _full((c, c)), vec, _full((c, c)), vec, vec],
        out_specs=tile,
        out_shape=jax.ShapeDtypeStruct((b, s, c), F32),
        scratch_shapes=[pltpu.VMEM((HALO + tt, c), F32), pltpu.VMEM((1, c), F32)],
        compiler_params=_params("parallel", "arbitrary"),
        name="rg_lru",
    )(x, gate, conv_w, conv_b.reshape(1, c), wa, ba.reshape(1, c), wx, bx.reshape(1, c), lam.reshape(1, c))


def _rwkv_prep_kernel(has_vmix, *refs):
    if has_vmix:
        (p_ref, mu_ref, w0_ref, w2_ref, a0_ref, a2_ref, g2_ref, vf_ref, v0_ref, v1_ref, v2_ref,
         r_ref, lw_ref, k_ref, v_ref, a_ref, g_ref, xbuf_ref) = refs
    else:
        (p_ref, mu_ref, w0_ref, w2_ref, a0_ref, a2_ref, g2_ref,
         r_ref, lw_ref, k_ref, v_ref, a_ref, g_ref, xbuf_ref) = refs
    tt = p_ref.shape[1]

    @pl.when(pl.program_id(1) == 0)
    def _():
        xbuf_ref[0:HALO, :] = jnp.zeros((HALO, xbuf_ref.shape[1]), F32)

    c = p_ref[0]
    xbuf_ref[HALO:HALO + tt, :] = c
    prev = xbuf_ref[HALO - 1:HALO - 1 + tt, :]
    xbuf_ref[0:HALO, :] = xbuf_ref[tt:tt + HALO, :]
    c = c + mu_ref[...] * (prev - c)

    r = c[:, 0:GROUP]
    k = c[:, GROUP:2 * GROUP]
    v = c[:, 2 * GROUP:3 * GROUP]
    wa_lo = c[:, 3 * GROUP:3 * GROUP + LANES]
    g_lo = c[:, 3 * GROUP + LANES:4 * GROUP]
    lw_ref[0] = -math.exp(-0.5) * jax.nn.sigmoid(w0_ref[...] + _mm(jnp.tanh(wa_lo), w2_ref[...], precision=HIGHEST))
    a_ref[0] = jax.nn.sigmoid(a0_ref[...] + _mm(wa_lo, a2_ref[...], precision=HIGHEST))
    g_ref[0] = _mm(jax.nn.sigmoid(g_lo), g2_ref[...], precision=HIGHEST)
    if has_vmix:
        mix = jax.nn.sigmoid(v0_ref[...] + _mm(_mm(v, v1_ref[...], precision=HIGHEST), v2_ref[...], precision=HIGHEST))
        v = v + (vf_ref[0] - v) * mix
    r_ref[0] = r
    k_ref[0] = k
    v_ref[0] = v


def _rwkv_prep(p_d, mu, w0, w2, a0, a2, g2, v_first, v_mix, tt):
    b, s, n = p_d.shape
    lora = w2.shape[0]
    w2p = jnp.zeros((LANES, GROUP), F32).at[:lora].set(w2)
    a2p = jnp.zeros((LANES, GROUP), F32).at[lora:].set(a2)
    tile_in = pl.BlockSpec((1, tt, n), lambda i, t: (i, t, 0))
    tile = pl.BlockSpec((1, tt, GROUP), lambda i, t: (i, t, 0))
    vec = _full((1, GROUP))
    args = [p_d, mu.reshape(1, n), w0.reshape(1, GROUP), w2p, a0.reshape(1, GROUP), a2p, g2]
    specs = [tile_in, _full((1, n)), vec, _full((LANES, GROUP)), vec, _full((LANES, GROUP)), _full(g2.shape)]
    if v_mix is not None:
        v0, v1, v2 = v_mix
        rank = v1.shape[1]
        v1p = jnp.zeros((GROUP, LANES), F32).at[:, :rank].set(v1)
        v2p = jnp.zeros((LANES, GROUP), F32).at[:rank].set(v2)
        args += [v_first, v0.reshape(1, GROUP), v1p, v2p]
        specs += [tile, vec, _full((GROUP, LANES)), _full((LANES, GROUP))]
    out = jax.ShapeDtypeStruct((b, s, GROUP), F32)
    return pl.pallas_call(
        functools.partial(_rwkv_prep_kernel, v_mix is not None),
        grid=(b, s // tt),
        in_specs=specs,
        out_specs=[tile] * 6,
        out_shape=[out] * 6,
        scratch_shapes=[pltpu.VMEM((HALO + tt, n), F32)],
        compiler_params=_params("parallel", "arbitrary"),
        name="rwkv7_prep",
    )(*args)


def _rwkv_kernel(r_ref, lw_ref, k_ref, v_ref, a_ref, g_ref, kk_ref, ka_ref, rk_ref, lnw_ref, lnb_ref,
                 o_ref, zt_ref):
    @pl.when(pl.program_id(1) == 0)
    def _():
        zt_ref[...] = jnp.zeros_like(zt_ref)

    r, lw, k, v, a, g = r_ref[0], lw_ref[0], k_ref[0], v_ref[0], a_ref[0], g_ref[0]
    kk = k * kk_ref[...]
    kk = kk * lax.rsqrt(jnp.sum(kk * kk, axis=-1, keepdims=True) + EPS)
    k = k * (1.0 + (a - 1.0) * ka_ref[...])
    b = _cumsum_rows(lw)
    e_pos = jnp.exp(b)
    e_neg = jnp.exp(-b)
    r_t = r * e_pos
    al_t = -kk * jnp.exp(b - lw)
    be_t = kk * a * e_neg
    k_t = k * e_neg

    t_idx, s_idx = _tri_masks(CHUNK)
    strict = s_idx < t_idx
    incl = s_idx <= t_idx
    ab_k = jnp.concatenate([be_t, k_t], axis=1)
    al_all = _bmm(al_t, ab_k, nt=True)
    r_all = _bmm(r_t, ab_k, nt=True)
    l_ab = jnp.where(strict, al_all[:, :, :CHUNK], 0.0)
    l_ak = jnp.where(strict, al_all[:, :, CHUNK:], 0.0)
    m_rb = jnp.where(incl, r_all[:, :, :CHUNK], 0.0)
    m_rk = jnp.where(incl, r_all[:, :, CHUNK:], 0.0)

    sol = _solve_unit_lower(-l_ab, jnp.concatenate([al_t, _bmm(l_ak, v)], axis=-1))
    w1, u0 = sol[:, :, :HEAD], sol[:, :, HEAD:]

    zt = zt_ref[...]
    u = _bmm(w1, zt, nt=True) + u0
    y = _bmm(r_t, zt, nt=True) + _bmm(m_rb, u) + _bmm(m_rk, v)
    b_last = b[:, CHUNK - 1:CHUNK]
    e_last = jnp.exp(b_last - b)
    uv_t = jnp.swapaxes(jnp.concatenate([u, v], axis=1), 1, 2)
    zt_ref[...] = zt * jnp.exp(b_last) + _bmm(uv_t, jnp.concatenate([kk * a * e_last, k * e_last], axis=1))

    mean = jnp.mean(y, axis=-1, keepdims=True)
    var = jnp.mean(jnp.square(y - mean), axis=-1, keepdims=True)
    yn = (y - mean) * lax.rsqrt(var + RWKV_GN_EPS) * lnw_ref[...] + lnb_ref[...]
    bonus = jnp.sum(r * k * rk_ref[...], axis=-1, keepdims=True) * v
    o_ref[0] = (yn + bonus) * g


def _rwkv_mixer(r, lw, k, v, a, g, k_k, k_a, r_k, ln_w, ln_b):
    b, h, s, d = r.shape
    par = _full((h, 1, d))
    return pl.pallas_call(
        _rwkv_kernel,
        grid=(b, s // CHUNK),
        in_specs=[_hm_spec(h)] * 6 + [par] * 5,
        out_specs=_hm_spec(h),
        out_shape=jax.ShapeDtypeStruct((b, h, s, d), F32),
        scratch_shapes=[pltpu.VMEM((h, d, d), F32)],
        compiler_params=_params("parallel", "arbitrary"),
        name="rwkv7",
    )(r, lw, k, v, a, g, k_k, k_a, r_k, ln_w, ln_b)


def _to_hm(t, groups=1):
    b, s, c = t.shape
    h = c // (groups * HEAD)
    return t.reshape(b, s, groups, h, HEAD).transpose(2, 0, 3, 1, 4).reshape(1, groups * b * h, s, HEAD)


def _from_hm(t, b):
    _, bh, s, d = t.shape
    return t.reshape(b, bh // b, s, d).transpose(0, 2, 1, 3).reshape(b, s, (bh // b) * d)


def _par_hm(p, b, width=HEAD):
    return jnp.tile(p.reshape(-1, 1, width), (b, 1, 1))


def _gdn_gate_layouts(alpha, beta):
    b, s, h = alpha.shape
    a_col = alpha.transpose(0, 2, 1).reshape(1, b * h, s, 1)
    b_col = beta.transpose(0, 2, 1).reshape(1, b * h, s, 1)
    a_row = alpha.reshape(b, s // CHUNK, CHUNK, h).transpose(1, 0, 3, 2).reshape(1, s // CHUNK, b * h, 1, CHUNK)
    return a_col, b_col, a_row


def _gdn_conv_layout(w, b):
    k, c = w.shape
    h = c // (3 * HEAD)
    w = jnp.broadcast_to(w.reshape(k, 3, 1, h, 1, HEAD), (k, 3, b, h, 1, HEAD))
    return w.reshape(k, 3 * b * h, 1, HEAD)


def _block_diag(w):
    n, a, b = w.shape
    out = jnp.zeros((n * a, n * b), w.dtype)
    for i in range(n):
        out = out.at[i * a:(i + 1) * a, i * b:(i + 1) * b].set(w[i])
    return out


def kernel(x, mem, norm_mix, w_in, w_out, hgrn_lb_logits, hgrn_norm, gdn_conv_w, gdn_A_log, gdn_dt_bias, gdn_norm, lru_conv_w, lru_conv_b, lru_wa, lru_ba, lru_wx, lru_bx, lru_lambda, rwkv_mu, rwkv_w0, rwkv_w2, rwkv_a0, rwkv_a2, rwkv_g2, rwkv_k_k, rwkv_k_a, rwkv_r_k, rwkv_ln_w, rwkv_ln_b, rwkv_v0, rwkv_v1, rwkv_v2, mem_norm, norm_xattn, xattn_wq, xattn_wk, xattn_wv, xattn_wo, norm_ffn, ffn_w_gate, ffn_w_up, ffn_w_down, moe_router, moe_w_gate, moe_w_up, moe_w_down, norm_final):
    bsz, seq, d = x.shape
    depth = w_in.shape[0]
    tok = bsz * seq
    mlen = mem.shape[1]
    g4 = 4 * GROUP

    lb = jnp.cumsum(jax.nn.softmax(hgrn_lb_logits.astype(F32), axis=0), axis=0)
    lb = lb - lb[0]

    n_small = 2 * HEADS
    off_b = g4
    off_ab = off_b + g4
    off_c = off_ab + n_small
    off_d = off_c + 2 * GROUP
    n_main = g4 + g4 + 2 * GROUP + g4

    xf = x.reshape(tok, d)
    mem_f = mem.reshape(bsz * mlen, d)
    v_first = None
    for l in range(depth):
        w = w_in[l]
        w_cat = jnp.concatenate([w[:, :off_ab], w[:, off_c:], w[:, off_ab:off_c],
                                 jnp.zeros((d, LANES - n_small), F32)], axis=1).astype(BF16)
        p = _norm_matmul(xf, norm_mix[l], w_cat, 256, n_main + LANES).reshape(bsz, seq, n_main + LANES)
        p_a = p[..., 0:g4]
        p_b = p[..., off_b:off_b + g4]
        p_c = p[..., off_ab:off_ab + 2 * GROUP]
        p_d = p[..., off_ab + 2 * GROUP:n_main]
        p_ab = p[..., n_main:n_main + n_small]

        y_a = _from_hm(_hgrn_mixer(_to_hm(p_a[..., 0:GROUP]), _to_hm(p_a[..., GROUP:2 * GROUP]),
                                   _to_hm(p_a[..., 2 * GROUP:3 * GROUP]), _to_hm(p_a[..., 3 * GROUP:]),
                                   _par_hm(lb[l], bsz), _par_hm(hgrn_norm[l], bsz)), bsz)

        y_b = _from_hm(_gdn_mixer(_to_hm(p_b[..., :3 * GROUP], 3), _to_hm(p_b[..., 3 * GROUP:]),
                                  *_gdn_gate_layouts(p_ab[..., :HEADS], p_ab[..., HEADS:]),
                                  _gdn_conv_layout(gdn_conv_w[l], bsz),
                                  _par_hm(gdn_A_log[l], bsz, 1), _par_hm(gdn_dt_bias[l], bsz, 1),
                                  gdn_norm[l].reshape(1, 1, HEAD)), bsz)

        y_c = _lru_mixer(p_c[..., :GROUP], p_c[..., GROUP:], lru_conv_w[l], lru_conv_b[l],
                         _block_diag(lru_wa[l]), lru_ba[l], _block_diag(lru_wx[l]), lru_bx[l], lru_lambda[l], 512)

        v_mix = None if l == 0 else (rwkv_v0[l - 1], rwkv_v1[l - 1], rwkv_v2[l - 1])
        r_d, lw_d, k_d, v_d, a_d, g_d = _rwkv_prep(p_d, rwkv_mu[l], rwkv_w0[l], rwkv_w2[l], rwkv_a0[l],
                                                   rwkv_a2[l], rwkv_g2[l], v_first, v_mix, 512)
        if l == 0:
            v_first = v_d
        y_d = _from_hm(_rwkv_mixer(_to_hm(r_d), _to_hm(lw_d), _to_hm(k_d), _to_hm(v_d), _to_hm(a_d), _to_hm(g_d),
                                   _par_hm(rwkv_k_k[l], bsz), _par_hm(rwkv_k_a[l], bsz), _par_hm(rwkv_r_k[l], bsz),
                                   _par_hm(rwkv_ln_w[l], bsz), _par_hm(rwkv_ln_b[l], bsz)), bsz)

        y = jnp.concatenate([y_a, y_b, y_c, y_d], axis=-1).reshape(tok, 4 * GROUP)
        xf = _matmul_residual(y, w_out[l].astype(BF16), xf, 512)

        w_kv = jnp.concatenate([xattn_wk[l], xattn_wv[l]], axis=1).astype(BF16)
        kv = _norm_matmul(mem_f, mem_norm, w_kv, 256, 2 * d).astype(BF16).reshape(bsz, mlen, 2 * d)
        xf = _cross_attention(xf.reshape(bsz, seq, d), norm_xattn[l], xattn_wq[l].astype(BF16),
                              kv[..., :d], kv[..., d:], xattn_wo[l].astype(BF16), 512).reshape(tok, d)

        if l % 2 == 0:
            j = l // 2
            ones = jnp.ones((1, tok, 1), F32)
            xf = _ffn(xf, norm_ffn[l], ffn_w_gate[j][None].astype(BF16), ffn_w_up[j][None].astype(BF16),
                      ffn_w_down[j][None].astype(BF16), ones, 1024, 256)
        else:
            j = l // 2
            comb = _router(xf, norm_ffn[l], moe_router[j], 1024)
            comb = comb[:, :N_EXPERTS].T[..., None]
            xf = _ffn(xf, norm_ffn[l], moe_w_gate[j].astype(BF16), moe_w_up[j].astype(BF16),
                      moe_w_down[j].astype(BF16), comb, 1024, 512)
    return _final_norm(xf, norm_final, 1024).reshape(bsz, seq, d)
```

```python
import functools
import math

import jax
import jax.numpy as jnp
from jax import lax
from jax.experimental import pallas as pl
from jax.experimental.pallas import tpu as pltpu

F32 = jnp.float32
BF16 = jnp.bfloat16
HIGHEST = lax.Precision.HIGHEST

GROUP = 256
HEAD = 64
HEADS = GROUP // HEAD
CHUNK = 64
SUB = 16
CONV_K = 4
HALO = 8
LRU_C = 8.0
RWKV_GN_EPS = 64e-5
XATTN_HEADS = 4
N_EXPERTS = 8
EPS = 1e-6
LANES = 128
VMEM_LIMIT_BYTES = 56 * 1024 * 1024

TOK_RWKV = 0
TOK_LRU = 4 * GROUP
TOK_GATES = 6 * GROUP
N_TOK = 6 * GROUP + LANES
HM_HGRN_QFI, HM_GDN_QKV, HM_HGRN_G, HM_GDN_Z = 0, 3, 6, 7
HM_GROUPS = 8

TM_IN_PROJ = 256
TM_OUT_PROJ = 512
TM_XATTN = 512
TM_FFN = 1024
TF_FFN = 256
TF_MOE = 512
TM_NORM = 1024
TT_SCAN = 512


def _params(*semantics):
    return pltpu.CompilerParams(dimension_semantics=semantics, vmem_limit_bytes=VMEM_LIMIT_BYTES)


def _full(shape):
    n = len(shape)
    return pl.BlockSpec(shape, lambda *_: (0,) * n)


def _bmm(a, b, nt=False, exact=False):
    dn = (((2,), (2 if nt else 1,)), ((0,), (0,)))
    if exact:
        return lax.dot_general(a, b, dn, precision=HIGHEST, preferred_element_type=F32)
    return lax.dot_general(a.astype(BF16), b.astype(BF16), dn, preferred_element_type=F32)


def _mm(a, b, precision=None):
    return jnp.dot(a, b, precision=precision, preferred_element_type=F32)


def _mm16(a, b):
    return jnp.dot(a.astype(BF16), b.astype(BF16), preferred_element_type=F32)


def _softplus(x):
    return jnp.maximum(x, 0.0) + jnp.log1p(jnp.exp(-jnp.abs(x)))


def _silu(x):
    return x * jax.nn.sigmoid(x)


def _rms(x, gain):
    return x * lax.rsqrt(jnp.mean(x * x, axis=-1, keepdims=True) + EPS) * gain


def _tri_masks(n):
    t = lax.broadcasted_iota(jnp.int32, (n, n), 0)
    s = lax.broadcasted_iota(jnp.int32, (n, n), 1)
    return t, s


def _cumsum_rows(x):
    g, c, _ = x.shape
    t, s = _tri_masks(c)
    tri = jnp.broadcast_to((s <= t).astype(F32), (g, c, c))
    return _bmm(tri, x, exact=True)


def _solve_unit_lower(low, rhs):
    c = low.shape[1]
    r = rhs.shape[2]
    assert c == 4 * SUB and SUB == 16
    t, s = _tri_masks(c)
    same_block = (t // SUB) == (s // SUB)
    d1 = jnp.where(same_block, low, 0.0)
    z = jnp.concatenate([rhs, jnp.where(same_block, 0.0, low)], axis=-1)
    d2 = _bmm(d1, d1)
    d4 = _bmm(d2, d2)
    d8 = _bmm(d4, d4)
    z = z + _bmm(d8, z)
    z = z + _bmm(d4, z)
    z = z + _bmm(d2, z)
    z = z - _bmm(d1, z)
    y = z[:, :, :r]
    n1 = z[:, :, r:]
    n2 = _bmm(n1, n1)
    y = y + _bmm(n2, y)
    return y - _bmm(n1, y)


def _store_heads(ref, val, lead=()):
    for h in range(val.shape[1] // HEAD):
        ref[lead + (h,)] = val[:, h * HEAD:(h + 1) * HEAD]


def _store_tokens(ref, o):
    nb = ref.shape[0]
    nh = o.shape[0] // nb
    for b in range(nb):
        ref[b] = jnp.concatenate([o[b * nh + h] for h in range(nh)], axis=-1)


def _in_proj_kernel(x_ref, g_ref, w_ref, tok_ref, hm_ref):
    hn = _rms(x_ref[0], g_ref[...]).astype(BF16)
    acc = _mm(hn, w_ref[...])
    n_tok = tok_ref.shape[2]
    tok_ref[0] = acc[:, :n_tok]
    for g in range(hm_ref.shape[0]):
        _store_heads(hm_ref, acc[:, n_tok + g * GROUP:n_tok + (g + 1) * GROUP], (g, 0))


def _in_proj(x, gain, w):
    b, s, d = x.shape
    n = w.shape[1]
    tm = TM_IN_PROJ
    return pl.pallas_call(
        _in_proj_kernel,
        grid=(b, s // tm),
        in_specs=[pl.BlockSpec((1, tm, d), lambda i, j: (i, j, 0)), _full((1, d)), _full((d, n))],
        out_specs=[pl.BlockSpec((1, tm, N_TOK), lambda i, j: (i, j, 0)),
                   pl.BlockSpec((HM_GROUPS, 1, HEADS, tm, HEAD), lambda i, j: (0, i, 0, j, 0))],
        out_shape=[jax.ShapeDtypeStruct((b, s, N_TOK), F32),
                   jax.ShapeDtypeStruct((HM_GROUPS, b, HEADS, s, HEAD), F32)],
        compiler_params=_params("parallel", "parallel"),
        name="in_proj",
    )(x, gain.reshape(1, d), w)


def _norm_mm_kernel(x_ref, g_ref, w_ref, o_ref):
    o_ref[...] = _mm(_rms(x_ref[...], g_ref[...]).astype(BF16), w_ref[...]).astype(o_ref.dtype)


def _norm_matmul(x, gain, w, tm, out_dtype):
    t, k = x.shape
    n = w.shape[1]
    return pl.pallas_call(
        _norm_mm_kernel,
        grid=(t // tm,),
        in_specs=[pl.BlockSpec((tm, k), lambda i: (i, 0)), _full((1, k)), _full((k, n))],
        out_specs=pl.BlockSpec((tm, n), lambda i: (i, 0)),
        out_shape=jax.ShapeDtypeStruct((t, n), out_dtype),
        compiler_params=_params("parallel"),
        name="norm_matmul",
    )(x, gain.reshape(1, k), w)


def _out_proj_kernel(ya_ref, yb_ref, yc_ref, yd_ref, w_ref, r_ref, o_ref):
    acc = r_ref[0]
    for m, y_ref in enumerate((ya_ref, yb_ref, yc_ref, yd_ref)):
        acc = acc + _mm(y_ref[0].astype(BF16), w_ref[m])
    o_ref[0] = acc


def _out_proj(ys, w, res):
    b, s, d = res.shape
    tm = TM_OUT_PROJ
    y_spec = pl.BlockSpec((1, tm, GROUP), lambda i, j: (i, j, 0))
    x_spec = pl.BlockSpec((1, tm, d), lambda i, j: (i, j, 0))
    return pl.pallas_call(
        _out_proj_kernel,
        grid=(b, s // tm),
        in_specs=[y_spec] * 4 + [_full(w.shape), x_spec],
        out_specs=x_spec,
        out_shape=jax.ShapeDtypeStruct((b, s, d), F32),
        compiler_params=_params("parallel", "parallel"),
        name="out_proj",
    )(*ys, w, res)


def _xattn_kernel(x_ref, g_ref, wq_ref, k_ref, v_ref, wo_ref, o_ref):
    x = x_ref[0]
    hn = _rms(x, g_ref[...]).astype(BF16)
    q = _mm(hn, wq_ref[...])
    d = q.shape[1] // XATTN_HEADS
    outs = []
    for h in range(XATTN_HEADS):
        qh = (q[:, h * d:(h + 1) * d] * d ** -0.5).astype(BF16)
        kh = k_ref[0, :, h * d:(h + 1) * d]
        vh = v_ref[0, :, h * d:(h + 1) * d]
        s = lax.dot_general(qh, kh, (((1,), (1,)), ((), ())), preferred_element_type=F32)
        s = s - jnp.max(s, axis=-1, keepdims=True)
        p = jnp.exp(s)
        p = p / jnp.sum(p, axis=-1, keepdims=True)
        outs.append(_mm(p.astype(BF16), vh))
    o = jnp.concatenate(outs, axis=-1).astype(BF16)
    o_ref[0] = x + _mm(o, wo_ref[...])


def _cross_attention(x, gain, wq, kv, wo):
    b, s, d = x.shape
    m = kv.shape[1]
    tm = TM_XATTN
    return pl.pallas_call(
        _xattn_kernel,
        grid=(b, s // tm),
        in_specs=[pl.BlockSpec((1, tm, d), lambda i, j: (i, j, 0)),
                  _full((1, d)),
                  _full((d, d)),
                  pl.BlockSpec((1, m, d), lambda i, j: (i, 0, 0)),
                  pl.BlockSpec((1, m, d), lambda i, j: (i, 0, 1)),
                  _full((d, d))],
        out_specs=pl.BlockSpec((1, tm, d), lambda i, j: (i, j, 0)),
        out_shape=jax.ShapeDtypeStruct((b, s, d), F32),
        compiler_params=_params("parallel", "parallel"),
        name="cross_attention",
    )(x, gain.reshape(1, d), wq, kv, kv, wo)


def _ffn_kernel(x_ref, g_ref, wg_ref, wu_ref, wd_ref, c_ref, o_ref, hn_ref, acc_ref):
    f = pl.program_id(1)

    @pl.when(f == 0)
    def _():
        hn_ref[...] = _rms(x_ref[...], g_ref[...]).astype(BF16)
        acc_ref[...] = x_ref[...]

    hn = hn_ref[...]
    gate = _mm(hn, wg_ref[0])
    up = _mm(hn, wu_ref[0])
    mid = _silu(gate) * up * c_ref[0]
    acc_ref[...] += _mm(mid.astype(BF16), wd_ref[0])

    @pl.when(f == pl.num_programs(1) - 1)
    def _():
        o_ref[...] = acc_ref[...]


def _ffn(x, gain, wg, wu, wd, combine, tf):
    t, d = x.shape
    e, _, ff = wg.shape
    nf = ff // tf
    tm = TM_FFN
    return pl.pallas_call(
        _ffn_kernel,
        grid=(t // tm, e * nf),
        in_specs=[pl.BlockSpec((tm, d), lambda i, f: (i, 0)),
                  _full((1, d)),
                  pl.BlockSpec((1, d, tf), lambda i, f: (f // nf, 0, f % nf)),
                  pl.BlockSpec((1, d, tf), lambda i, f: (f // nf, 0, f % nf)),
                  pl.BlockSpec((1, tf, d), lambda i, f: (f // nf, f % nf, 0)),
                  pl.BlockSpec((1, tm, 1), lambda i, f: (f // nf, i, 0))],
        out_specs=pl.BlockSpec((tm, d), lambda i, f: (i, 0)),
        out_shape=jax.ShapeDtypeStruct((t, d), F32),
        scratch_shapes=[pltpu.VMEM((tm, d), BF16), pltpu.VMEM((tm, d), F32)],
        compiler_params=_params("parallel", "arbitrary"),
        name="ffn",
    )(x, gain.reshape(1, d), wg, wu, wd, combine)


def _router_kernel(x_ref, g_ref, w_ref, o_ref):
    hn = _rms(x_ref[...], g_ref[...])
    logits = _mm(hn, w_ref[...], precision=HIGHEST)
    lane = lax.broadcasted_iota(jnp.int32, logits.shape, 1)
    neg = jnp.float32(-jnp.inf)
    logits = jnp.where(lane < N_EXPERTS, logits, neg)
    m1 = jnp.max(logits, axis=-1, keepdims=True)
    i1 = jnp.min(jnp.where(logits == m1, lane, LANES), axis=-1, keepdims=True)
    rest = jnp.where(lane == i1, neg, logits)
    m2 = jnp.max(rest, axis=-1, keepdims=True)
    i2 = jnp.min(jnp.where(rest == m2, lane, LANES), axis=-1, keepdims=True)
    e2 = jnp.exp(m2 - m1)
    g1 = 1.0 / (1.0 + e2)
    g2 = e2 / (1.0 + e2)
    o_ref[...] = jnp.where(lane == i1, g1, 0.0) + jnp.where(lane == i2, g2, 0.0)


def _router(x, gain, w_router):
    t, d = x.shape
    tm = TM_NORM
    w = jnp.zeros((d, LANES), F32).at[:, :N_EXPERTS].set(w_router)
    return pl.pallas_call(
        _router_kernel,
        grid=(t // tm,),
        in_specs=[pl.BlockSpec((tm, d), lambda i: (i, 0)), _full((1, d)), _full((d, LANES))],
        out_specs=pl.BlockSpec((tm, LANES), lambda i: (i, 0)),
        out_shape=jax.ShapeDtypeStruct((t, LANES), F32),
        compiler_params=_params("parallel"),
        name="router",
    )(x, gain.reshape(1, d), w)


def _final_norm_kernel(x_ref, g_ref, o_ref):
    o_ref[...] = _rms(x_ref[...], g_ref[...])


def _final_norm(x, gain):
    t, d = x.shape
    tm = TM_NORM
    return pl.pallas_call(
        _final_norm_kernel,
        grid=(t // tm,),
        in_specs=[pl.BlockSpec((tm, d), lambda i: (i, 0)), _full((1, d))],
        out_specs=pl.BlockSpec((tm, d), lambda i: (i, 0)),
        out_shape=jax.ShapeDtypeStruct((t, d), F32),
        compiler_params=_params("parallel"),
        name="final_norm",
    )(x, gain.reshape(1, d))


def _hm_groups_spec(n, first):
    def spec(b, h):
        return pl.BlockSpec((n, b, h, CHUNK, HEAD), lambda c: (first // n, 0, 0, c, 0))
    assert first % n == 0
    return spec


def _tok_chunk_spec(b):
    return pl.BlockSpec((b, CHUNK, GROUP), lambda c: (0, c, 0))


def _hgrn_kernel(qfi_ref, g_ref, loglb_ref, log1mlb_ref, onemlb_ref, ng_ref, o_ref, st_ref, rows_ref):
    @pl.when(pl.program_id(0) == 0)
    def _():
        st_ref[...] = jnp.zeros_like(st_ref)

    shape = st_ref.shape[:1] + (CHUNK, HEAD)
    q, fl, v = (qfi_ref[j].reshape(shape) for j in range(3))
    g = g_ref[0].reshape(shape)
    log_sig = jnp.minimum(fl, 0.0) - jnp.log1p(jnp.exp(-jnp.abs(fl)))
    la = loglb_ref[...]
    lc = log1mlb_ref[...] + log_sig
    log_f = jnp.maximum(la, lc) + jnp.log1p(jnp.exp(-jnp.abs(la - lc)))
    k = onemlb_ref[...] * jax.nn.sigmoid(-fl)
    b = _cumsum_rows(log_f)

    rows_ref[0] = b
    rows_ref[1] = k
    rows_ref[2] = v
    s_sub = lax.broadcasted_iota(jnp.int32, (SUB, CHUNK), 1)
    zeros_sub = jnp.zeros((shape[0], SUB, CHUNK), F32)
    a_rows = [zeros_sub]
    for i in range(1, CHUNK // SUB):
        bref = rows_ref[0, :, i * SUB - 1:i * SUB, :]
        qd = q[:, i * SUB:(i + 1) * SUB] * jnp.exp(b[:, i * SUB:(i + 1) * SUB] - bref)
        kd = k * jnp.exp(jnp.minimum(bref - b, 0.0))
        a_i = _bmm(qd, kd, nt=True)
        a_rows.append(jnp.where(s_sub < i * SUB, a_i, 0.0))
    o = _bmm(jnp.concatenate(a_rows, axis=1), v)

    t_loc = lax.broadcasted_iota(jnp.int32, (shape[0], SUB, HEAD), 1)
    diag = []
    for i in range(CHUNK // SUB):
        sl = slice(i * SUB, (i + 1) * SUB)
        bb, qq = b[:, sl], q[:, sl]
        acc = jnp.zeros((shape[0], SUB, HEAD), F32)
        for s in range(SUB):
            row = i * SUB + s
            e = jnp.where(t_loc >= s, jnp.exp(jnp.minimum(bb - rows_ref[0, :, row:row + 1, :], 0.0)), 0.0)
            w = jnp.sum(e * qq * rows_ref[1, :, row:row + 1, :], axis=-1, keepdims=True)
            acc = acc + w * rows_ref[2, :, row:row + 1, :]
        diag.append(acc)
    o = o + jnp.concatenate(diag, axis=1)

    st = st_ref[...]
    o = o + _bmm(q * jnp.exp(b), st, nt=True)
    b_last = b[:, CHUNK - 1:CHUNK]
    st_ref[...] = st * jnp.exp(b_last) + _bmm(jnp.swapaxes(v, 1, 2), k * jnp.exp(b_last - b))

    _store_tokens(o_ref, _rms(o, ng_ref[...]) * _silu(g))


def _hgrn_mixer(hm, lb, norm_g):
    _, b, h, s, d = hm.shape
    g = b * h
    par = _full((g, 1, d))
    return pl.pallas_call(
        _hgrn_kernel,
        grid=(s // CHUNK,),
        in_specs=[_hm_groups_spec(3, HM_HGRN_QFI)(b, h), _hm_groups_spec(1, HM_HGRN_G)(b, h)] + [par] * 4,
        out_specs=_tok_chunk_spec(b),
        out_shape=jax.ShapeDtypeStruct((b, s, h * d), F32),
        scratch_shapes=[pltpu.VMEM((g, d, d), F32), pltpu.VMEM((3, g, CHUNK, d), F32)],
        compiler_params=_params("arbitrary"),
        name="hgrn2",
    )(hm, hm, jnp.log(lb), jnp.log1p(-lb), 1.0 - lb, norm_g)


def _gdn_kernel(qkv_ref, z_ref, acol_ref, bcol_ref, arow_ref, cw_ref, alog_ref, dtb_ref, ng_ref,
                o_ref, xbuf_ref, st_ref):
    g3 = xbuf_ref.shape[0]
    g = g3 // 3

    @pl.when(pl.program_id(0) == 0)
    def _():
        st_ref[...] = jnp.zeros_like(st_ref)
        xbuf_ref[:, 0:HALO, :] = jnp.zeros((g3, HALO, HEAD), F32)

    xbuf_ref[:, HALO:HALO + CHUNK, :] = qkv_ref[...].reshape(g3, CHUNK, HEAD)
    conv = jnp.zeros((g3, CHUNK, HEAD), F32)
    for j in range(CONV_K):
        conv = conv + cw_ref[j] * xbuf_ref[:, HALO - (CONV_K - 1) + j:HALO - (CONV_K - 1) + j + CHUNK, :]
    xbuf_ref[:, 0:HALO, :] = xbuf_ref[:, CHUNK:CHUNK + HALO, :]
    qkv = _silu(conv)
    q, k, v = qkv[0:g], qkv[g:2 * g], qkv[2 * g:3 * g]
    q = q * lax.rsqrt(jnp.sum(q * q, axis=-1, keepdims=True) + EPS) * HEAD ** -0.5
    k = k * lax.rsqrt(jnp.sum(k * k, axis=-1, keepdims=True) + EPS)

    neg_a = -jnp.exp(alog_ref[...])
    g_col = neg_a * _softplus(acol_ref[0] + dtb_ref[...])
    g_row = neg_a * _softplus(arow_ref[0, 0] + dtb_ref[...])
    beta = jax.nn.sigmoid(bcol_ref[0])
    t_idx, s_idx = _tri_masks(CHUNK)
    causal = s_idx <= t_idx
    gc_col = jnp.sum(jnp.where(causal, g_row, 0.0), axis=-1, keepdims=True)
    gc_row = jnp.sum(jnp.where(t_idx <= s_idx, g_col, 0.0), axis=1, keepdims=True)
    rel = jnp.where(causal, jnp.exp(jnp.minimum(gc_col - gc_row, 0.0)), 0.0)

    kb = k * beta
    lower = jnp.where(s_idx < t_idx, _bmm(kb, k, nt=True) * rel, 0.0)
    uw = _solve_unit_lower(lower, jnp.concatenate([v * beta, kb * jnp.exp(gc_col)], axis=-1))
    u, w = uw[:, :, :HEAD], uw[:, :, HEAD:]
    scores = _bmm(q, k, nt=True) * rel

    st = st_ref[...]
    v_new = u - _bmm(w, st)
    o = _bmm(q * jnp.exp(gc_col), st) + _bmm(scores, v_new)
    g_last = gc_col[:, CHUNK - 1:CHUNK]
    kd = k * jnp.exp(g_last - gc_col)
    st_ref[...] = st * jnp.exp(g_last) + _bmm(jnp.swapaxes(kd, 1, 2), v_new)

    _store_tokens(o_ref, _rms(o, ng_ref[...]) * _silu(z_ref[0].reshape(g, CHUNK, HEAD)))


def _gdn_mixer(hm, a_col, b_col, a_row, conv_w, a_log, dt_bias, norm_g):
    _, b, h, s, d = hm.shape
    g = b * h
    col = pl.BlockSpec((1, g, CHUNK, 1), lambda c: (0, 0, c, 0))
    return pl.pallas_call(
        _gdn_kernel,
        grid=(s // CHUNK,),
        in_specs=[_hm_groups_spec(3, HM_GDN_QKV)(b, h), _hm_groups_spec(1, HM_GDN_Z)(b, h), col, col,
                  pl.BlockSpec((1, 1, g, 1, CHUNK), lambda c: (0, c, 0, 0, 0)),
                  _full((CONV_K, 3 * g, 1, d)), _full((g, 1, 1)), _full((g, 1, 1)), _full((1, 1, d))],
        out_specs=_tok_chunk_spec(b),
        out_shape=jax.ShapeDtypeStruct((b, s, h * d), F32),
        scratch_shapes=[pltpu.VMEM((3 * g, HALO + CHUNK, d), F32), pltpu.VMEM((g, d, d), F32)],
        compiler_params=_params("arbitrary"),
        name="gated_deltanet",
    )(hm, hm, a_col, b_col, a_row, conv_w, a_log, dt_bias, norm_g)


def _lru_kernel(x_ref, gate_ref, cw_ref, cb_ref, wa_ref, ba_ref, wx_ref, bx_ref, lam_ref, o_ref,
                xbuf_ref, h_ref):
    tt = x_ref.shape[1]

    @pl.when(pl.program_id(1) == 0)
    def _():
        h_ref[...] = jnp.zeros_like(h_ref)
        xbuf_ref[0:HALO, :] = jnp.zeros((HALO, GROUP), F32)

    xbuf_ref[HALO:HALO + tt, :] = x_ref[0]
    xc = jnp.zeros((tt, GROUP), F32) + cb_ref[...]
    for j in range(CONV_K):
        xc = xc + cw_ref[j:j + 1, :] * xbuf_ref[HALO - (CONV_K - 1) + j:HALO - (CONV_K - 1) + j + tt, :]
    xbuf_ref[0:HALO, :] = xbuf_ref[tt:tt + HALO, :]

    r = jax.nn.sigmoid(_mm16(xc, wa_ref[...]) + ba_ref[...])
    i = jax.nn.sigmoid(_mm16(xc, wx_ref[...]) + bx_ref[...])
    log_a = -LRU_C * r * _softplus(-lam_ref[...])
    a = jnp.exp(log_a)
    u = jnp.sqrt(1.0 - a * a) * (i * xc)

    row = lax.broadcasted_iota(jnp.int32, (tt, GROUP), 0)
    shift = 1
    while shift < tt:
        keep = row >= shift
        a_prev = pltpu.roll(a, shift, 0)
        u_prev = pltpu.roll(u, shift, 0)
        u = jnp.where(keep, a * u_prev + u, u)
        a = jnp.where(keep, a * a_prev, a)
        shift *= 2
    hs = u + a * h_ref[...]
    h_ref[...] = hs[tt - 1:tt, :]

    gate = gate_ref[0]
    gelu = 0.5 * gate * (1.0 + jnp.tanh(math.sqrt(2.0 / math.pi) * (gate + 0.044715 * gate * gate * gate)))
    o_ref[0] = hs * gelu


def _lru_mixer(tok, conv_w, conv_b, wa, ba, wx, bx, lam):
    b, s, _ = tok.shape
    c = GROUP
    tt = TT_SCAN
    x_spec = pl.BlockSpec((1, tt, c), lambda i, t: (i, t, TOK_LRU // GROUP))
    gate_spec = pl.BlockSpec((1, tt, c), lambda i, t: (i, t, TOK_LRU // GROUP + 1))
    vec = _full((1, c))
    return pl.pallas_call(
        _lru_kernel,
        grid=(b, s // tt),
        in_specs=[x_spec, gate_spec, _full((CONV_K, c)), vec, _full((c, c)), vec, _full((c, c)), vec, vec],
        out_specs=pl.BlockSpec((1, tt, c), lambda i, t: (i, t, 0)),
        out_shape=jax.ShapeDtypeStruct((b, s, c), F32),
        scratch_shapes=[pltpu.VMEM((HALO + tt, c), F32), pltpu.VMEM((1, c), F32)],
        compiler_params=_params("parallel", "arbitrary"),
        name="rg_lru",
    )(tok, tok, conv_w, conv_b.reshape(1, c), wa, ba.reshape(1, c), wx, bx.reshape(1, c), lam.reshape(1, c))


def _rwkv_prep_kernel(has_vmix, keep_v, *refs):
    refs = list(refs)
    p_ref, mu_ref, w0_ref, w2_ref, a0_ref, a2_ref, g2_ref = refs[:7]
    del refs[:7]
    if has_vmix:
        vf_ref, v0_ref, v1_ref, v2_ref = refs[:4]
        del refs[:4]
    out_ref = refs.pop(0)
    vtok_ref = refs.pop(0) if keep_v else None
    xbuf_ref = refs.pop(0)
    tt = p_ref.shape[1]

    @pl.when(pl.program_id(1) == 0)
    def _():
        xbuf_ref[0:HALO, :] = jnp.zeros((HALO, xbuf_ref.shape[1]), F32)

    c = p_ref[0]
    xbuf_ref[HALO:HALO + tt, :] = c
    prev = xbuf_ref[HALO - 1:HALO - 1 + tt, :]
    xbuf_ref[0:HALO, :] = xbuf_ref[tt:tt + HALO, :]
    c = c + mu_ref[...] * (prev - c)

    r = c[:, 0:GROUP]
    k = c[:, GROUP:2 * GROUP]
    v = c[:, 2 * GROUP:3 * GROUP]
    wa_lo = c[:, 3 * GROUP:3 * GROUP + LANES]
    g_lo = c[:, 3 * GROUP + LANES:4 * GROUP]
    log_w = -math.exp(-0.5) * jax.nn.sigmoid(w0_ref[...] + _mm16(jnp.tanh(wa_lo), w2_ref[...]))
    a = jax.nn.sigmoid(a0_ref[...] + _mm16(wa_lo, a2_ref[...]))
    g = _mm16(jax.nn.sigmoid(g_lo), g2_ref[...])
    if has_vmix:
        mix = jax.nn.sigmoid(v0_ref[...] + _mm16(_mm16(v, v1_ref[...]), v2_ref[...]))
        v = v + (vf_ref[0] - v) * mix
    if keep_v:
        vtok_ref[0] = v
    for j, val in enumerate((r, log_w, k, v, a, g)):
        _store_heads(out_ref, val, (j, 0))


def _rwkv_prep(tok, mu, w0, w2, a0, a2, g2, v_first, v_mix):
    b, s, _ = tok.shape
    n = 4 * GROUP
    tt = TT_SCAN
    lora = w2.shape[0]
    w2p = jnp.zeros((LANES, GROUP), F32).at[:lora].set(w2)
    a2p = jnp.zeros((LANES, GROUP), F32).at[lora:].set(a2)
    tile = pl.BlockSpec((1, tt, GROUP), lambda i, t: (i, t, 0))
    vec = _full((1, GROUP))
    args = [tok, mu.reshape(1, n), w0.reshape(1, GROUP), w2p, a0.reshape(1, GROUP), a2p, g2]
    specs = [pl.BlockSpec((1, tt, n), lambda i, t: (i, t, TOK_RWKV // n)), _full((1, n)), vec,
             _full((LANES, GROUP)), vec, _full((LANES, GROUP)), _full(g2.shape)]
    keep_v = v_mix is None
    if v_mix is not None:
        v0, v1, v2 = v_mix
        rank = v1.shape[1]
        v1p = jnp.zeros((GROUP, LANES), F32).at[:, :rank].set(v1)
        v2p = jnp.zeros((LANES, GROUP), F32).at[:rank].set(v2)
        args += [v_first, v0.reshape(1, GROUP), v1p, v2p]
        specs += [tile, vec, _full((GROUP, LANES)), _full((LANES, GROUP))]
    out_specs = [pl.BlockSpec((6, 1, HEADS, tt, HEAD), lambda i, t: (0, i, 0, t, 0))]
    out_shape = [jax.ShapeDtypeStruct((6, b, HEADS, s, HEAD), F32)]
    if keep_v:
        out_specs.append(tile)
        out_shape.append(jax.ShapeDtypeStruct((b, s, GROUP), F32))
    outs = pl.pallas_call(
        functools.partial(_rwkv_prep_kernel, v_mix is not None, keep_v),
        grid=(b, s // tt),
        in_specs=specs,
        out_specs=out_specs,
        out_shape=out_shape,
        scratch_shapes=[pltpu.VMEM((HALO + tt, n), F32)],
        compiler_params=_params("parallel", "arbitrary"),
        name="rwkv7_prep",
    )(*args)
    return (outs[0], outs[1]) if keep_v else (outs[0], v_first)


def _rwkv_kernel(in_ref, kk_ref, ka_ref, rk_ref, lnw_ref, lnb_ref, o_ref, zt_ref):
    @pl.when(pl.program_id(0) == 0)
    def _():
        zt_ref[...] = jnp.zeros_like(zt_ref)

    shape = zt_ref.shape[:1] + (CHUNK, HEAD)
    r, lw, k, v, a, g = (in_ref[j].reshape(shape) for j in range(6))
    kk = k * kk_ref[...]
    kk = kk * lax.rsqrt(jnp.sum(kk * kk, axis=-1, keepdims=True) + EPS)
    k = k * (1.0 + (a - 1.0) * ka_ref[...])
    b = _cumsum_rows(lw)
    e_pos = jnp.exp(b)
    e_neg = jnp.exp(-b)
    r_t = r * e_pos
    al_t = -kk * jnp.exp(b - lw)
    be_t = kk * a * e_neg
    k_t = k * e_neg

    t_idx, s_idx = _tri_masks(CHUNK)
    strict = s_idx < t_idx
    incl = s_idx <= t_idx
    ab_k = jnp.concatenate([be_t, k_t], axis=1)
    al_all = _bmm(al_t, ab_k, nt=True)
    r_all = _bmm(r_t, ab_k, nt=True)
    l_ab = jnp.where(strict, al_all[:, :, :CHUNK], 0.0)
    l_ak = jnp.where(strict, al_all[:, :, CHUNK:], 0.0)
    m_rb = jnp.where(incl, r_all[:, :, :CHUNK], 0.0)
    m_rk = jnp.where(incl, r_all[:, :, CHUNK:], 0.0)

    sol = _solve_unit_lower(-l_ab, jnp.concatenate([al_t, _bmm(l_ak, v)], axis=-1))
    w1, u0 = sol[:, :, :HEAD], sol[:, :, HEAD:]

    zt = zt_ref[...]
    u = _bmm(w1, zt, nt=True) + u0
    y = _bmm(r_t, zt, nt=True) + _bmm(m_rb, u) + _bmm(m_rk, v)
    b_last = b[:, CHUNK - 1:CHUNK]
    e_last = jnp.exp(b_last - b)
    uv_t = jnp.swapaxes(jnp.concatenate([u, v], axis=1), 1, 2)
    zt_ref[...] = zt * jnp.exp(b_last) + _bmm(uv_t, jnp.concatenate([kk * a * e_last, k * e_last], axis=1))

    mean = jnp.mean(y, axis=-1, keepdims=True)
    var = jnp.mean(jnp.square(y - mean), axis=-1, keepdims=True)
    yn = (y - mean) * lax.rsqrt(var + RWKV_GN_EPS) * lnw_ref[...] + lnb_ref[...]
    bonus = jnp.sum(r * k * rk_ref[...], axis=-1, keepdims=True) * v
    _store_tokens(o_ref, (yn + bonus) * g)


def _rwkv_mixer(hm, k_k, k_a, r_k, ln_w, ln_b):
    _, b, h, s, d = hm.shape
    g = b * h
    par = _full((g, 1, d))
    return pl.pallas_call(
        _rwkv_kernel,
        grid=(s // CHUNK,),
        in_specs=[_hm_groups_spec(6, 0)(b, h)] + [par] * 5,
        out_specs=_tok_chunk_spec(b),
        out_shape=jax.ShapeDtypeStruct((b, s, h * d), F32),
        scratch_shapes=[pltpu.VMEM((g, d, d), F32)],
        compiler_params=_params("arbitrary"),
        name="rwkv7",
    )(hm, k_k, k_a, r_k, ln_w, ln_b)


def _par_hm(p, b, width=HEAD):
    return jnp.tile(p.reshape(-1, 1, width), (b, 1, 1))


def _gdn_gate_layouts(alpha, beta):
    b, s, h = alpha.shape
    a_col = alpha.transpose(0, 2, 1).reshape(1, b * h, s, 1)
    b_col = beta.transpose(0, 2, 1).reshape(1, b * h, s, 1)
    a_row = alpha.reshape(b, s // CHUNK, CHUNK, h).transpose(1, 0, 3, 2).reshape(1, s // CHUNK, b * h, 1, CHUNK)
    return a_col, b_col, a_row


def _gdn_conv_layout(w, b):
    k, c = w.shape
    h = c // (3 * HEAD)
    w = jnp.broadcast_to(w.reshape(k, 3, 1, h, 1, HEAD), (k, 3, b, h, 1, HEAD))
    return w.reshape(k, 3 * b * h, 1, HEAD)


def _block_diag(w):
    n, a, b = w.shape
    out = jnp.zeros((n * a, n * b), w.dtype)
    for i in range(n):
        out = out.at[i * a:(i + 1) * a, i * b:(i + 1) * b].set(w[i])
    return out


def _in_proj_weight(w):
    d = w.shape[0]
    g4 = 4 * GROUP
    gates = g4 + g4
    lru = gates + 2 * HEADS
    rwkv = lru + 2 * GROUP
    cols = [w[:, rwkv:rwkv + g4], w[:, lru:rwkv], w[:, gates:lru], jnp.zeros((d, LANES - 2 * HEADS), w.dtype),
            w[:, 0:3 * GROUP], w[:, g4:g4 + 3 * GROUP], w[:, 3 * GROUP:g4], w[:, g4 + 3 * GROUP:gates]]
    return jnp.concatenate(cols, axis=1).astype(BF16)


def kernel(x, mem, norm_mix, w_in, w_out, hgrn_lb_logits, hgrn_norm, gdn_conv_w, gdn_A_log, gdn_dt_bias, gdn_norm, lru_conv_w, lru_conv_b, lru_wa, lru_ba, lru_wx, lru_bx, lru_lambda, rwkv_mu, rwkv_w0, rwkv_w2, rwkv_a0, rwkv_a2, rwkv_g2, rwkv_k_k, rwkv_k_a, rwkv_r_k, rwkv_ln_w, rwkv_ln_b, rwkv_v0, rwkv_v1, rwkv_v2, mem_norm, norm_xattn, xattn_wq, xattn_wk, xattn_wv, xattn_wo, norm_ffn, ffn_w_gate, ffn_w_up, ffn_w_down, moe_router, moe_w_gate, moe_w_up, moe_w_down, norm_final):
    bsz, seq, d = x.shape
    depth = w_in.shape[0]
    tok = bsz * seq
    mlen = mem.shape[1]

    lb = jnp.cumsum(jax.nn.softmax(hgrn_lb_logits.astype(F32), axis=0), axis=0)
    lb = lb - lb[0]
    mem_f = mem.reshape(bsz * mlen, d)
    v_first = None
    for l in range(depth):
        p_tok, p_hm = _in_proj(x, norm_mix[l], _in_proj_weight(w_in[l]))

        y_a = _hgrn_mixer(p_hm, _par_hm(lb[l], bsz), _par_hm(hgrn_norm[l], bsz))

        gates = p_tok[..., TOK_GATES:TOK_GATES + 2 * HEADS]
        y_b = _gdn_mixer(p_hm, *_gdn_gate_layouts(gates[..., :HEADS], gates[..., HEADS:]),
                         _gdn_conv_layout(gdn_conv_w[l], bsz), _par_hm(gdn_A_log[l], bsz, 1),
                         _par_hm(gdn_dt_bias[l], bsz, 1), gdn_norm[l].reshape(1, 1, HEAD))

        y_c = _lru_mixer(p_tok, lru_conv_w[l], lru_conv_b[l], _block_diag(lru_wa[l]), lru_ba[l],
                         _block_diag(lru_wx[l]), lru_bx[l], lru_lambda[l])

        v_mix = None if l == 0 else (rwkv_v0[l - 1], rwkv_v1[l - 1], rwkv_v2[l - 1])
        d_hm, v_first = _rwkv_prep(p_tok, rwkv_mu[l], rwkv_w0[l], rwkv_w2[l], rwkv_a0[l], rwkv_a2[l],
                                   rwkv_g2[l], v_first, v_mix)
        y_d = _rwkv_mixer(d_hm, _par_hm(rwkv_k_k[l], bsz), _par_hm(rwkv_k_a[l], bsz), _par_hm(rwkv_r_k[l], bsz),
                          _par_hm(rwkv_ln_w[l], bsz), _par_hm(rwkv_ln_b[l], bsz))

        x = _out_proj((y_a, y_b, y_c, y_d), w_out[l].reshape(4, GROUP, d).astype(BF16), x)

        w_kv = jnp.concatenate([xattn_wk[l], xattn_wv[l]], axis=1).astype(BF16)
        kv = _norm_matmul(mem_f, mem_norm, w_kv, mlen, BF16).reshape(bsz, mlen, 2 * d)
        x = _cross_attention(x, norm_xattn[l], xattn_wq[l].astype(BF16), kv, xattn_wo[l].astype(BF16))

        xf = x.reshape(tok, d)
        j = l // 2
        if l % 2 == 0:
            ones = jnp.ones((1, tok, 1), F32)
            xf = _ffn(xf, norm_ffn[l], ffn_w_gate[j][None].astype(BF16), ffn_w_up[j][None].astype(BF16),
                      ffn_w_down[j][None].astype(BF16), ones, TF_FFN)
        else:
            comb = _router(xf, norm_ffn[l], moe_router[j])
            comb = comb[:, :N_EXPERTS].T[..., None]
            xf = _ffn(xf, norm_ffn[l], moe_w_gate[j].astype(BF16), moe_w_up[j].astype(BF16),
                      moe_w_down[j].astype(BF16), comb, TF_MOE)
        x = xf.reshape(bsz, seq, d)
    return _final_norm(x.reshape(tok, d), norm_final).reshape(bsz, seq, d)
```

```python
import functools
import math

import jax
import jax.numpy as jnp
from jax import lax
from jax.experimental import pallas as pl
from jax.experimental.pallas import tpu as pltpu

F32 = jnp.float32
BF16 = jnp.bfloat16
HIGHEST = lax.Precision.HIGHEST

GROUP = 256
HEAD = 64
HEADS = GROUP // HEAD
CHUNK = 64
SUB = 16
CONV_K = 4
HALO = 8
LRU_C = 8.0
RWKV_GN_EPS = 64e-5
XATTN_HEADS = 4
N_EXPERTS = 8
EPS = 1e-6
LANES = 128
VMEM_LIMIT_BYTES = 56 * 1024 * 1024

TOK_RWKV = 0
TOK_LRU = 4 * GROUP
TOK_GATES = 6 * GROUP
N_TOK = 6 * GROUP + LANES
HM_HGRN_QFI, HM_GDN_QKV, HM_HGRN_G, HM_GDN_Z = 0, 3, 6, 7
HM_GROUPS = 8

TM_IN_PROJ = 256
TM_OUT_PROJ = 512
TM_XATTN = 512
TM_FFN = 1024
TF_FFN = 256
TM_NORM = 1024
TT_SCAN = 512
MOE_TB = 512
MOE_TILE = 512
MOE_F_SPLIT = 2
ROW_ALIGN = 16
SPARE_TILES = 2


def _params(*semantics):
    return pltpu.CompilerParams(dimension_semantics=semantics, vmem_limit_bytes=VMEM_LIMIT_BYTES)


def _full(shape):
    n = len(shape)
    return pl.BlockSpec(shape, lambda *_: (0,) * n)


def _bmm(a, b, nt=False, exact=False):
    dn = (((2,), (2 if nt else 1,)), ((0,), (0,)))
    if exact:
        return lax.dot_general(a, b, dn, precision=HIGHEST, preferred_element_type=F32)
    return lax.dot_general(a.astype(BF16), b.astype(BF16), dn, preferred_element_type=F32)


def _mm(a, b, precision=None):
    return jnp.dot(a, b, precision=precision, preferred_element_type=F32)


def _mm16(a, b):
    return jnp.dot(a.astype(BF16), b.astype(BF16), preferred_element_type=F32)


def _softplus(x):
    return jnp.maximum(x, 0.0) + jnp.log1p(jnp.exp(-jnp.abs(x)))


def _silu(x):
    return x * jax.nn.sigmoid(x)


def _rms(x, gain):
    return x * lax.rsqrt(jnp.mean(x * x, axis=-1, keepdims=True) + EPS) * gain


def _tri_masks(n):
    t = lax.broadcasted_iota(jnp.int32, (n, n), 0)
    s = lax.broadcasted_iota(jnp.int32, (n, n), 1)
    return t, s


def _cumsum_rows(x):
    g, c, _ = x.shape
    t, s = _tri_masks(c)
    tri = jnp.broadcast_to((s <= t).astype(F32), (g, c, c))
    return _bmm(tri, x, exact=True)


def _solve_unit_lower(low, rhs):
    c = low.shape[1]
    r = rhs.shape[2]
    assert c == 4 * SUB and SUB == 16
    t, s = _tri_masks(c)
    same_block = (t // SUB) == (s // SUB)
    d1 = jnp.where(same_block, low, 0.0)
    z = jnp.concatenate([rhs, jnp.where(same_block, 0.0, low)], axis=-1)
    d2 = _bmm(d1, d1)
    d4 = _bmm(d2, d2)
    d8 = _bmm(d4, d4)
    z = z + _bmm(d8, z)
    z = z + _bmm(d4, z)
    z = z + _bmm(d2, z)
    z = z - _bmm(d1, z)
    y = z[:, :, :r]
    n1 = z[:, :, r:]
    n2 = _bmm(n1, n1)
    y = y + _bmm(n2, y)
    return y - _bmm(n1, y)


def _store_heads(ref, val, lead=()):
    for h in range(val.shape[1] // HEAD):
        ref[lead + (h,)] = val[:, h * HEAD:(h + 1) * HEAD]


def _store_tokens(ref, o):
    nb = ref.shape[0]
    nh = o.shape[0] // nb
    for b in range(nb):
        ref[b] = jnp.concatenate([o[b * nh + h] for h in range(nh)], axis=-1)


def _in_proj_kernel(x_ref, g_ref, w_ref, tok_ref, hm_ref):
    hn = _rms(x_ref[0], g_ref[...]).astype(BF16)
    acc = _mm(hn, w_ref[...])
    n_tok = tok_ref.shape[2]
    tok_ref[0] = acc[:, :n_tok]
    for g in range(hm_ref.shape[0]):
        _store_heads(hm_ref, acc[:, n_tok + g * GROUP:n_tok + (g + 1) * GROUP], (g, 0))


def _in_proj(x, gain, w):
    b, s, d = x.shape
    n = w.shape[1]
    tm = TM_IN_PROJ
    return pl.pallas_call(
        _in_proj_kernel,
        grid=(b, s // tm),
        in_specs=[pl.BlockSpec((1, tm, d), lambda i, j: (i, j, 0)), _full((1, d)), _full((d, n))],
        out_specs=[pl.BlockSpec((1, tm, N_TOK), lambda i, j: (i, j, 0)),
                   pl.BlockSpec((HM_GROUPS, 1, HEADS, tm, HEAD), lambda i, j: (0, i, 0, j, 0))],
        out_shape=[jax.ShapeDtypeStruct((b, s, N_TOK), F32),
                   jax.ShapeDtypeStruct((HM_GROUPS, b, HEADS, s, HEAD), F32)],
        compiler_params=_params("parallel", "parallel"),
        name="in_proj",
    )(x, gain.reshape(1, d), w)


def _norm_mm_kernel(x_ref, g_ref, w_ref, o_ref):
    o_ref[...] = _mm(_rms(x_ref[...], g_ref[...]).astype(BF16), w_ref[...]).astype(o_ref.dtype)


def _norm_matmul(x, gain, w, tm, out_dtype):
    t, k = x.shape
    n = w.shape[1]
    return pl.pallas_call(
        _norm_mm_kernel,
        grid=(t // tm,),
        in_specs=[pl.BlockSpec((tm, k), lambda i: (i, 0)), _full((1, k)), _full((k, n))],
        out_specs=pl.BlockSpec((tm, n), lambda i: (i, 0)),
        out_shape=jax.ShapeDtypeStruct((t, n), out_dtype),
        compiler_params=_params("parallel"),
        name="norm_matmul",
    )(x, gain.reshape(1, k), w)


def _out_proj_kernel(ya_ref, yb_ref, yc_ref, yd_ref, w_ref, r_ref, o_ref):
    acc = r_ref[0]
    for m, y_ref in enumerate((ya_ref, yb_ref, yc_ref, yd_ref)):
        acc = acc + _mm(y_ref[0].astype(BF16), w_ref[m])
    o_ref[0] = acc


def _out_proj(ys, w, res):
    b, s, d = res.shape
    tm = TM_OUT_PROJ
    y_spec = pl.BlockSpec((1, tm, GROUP), lambda i, j: (i, j, 0))
    x_spec = pl.BlockSpec((1, tm, d), lambda i, j: (i, j, 0))
    return pl.pallas_call(
        _out_proj_kernel,
        grid=(b, s // tm),
        in_specs=[y_spec] * 4 + [_full(w.shape), x_spec],
        out_specs=x_spec,
        out_shape=jax.ShapeDtypeStruct((b, s, d), F32),
        compiler_params=_params("parallel", "parallel"),
        name="out_proj",
    )(*ys, w, res)


def _xattn_kernel(x_ref, g_ref, wq_ref, k_ref, v_ref, wo_ref, o_ref):
    x = x_ref[0]
    hn = _rms(x, g_ref[...]).astype(BF16)
    q = _mm(hn, wq_ref[...])
    d = q.shape[1] // XATTN_HEADS
    outs = []
    for h in range(XATTN_HEADS):
        qh = (q[:, h * d:(h + 1) * d] * d ** -0.5).astype(BF16)
        kh = k_ref[0, :, h * d:(h + 1) * d]
        vh = v_ref[0, :, h * d:(h + 1) * d]
        s = lax.dot_general(qh, kh, (((1,), (1,)), ((), ())), preferred_element_type=F32)
        s = s - jnp.max(s, axis=-1, keepdims=True)
        p = jnp.exp(s)
        p = p / jnp.sum(p, axis=-1, keepdims=True)
        outs.append(_mm(p.astype(BF16), vh))
    o = jnp.concatenate(outs, axis=-1).astype(BF16)
    o_ref[0] = x + _mm(o, wo_ref[...])


def _cross_attention(x, gain, wq, kv, wo):
    b, s, d = x.shape
    m = kv.shape[1]
    tm = TM_XATTN
    return pl.pallas_call(
        _xattn_kernel,
        grid=(b, s // tm),
        in_specs=[pl.BlockSpec((1, tm, d), lambda i, j: (i, j, 0)),
                  _full((1, d)),
                  _full((d, d)),
                  pl.BlockSpec((1, m, d), lambda i, j: (i, 0, 0)),
                  pl.BlockSpec((1, m, d), lambda i, j: (i, 0, 1)),
                  _full((d, d))],
        out_specs=pl.BlockSpec((1, tm, d), lambda i, j: (i, j, 0)),
        out_shape=jax.ShapeDtypeStruct((b, s, d), F32),
        compiler_params=_params("parallel", "parallel"),
        name="cross_attention",
    )(x, gain.reshape(1, d), wq, kv, kv, wo)


def _ffn_kernel(x_ref, g_ref, wg_ref, wu_ref, wd_ref, c_ref, o_ref, hn_ref, acc_ref):
    f = pl.program_id(1)

    @pl.when(f == 0)
    def _():
        hn_ref[...] = _rms(x_ref[...], g_ref[...]).astype(BF16)
        acc_ref[...] = x_ref[...]

    hn = hn_ref[...]
    gate = _mm(hn, wg_ref[0])
    up = _mm(hn, wu_ref[0])
    mid = _silu(gate) * up * c_ref[0]
    acc_ref[...] += _mm(mid.astype(BF16), wd_ref[0])

    @pl.when(f == pl.num_programs(1) - 1)
    def _():
        o_ref[...] = acc_ref[...]


def _ffn(x, gain, wg, wu, wd, combine, tf):
    t, d = x.shape
    e, _, ff = wg.shape
    nf = ff // tf
    tm = TM_FFN
    return pl.pallas_call(
        _ffn_kernel,
        grid=(t // tm, e * nf),
        in_specs=[pl.BlockSpec((tm, d), lambda i, f: (i, 0)),
                  _full((1, d)),
                  pl.BlockSpec((1, d, tf), lambda i, f: (f // nf, 0, f % nf)),
                  pl.BlockSpec((1, d, tf), lambda i, f: (f // nf, 0, f % nf)),
                  pl.BlockSpec((1, tf, d), lambda i, f: (f // nf, f % nf, 0)),
                  pl.BlockSpec((1, tm, 1), lambda i, f: (f // nf, i, 0))],
        out_specs=pl.BlockSpec((tm, d), lambda i, f: (i, 0)),
        out_shape=jax.ShapeDtypeStruct((t, d), F32),
        scratch_shapes=[pltpu.VMEM((tm, d), BF16), pltpu.VMEM((tm, d), F32)],
        compiler_params=_params("parallel", "arbitrary"),
        name="ffn",
    )(x, gain.reshape(1, d), wg, wu, wd, combine)


def _router_kernel(x_ref, g_ref, w_ref, hn_ref, comb_ref, pos_ref, post_ref, cnt_ref):
    hn = _rms(x_ref[...], g_ref[...])
    hn_ref[...] = hn.astype(BF16)
    logits = _mm(hn, w_ref[...], precision=HIGHEST)
    lane = lax.broadcasted_iota(jnp.int32, logits.shape, 1)
    neg = jnp.float32(-jnp.inf)
    logits = jnp.where(lane < N_EXPERTS, logits, neg)
    m1 = jnp.max(logits, axis=-1, keepdims=True)
    i1 = jnp.min(jnp.where(logits == m1, lane, LANES), axis=-1, keepdims=True)
    rest = jnp.where(lane == i1, neg, logits)
    m2 = jnp.max(rest, axis=-1, keepdims=True)
    i2 = jnp.min(jnp.where(rest == m2, lane, LANES), axis=-1, keepdims=True)
    e2 = jnp.exp(m2 - m1)
    comb = jnp.where(lane == i1, 1.0 / (1.0 + e2), 0.0) + jnp.where(lane == i2, e2 / (1.0 + e2), 0.0)
    comb_ref[...] = comb

    sel = (comb > 0.0).astype(F32)
    tb = sel.shape[0]
    row = lax.broadcasted_iota(jnp.int32, sel.shape, 0)
    run = sel
    shift = 1
    while shift < tb:
        run = run + jnp.where(row >= shift, pltpu.roll(run, shift, 0), 0.0)
        shift *= 2
    pos = jnp.where(sel > 0.0, run - 1.0, -1.0)
    pos_ref[...] = pos
    post_ref[0] = jnp.transpose(pos)[:N_EXPERTS, :]
    cnt_ref[0] = run[tb - 1:tb, :].astype(jnp.int32)


def _router(x, gain, w_router):
    t, d = x.shape
    tb = MOE_TB
    nb = t // tb
    w = jnp.zeros((d, LANES), F32).at[:, :N_EXPERTS].set(w_router)
    tile = pl.BlockSpec((tb, LANES), lambda i: (i, 0))
    return pl.pallas_call(
        _router_kernel,
        grid=(nb,),
        in_specs=[pl.BlockSpec((tb, d), lambda i: (i, 0)), _full((1, d)), _full((d, LANES))],
        out_specs=[pl.BlockSpec((tb, d), lambda i: (i, 0)), tile, tile,
                   pl.BlockSpec((1, N_EXPERTS, tb), lambda i: (i, 0, 0)),
                   pl.BlockSpec((1, 1, LANES), lambda i: (i, 0, 0))],
        out_shape=[jax.ShapeDtypeStruct((t, d), BF16), jax.ShapeDtypeStruct((t, LANES), F32),
                   jax.ShapeDtypeStruct((t, LANES), F32), jax.ShapeDtypeStruct((nb, N_EXPERTS, tb), F32),
                   jax.ShapeDtypeStruct((nb, 1, LANES), jnp.int32)],
        compiler_params=_params("parallel"),
        name="router",
    )(x, gain.reshape(1, d), w)


def _route_tables(cnt, n_tiles_max):
    nb, ne = cnt.shape
    cpad = (cnt + (ROW_ALIGN - 1)) // ROW_ALIGN * ROW_ALIGN
    off = jnp.cumsum(cpad, axis=0) - cpad
    total = jnp.sum(cpad, axis=0)
    ntile = (total + MOE_TILE - 1) // MOE_TILE
    first = jnp.cumsum(ntile) - ntile
    n_tiles = jnp.sum(ntile)
    region = (first + SPARE_TILES * jnp.arange(ne, dtype=jnp.int32)) * MOE_TILE
    dst = region[None, :] + off
    win_rel = jnp.clip(off, 0, jnp.maximum(ntile * MOE_TILE - MOE_TB, 0)[None, :])
    win = first[None, :] * MOE_TILE + win_rel
    shift = off - win_rel
    t = jnp.minimum(jnp.arange(n_tiles_max, dtype=jnp.int32), n_tiles - 1)
    tile_e = jnp.clip(jnp.searchsorted(jnp.cumsum(ntile), t, side="right"), 0, ne - 1).astype(jnp.int32)
    tile_blk = t + SPARE_TILES * tile_e
    valid = (jnp.arange(n_tiles_max, dtype=jnp.int32) < n_tiles).astype(jnp.int32)
    n_blocks = n_tiles_max + SPARE_TILES * ne
    used = jnp.zeros((n_blocks,), jnp.int32).at[tile_blk].max(valid)
    fill = jnp.argsort(used, stable=True).astype(jnp.int32)
    n_fill = (n_blocks - n_tiles).astype(jnp.int32).reshape(1)
    flat = lambda a: a.reshape(-1).astype(jnp.int32)
    return (flat(dst), fill, n_fill), (flat(win), flat(shift), flat(cnt)), (tile_e, tile_blk.astype(jnp.int32), valid)


def _dispatch_kernel(dst_ref, fill_ref, nfill_ref, hn_ref, post_ref, xs_ref, seg_ref, sem):
    blk = pl.program_id(0)
    tb = hn_ref.shape[0]
    assert tb == MOE_TILE
    rank = lax.broadcasted_iota(jnp.int32, (tb, tb), 0).astype(F32)
    for e in range(N_EXPERTS):
        onehot = (rank == post_ref[0, e:e + 1, :]).astype(BF16)
        seg_ref[e] = _mm(onehot, hn_ref[...]).astype(BF16)

    def seg_copy(e, src, extra):
        row = pl.multiple_of(dst_ref[blk * N_EXPERTS + e] + extra, ROW_ALIGN)
        return pltpu.make_async_copy(seg_ref.at[src], xs_ref.at[pl.ds(row, tb)], sem.at[e])

    for e in range(N_EXPERTS):
        seg_copy(e, e, 0).start()
    for e in range(N_EXPERTS):
        seg_copy(e, e, 0).wait()

    @pl.when(blk == pl.num_programs(0) - 1)
    def _():
        seg_ref[0] = jnp.zeros(seg_ref.shape[1:], BF16)
        for e in range(N_EXPERTS):
            seg_copy(e, 0, tb).start()
        for e in range(N_EXPERTS):
            seg_copy(e, 0, tb).wait()

        def fill_block(k, carry):
            row = pl.multiple_of(fill_ref[k] * MOE_TILE, MOE_TILE)
            copy = pltpu.make_async_copy(seg_ref.at[0], xs_ref.at[pl.ds(row, tb)], sem.at[0])
            copy.start()
            copy.wait()
            return carry

        lax.fori_loop(0, nfill_ref[0], fill_block, 0)


def _dispatch(tables, hn, pos_t, n_rows):
    t, d = hn.shape
    tb = MOE_TB
    return pl.pallas_call(
        _dispatch_kernel,
        grid_spec=pltpu.PrefetchScalarGridSpec(
            num_scalar_prefetch=3,
            grid=(t // tb,),
            in_specs=[pl.BlockSpec((tb, d), lambda i, *_: (i, 0)),
                      pl.BlockSpec((1, N_EXPERTS, tb), lambda i, *_: (i, 0, 0))],
            out_specs=pl.BlockSpec(memory_space=pl.ANY),
            scratch_shapes=[pltpu.VMEM((N_EXPERTS, tb, d), BF16), pltpu.SemaphoreType.DMA((N_EXPERTS,))]),
        out_shape=jax.ShapeDtypeStruct((n_rows, d), BF16),
        compiler_params=_params("arbitrary"),
        name="moe_dispatch",
    )(*tables, hn, pos_t)


def _gate_up_kernel(te_ref, tb_ref, ok_ref, x_ref, wg_ref, wu_ref, o_ref):
    t = pl.program_id(1)

    @pl.when(ok_ref[t] > 0)
    def _():
        x = x_ref[...]
        o_ref[...] = (_silu(_mm(x, wg_ref[0])) * _mm(x, wu_ref[0])).astype(o_ref.dtype)

    @pl.when(ok_ref[t] == 0)
    def _():
        o_ref[...] = jnp.zeros_like(o_ref)


def _gate_up(tile_e, tile_blk, valid, xs, wg, wu):
    d = xs.shape[1]
    ff = wg.shape[2]
    nt = tile_e.shape[0]
    tf = ff // MOE_F_SPLIT
    w_spec = pl.BlockSpec((1, d, tf), lambda f, t, te, tb, ok: (te[t], 0, f))
    return pl.pallas_call(
        _gate_up_kernel,
        grid_spec=pltpu.PrefetchScalarGridSpec(
            num_scalar_prefetch=3,
            grid=(MOE_F_SPLIT, nt),
            in_specs=[pl.BlockSpec((MOE_TILE, d), lambda f, t, te, tb, ok: (tb[t], 0)), w_spec, w_spec],
            out_specs=pl.BlockSpec((MOE_TILE, tf), lambda f, t, te, tb, ok: (t, f))),
        out_shape=jax.ShapeDtypeStruct((nt * MOE_TILE, ff), BF16),
        compiler_params=_params("arbitrary", "arbitrary"),
        name="moe_gate_up",
    )(tile_e, tile_blk, valid, xs, wg, wu)


def _down_kernel(te_ref, ok_ref, m_ref, wd_ref, o_ref):
    t = pl.program_id(0)

    @pl.when(ok_ref[t] > 0)
    def _():
        o_ref[...] = _mm(m_ref[...], wd_ref[0]).astype(o_ref.dtype)

    @pl.when(ok_ref[t] == 0)
    def _():
        o_ref[...] = jnp.zeros_like(o_ref)


def _down(tile_e, valid, mid, wd):
    ff, d = wd.shape[1:]
    nt = tile_e.shape[0]
    return pl.pallas_call(
        _down_kernel,
        grid_spec=pltpu.PrefetchScalarGridSpec(
            num_scalar_prefetch=2,
            grid=(nt,),
            in_specs=[pl.BlockSpec((MOE_TILE, ff), lambda t, te, ok: (t, 0)),
                      pl.BlockSpec((1, ff, d), lambda t, te, ok: (te[t], 0, 0))],
            out_specs=pl.BlockSpec((MOE_TILE, d), lambda t, te, ok: (t, 0))),
        out_shape=jax.ShapeDtypeStruct((nt * MOE_TILE, d), BF16),
        compiler_params=_params("arbitrary"),
        name="moe_down",
    )(tile_e, valid, mid, wd)


def _combine_kernel(win_ref, shift_ref, cnt_ref, x_ref, comb_ref, pos_ref, ys_ref, o_ref, win_buf, sem):
    blk = pl.program_id(0)
    tb = x_ref.shape[0]

    def win_copy(e):
        row = pl.multiple_of(win_ref[blk * N_EXPERTS + e], ROW_ALIGN)
        return pltpu.make_async_copy(ys_ref.at[pl.ds(row, tb)], win_buf.at[e], sem.at[e])

    for e in range(N_EXPERTS):
        @pl.when(cnt_ref[blk * N_EXPERTS + e] > 0)
        def _(e=e):
            win_copy(e).start()

    o_ref[...] = x_ref[...]
    col = lax.broadcasted_iota(jnp.int32, (tb, tb), 1).astype(F32)
    for e in range(N_EXPERTS):
        @pl.when(cnt_ref[blk * N_EXPERTS + e] > 0)
        def _(e=e):
            win_copy(e).wait()
            pos = pos_ref[...][:, e:e + 1]
            target = jnp.where(pos >= 0.0, pos + shift_ref[blk * N_EXPERTS + e].astype(F32), -1.0)
            onehot = (col == target).astype(BF16)
            o_ref[...] += comb_ref[...][:, e:e + 1] * _mm(onehot, win_buf[e])


def _combine(win, shift, cnt, x, comb, pos, ys):
    t, d = x.shape
    tb = MOE_TB
    lanes = pl.BlockSpec((tb, LANES), lambda i, *_: (i, 0))
    return pl.pallas_call(
        _combine_kernel,
        grid_spec=pltpu.PrefetchScalarGridSpec(
            num_scalar_prefetch=3,
            grid=(t // tb,),
            in_specs=[pl.BlockSpec((tb, d), lambda i, *_: (i, 0)), lanes, lanes,
                      pl.BlockSpec(memory_space=pl.ANY)],
            out_specs=pl.BlockSpec((tb, d), lambda i, *_: (i, 0)),
            scratch_shapes=[pltpu.VMEM((N_EXPERTS, tb, d), BF16), pltpu.SemaphoreType.DMA((N_EXPERTS,))]),
        out_shape=jax.ShapeDtypeStruct((t, d), F32),
        compiler_params=_params("arbitrary"),
        name="moe_combine",
    )(win, shift, cnt, x, comb, pos, ys)


def _moe(x, gain, w_router, wg, wu, wd):
    t, d = x.shape
    nb = t // MOE_TB
    n_tiles_max = (2 * t + nb * N_EXPERTS * (ROW_ALIGN - 1)) // MOE_TILE + N_EXPERTS
    hn, comb, pos, pos_t, cnt = _router(x, gain, w_router)
    write_tables, read_tables, (tile_e, tile_blk, valid) = _route_tables(cnt[:, 0, :N_EXPERTS], n_tiles_max)
    n_rows = (n_tiles_max + SPARE_TILES * N_EXPERTS) * MOE_TILE
    xs = _dispatch(write_tables, hn, pos_t, n_rows)
    mid = _gate_up(tile_e, tile_blk, valid, xs, wg, wu)
    ys = _down(tile_e, valid, mid, wd)
    return _combine(*read_tables, x, comb, pos, ys)


def _final_norm_kernel(x_ref, g_ref, o_ref):
    o_ref[...] = _rms(x_ref[...], g_ref[...])


def _final_norm(x, gain):
    t, d = x.shape
    tm = TM_NORM
    return pl.pallas_call(
        _final_norm_kernel,
        grid=(t // tm,),
        in_specs=[pl.BlockSpec((tm, d), lambda i: (i, 0)), _full((1, d))],
        out_specs=pl.BlockSpec((tm, d), lambda i: (i, 0)),
        out_shape=jax.ShapeDtypeStruct((t, d), F32),
        compiler_params=_params("parallel"),
        name="final_norm",
    )(x, gain.reshape(1, d))


def _hm_groups_spec(n, first):
    def spec(b, h):
        return pl.BlockSpec((n, b, h, CHUNK, HEAD), lambda c: (first // n, 0, 0, c, 0))
    assert first % n == 0
    return spec


def _tok_chunk_spec(b):
    return pl.BlockSpec((b, CHUNK, GROUP), lambda c: (0, c, 0))


def _hgrn_kernel(qfi_ref, g_ref, loglb_ref, log1mlb_ref, onemlb_ref, ng_ref, o_ref, st_ref, rows_ref):
    @pl.when(pl.program_id(0) == 0)
    def _():
        st_ref[...] = jnp.zeros_like(st_ref)

    shape = st_ref.shape[:1] + (CHUNK, HEAD)
    q, fl, v = (qfi_ref[j].reshape(shape) for j in range(3))
    g = g_ref[0].reshape(shape)
    log_sig = jnp.minimum(fl, 0.0) - jnp.log1p(jnp.exp(-jnp.abs(fl)))
    la = loglb_ref[...]
    lc = log1mlb_ref[...] + log_sig
    log_f = jnp.maximum(la, lc) + jnp.log1p(jnp.exp(-jnp.abs(la - lc)))
    k = onemlb_ref[...] * jax.nn.sigmoid(-fl)
    b = _cumsum_rows(log_f)

    rows_ref[0] = b
    rows_ref[1] = k
    rows_ref[2] = v
    s_sub = lax.broadcasted_iota(jnp.int32, (SUB, CHUNK), 1)
    zeros_sub = jnp.zeros((shape[0], SUB, CHUNK), F32)
    a_rows = [zeros_sub]
    for i in range(1, CHUNK // SUB):
        bref = rows_ref[0, :, i * SUB - 1:i * SUB, :]
        qd = q[:, i * SUB:(i + 1) * SUB] * jnp.exp(b[:, i * SUB:(i + 1) * SUB] - bref)
        kd = k * jnp.exp(jnp.minimum(bref - b, 0.0))
        a_i = _bmm(qd, kd, nt=True)
        a_rows.append(jnp.where(s_sub < i * SUB, a_i, 0.0))
    o = _bmm(jnp.concatenate(a_rows, axis=1), v)

    t_loc = lax.broadcasted_iota(jnp.int32, (shape[0], SUB, HEAD), 1)
    diag = []
    for i in range(CHUNK // SUB):
        sl = slice(i * SUB, (i + 1) * SUB)
        bb, qq = b[:, sl], q[:, sl]
        acc = jnp.zeros((shape[0], SUB, HEAD), F32)
        for s in range(SUB):
            row = i * SUB + s
            e = jnp.where(t_loc >= s, jnp.exp(jnp.minimum(bb - rows_ref[0, :, row:row + 1, :], 0.0)), 0.0)
            w = jnp.sum(e * qq * rows_ref[1, :, row:row + 1, :], axis=-1, keepdims=True)
            acc = acc + w * rows_ref[2, :, row:row + 1, :]
        diag.append(acc)
    o = o + jnp.concatenate(diag, axis=1)

    st = st_ref[...]
    o = o + _bmm(q * jnp.exp(b), st, nt=True)
    b_last = b[:, CHUNK - 1:CHUNK]
    st_ref[...] = st * jnp.exp(b_last) + _bmm(jnp.swapaxes(v, 1, 2), k * jnp.exp(b_last - b))

    _store_tokens(o_ref, _rms(o, ng_ref[...]) * _silu(g))


def _hgrn_mixer(hm, lb, norm_g):
    _, b, h, s, d = hm.shape
    g = b * h
    par = _full((g, 1, d))
    return pl.pallas_call(
        _hgrn_kernel,
        grid=(s // CHUNK,),
        in_specs=[_hm_groups_spec(3, HM_HGRN_QFI)(b, h), _hm_groups_spec(1, HM_HGRN_G)(b, h)] + [par] * 4,
        out_specs=_tok_chunk_spec(b),
        out_shape=jax.ShapeDtypeStruct((b, s, h * d), F32),
        scratch_shapes=[pltpu.VMEM((g, d, d), F32), pltpu.VMEM((3, g, CHUNK, d), F32)],
        compiler_params=_params("arbitrary"),
        name="hgrn2",
    )(hm, hm, jnp.log(lb), jnp.log1p(-lb), 1.0 - lb, norm_g)


def _gdn_kernel(qkv_ref, z_ref, gates_ref, cw_ref, alog_ref, dtb_ref, ng_ref, o_ref, xbuf_ref, st_ref):
    g3 = xbuf_ref.shape[0]
    g = g3 // 3

    @pl.when(pl.program_id(0) == 0)
    def _():
        st_ref[...] = jnp.zeros_like(st_ref)
        xbuf_ref[:, 0:HALO, :] = jnp.zeros((g3, HALO, HEAD), F32)

    xbuf_ref[:, HALO:HALO + CHUNK, :] = qkv_ref[...].reshape(g3, CHUNK, HEAD)
    conv = jnp.zeros((g3, CHUNK, HEAD), F32)
    for j in range(CONV_K):
        conv = conv + cw_ref[j] * xbuf_ref[:, HALO - (CONV_K - 1) + j:HALO - (CONV_K - 1) + j + CHUNK, :]
    xbuf_ref[:, 0:HALO, :] = xbuf_ref[:, CHUNK:CHUNK + HALO, :]
    qkv = _silu(conv)
    q, k, v = qkv[0:g], qkv[g:2 * g], qkv[2 * g:3 * g]
    q = q * lax.rsqrt(jnp.sum(q * q, axis=-1, keepdims=True) + EPS) * HEAD ** -0.5
    k = k * lax.rsqrt(jnp.sum(k * k, axis=-1, keepdims=True) + EPS)

    t_idx, s_idx = _tri_masks(CHUNK)
    causal = s_idx <= t_idx
    eye = s_idx == t_idx
    g_row = -jnp.exp(alog_ref[...]) * _softplus(gates_ref[0, 0] + dtb_ref[...])
    g_col = jnp.sum(jnp.where(eye, g_row, 0.0), axis=-1, keepdims=True)
    beta = jnp.sum(jnp.where(eye, jax.nn.sigmoid(gates_ref[1, 0]), 0.0), axis=-1, keepdims=True)
    gc_col = jnp.sum(jnp.where(causal, g_row, 0.0), axis=-1, keepdims=True)
    gc_row = jnp.sum(jnp.where(t_idx <= s_idx, g_col, 0.0), axis=1, keepdims=True)
    rel = jnp.where(causal, jnp.exp(jnp.minimum(gc_col - gc_row, 0.0)), 0.0)

    kb = k * beta
    lower = jnp.where(s_idx < t_idx, _bmm(kb, k, nt=True) * rel, 0.0)
    uw = _solve_unit_lower(lower, jnp.concatenate([v * beta, kb * jnp.exp(gc_col)], axis=-1))
    u, w = uw[:, :, :HEAD], uw[:, :, HEAD:]
    scores = _bmm(q, k, nt=True) * rel

    st = st_ref[...]
    v_new = u - _bmm(w, st)
    o = _bmm(q * jnp.exp(gc_col), st) + _bmm(scores, v_new)
    g_last = gc_col[:, CHUNK - 1:CHUNK]
    kd = k * jnp.exp(g_last - gc_col)
    st_ref[...] = st * jnp.exp(g_last) + _bmm(jnp.swapaxes(kd, 1, 2), v_new)

    _store_tokens(o_ref, _rms(o, ng_ref[...]) * _silu(z_ref[0].reshape(g, CHUNK, HEAD)))


def _gdn_mixer(hm, gates, conv_w, a_log, dt_bias, norm_g):
    _, b, h, s, d = hm.shape
    g = b * h
    return pl.pallas_call(
        _gdn_kernel,
        grid=(s // CHUNK,),
        in_specs=[_hm_groups_spec(3, HM_GDN_QKV)(b, h), _hm_groups_spec(1, HM_GDN_Z)(b, h),
                  pl.BlockSpec((2, 1, g, 1, CHUNK), lambda c: (0, c, 0, 0, 0)),
                  _full((CONV_K, 3 * g, 1, d)), _full((g, 1, 1)), _full((g, 1, 1)), _full((1, 1, d))],
        out_specs=_tok_chunk_spec(b),
        out_shape=jax.ShapeDtypeStruct((b, s, h * d), F32),
        scratch_shapes=[pltpu.VMEM((3 * g, HALO + CHUNK, d), F32), pltpu.VMEM((g, d, d), F32)],
        compiler_params=_params("arbitrary"),
        name="gated_deltanet",
    )(hm, hm, gates, conv_w, a_log, dt_bias, norm_g)


def _lru_kernel(x_ref, gate_ref, cw_ref, cb_ref, wa_ref, ba_ref, wx_ref, bx_ref, lam_ref, o_ref,
                xbuf_ref, h_ref):
    tt = x_ref.shape[1]

    @pl.when(pl.program_id(1) == 0)
    def _():
        h_ref[...] = jnp.zeros_like(h_ref)
        xbuf_ref[0:HALO, :] = jnp.zeros((HALO, GROUP), F32)

    xbuf_ref[HALO:HALO + tt, :] = x_ref[0]
    xc = jnp.zeros((tt, GROUP), F32) + cb_ref[...]
    for j in range(CONV_K):
        xc = xc + cw_ref[j:j + 1, :] * xbuf_ref[HALO - (CONV_K - 1) + j:HALO - (CONV_K - 1) + j + tt, :]
    xbuf_ref[0:HALO, :] = xbuf_ref[tt:tt + HALO, :]

    r = jax.nn.sigmoid(_mm16(xc, wa_ref[...]) + ba_ref[...])
    i = jax.nn.sigmoid(_mm16(xc, wx_ref[...]) + bx_ref[...])
    log_a = -LRU_C * r * _softplus(-lam_ref[...])
    a = jnp.exp(log_a)
    u = jnp.sqrt(1.0 - a * a) * (i * xc)

    row = lax.broadcasted_iota(jnp.int32, (tt, GROUP), 0)
    shift = 1
    while shift < tt:
        keep = row >= shift
        a_prev = pltpu.roll(a, shift, 0)
        u_prev = pltpu.roll(u, shift, 0)
        u = jnp.where(keep, a * u_prev + u, u)
        a = jnp.where(keep, a * a_prev, a)
        shift *= 2
    hs = u + a * h_ref[...]
    h_ref[...] = hs[tt - 1:tt, :]

    gate = gate_ref[0]
    gelu = 0.5 * gate * (1.0 + jnp.tanh(math.sqrt(2.0 / math.pi) * (gate + 0.044715 * gate * gate * gate)))
    o_ref[0] = hs * gelu


def _lru_mixer(tok, conv_w, conv_b, wa, ba, wx, bx, lam):
    b, s, _ = tok.shape
    c = GROUP
    tt = TT_SCAN
    x_spec = pl.BlockSpec((1, tt, c), lambda i, t: (i, t, TOK_LRU // GROUP))
    gate_spec = pl.BlockSpec((1, tt, c), lambda i, t: (i, t, TOK_LRU // GROUP + 1))
    vec = _full((1, c))
    return pl.pallas_call(
        _lru_kernel,
        grid=(b, s // tt),
        in_specs=[x_spec, gate_spec, _full((CONV_K, c)), vec, _full((c, c)), vec, _full((c, c)), vec, vec],
        out_specs=pl.BlockSpec((1, tt, c), lambda i, t: (i, t, 0)),
        out_shape=jax.ShapeDtypeStruct((b, s, c), F32),
        scratch_shapes=[pltpu.VMEM((HALO + tt, c), F32), pltpu.VMEM((1, c), F32)],
        compiler_params=_params("parallel", "arbitrary"),
        name="rg_lru",
    )(tok, tok, conv_w, conv_b.reshape(1, c), wa, ba.reshape(1, c), wx, bx.reshape(1, c), lam.reshape(1, c))


def _rwkv_prep_kernel(has_vmix, keep_v, *refs):
    refs = list(refs)
    p_ref, mu_ref, w0_ref, w2_ref, a0_ref, a2_ref, g2_ref = refs[:7]
    del refs[:7]
    if has_vmix:
        vf_ref, v0_ref, v1_ref, v2_ref = refs[:4]
        del refs[:4]
    out_ref = refs.pop(0)
    vtok_ref = refs.pop(0) if keep_v else None
    xbuf_ref = refs.pop(0)
    tt = p_ref.shape[1]

    @pl.when(pl.program_id(1) == 0)
    def _():
        xbuf_ref[0:HALO, :] = jnp.zeros((HALO, xbuf_ref.shape[1]), F32)

    c = p_ref[0]
    xbuf_ref[HALO:HALO + tt, :] = c
    prev = xbuf_ref[HALO - 1:HALO - 1 + tt, :]
    xbuf_ref[0:HALO, :] = xbuf_ref[tt:tt + HALO, :]
    c = c + mu_ref[...] * (prev - c)

    r = c[:, 0:GROUP]
    k = c[:, GROUP:2 * GROUP]
    v = c[:, 2 * GROUP:3 * GROUP]
    wa_lo = c[:, 3 * GROUP:3 * GROUP + LANES]
    g_lo = c[:, 3 * GROUP + LANES:4 * GROUP]
    log_w = -math.exp(-0.5) * jax.nn.sigmoid(w0_ref[...] + _mm16(jnp.tanh(wa_lo), w2_ref[...]))
    a = jax.nn.sigmoid(a0_ref[...] + _mm16(wa_lo, a2_ref[...]))
    g = _mm16(jax.nn.sigmoid(g_lo), g2_ref[...])
    if has_vmix:
        mix = jax.nn.sigmoid(v0_ref[...] + _mm16(_mm16(v, v1_ref[...]), v2_ref[...]))
        v = v + (vf_ref[0] - v) * mix
    if keep_v:
        vtok_ref[0] = v
    for j, val in enumerate((r, log_w, k, v, a, g)):
        _store_heads(out_ref, val, (j, 0))


def _rwkv_prep(tok, mu, w0, w2, a0, a2, g2, v_first, v_mix):
    b, s, _ = tok.shape
    n = 4 * GROUP
    tt = TT_SCAN
    lora = w2.shape[0]
    w2p = jnp.zeros((LANES, GROUP), F32).at[:lora].set(w2)
    a2p = jnp.zeros((LANES, GROUP), F32).at[lora:].set(a2)
    tile = pl.BlockSpec((1, tt, GROUP), lambda i, t: (i, t, 0))
    vec = _full((1, GROUP))
    args = [tok, mu.reshape(1, n), w0.reshape(1, GROUP), w2p, a0.reshape(1, GROUP), a2p, g2]
    specs = [pl.BlockSpec((1, tt, n), lambda i, t: (i, t, TOK_RWKV // n)), _full((1, n)), vec,
             _full((LANES, GROUP)), vec, _full((LANES, GROUP)), _full(g2.shape)]
    keep_v = v_mix is None
    if v_mix is not None:
        v0, v1, v2 = v_mix
        rank = v1.shape[1]
        v1p = jnp.zeros((GROUP, LANES), F32).at[:, :rank].set(v1)
        v2p = jnp.zeros((LANES, GROUP), F32).at[:rank].set(v2)
        args += [v_first, v0.reshape(1, GROUP), v1p, v2p]
        specs += [tile, vec, _full((GROUP, LANES)), _full((LANES, GROUP))]
    out_specs = [pl.BlockSpec((6, 1, HEADS, tt, HEAD), lambda i, t: (0, i, 0, t, 0))]
    out_shape = [jax.ShapeDtypeStruct((6, b, HEADS, s, HEAD), F32)]
    if keep_v:
        out_specs.append(tile)
        out_shape.append(jax.ShapeDtypeStruct((b, s, GROUP), F32))
    outs = pl.pallas_call(
        functools.partial(_rwkv_prep_kernel, v_mix is not None, keep_v),
        grid=(b, s // tt),
        in_specs=specs,
        out_specs=out_specs,
        out_shape=out_shape,
        scratch_shapes=[pltpu.VMEM((HALO + tt, n), F32)],
        compiler_params=_params("parallel", "arbitrary"),
        name="rwkv7_prep",
    )(*args)
    return (outs[0], outs[1]) if keep_v else (outs[0], v_first)


def _rwkv_kernel(in_ref, kk_ref, ka_ref, rk_ref, lnw_ref, lnb_ref, o_ref, zt_ref):
    @pl.when(pl.program_id(0) == 0)
    def _():
        zt_ref[...] = jnp.zeros_like(zt_ref)

    shape = zt_ref.shape[:1] + (CHUNK, HEAD)
    r, lw, k, v, a, g = (in_ref[j].reshape(shape) for j in range(6))
    kk = k * kk_ref[...]
    kk = kk * lax.rsqrt(jnp.sum(kk * kk, axis=-1, keepdims=True) + EPS)
    k = k * (1.0 + (a - 1.0) * ka_ref[...])
    b = _cumsum_rows(lw)
    e_pos = jnp.exp(b)
    e_neg = jnp.exp(-b)
    r_t = r * e_pos
    al_t = -kk * jnp.exp(b - lw)
    be_t = kk * a * e_neg
    k_t = k * e_neg

    t_idx, s_idx = _tri_masks(CHUNK)
    strict = s_idx < t_idx
    incl = s_idx <= t_idx
    ab_k = jnp.concatenate([be_t, k_t], axis=1)
    al_all = _bmm(al_t, ab_k, nt=True)
    r_all = _bmm(r_t, ab_k, nt=True)
    l_ab = jnp.where(strict, al_all[:, :, :CHUNK], 0.0)
    l_ak = jnp.where(strict, al_all[:, :, CHUNK:], 0.0)
    m_rb = jnp.where(incl, r_all[:, :, :CHUNK], 0.0)
    m_rk = jnp.where(incl, r_all[:, :, CHUNK:], 0.0)

    sol = _solve_unit_lower(-l_ab, jnp.concatenate([al_t, _bmm(l_ak, v)], axis=-1))
    w1, u0 = sol[:, :, :HEAD], sol[:, :, HEAD:]

    zt = zt_ref[...]
    u = _bmm(w1, zt, nt=True) + u0
    y = _bmm(r_t, zt, nt=True) + _bmm(m_rb, u) + _bmm(m_rk, v)
    b_last = b[:, CHUNK - 1:CHUNK]
    e_last = jnp.exp(b_last - b)
    uv_t = jnp.swapaxes(jnp.concatenate([u, v], axis=1), 1, 2)
    zt_ref[...] = zt * jnp.exp(b_last) + _bmm(uv_t, jnp.concatenate([kk * a * e_last, k * e_last], axis=1))

    mean = jnp.mean(y, axis=-1, keepdims=True)
    var = jnp.mean(jnp.square(y - mean), axis=-1, keepdims=True)
    yn = (y - mean) * lax.rsqrt(var + RWKV_GN_EPS) * lnw_ref[...] + lnb_ref[...]
    bonus = jnp.sum(r * k * rk_ref[...], axis=-1, keepdims=True) * v
    _store_tokens(o_ref, (yn + bonus) * g)


def _rwkv_mixer(hm, k_k, k_a, r_k, ln_w, ln_b):
    _, b, h, s, d = hm.shape
    g = b * h
    par = _full((g, 1, d))
    return pl.pallas_call(
        _rwkv_kernel,
        grid=(s // CHUNK,),
        in_specs=[_hm_groups_spec(6, 0)(b, h)] + [par] * 5,
        out_specs=_tok_chunk_spec(b),
        out_shape=jax.ShapeDtypeStruct((b, s, h * d), F32),
        scratch_shapes=[pltpu.VMEM((g, d, d), F32)],
        compiler_params=_params("arbitrary"),
        name="rwkv7",
    )(hm, k_k, k_a, r_k, ln_w, ln_b)


def _par_hm(p, b, width=HEAD):
    return jnp.tile(p.reshape(-1, 1, width), (b, 1, 1))


def _gdn_gate_layout(gates):
    b, s, h2 = gates.shape
    g = gates.reshape(b, s // CHUNK, CHUNK, 2, h2 // 2).transpose(3, 1, 0, 4, 2)
    return g.reshape(2, s // CHUNK, b * (h2 // 2), 1, CHUNK)


def _gdn_conv_layout(w, b):
    k, c = w.shape
    h = c // (3 * HEAD)
    w = jnp.broadcast_to(w.reshape(k, 3, 1, h, 1, HEAD), (k, 3, b, h, 1, HEAD))
    return w.reshape(k, 3 * b * h, 1, HEAD)


def _block_diag(w):
    n, a, b = w.shape
    out = jnp.zeros((n * a, n * b), w.dtype)
    for i in range(n):
        out = out.at[i * a:(i + 1) * a, i * b:(i + 1) * b].set(w[i])
    return out


def _in_proj_weight(w):
    d = w.shape[0]
    g4 = 4 * GROUP
    gates = g4 + g4
    lru = gates + 2 * HEADS
    rwkv = lru + 2 * GROUP
    cols = [w[:, rwkv:rwkv + g4], w[:, lru:rwkv], w[:, gates:lru], jnp.zeros((d, LANES - 2 * HEADS), w.dtype),
            w[:, 0:3 * GROUP], w[:, g4:g4 + 3 * GROUP], w[:, 3 * GROUP:g4], w[:, g4 + 3 * GROUP:gates]]
    return jnp.concatenate(cols, axis=1).astype(BF16)


def kernel(x, mem, norm_mix, w_in, w_out, hgrn_lb_logits, hgrn_norm, gdn_conv_w, gdn_A_log, gdn_dt_bias, gdn_norm, lru_conv_w, lru_conv_b, lru_wa, lru_ba, lru_wx, lru_bx, lru_lambda, rwkv_mu, rwkv_w0, rwkv_w2, rwkv_a0, rwkv_a2, rwkv_g2, rwkv_k_k, rwkv_k_a, rwkv_r_k, rwkv_ln_w, rwkv_ln_b, rwkv_v0, rwkv_v1, rwkv_v2, mem_norm, norm_xattn, xattn_wq, xattn_wk, xattn_wv, xattn_wo, norm_ffn, ffn_w_gate, ffn_w_up, ffn_w_down, moe_router, moe_w_gate, moe_w_up, moe_w_down, norm_final):
    bsz, seq, d = x.shape
    depth = w_in.shape[0]
    tok = bsz * seq
    mlen = mem.shape[1]

    lb = jnp.cumsum(jax.nn.softmax(hgrn_lb_logits.astype(F32), axis=0), axis=0)
    lb = lb - lb[0]
    mem_f = mem.reshape(bsz * mlen, d)
    v_first = None
    for l in range(depth):
        p_tok, p_hm = _in_proj(x, norm_mix[l], _in_proj_weight(w_in[l]))

        y_a = _hgrn_mixer(p_hm, _par_hm(lb[l], bsz), _par_hm(hgrn_norm[l], bsz))

        gates = p_tok[..., TOK_GATES:TOK_GATES + 2 * HEADS]
        y_b = _gdn_mixer(p_hm, _gdn_gate_layout(gates),
                         _gdn_conv_layout(gdn_conv_w[l], bsz), _par_hm(gdn_A_log[l], bsz, 1),
                         _par_hm(gdn_dt_bias[l], bsz, 1), gdn_norm[l].reshape(1, 1, HEAD))

        y_c = _lru_mixer(p_tok, lru_conv_w[l], lru_conv_b[l], _block_diag(lru_wa[l]), lru_ba[l],
                         _block_diag(lru_wx[l]), lru_bx[l], lru_lambda[l])

        v_mix = None if l == 0 else (rwkv_v0[l - 1], rwkv_v1[l - 1], rwkv_v2[l - 1])
        d_hm, v_first = _rwkv_prep(p_tok, rwkv_mu[l], rwkv_w0[l], rwkv_w2[l], rwkv_a0[l], rwkv_a2[l],
                                   rwkv_g2[l], v_first, v_mix)
        y_d = _rwkv_mixer(d_hm, _par_hm(rwkv_k_k[l], bsz), _par_hm(rwkv_k_a[l], bsz), _par_hm(rwkv_r_k[l], bsz),
                          _par_hm(rwkv_ln_w[l], bsz), _par_hm(rwkv_ln_b[l], bsz))

        x = _out_proj((y_a, y_b, y_c, y_d), w_out[l].reshape(4, GROUP, d).astype(BF16), x)

        w_kv = jnp.concatenate([xattn_wk[l], xattn_wv[l]], axis=1).astype(BF16)
        kv = _norm_matmul(mem_f, mem_norm, w_kv, mlen, BF16).reshape(bsz, mlen, 2 * d)
        x = _cross_attention(x, norm_xattn[l], xattn_wq[l].astype(BF16), kv, xattn_wo[l].astype(BF16))

        xf = x.reshape(tok, d)
        j = l // 2
        if l % 2 == 0:
            ones = jnp.ones((1, tok, 1), F32)
            xf = _ffn(xf, norm_ffn[l], ffn_w_gate[j][None].astype(BF16), ffn_w_up[j][None].astype(BF16),
                      ffn_w_down[j][None].astype(BF16), ones, TF_FFN)
        else:
            xf = _moe(xf, norm_ffn[l], moe_router[j], moe_w_gate[j].astype(BF16), moe_w_up[j].astype(BF16),
                      moe_w_down[j].astype(BF16))
        x = xf.reshape(bsz, seq, d)
    return _final_norm(x.reshape(tok, d), norm_final).reshape(bsz, seq, d)
```

```python
import functools
import math

import jax
import jax.numpy as jnp
from jax import lax
from jax.experimental import pallas as pl
from jax.experimental.pallas import tpu as pltpu

F32 = jnp.float32
BF16 = jnp.bfloat16
HIGHEST = lax.Precision.HIGHEST

GROUP = 256
HEAD = 64
HEADS = GROUP // HEAD
CHUNK = 64
SUB = 16
HGRN_SUB = 8
LOG2_E = 1.4426950408889634
BIG_EXPONENT = 1e30
CONV_K = 4
HALO = 8
LRU_C = 8.0
RWKV_GN_EPS = 64e-5
XATTN_HEADS = 4
N_EXPERTS = 8
EPS = 1e-6
LANES = 128
VMEM_LIMIT_BYTES = 56 * 1024 * 1024

TOK_RWKV = 0
TOK_LRU = 4 * GROUP
TOK_GATES = 6 * GROUP
N_TOK = 6 * GROUP + LANES
HM_HGRN_QFI, HM_GDN_QKV, HM_HGRN_G, HM_GDN_Z = 0, 3, 6, 7
HM_GROUPS = 8

TM_IN_PROJ = 512
TM_OUT_PROJ = 512
TM_XATTN = 512
TM_FFN = 1024
TF_FFN = 1408
TM_NORM = 1024
TT_SCAN = 512
MOE_TB = 512
MOE_TILE = 512
MOE_F_SPLIT = 2
ROW_ALIGN = 16
SPARE_TILES = 2
MOE_SEG = 128
MOE_WIN = 256


def _params(*semantics):
    return pltpu.CompilerParams(dimension_semantics=semantics, vmem_limit_bytes=VMEM_LIMIT_BYTES)


def _full(shape):
    n = len(shape)
    return pl.BlockSpec(shape, lambda *_: (0,) * n)


def _bmm(a, b, nt=False, exact=False):
    dn = (((2,), (2 if nt else 1,)), ((0,), (0,)))
    if exact:
        return lax.dot_general(a, b, dn, precision=HIGHEST, preferred_element_type=F32)
    return lax.dot_general(a.astype(BF16), b.astype(BF16), dn, preferred_element_type=F32)


def _mm(a, b, precision=None):
    return jnp.dot(a, b, precision=precision, preferred_element_type=F32)


def _mm16(a, b):
    return jnp.dot(a.astype(BF16), b.astype(BF16), preferred_element_type=F32)


def _softplus(x):
    return jnp.maximum(x, 0.0) + jnp.log1p(jnp.exp(-jnp.abs(x)))


def _silu(x):
    return x * jax.nn.sigmoid(x)


def _rms(x, gain):
    return x * lax.rsqrt(jnp.mean(x * x, axis=-1, keepdims=True) + EPS) * gain


def _tri_masks(n):
    t = lax.broadcasted_iota(jnp.int32, (n, n), 0)
    s = lax.broadcasted_iota(jnp.int32, (n, n), 1)
    return t, s


def _cumsum_rows(x):
    g, c, _ = x.shape
    t, s = _tri_masks(c)
    tri = jnp.broadcast_to((s <= t).astype(F32), (g, c, c))
    return _bmm(tri, x, exact=True)


def _solve_unit_lower(low, rhs):
    c = low.shape[1]
    r = rhs.shape[2]
    assert c == 4 * SUB and SUB == 16
    t, s = _tri_masks(c)
    same_block = (t // SUB) == (s // SUB)
    d1 = jnp.where(same_block, low, 0.0)
    z = jnp.concatenate([rhs, jnp.where(same_block, 0.0, low)], axis=-1)
    d2 = _bmm(d1, d1)
    d4 = _bmm(d2, d2)
    d8 = _bmm(d4, d4)
    z = z + _bmm(d8, z)
    z = z + _bmm(d4, z)
    z = z + _bmm(d2, z)
    z = z - _bmm(d1, z)
    y = z[:, :, :r]
    n1 = z[:, :, r:]
    n2 = _bmm(n1, n1)
    y = y + _bmm(n2, y)
    return y - _bmm(n1, y)


def _store_heads(ref, val, lead=()):
    for h in range(val.shape[1] // HEAD):
        ref[lead + (h,)] = val[:, h * HEAD:(h + 1) * HEAD]


def _store_tokens(ref, o):
    nb = ref.shape[0]
    nh = o.shape[0] // nb
    for b in range(nb):
        ref[b] = jnp.concatenate([o[b * nh + h] for h in range(nh)], axis=-1)


def _in_proj_kernel(x_ref, g_ref, w_ref, tok_ref, hm_ref):
    hn = _rms(x_ref[0], g_ref[...]).astype(BF16)
    acc = _mm(hn, w_ref[...])
    n_tok = tok_ref.shape[2]
    tok_ref[0] = acc[:, :n_tok]
    for g in range(hm_ref.shape[0]):
        _store_heads(hm_ref, acc[:, n_tok + g * GROUP:n_tok + (g + 1) * GROUP], (g, 0))


def _in_proj(x, gain, w):
    b, s, d = x.shape
    n = w.shape[1]
    tm = TM_IN_PROJ
    return pl.pallas_call(
        _in_proj_kernel,
        grid=(b, s // tm),
        in_specs=[pl.BlockSpec((1, tm, d), lambda i, j: (i, j, 0)), _full((1, d)), _full((d, n))],
        out_specs=[pl.BlockSpec((1, tm, N_TOK), lambda i, j: (i, j, 0)),
                   pl.BlockSpec((HM_GROUPS, 1, HEADS, tm, HEAD), lambda i, j: (0, i, 0, j, 0))],
        out_shape=[jax.ShapeDtypeStruct((b, s, N_TOK), F32),
                   jax.ShapeDtypeStruct((HM_GROUPS, b, HEADS, s, HEAD), F32)],
        compiler_params=_params("parallel", "parallel"),
        name="in_proj",
    )(x, gain.reshape(1, d), w)


def _norm_mm_kernel(x_ref, g_ref, w_ref, o_ref):
    o_ref[...] = _mm(_rms(x_ref[...], g_ref[...]).astype(BF16), w_ref[...]).astype(o_ref.dtype)


def _norm_matmul(x, gain, w, tm, out_dtype):
    t, k = x.shape
    n = w.shape[1]
    return pl.pallas_call(
        _norm_mm_kernel,
        grid=(t // tm,),
        in_specs=[pl.BlockSpec((tm, k), lambda i: (i, 0)), _full((1, k)), _full((k, n))],
        out_specs=pl.BlockSpec((tm, n), lambda i: (i, 0)),
        out_shape=jax.ShapeDtypeStruct((t, n), out_dtype),
        compiler_params=_params("parallel"),
        name="norm_matmul",
    )(x, gain.reshape(1, k), w)


def _out_proj_kernel(ya_ref, yb_ref, yc_ref, yd_ref, w_ref, r_ref, o_ref):
    acc = r_ref[0]
    for m, y_ref in enumerate((ya_ref, yb_ref, yc_ref, yd_ref)):
        acc = acc + _mm(y_ref[0].astype(BF16), w_ref[m])
    o_ref[0] = acc


def _out_proj(ys, w, res):
    b, s, d = res.shape
    tm = TM_OUT_PROJ
    y_spec = pl.BlockSpec((1, tm, GROUP), lambda i, j: (i, j, 0))
    x_spec = pl.BlockSpec((1, tm, d), lambda i, j: (i, j, 0))
    return pl.pallas_call(
        _out_proj_kernel,
        grid=(b, s // tm),
        in_specs=[y_spec] * 4 + [_full(w.shape), x_spec],
        out_specs=x_spec,
        out_shape=jax.ShapeDtypeStruct((b, s, d), F32),
        compiler_params=_params("parallel", "parallel"),
        name="out_proj",
    )(*ys, w, res)


def _xattn_kernel(x_ref, g_ref, wq_ref, k_ref, v_ref, wo_ref, o_ref):
    x = x_ref[0]
    hn = _rms(x, g_ref[...]).astype(BF16)
    q = _mm(hn, wq_ref[...])
    d = q.shape[1] // XATTN_HEADS
    outs = []
    for h in range(XATTN_HEADS):
        qh = (q[:, h * d:(h + 1) * d] * d ** -0.5).astype(BF16)
        kh = k_ref[0, :, h * d:(h + 1) * d]
        vh = v_ref[0, :, h * d:(h + 1) * d]
        s = lax.dot_general(qh, kh, (((1,), (1,)), ((), ())), preferred_element_type=F32)
        s = s - jnp.max(s, axis=-1, keepdims=True)
        p = jnp.exp(s)
        p = p / jnp.sum(p, axis=-1, keepdims=True)
        outs.append(_mm(p.astype(BF16), vh))
    o = jnp.concatenate(outs, axis=-1).astype(BF16)
    o_ref[0] = x + _mm(o, wo_ref[...])


def _cross_attention(x, gain, wq, kv, wo):
    b, s, d = x.shape
    m = kv.shape[1]
    tm = TM_XATTN
    return pl.pallas_call(
        _xattn_kernel,
        grid=(b, s // tm),
        in_specs=[pl.BlockSpec((1, tm, d), lambda i, j: (i, j, 0)),
                  _full((1, d)),
                  _full((d, d)),
                  pl.BlockSpec((1, m, d), lambda i, j: (i, 0, 0)),
                  pl.BlockSpec((1, m, d), lambda i, j: (i, 0, 1)),
                  _full((d, d))],
        out_specs=pl.BlockSpec((1, tm, d), lambda i, j: (i, j, 0)),
        out_shape=jax.ShapeDtypeStruct((b, s, d), F32),
        compiler_params=_params("parallel", "parallel"),
        name="cross_attention",
    )(x, gain.reshape(1, d), wq, kv, kv, wo)


def _ffn_kernel(x_ref, g_ref, wg_ref, wu_ref, wd_ref, c_ref, o_ref, hn_ref, acc_ref):
    f = pl.program_id(1)

    @pl.when(f == 0)
    def _():
        hn_ref[...] = _rms(x_ref[...], g_ref[...]).astype(BF16)
        acc_ref[...] = x_ref[...]

    hn = hn_ref[...]
    gate = _mm(hn, wg_ref[0])
    up = _mm(hn, wu_ref[0])
    mid = _silu(gate) * up * c_ref[0]
    acc_ref[...] += _mm(mid.astype(BF16), wd_ref[0])

    @pl.when(f == pl.num_programs(1) - 1)
    def _():
        o_ref[...] = acc_ref[...]


def _ffn(x, gain, wg, wu, wd, combine, tf):
    t, d = x.shape
    e, _, ff = wg.shape
    nf = ff // tf
    tm = TM_FFN
    return pl.pallas_call(
        _ffn_kernel,
        grid=(t // tm, e * nf),
        in_specs=[pl.BlockSpec((tm, d), lambda i, f: (i, 0)),
                  _full((1, d)),
                  pl.BlockSpec((1, d, tf), lambda i, f: (f // nf, 0, f % nf)),
                  pl.BlockSpec((1, d, tf), lambda i, f: (f // nf, 0, f % nf)),
                  pl.BlockSpec((1, tf, d), lambda i, f: (f // nf, f % nf, 0)),
                  pl.BlockSpec((1, tm, 1), lambda i, f: (f // nf, i, 0))],
        out_specs=pl.BlockSpec((tm, d), lambda i, f: (i, 0)),
        out_shape=jax.ShapeDtypeStruct((t, d), F32),
        scratch_shapes=[pltpu.VMEM((tm, d), BF16), pltpu.VMEM((tm, d), F32)],
        compiler_params=_params("parallel", "arbitrary"),
        name="ffn",
    )(x, gain.reshape(1, d), wg, wu, wd, combine)


def _router_kernel(x_ref, g_ref, w_ref, hn_ref, comb_ref, pos_ref, post_ref, cnt_ref):
    hn = _rms(x_ref[...], g_ref[...])
    hn_ref[...] = hn.astype(BF16)
    logits = _mm(hn, w_ref[...], precision=HIGHEST)
    lane = lax.broadcasted_iota(jnp.int32, logits.shape, 1)
    neg = jnp.float32(-jnp.inf)
    logits = jnp.where(lane < N_EXPERTS, logits, neg)
    m1 = jnp.max(logits, axis=-1, keepdims=True)
    i1 = jnp.min(jnp.where(logits == m1, lane, LANES), axis=-1, keepdims=True)
    rest = jnp.where(lane == i1, neg, logits)
    m2 = jnp.max(rest, axis=-1, keepdims=True)
    i2 = jnp.min(jnp.where(rest == m2, lane, LANES), axis=-1, keepdims=True)
    e2 = jnp.exp(m2 - m1)
    comb = jnp.where(lane == i1, 1.0 / (1.0 + e2), 0.0) + jnp.where(lane == i2, e2 / (1.0 + e2), 0.0)
    comb_ref[...] = comb

    sel = (comb > 0.0).astype(F32)
    tb = sel.shape[0]
    row = lax.broadcasted_iota(jnp.int32, sel.shape, 0)
    run = sel
    shift = 1
    while shift < tb:
        run = run + jnp.where(row >= shift, pltpu.roll(run, shift, 0), 0.0)
        shift *= 2
    pos = jnp.where(sel > 0.0, run - 1.0, -1.0)
    pos_ref[...] = pos
    post_ref[0] = jnp.transpose(pos)[:N_EXPERTS, :]
    cnt_ref[0] = run[tb - 1:tb, :].astype(jnp.int32)


def _router(x, gain, w_router):
    t, d = x.shape
    tb = MOE_TB
    nb = t // tb
    w = jnp.zeros((d, LANES), F32).at[:, :N_EXPERTS].set(w_router)
    tile = pl.BlockSpec((tb, LANES), lambda i: (i, 0))
    return pl.pallas_call(
        _router_kernel,
        grid=(nb,),
        in_specs=[pl.BlockSpec((tb, d), lambda i: (i, 0)), _full((1, d)), _full((d, LANES))],
        out_specs=[pl.BlockSpec((tb, d), lambda i: (i, 0)), tile, tile,
                   pl.BlockSpec((1, N_EXPERTS, tb), lambda i: (i, 0, 0)),
                   pl.BlockSpec((1, 1, LANES), lambda i: (i, 0, 0))],
        out_shape=[jax.ShapeDtypeStruct((t, d), BF16), jax.ShapeDtypeStruct((t, LANES), F32),
                   jax.ShapeDtypeStruct((t, LANES), F32), jax.ShapeDtypeStruct((nb, N_EXPERTS, tb), F32),
                   jax.ShapeDtypeStruct((nb, 1, LANES), jnp.int32)],
        compiler_params=_params("parallel"),
        name="router",
    )(x, gain.reshape(1, d), w)


def _route_tables(cnt, n_tiles_max):
    nb, ne = cnt.shape
    cpad = (cnt + (ROW_ALIGN - 1)) // ROW_ALIGN * ROW_ALIGN
    off = jnp.cumsum(cpad, axis=0) - cpad
    total = jnp.sum(cpad, axis=0)
    ntile = (total + MOE_TILE - 1) // MOE_TILE
    first = jnp.cumsum(ntile) - ntile
    n_tiles = jnp.sum(ntile)
    region = (first + SPARE_TILES * jnp.arange(ne, dtype=jnp.int32)) * MOE_TILE
    dst = region[None, :] + off
    tail = region + total
    sub = jnp.arange(MOE_TB // MOE_WIN, dtype=jnp.int32)[None, None, :] * MOE_WIN
    want = off[:, :, None] + sub
    win_rel = jnp.clip(want, 0, jnp.maximum(ntile * MOE_TILE - MOE_WIN, 0)[None, :, None])
    win = first[None, :, None] * MOE_TILE + win_rel
    shift = want - win_rel
    t = jnp.minimum(jnp.arange(n_tiles_max, dtype=jnp.int32), n_tiles - 1)
    tile_e = jnp.minimum(jnp.sum(t[:, None] >= jnp.cumsum(ntile)[None, :], axis=1), ne - 1).astype(jnp.int32)
    tile_blk = t + SPARE_TILES * tile_e
    valid = (jnp.arange(n_tiles_max, dtype=jnp.int32) < n_tiles).astype(jnp.int32)
    n_blocks = n_tiles_max + SPARE_TILES * ne
    blocks = jnp.arange(n_blocks, dtype=jnp.int32)
    unused = 1 - jnp.max((tile_blk[:, None] == blocks[None, :]) * valid[:, None], axis=0)
    rank = jnp.cumsum(unused) - 1
    fill = jnp.sum(jnp.where((rank[None, :] == blocks[:, None]) & (unused[None, :] > 0), blocks[None, :], 0), axis=1)
    n_fill = jnp.sum(unused).astype(jnp.int32).reshape(1)
    flat = lambda a: a.reshape(-1).astype(jnp.int32)
    return ((flat(cpad), flat(dst), flat(tail), flat(fill), n_fill), (flat(cnt), flat(win), flat(shift)),
            (tile_e, tile_blk.astype(jnp.int32), valid))


def _dispatch_kernel(cpad_ref, dst_ref, tail_ref, fill_ref, nfill_ref, hn_ref, post_ref, xs_ref, seg_ref, sem):
    blk = pl.program_id(0)
    tb = hn_ref.shape[0]
    assert tb == MOE_TILE
    pieces = [(e, j) for e in range(N_EXPERTS) for j in range(tb // MOE_SEG)]

    def needed(e, j):
        return cpad_ref[blk * N_EXPERTS + e] > j * MOE_SEG

    def piece_copy(e, j):
        row = pl.multiple_of(dst_ref[blk * N_EXPERTS + e] + j * MOE_SEG, ROW_ALIGN)
        return pltpu.make_async_copy(seg_ref.at[e, pl.ds(j * MOE_SEG, MOE_SEG)], xs_ref.at[pl.ds(row, MOE_SEG)],
                                     sem.at[e, j])

    rank = lax.broadcasted_iota(jnp.int32, (MOE_SEG, tb), 0).astype(F32)
    for e, j in pieces:
        @pl.when(needed(e, j))
        def _(e=e, j=j):
            onehot = (rank + float(j * MOE_SEG) == post_ref[0, e:e + 1, :]).astype(BF16)
            seg_ref[e, j * MOE_SEG:(j + 1) * MOE_SEG, :] = _mm(onehot, hn_ref[...]).astype(BF16)
            piece_copy(e, j).start()
    for e, j in pieces:
        @pl.when(needed(e, j))
        def _(e=e, j=j):
            piece_copy(e, j).wait()

    @pl.when(blk == pl.num_programs(0) - 1)
    def _():
        seg_ref[0] = jnp.zeros(seg_ref.shape[1:], BF16)

        def tail_copy(e):
            row = pl.multiple_of(tail_ref[e], ROW_ALIGN)
            return pltpu.make_async_copy(seg_ref.at[0], xs_ref.at[pl.ds(row, tb)], sem.at[e, 0])

        for e in range(N_EXPERTS):
            tail_copy(e).start()
        for e in range(N_EXPERTS):
            tail_copy(e).wait()

        def fill_block(k, carry):
            row = pl.multiple_of(fill_ref[k] * MOE_TILE, MOE_TILE)
            copy = pltpu.make_async_copy(seg_ref.at[0], xs_ref.at[pl.ds(row, tb)], sem.at[0, 0])
            copy.start()
            copy.wait()
            return carry

        lax.fori_loop(0, nfill_ref[0], fill_block, 0)


def _dispatch(tables, hn, pos_t, n_rows):
    t, d = hn.shape
    tb = MOE_TB
    return pl.pallas_call(
        _dispatch_kernel,
        grid_spec=pltpu.PrefetchScalarGridSpec(
            num_scalar_prefetch=len(tables),
            grid=(t // tb,),
            in_specs=[pl.BlockSpec((tb, d), lambda i, *_: (i, 0)),
                      pl.BlockSpec((1, N_EXPERTS, tb), lambda i, *_: (i, 0, 0))],
            out_specs=pl.BlockSpec(memory_space=pl.ANY),
            scratch_shapes=[pltpu.VMEM((N_EXPERTS, tb, d), BF16),
                            pltpu.SemaphoreType.DMA((N_EXPERTS, tb // MOE_SEG))]),
        out_shape=jax.ShapeDtypeStruct((n_rows, d), BF16),
        compiler_params=_params("arbitrary"),
        name="moe_dispatch",
    )(*tables, hn, pos_t)


def _gate_up_kernel(te_ref, tb_ref, ok_ref, x_ref, wg_ref, wu_ref, o_ref):
    t = pl.program_id(1)

    @pl.when(ok_ref[t] > 0)
    def _():
        x = x_ref[...]
        o_ref[...] = (_silu(_mm(x, wg_ref[0])) * _mm(x, wu_ref[0])).astype(o_ref.dtype)

    @pl.when(ok_ref[t] == 0)
    def _():
        o_ref[...] = jnp.zeros_like(o_ref)


def _gate_up(tile_e, tile_blk, valid, xs, wg, wu):
    d = xs.shape[1]
    ff = wg.shape[2]
    nt = tile_e.shape[0]
    tf = ff // MOE_F_SPLIT
    w_spec = pl.BlockSpec((1, d, tf), lambda f, t, te, tb, ok: (te[t], 0, f))
    return pl.pallas_call(
        _gate_up_kernel,
        grid_spec=pltpu.PrefetchScalarGridSpec(
            num_scalar_prefetch=3,
            grid=(MOE_F_SPLIT, nt),
            in_specs=[pl.BlockSpec((MOE_TILE, d), lambda f, t, te, tb, ok: (tb[t], 0)), w_spec, w_spec],
            out_specs=pl.BlockSpec((MOE_TILE, tf), lambda f, t, te, tb, ok: (t, f))),
        out_shape=jax.ShapeDtypeStruct((nt * MOE_TILE, ff), BF16),
        compiler_params=_params("arbitrary", "arbitrary"),
        name="moe_gate_up",
    )(tile_e, tile_blk, valid, xs, wg, wu)


def _down_kernel(te_ref, ok_ref, m_ref, wd_ref, o_ref):
    t = pl.program_id(0)

    @pl.when(ok_ref[t] > 0)
    def _():
        o_ref[...] = _mm(m_ref[...], wd_ref[0]).astype(o_ref.dtype)

    @pl.when(ok_ref[t] == 0)
    def _():
        o_ref[...] = jnp.zeros_like(o_ref)


def _down(tile_e, valid, mid, wd):
    ff, d = wd.shape[1:]
    nt = tile_e.shape[0]
    return pl.pallas_call(
        _down_kernel,
        grid_spec=pltpu.PrefetchScalarGridSpec(
            num_scalar_prefetch=2,
            grid=(nt,),
            in_specs=[pl.BlockSpec((MOE_TILE, ff), lambda t, te, ok: (t, 0)),
                      pl.BlockSpec((1, ff, d), lambda t, te, ok: (te[t], 0, 0))],
            out_specs=pl.BlockSpec((MOE_TILE, d), lambda t, te, ok: (t, 0))),
        out_shape=jax.ShapeDtypeStruct((nt * MOE_TILE, d), BF16),
        compiler_params=_params("arbitrary"),
        name="moe_down",
    )(tile_e, valid, mid, wd)


def _combine_kernel(cnt_ref, win_ref, shift_ref, x_ref, comb_ref, pos_ref, ys_ref, o_ref, win_buf, sem):
    blk = pl.program_id(0)
    tb = x_ref.shape[0]
    n_sub = tb // MOE_WIN
    pieces = [(e, j) for e in range(N_EXPERTS) for j in range(n_sub)]

    def needed(e, j):
        return cnt_ref[blk * N_EXPERTS + e] > j * MOE_WIN

    def win_copy(e, j):
        row = pl.multiple_of(win_ref[(blk * N_EXPERTS + e) * n_sub + j], ROW_ALIGN)
        return pltpu.make_async_copy(ys_ref.at[pl.ds(row, MOE_WIN)], win_buf.at[e, j], sem.at[e, j])

    for e, j in pieces:
        @pl.when(needed(e, j))
        def _(e=e, j=j):
            win_copy(e, j).start()

    o_ref[...] = x_ref[...]
    col = lax.broadcasted_iota(jnp.int32, (tb, MOE_WIN), 1).astype(F32)
    for e, j in pieces:
        @pl.when(needed(e, j))
        def _(e=e, j=j):
            win_copy(e, j).wait()
            pos = pos_ref[...][:, e:e + 1]
            in_sub = (pos >= float(j * MOE_WIN)) & (pos < float((j + 1) * MOE_WIN))
            shift = shift_ref[(blk * N_EXPERTS + e) * n_sub + j].astype(F32)
            target = jnp.where(in_sub, pos - float(j * MOE_WIN) + shift, -1.0)
            onehot = (col == target).astype(BF16)
            o_ref[...] += comb_ref[...][:, e:e + 1] * _mm(onehot, win_buf[e, j])


def _combine(tables, x, comb, pos, ys):
    t, d = x.shape
    tb = MOE_TB
    lanes = pl.BlockSpec((tb, LANES), lambda i, *_: (i, 0))
    return pl.pallas_call(
        _combine_kernel,
        grid_spec=pltpu.PrefetchScalarGridSpec(
            num_scalar_prefetch=len(tables),
            grid=(t // tb,),
            in_specs=[pl.BlockSpec((tb, d), lambda i, *_: (i, 0)), lanes, lanes,
                      pl.BlockSpec(memory_space=pl.ANY)],
            out_specs=pl.BlockSpec((tb, d), lambda i, *_: (i, 0)),
            scratch_shapes=[pltpu.VMEM((N_EXPERTS, tb // MOE_WIN, MOE_WIN, d), BF16),
                            pltpu.SemaphoreType.DMA((N_EXPERTS, tb // MOE_WIN))]),
        out_shape=jax.ShapeDtypeStruct((t, d), F32),
        compiler_params=_params("arbitrary"),
        name="moe_combine",
    )(*tables, x, comb, pos, ys)


def _moe(x, gain, w_router, wg, wu, wd):
    t, d = x.shape
    nb = t // MOE_TB
    n_tiles_max = (2 * t + nb * N_EXPERTS * (ROW_ALIGN - 1)) // MOE_TILE + N_EXPERTS
    hn, comb, pos, pos_t, cnt = _router(x, gain, w_router)
    write_tables, read_tables, (tile_e, tile_blk, valid) = _route_tables(cnt[:, 0, :N_EXPERTS], n_tiles_max)
    n_rows = (n_tiles_max + SPARE_TILES * N_EXPERTS) * MOE_TILE
    xs = _dispatch(write_tables, hn, pos_t, n_rows)
    mid = _gate_up(tile_e, tile_blk, valid, xs, wg, wu)
    ys = _down(tile_e, valid, mid, wd)
    return _combine(read_tables, x, comb, pos, ys)


def _final_norm_kernel(x_ref, g_ref, o_ref):
    o_ref[...] = _rms(x_ref[...], g_ref[...])


def _final_norm(x, gain):
    t, d = x.shape
    tm = TM_NORM
    return pl.pallas_call(
        _final_norm_kernel,
        grid=(t // tm,),
        in_specs=[pl.BlockSpec((tm, d), lambda i: (i, 0)), _full((1, d))],
        out_specs=pl.BlockSpec((tm, d), lambda i: (i, 0)),
        out_shape=jax.ShapeDtypeStruct((t, d), F32),
        compiler_params=_params("parallel"),
        name="final_norm",
    )(x, gain.reshape(1, d))


def _hm_groups_spec(n, first):
    def spec(b, h):
        return pl.BlockSpec((n, b, h, CHUNK, HEAD), lambda c: (first // n, 0, 0, c, 0))
    assert first % n == 0
    return spec


def _tok_chunk_spec(b):
    return pl.BlockSpec((b, CHUNK, GROUP), lambda c: (0, c, 0))


def _hgrn_kernel(qfi_ref, g_ref, loglb_ref, log1mlb_ref, onemlb_ref, ng_ref, o_ref, st_ref, rows_ref):
    @pl.when(pl.program_id(0) == 0)
    def _():
        st_ref[...] = jnp.zeros_like(st_ref)

    shape = st_ref.shape[:1] + (CHUNK, HEAD)
    q, fl, v = (qfi_ref[j].reshape(shape) for j in range(3))
    g = g_ref[0].reshape(shape)
    log_sig = jnp.minimum(fl, 0.0) - jnp.log(1.0 + jnp.exp(-jnp.abs(fl)))
    la = loglb_ref[...]
    lc = log1mlb_ref[...] + log_sig
    log_f = jnp.maximum(la, lc) + jnp.log(1.0 + jnp.exp(-jnp.abs(la - lc)))
    k = onemlb_ref[...] * jax.nn.sigmoid(-fl)
    b = _cumsum_rows(log_f) * LOG2_E

    rows_ref[0] = b
    rows_ref[1] = k
    rows_ref[2] = v
    sub = HGRN_SUB
    a_rows = [jnp.zeros((shape[0], sub, CHUNK), F32)]
    for i in range(1, CHUNK // sub):
        bref = rows_ref[0, :, i * sub - 1:i * sub, :]
        qd = q[:, i * sub:(i + 1) * sub] * jnp.exp2(b[:, i * sub:(i + 1) * sub] - bref)
        kd = k[:, :i * sub] * jnp.exp2(bref - b[:, :i * sub])
        later = jnp.zeros((shape[0], CHUNK - i * sub, HEAD), F32)
        a_rows.append(_bmm(qd, jnp.concatenate([kd, later], axis=1), nt=True))
    o = _bmm(jnp.concatenate(a_rows, axis=1), v)

    t_loc = lax.broadcasted_iota(jnp.int32, (shape[0], sub, HEAD), 1)
    diag = []
    for i in range(CHUNK // sub):
        sl = slice(i * sub, (i + 1) * sub)
        bb, qq = b[:, sl], q[:, sl]
        acc = jnp.zeros((shape[0], sub, HEAD), F32)
        for s in range(sub):
            row = i * sub + s
            e = jnp.exp2(jnp.where(t_loc >= s, bb - rows_ref[0, :, row:row + 1, :], -BIG_EXPONENT))
            w = jnp.sum(e * qq * rows_ref[1, :, row:row + 1, :], axis=-1, keepdims=True)
            acc = acc + w * rows_ref[2, :, row:row + 1, :]
        diag.append(acc)
    o = o + jnp.concatenate(diag, axis=1)

    st = st_ref[...]
    o = o + _bmm(q * jnp.exp2(b), st, nt=True)
    b_last = b[:, CHUNK - 1:CHUNK]
    st_ref[...] = st * jnp.exp2(b_last) + _bmm(jnp.swapaxes(v, 1, 2), k * jnp.exp2(b_last - b))

    _store_tokens(o_ref, _rms(o, ng_ref[...]) * _silu(g))


def _hgrn_mixer(hm, lb, norm_g):
    _, b, h, s, d = hm.shape
    g = b * h
    par = _full((g, 1, d))
    return pl.pallas_call(
        _hgrn_kernel,
        grid=(s // CHUNK,),
        in_specs=[_hm_groups_spec(3, HM_HGRN_QFI)(b, h), _hm_groups_spec(1, HM_HGRN_G)(b, h)] + [par] * 4,
        out_specs=_tok_chunk_spec(b),
        out_shape=jax.ShapeDtypeStruct((b, s, h * d), F32),
        scratch_shapes=[pltpu.VMEM((g, d, d), F32), pltpu.VMEM((3, g, CHUNK, d), F32)],
        compiler_params=_params("arbitrary"),
        name="hgrn2",
    )(hm, hm, jnp.log(lb), jnp.log1p(-lb), 1.0 - lb, norm_g)


def _gdn_kernel(qkv_ref, z_ref, gates_ref, cw_ref, alog_ref, dtb_ref, ng_ref, o_ref, xbuf_ref, st_ref):
    g3 = xbuf_ref.shape[0]
    g = g3 // 3

    @pl.when(pl.program_id(0) == 0)
    def _():
        st_ref[...] = jnp.zeros_like(st_ref)
        xbuf_ref[:, 0:HALO, :] = jnp.zeros((g3, HALO, HEAD), F32)

    xbuf_ref[:, HALO:HALO + CHUNK, :] = qkv_ref[...].reshape(g3, CHUNK, HEAD)
    conv = jnp.zeros((g3, CHUNK, HEAD), F32)
    for j in range(CONV_K):
        conv = conv + cw_ref[j] * xbuf_ref[:, HALO - (CONV_K - 1) + j:HALO - (CONV_K - 1) + j + CHUNK, :]
    xbuf_ref[:, 0:HALO, :] = xbuf_ref[:, CHUNK:CHUNK + HALO, :]
    qkv = _silu(conv)
    q, k, v = qkv[0:g], qkv[g:2 * g], qkv[2 * g:3 * g]
    q = q * lax.rsqrt(jnp.sum(q * q, axis=-1, keepdims=True) + EPS) * HEAD ** -0.5
    k = k * lax.rsqrt(jnp.sum(k * k, axis=-1, keepdims=True) + EPS)

    t_idx, s_idx = _tri_masks(CHUNK)
    causal = s_idx <= t_idx
    eye = s_idx == t_idx
    g_row = -jnp.exp(alog_ref[...]) * _softplus(gates_ref[0, 0] + dtb_ref[...])
    g_col = jnp.sum(jnp.where(eye, g_row, 0.0), axis=-1, keepdims=True)
    beta = jnp.sum(jnp.where(eye, jax.nn.sigmoid(gates_ref[1, 0]), 0.0), axis=-1, keepdims=True)
    gc_col = jnp.sum(jnp.where(causal, g_row, 0.0), axis=-1, keepdims=True)
    gc_row = jnp.sum(jnp.where(t_idx <= s_idx, g_col, 0.0), axis=1, keepdims=True)
    rel = jnp.where(causal, jnp.exp(jnp.minimum(gc_col - gc_row, 0.0)), 0.0)

    kb = k * beta
    lower = jnp.where(s_idx < t_idx, _bmm(kb, k, nt=True) * rel, 0.0)
    uw = _solve_unit_lower(lower, jnp.concatenate([v * beta, kb * jnp.exp(gc_col)], axis=-1))
    u, w = uw[:, :, :HEAD], uw[:, :, HEAD:]
    scores = _bmm(q, k, nt=True) * rel

    st = st_ref[...]
    v_new = u - _bmm(w, st)
    o = _bmm(q * jnp.exp(gc_col), st) + _bmm(scores, v_new)
    g_last = gc_col[:, CHUNK - 1:CHUNK]
    kd = k * jnp.exp(g_last - gc_col)
    st_ref[...] = st * jnp.exp(g_last) + _bmm(jnp.swapaxes(kd, 1, 2), v_new)

    _store_tokens(o_ref, _rms(o, ng_ref[...]) * _silu(z_ref[0].reshape(g, CHUNK, HEAD)))


def _gdn_mixer(hm, gates, conv_w, a_log, dt_bias, norm_g):
    _, b, h, s, d = hm.shape
    g = b * h
    return pl.pallas_call(
        _gdn_kernel,
        grid=(s // CHUNK,),
        in_specs=[_hm_groups_spec(3, HM_GDN_QKV)(b, h), _hm_groups_spec(1, HM_GDN_Z)(b, h),
                  pl.BlockSpec((2, 1, g, 1, CHUNK), lambda c: (0, c, 0, 0, 0)),
                  _full((CONV_K, 3 * g, 1, d)), _full((g, 1, 1)), _full((g, 1, 1)), _full((1, 1, d))],
        out_specs=_tok_chunk_spec(b),
        out_shape=jax.ShapeDtypeStruct((b, s, h * d), F32),
        scratch_shapes=[pltpu.VMEM((3 * g, HALO + CHUNK, d), F32), pltpu.VMEM((g, d, d), F32)],
        compiler_params=_params("arbitrary"),
        name="gated_deltanet",
    )(hm, hm, gates, conv_w, a_log, dt_bias, norm_g)


def _lru_kernel(x_ref, gate_ref, cw_ref, cb_ref, wa_ref, ba_ref, wx_ref, bx_ref, lam_ref, o_ref,
                xbuf_ref, h_ref):
    tt = x_ref.shape[1]

    @pl.when(pl.program_id(1) == 0)
    def _():
        h_ref[...] = jnp.zeros_like(h_ref)
        xbuf_ref[0:HALO, :] = jnp.zeros((HALO, GROUP), F32)

    xbuf_ref[HALO:HALO + tt, :] = x_ref[0]
    xc = jnp.zeros((tt, GROUP), F32) + cb_ref[...]
    for j in range(CONV_K):
        xc = xc + cw_ref[j:j + 1, :] * xbuf_ref[HALO - (CONV_K - 1) + j:HALO - (CONV_K - 1) + j + tt, :]
    xbuf_ref[0:HALO, :] = xbuf_ref[tt:tt + HALO, :]

    r = jax.nn.sigmoid(_mm16(xc, wa_ref[...]) + ba_ref[...])
    i = jax.nn.sigmoid(_mm16(xc, wx_ref[...]) + bx_ref[...])
    log_a = -LRU_C * r * _softplus(-lam_ref[...])
    a = jnp.exp(log_a)
    u = jnp.sqrt(1.0 - a * a) * (i * xc)

    row = lax.broadcasted_iota(jnp.int32, (tt, GROUP), 0)
    shift = 1
    while shift < tt:
        keep = row >= shift
        a_prev = pltpu.roll(a, shift, 0)
        u_prev = pltpu.roll(u, shift, 0)
        u = jnp.where(keep, a * u_prev + u, u)
        a = jnp.where(keep, a * a_prev, a)
        shift *= 2
    hs = u + a * h_ref[...]
    h_ref[...] = hs[tt - 1:tt, :]

    gate = gate_ref[0]
    gelu = 0.5 * gate * (1.0 + jnp.tanh(math.sqrt(2.0 / math.pi) * (gate + 0.044715 * gate * gate * gate)))
    o_ref[0] = hs * gelu


def _lru_mixer(tok, conv_w, conv_b, wa, ba, wx, bx, lam):
    b, s, _ = tok.shape
    c = GROUP
    tt = TT_SCAN
    x_spec = pl.BlockSpec((1, tt, c), lambda i, t: (i, t, TOK_LRU // GROUP))
    gate_spec = pl.BlockSpec((1, tt, c), lambda i, t: (i, t, TOK_LRU // GROUP + 1))
    vec = _full((1, c))
    return pl.pallas_call(
        _lru_kernel,
        grid=(b, s // tt),
        in_specs=[x_spec, gate_spec, _full((CONV_K, c)), vec, _full((c, c)), vec, _full((c, c)), vec, vec],
        out_specs=pl.BlockSpec((1, tt, c), lambda i, t: (i, t, 0)),
        out_shape=jax.ShapeDtypeStruct((b, s, c), F32),
        scratch_shapes=[pltpu.VMEM((HALO + tt, c), F32), pltpu.VMEM((1, c), F32)],
        compiler_params=_params("parallel", "arbitrary"),
        name="rg_lru",
    )(tok, tok, conv_w, conv_b.reshape(1, c), wa, ba.reshape(1, c), wx, bx.reshape(1, c), lam.reshape(1, c))


def _rwkv_prep_kernel(has_vmix, keep_v, *refs):
    refs = list(refs)
    p_ref, mu_ref, w0_ref, w2_ref, a0_ref, a2_ref, g2_ref = refs[:7]
    del refs[:7]
    if has_vmix:
        vf_ref, v0_ref, v1_ref, v2_ref = refs[:4]
        del refs[:4]
    out_ref = refs.pop(0)
    vtok_ref = refs.pop(0) if keep_v else None
    xbuf_ref = refs.pop(0)
    tt = p_ref.shape[1]

    @pl.when(pl.program_id(1) == 0)
    def _():
        xbuf_ref[0:HALO, :] = jnp.zeros((HALO, xbuf_ref.shape[1]), F32)

    c = p_ref[0]
    xbuf_ref[HALO:HALO + tt, :] = c
    prev = xbuf_ref[HALO - 1:HALO - 1 + tt, :]
    xbuf_ref[0:HALO, :] = xbuf_ref[tt:tt + HALO, :]
    c = c + mu_ref[...] * (prev - c)

    r = c[:, 0:GROUP]
    k = c[:, GROUP:2 * GROUP]
    v = c[:, 2 * GROUP:3 * GROUP]
    wa_lo = c[:, 3 * GROUP:3 * GROUP + LANES]
    g_lo = c[:, 3 * GROUP + LANES:4 * GROUP]
    log_w = -math.exp(-0.5) * jax.nn.sigmoid(w0_ref[...] + _mm16(jnp.tanh(wa_lo), w2_ref[...]))
    a = jax.nn.sigmoid(a0_ref[...] + _mm16(wa_lo, a2_ref[...]))
    g = _mm16(jax.nn.sigmoid(g_lo), g2_ref[...])
    if has_vmix:
        mix = jax.nn.sigmoid(v0_ref[...] + _mm16(_mm16(v, v1_ref[...]), v2_ref[...]))
        v = v + (vf_ref[0] - v) * mix
    if keep_v:
        vtok_ref[0] = v
    for j, val in enumerate((r, log_w, k, v, a, g)):
        _store_heads(out_ref, val, (j, 0))


def _rwkv_prep(tok, mu, w0, w2, a0, a2, g2, v_first, v_mix):
    b, s, _ = tok.shape
    n = 4 * GROUP
    tt = TT_SCAN
    lora = w2.shape[0]
    w2p = jnp.zeros((LANES, GROUP), F32).at[:lora].set(w2)
    a2p = jnp.zeros((LANES, GROUP), F32).at[lora:].set(a2)
    tile = pl.BlockSpec((1, tt, GROUP), lambda i, t: (i, t, 0))
    vec = _full((1, GROUP))
    args = [tok, mu.reshape(1, n), w0.reshape(1, GROUP), w2p, a0.reshape(1, GROUP), a2p, g2]
    specs = [pl.BlockSpec((1, tt, n), lambda i, t: (i, t, TOK_RWKV // n)), _full((1, n)), vec,
             _full((LANES, GROUP)), vec, _full((LANES, GROUP)), _full(g2.shape)]
    keep_v = v_mix is None
    if v_mix is not None:
        v0, v1, v2 = v_mix
        rank = v1.shape[1]
        v1p = jnp.zeros((GROUP, LANES), F32).at[:, :rank].set(v1)
        v2p = jnp.zeros((LANES, GROUP), F32).at[:rank].set(v2)
        args += [v_first, v0.reshape(1, GROUP), v1p, v2p]
        specs += [tile, vec, _full((GROUP, LANES)), _full((LANES, GROUP))]
    out_specs = [pl.BlockSpec((6, 1, HEADS, tt, HEAD), lambda i, t: (0, i, 0, t, 0))]
    out_shape = [jax.ShapeDtypeStruct((6, b, HEADS, s, HEAD), F32)]
    if keep_v:
        out_specs.append(tile)
        out_shape.append(jax.ShapeDtypeStruct((b, s, GROUP), F32))
    outs = pl.pallas_call(
        functools.partial(_rwkv_prep_kernel, v_mix is not None, keep_v),
        grid=(b, s // tt),
        in_specs=specs,
        out_specs=out_specs,
        out_shape=out_shape,
        scratch_shapes=[pltpu.VMEM((HALO + tt, n), F32)],
        compiler_params=_params("parallel", "arbitrary"),
        name="rwkv7_prep",
    )(*args)
    return (outs[0], outs[1]) if keep_v else (outs[0], v_first)


def _rwkv_kernel(in_ref, kk_ref, ka_ref, rk_ref, lnw_ref, lnb_ref, o_ref, zt_ref):
    @pl.when(pl.program_id(0) == 0)
    def _():
        zt_ref[...] = jnp.zeros_like(zt_ref)

    shape = zt_ref.shape[:1] + (CHUNK, HEAD)
    r, lw, k, v, a, g = (in_ref[j].reshape(shape) for j in range(6))
    kk = k * kk_ref[...]
    kk = kk * lax.rsqrt(jnp.sum(kk * kk, axis=-1, keepdims=True) + EPS)
    k = k * (1.0 + (a - 1.0) * ka_ref[...])
    b = _cumsum_rows(lw)
    e_pos = jnp.exp(b)
    e_neg = jnp.exp(-b)
    r_t = r * e_pos
    al_t = -kk * jnp.exp(b - lw)
    be_t = kk * a * e_neg
    k_t = k * e_neg

    t_idx, s_idx = _tri_masks(CHUNK)
    strict = s_idx < t_idx
    incl = s_idx <= t_idx
    ab_k = jnp.concatenate([be_t, k_t], axis=1)
    al_all = _bmm(al_t, ab_k, nt=True)
    r_all = _bmm(r_t, ab_k, nt=True)
    l_ab = jnp.where(strict, al_all[:, :, :CHUNK], 0.0)
    l_ak = jnp.where(strict, al_all[:, :, CHUNK:], 0.0)
    m_rb = jnp.where(incl, r_all[:, :, :CHUNK], 0.0)
    m_rk = jnp.where(incl, r_all[:, :, CHUNK:], 0.0)

    sol = _solve_unit_lower(-l_ab, jnp.concatenate([al_t, _bmm(l_ak, v)], axis=-1))
    w1, u0 = sol[:, :, :HEAD], sol[:, :, HEAD:]

    zt = zt_ref[...]
    u = _bmm(w1, zt, nt=True) + u0
    y = _bmm(r_t, zt, nt=True) + _bmm(m_rb, u) + _bmm(m_rk, v)
    b_last = b[:, CHUNK - 1:CHUNK]
    e_last = jnp.exp(b_last - b)
    uv_t = jnp.swapaxes(jnp.concatenate([u, v], axis=1), 1, 2)
    zt_ref[...] = zt * jnp.exp(b_last) + _bmm(uv_t, jnp.concatenate([kk * a * e_last, k * e_last], axis=1))

    mean = jnp.mean(y, axis=-1, keepdims=True)
    var = jnp.mean(jnp.square(y - mean), axis=-1, keepdims=True)
    yn = (y - mean) * lax.rsqrt(var + RWKV_GN_EPS) * lnw_ref[...] + lnb_ref[...]
    bonus = jnp.sum(r * k * rk_ref[...], axis=-1, keepdims=True) * v
    _store_tokens(o_ref, (yn + bonus) * g)


def _rwkv_mixer(hm, k_k, k_a, r_k, ln_w, ln_b):
    _, b, h, s, d = hm.shape
    g = b * h
    par = _full((g, 1, d))
    return pl.pallas_call(
        _rwkv_kernel,
        grid=(s // CHUNK,),
        in_specs=[_hm_groups_spec(6, 0)(b, h)] + [par] * 5,
        out_specs=_tok_chunk_spec(b),
        out_shape=jax.ShapeDtypeStruct((b, s, h * d), F32),
        scratch_shapes=[pltpu.VMEM((g, d, d), F32)],
        compiler_params=_params("arbitrary"),
        name="rwkv7",
    )(hm, k_k, k_a, r_k, ln_w, ln_b)


def _par_hm(p, b, width=HEAD):
    return jnp.tile(p.reshape(-1, 1, width), (b, 1, 1))


def _gdn_gate_layout(gates):
    b, s, h2 = gates.shape
    g = gates.reshape(b, s // CHUNK, CHUNK, 2, h2 // 2).transpose(3, 1, 0, 4, 2)
    return g.reshape(2, s // CHUNK, b * (h2 // 2), 1, CHUNK)


def _gdn_conv_layout(w, b):
    k, c = w.shape
    h = c // (3 * HEAD)
    w = jnp.broadcast_to(w.reshape(k, 3, 1, h, 1, HEAD), (k, 3, b, h, 1, HEAD))
    return w.reshape(k, 3 * b * h, 1, HEAD)


def _block_diag(w):
    n, a, b = w.shape
    out = jnp.zeros((n * a, n * b), w.dtype)
    for i in range(n):
        out = out.at[i * a:(i + 1) * a, i * b:(i + 1) * b].set(w[i])
    return out


def _in_proj_weight(w):
    d = w.shape[0]
    g4 = 4 * GROUP
    gates = g4 + g4
    lru = gates + 2 * HEADS
    rwkv = lru + 2 * GROUP
    cols = [w[:, rwkv:rwkv + g4], w[:, lru:rwkv], w[:, gates:lru], jnp.zeros((d, LANES - 2 * HEADS), w.dtype),
            w[:, 0:3 * GROUP], w[:, g4:g4 + 3 * GROUP], w[:, 3 * GROUP:g4], w[:, g4 + 3 * GROUP:gates]]
    return jnp.concatenate(cols, axis=1).astype(BF16)


def kernel(x, mem, norm_mix, w_in, w_out, hgrn_lb_logits, hgrn_norm, gdn_conv_w, gdn_A_log, gdn_dt_bias, gdn_norm, lru_conv_w, lru_conv_b, lru_wa, lru_ba, lru_wx, lru_bx, lru_lambda, rwkv_mu, rwkv_w0, rwkv_w2, rwkv_a0, rwkv_a2, rwkv_g2, rwkv_k_k, rwkv_k_a, rwkv_r_k, rwkv_ln_w, rwkv_ln_b, rwkv_v0, rwkv_v1, rwkv_v2, mem_norm, norm_xattn, xattn_wq, xattn_wk, xattn_wv, xattn_wo, norm_ffn, ffn_w_gate, ffn_w_up, ffn_w_down, moe_router, moe_w_gate, moe_w_up, moe_w_down, norm_final):
    bsz, seq, d = x.shape
    depth = w_in.shape[0]
    tok = bsz * seq
    mlen = mem.shape[1]

    lb = jnp.cumsum(jax.nn.softmax(hgrn_lb_logits.astype(F32), axis=0), axis=0)
    lb = lb - lb[0]
    mem_f = mem.reshape(bsz * mlen, d)
    v_first = None
    for l in range(depth):
        p_tok, p_hm = _in_proj(x, norm_mix[l], _in_proj_weight(w_in[l]))

        y_a = _hgrn_mixer(p_hm, _par_hm(lb[l], bsz), _par_hm(hgrn_norm[l], bsz))

        gates = p_tok[..., TOK_GATES:TOK_GATES + 2 * HEADS]
        y_b = _gdn_mixer(p_hm, _gdn_gate_layout(gates),
                         _gdn_conv_layout(gdn_conv_w[l], bsz), _par_hm(gdn_A_log[l], bsz, 1),
                         _par_hm(gdn_dt_bias[l], bsz, 1), gdn_norm[l].reshape(1, 1, HEAD))

        y_c = _lru_mixer(p_tok, lru_conv_w[l], lru_conv_b[l], _block_diag(lru_wa[l]), lru_ba[l],
                         _block_diag(lru_wx[l]), lru_bx[l], lru_lambda[l])

        v_mix = None if l == 0 else (rwkv_v0[l - 1], rwkv_v1[l - 1], rwkv_v2[l - 1])
        d_hm, v_first = _rwkv_prep(p_tok, rwkv_mu[l], rwkv_w0[l], rwkv_w2[l], rwkv_a0[l], rwkv_a2[l],
                                   rwkv_g2[l], v_first, v_mix)
        y_d = _rwkv_mixer(d_hm, _par_hm(rwkv_k_k[l], bsz), _par_hm(rwkv_k_a[l], bsz), _par_hm(rwkv_r_k[l], bsz),
                          _par_hm(rwkv_ln_w[l], bsz), _par_hm(rwkv_ln_b[l], bsz))

        x = _out_proj((y_a, y_b, y_c, y_d), w_out[l].reshape(4, GROUP, d).astype(BF16), x)

        w_kv = jnp.concatenate([xattn_wk[l], xattn_wv[l]], axis=1).astype(BF16)
        kv = _norm_matmul(mem_f, mem_norm, w_kv, mlen, BF16).reshape(bsz, mlen, 2 * d)
        x = _cross_attention(x, norm_xattn[l], xattn_wq[l].astype(BF16), kv, xattn_wo[l].astype(BF16))

        xf = x.reshape(tok, d)
        j = l // 2
        if l % 2 == 0:
            ones = jnp.ones((1, tok, 1), F32)
            xf = _ffn(xf, norm_ffn[l], ffn_w_gate[j][None].astype(BF16), ffn_w_up[j][None].astype(BF16),
                      ffn_w_down[j][None].astype(BF16), ones, TF_FFN)
        else:
            xf = _moe(xf, norm_ffn[l], moe_router[j], moe_w_gate[j].astype(BF16), moe_w_up[j].astype(BF16),
                      moe_w_down[j].astype(BF16))
        x = xf.reshape(bsz, seq, d)
    return _final_norm(x.reshape(tok, d), norm_final).reshape(bsz, seq, d)
```

```python
import functools
import math

import jax
import jax.numpy as jnp
from jax import lax
from jax.experimental import pallas as pl
from jax.experimental.pallas import tpu as pltpu

F32 = jnp.float32
BF16 = jnp.bfloat16
HIGHEST = lax.Precision.HIGHEST

GROUP = 256
HEAD = 64
HEADS = GROUP // HEAD
CHUNK = 64
SUB = 16
HGRN_SUB = 8
LOG2_E = 1.4426950408889634
BIG_EXPONENT = 1e30
CONV_K = 4
HALO = 8
LRU_C = 8.0
RWKV_GN_EPS = 64e-5
XATTN_HEADS = 4
N_EXPERTS = 8
EPS = 1e-6
LANES = 128
VMEM_LIMIT_BYTES = 56 * 1024 * 1024

TOK_RWKV = 0
TOK_LRU = 4 * GROUP
TOK_GATES = 6 * GROUP
N_TOK = 6 * GROUP + LANES
HM_HGRN_QFI, HM_GDN_QKV, HM_HGRN_G, HM_GDN_Z = 0, 3, 6, 7
HM_GROUPS = 8

TM_IN_PROJ = 512
TM_XATTN = 512
TM_FFN = 1024
TF_FFN = 1408
TM_NORM = 1024
TT_SCAN = 512
MOE_TB = 512
MOE_TILE = 512
MOE_F_SPLIT = 2
ROW_ALIGN = 16
SPARE_TILES = 2
MOE_SEG = 128
MOE_WIN = 256


def _params(*semantics):
    return pltpu.CompilerParams(dimension_semantics=semantics, vmem_limit_bytes=VMEM_LIMIT_BYTES)


def _full(shape):
    n = len(shape)
    return pl.BlockSpec(shape, lambda *_: (0,) * n)


def _bmm(a, b, nt=False, exact=False):
    dn = (((2,), (2 if nt else 1,)), ((0,), (0,)))
    if exact:
        return lax.dot_general(a, b, dn, precision=HIGHEST, preferred_element_type=F32)
    return lax.dot_general(a.astype(BF16), b.astype(BF16), dn, preferred_element_type=F32)


def _mm(a, b, precision=None):
    return jnp.dot(a, b, precision=precision, preferred_element_type=F32)


def _mm16(a, b):
    return jnp.dot(a.astype(BF16), b.astype(BF16), preferred_element_type=F32)


def _softplus(x):
    return jnp.maximum(x, 0.0) + jnp.log1p(jnp.exp(-jnp.abs(x)))


def _silu(x):
    return x * jax.nn.sigmoid(x)


def _rms(x, gain):
    return x * lax.rsqrt(jnp.mean(x * x, axis=-1, keepdims=True) + EPS) * gain


def _tri_masks(n):
    t = lax.broadcasted_iota(jnp.int32, (n, n), 0)
    s = lax.broadcasted_iota(jnp.int32, (n, n), 1)
    return t, s


def _cumsum_rows(x):
    g, c, _ = x.shape
    t, s = _tri_masks(c)
    tri = jnp.broadcast_to((s <= t).astype(BF16), (g, c, c))
    hi = x.astype(BF16)
    rest = x - hi.astype(F32)
    mid = rest.astype(BF16)
    lo = (rest - mid.astype(F32)).astype(BF16)
    return _bmm(tri, hi) + _bmm(tri, mid) + _bmm(tri, lo)


def _solve_unit_lower(low, rhs):
    c = low.shape[1]
    r = rhs.shape[2]
    assert c == 4 * SUB and SUB == 16
    t, s = _tri_masks(c)
    same_block = (t // SUB) == (s // SUB)
    d1 = jnp.where(same_block, low, 0.0)
    z = jnp.concatenate([rhs, jnp.where(same_block, 0.0, low)], axis=-1)
    d2 = _bmm(d1, d1)
    d4 = _bmm(d2, d2)
    d8 = _bmm(d4, d4)
    z = z + _bmm(d8, z)
    z = z + _bmm(d4, z)
    z = z + _bmm(d2, z)
    z = z - _bmm(d1, z)
    y = z[:, :, :r]
    n1 = z[:, :, r:]
    n2 = _bmm(n1, n1)
    y = y + _bmm(n2, y)
    return y - _bmm(n1, y)


def _store_heads(ref, val, lead=()):
    for h in range(val.shape[1] // HEAD):
        ref[lead + (h,)] = val[:, h * HEAD:(h + 1) * HEAD]


def _store_tokens(ref, o):
    nb = ref.shape[0]
    nh = o.shape[0] // nb
    for b in range(nb):
        ref[b] = jnp.concatenate([o[b * nh + h] for h in range(nh)], axis=-1)


def _in_proj_kernel(x_ref, g_ref, w_ref, tok_ref, hm_ref):
    hn = _rms(x_ref[0], g_ref[...]).astype(BF16)
    acc = _mm(hn, w_ref[...])
    n_tok = tok_ref.shape[2]
    tok_ref[0] = acc[:, :n_tok]
    for g in range(hm_ref.shape[0]):
        _store_heads(hm_ref, acc[:, n_tok + g * GROUP:n_tok + (g + 1) * GROUP], (g, 0))


def _in_proj(x, gain, w):
    b, s, d = x.shape
    n = w.shape[1]
    tm = TM_IN_PROJ
    return pl.pallas_call(
        _in_proj_kernel,
        grid=(b, s // tm),
        in_specs=[pl.BlockSpec((1, tm, d), lambda i, j: (i, j, 0)), _full((1, d)), _full((d, n))],
        out_specs=[pl.BlockSpec((1, tm, N_TOK), lambda i, j: (i, j, 0)),
                   pl.BlockSpec((HM_GROUPS, 1, HEADS, tm, HEAD), lambda i, j: (0, i, 0, j, 0))],
        out_shape=[jax.ShapeDtypeStruct((b, s, N_TOK), F32),
                   jax.ShapeDtypeStruct((HM_GROUPS, b, HEADS, s, HEAD), F32)],
        compiler_params=_params("parallel", "parallel"),
        name="in_proj",
    )(x, gain.reshape(1, d), w)


def _norm_mm_kernel(x_ref, g_ref, w_ref, o_ref):
    o_ref[...] = _mm(_rms(x_ref[...], g_ref[...]).astype(BF16), w_ref[...]).astype(o_ref.dtype)


def _norm_matmul(x, gain, w, tm, out_dtype):
    t, k = x.shape
    n = w.shape[1]
    return pl.pallas_call(
        _norm_mm_kernel,
        grid=(t // tm,),
        in_specs=[pl.BlockSpec((tm, k), lambda i: (i, 0)), _full((1, k)), _full((k, n))],
        out_specs=pl.BlockSpec((tm, n), lambda i: (i, 0)),
        out_shape=jax.ShapeDtypeStruct((t, n), out_dtype),
        compiler_params=_params("parallel"),
        name="norm_matmul",
    )(x, gain.reshape(1, k), w)


def _mix_attn_kernel(ya_ref, yb_ref, yc_ref, yd_ref, wout_ref, x_ref, g_ref, wq_ref, k_ref, v_ref, wo_ref, o_ref):
    x = x_ref[0]
    for m, y_ref in enumerate((ya_ref, yb_ref, yc_ref, yd_ref)):
        x = x + _mm(y_ref[0].astype(BF16), wout_ref[m])
    hn = _rms(x, g_ref[...]).astype(BF16)
    q = _mm(hn, wq_ref[...])
    d = q.shape[1] // XATTN_HEADS
    outs = []
    for h in range(XATTN_HEADS):
        qh = (q[:, h * d:(h + 1) * d] * d ** -0.5).astype(BF16)
        kh = k_ref[0, :, h * d:(h + 1) * d]
        vh = v_ref[0, :, h * d:(h + 1) * d]
        s = lax.dot_general(qh, kh, (((1,), (1,)), ((), ())), preferred_element_type=F32)
        s = s - jnp.max(s, axis=-1, keepdims=True)
        p = jnp.exp(s)
        p = p / jnp.sum(p, axis=-1, keepdims=True)
        outs.append(_mm(p.astype(BF16), vh))
    o = jnp.concatenate(outs, axis=-1).astype(BF16)
    o_ref[0] = x + _mm(o, wo_ref[...])


def _mix_attention(ys, w_out, x, gain, wq, kv, wo):
    b, s, d = x.shape
    m = kv.shape[1]
    tm = TM_XATTN
    y_spec = pl.BlockSpec((1, tm, GROUP), lambda i, j: (i, j, 0))
    x_spec = pl.BlockSpec((1, tm, d), lambda i, j: (i, j, 0))
    return pl.pallas_call(
        _mix_attn_kernel,
        grid=(b, s // tm),
        in_specs=[y_spec] * 4 + [_full(w_out.shape), x_spec, _full((1, d)), _full((d, d)),
                                 pl.BlockSpec((1, m, d), lambda i, j: (i, 0, 0)),
                                 pl.BlockSpec((1, m, d), lambda i, j: (i, 0, 1)),
                                 _full((d, d))],
        out_specs=x_spec,
        out_shape=jax.ShapeDtypeStruct((b, s, d), F32),
        compiler_params=_params("parallel", "parallel"),
        name="mix_attention",
    )(*ys, w_out, x, gain.reshape(1, d), wq, kv, kv, wo)


def _ffn_kernel(x_ref, g_ref, wg_ref, wu_ref, wd_ref, c_ref, o_ref, hn_ref, acc_ref):
    f = pl.program_id(1)

    @pl.when(f == 0)
    def _():
        hn_ref[...] = _rms(x_ref[...], g_ref[...]).astype(BF16)
        acc_ref[...] = x_ref[...]

    hn = hn_ref[...]
    gate = _mm(hn, wg_ref[0])
    up = _mm(hn, wu_ref[0])
    mid = _silu(gate) * up * c_ref[0]
    acc_ref[...] += _mm(mid.astype(BF16), wd_ref[0])

    @pl.when(f == pl.num_programs(1) - 1)
    def _():
        o_ref[...] = acc_ref[...]


def _ffn(x, gain, wg, wu, wd, combine, tf):
    t, d = x.shape
    e, _, ff = wg.shape
    nf = ff // tf
    tm = TM_FFN
    return pl.pallas_call(
        _ffn_kernel,
        grid=(t // tm, e * nf),
        in_specs=[pl.BlockSpec((tm, d), lambda i, f: (i, 0)),
                  _full((1, d)),
                  pl.BlockSpec((1, d, tf), lambda i, f: (f // nf, 0, f % nf)),
                  pl.BlockSpec((1, d, tf), lambda i, f: (f // nf, 0, f % nf)),
                  pl.BlockSpec((1, tf, d), lambda i, f: (f // nf, f % nf, 0)),
                  pl.BlockSpec((1, tm, 1), lambda i, f: (f // nf, i, 0))],
        out_specs=pl.BlockSpec((tm, d), lambda i, f: (i, 0)),
        out_shape=jax.ShapeDtypeStruct((t, d), F32),
        scratch_shapes=[pltpu.VMEM((tm, d), BF16), pltpu.VMEM((tm, d), F32)],
        compiler_params=_params("parallel", "arbitrary"),
        name="ffn",
    )(x, gain.reshape(1, d), wg, wu, wd, combine)


def _router_kernel(x_ref, g_ref, w_ref, hn_ref, comb_ref, pos_ref, post_ref, cnt_ref):
    hn = _rms(x_ref[...], g_ref[...])
    hn_ref[...] = hn.astype(BF16)
    logits = _mm(hn, w_ref[...], precision=HIGHEST)
    lane = lax.broadcasted_iota(jnp.int32, logits.shape, 1)
    neg = jnp.float32(-jnp.inf)
    logits = jnp.where(lane < N_EXPERTS, logits, neg)
    m1 = jnp.max(logits, axis=-1, keepdims=True)
    i1 = jnp.min(jnp.where(logits == m1, lane, LANES), axis=-1, keepdims=True)
    rest = jnp.where(lane == i1, neg, logits)
    m2 = jnp.max(rest, axis=-1, keepdims=True)
    i2 = jnp.min(jnp.where(rest == m2, lane, LANES), axis=-1, keepdims=True)
    e2 = jnp.exp(m2 - m1)
    comb = jnp.where(lane == i1, 1.0 / (1.0 + e2), 0.0) + jnp.where(lane == i2, e2 / (1.0 + e2), 0.0)
    comb_ref[...] = comb

    sel = (comb > 0.0).astype(F32)
    tb = sel.shape[0]
    row = lax.broadcasted_iota(jnp.int32, sel.shape, 0)
    run = sel
    shift = 1
    while shift < tb:
        run = run + jnp.where(row >= shift, pltpu.roll(run, shift, 0), 0.0)
        shift *= 2
    pos = jnp.where(sel > 0.0, run - 1.0, -1.0)
    pos_ref[...] = pos
    post_ref[0] = jnp.transpose(pos)[:N_EXPERTS, :]
    cnt_ref[0] = run[tb - 1:tb, :].astype(jnp.int32)


def _router(x, gain, w_router):
    t, d = x.shape
    tb = MOE_TB
    nb = t // tb
    w = jnp.zeros((d, LANES), F32).at[:, :N_EXPERTS].set(w_router)
    tile = pl.BlockSpec((tb, LANES), lambda i: (i, 0))
    return pl.pallas_call(
        _router_kernel,
        grid=(nb,),
        in_specs=[pl.BlockSpec((tb, d), lambda i: (i, 0)), _full((1, d)), _full((d, LANES))],
        out_specs=[pl.BlockSpec((tb, d), lambda i: (i, 0)), tile, tile,
                   pl.BlockSpec((1, N_EXPERTS, tb), lambda i: (i, 0, 0)),
                   pl.BlockSpec((1, 1, LANES), lambda i: (i, 0, 0))],
        out_shape=[jax.ShapeDtypeStruct((t, d), BF16), jax.ShapeDtypeStruct((t, LANES), F32),
                   jax.ShapeDtypeStruct((t, LANES), F32), jax.ShapeDtypeStruct((nb, N_EXPERTS, tb), F32),
                   jax.ShapeDtypeStruct((nb, 1, LANES), jnp.int32)],
        compiler_params=_params("parallel"),
        name="router",
    )(x, gain.reshape(1, d), w)


def _route_tables(cnt, n_tiles_max):
    nb, ne = cnt.shape
    cpad = (cnt + (ROW_ALIGN - 1)) // ROW_ALIGN * ROW_ALIGN
    off = jnp.cumsum(cpad, axis=0) - cpad
    total = jnp.sum(cpad, axis=0)
    ntile = (total + MOE_TILE - 1) // MOE_TILE
    first = jnp.cumsum(ntile) - ntile
    n_tiles = jnp.sum(ntile)
    region = (first + SPARE_TILES * jnp.arange(ne, dtype=jnp.int32)) * MOE_TILE
    dst = region[None, :] + off
    tail = region + total
    sub = jnp.arange(MOE_TB // MOE_WIN, dtype=jnp.int32)[None, None, :] * MOE_WIN
    want = off[:, :, None] + sub
    win_rel = jnp.clip(want, 0, jnp.maximum(ntile * MOE_TILE - MOE_WIN, 0)[None, :, None])
    win = first[None, :, None] * MOE_TILE + win_rel
    shift = want - win_rel
    t = jnp.minimum(jnp.arange(n_tiles_max, dtype=jnp.int32), n_tiles - 1)
    tile_e = jnp.minimum(jnp.sum(t[:, None] >= jnp.cumsum(ntile)[None, :], axis=1), ne - 1).astype(jnp.int32)
    tile_blk = t + SPARE_TILES * tile_e
    valid = (jnp.arange(n_tiles_max, dtype=jnp.int32) < n_tiles).astype(jnp.int32)
    n_blocks = n_tiles_max + SPARE_TILES * ne
    blocks = jnp.arange(n_blocks, dtype=jnp.int32)
    unused = 1 - jnp.max((tile_blk[:, None] == blocks[None, :]) * valid[:, None], axis=0)
    rank = jnp.cumsum(unused) - 1
    fill = jnp.sum(jnp.where((rank[None, :] == blocks[:, None]) & (unused[None, :] > 0), blocks[None, :], 0), axis=1)
    n_fill = jnp.sum(unused).astype(jnp.int32).reshape(1)
    flat = lambda a: a.reshape(-1).astype(jnp.int32)
    return ((flat(cpad), flat(dst), flat(tail), flat(fill), n_fill), (flat(cnt), flat(win), flat(shift)),
            (tile_e, tile_blk.astype(jnp.int32), valid))


def _dispatch_kernel(cpad_ref, dst_ref, tail_ref, fill_ref, nfill_ref, hn_ref, post_ref, xs_ref, seg_ref, sem):
    blk = pl.program_id(0)
    tb = hn_ref.shape[0]
    assert tb == MOE_TILE
    pieces = [(e, j) for e in range(N_EXPERTS) for j in range(tb // MOE_SEG)]

    def needed(e, j):
        return cpad_ref[blk * N_EXPERTS + e] > j * MOE_SEG

    def piece_copy(e, j):
        row = pl.multiple_of(dst_ref[blk * N_EXPERTS + e] + j * MOE_SEG, ROW_ALIGN)
        return pltpu.make_async_copy(seg_ref.at[e, pl.ds(j * MOE_SEG, MOE_SEG)], xs_ref.at[pl.ds(row, MOE_SEG)],
                                     sem.at[e, j])

    rank = lax.broadcasted_iota(jnp.int32, (MOE_SEG, tb), 0).astype(F32)
    for e, j in pieces:
        @pl.when(needed(e, j))
        def _(e=e, j=j):
            onehot = (rank + float(j * MOE_SEG) == post_ref[0, e:e + 1, :]).astype(BF16)
            seg_ref[e, j * MOE_SEG:(j + 1) * MOE_SEG, :] = _mm(onehot, hn_ref[...]).astype(BF16)
            piece_copy(e, j).start()
    for e, j in pieces:
        @pl.when(needed(e, j))
        def _(e=e, j=j):
            piece_copy(e, j).wait()

    @pl.when(blk == pl.num_programs(0) - 1)
    def _():
        seg_ref[0] = jnp.zeros(seg_ref.shape[1:], BF16)

        def tail_copy(e):
            row = pl.multiple_of(tail_ref[e], ROW_ALIGN)
            return pltpu.make_async_copy(seg_ref.at[0], xs_ref.at[pl.ds(row, tb)], sem.at[e, 0])

        for e in range(N_EXPERTS):
            tail_copy(e).start()
        for e in range(N_EXPERTS):
            tail_copy(e).wait()

        def fill_block(k, carry):
            row = pl.multiple_of(fill_ref[k] * MOE_TILE, MOE_TILE)
            copy = pltpu.make_async_copy(seg_ref.at[0], xs_ref.at[pl.ds(row, tb)], sem.at[0, 0])
            copy.start()
            copy.wait()
            return carry

        lax.fori_loop(0, nfill_ref[0], fill_block, 0)


def _dispatch(tables, hn, pos_t, n_rows):
    t, d = hn.shape
    tb = MOE_TB
    return pl.pallas_call(
        _dispatch_kernel,
        grid_spec=pltpu.PrefetchScalarGridSpec(
            num_scalar_prefetch=len(tables),
            grid=(t // tb,),
            in_specs=[pl.BlockSpec((tb, d), lambda i, *_: (i, 0)),
                      pl.BlockSpec((1, N_EXPERTS, tb), lambda i, *_: (i, 0, 0))],
            out_specs=pl.BlockSpec(memory_space=pl.ANY),
            scratch_shapes=[pltpu.VMEM((N_EXPERTS, tb, d), BF16),
                            pltpu.SemaphoreType.DMA((N_EXPERTS, tb // MOE_SEG))]),
        out_shape=jax.ShapeDtypeStruct((n_rows, d), BF16),
        compiler_params=_params("arbitrary"),
        name="moe_dispatch",
    )(*tables, hn, pos_t)


def _gate_up_kernel(te_ref, tb_ref, ok_ref, x_ref, wg_ref, wu_ref, o_ref):
    t = pl.program_id(1)

    @pl.when(ok_ref[t] > 0)
    def _():
        x = x_ref[...]
        o_ref[...] = (_silu(_mm(x, wg_ref[0])) * _mm(x, wu_ref[0])).astype(o_ref.dtype)

    @pl.when(ok_ref[t] == 0)
    def _():
        o_ref[...] = jnp.zeros_like(o_ref)


def _gate_up(tile_e, tile_blk, valid, xs, wg, wu):
    d = xs.shape[1]
    ff = wg.shape[2]
    nt = tile_e.shape[0]
    tf = ff // MOE_F_SPLIT
    w_spec = pl.BlockSpec((1, d, tf), lambda f, t, te, tb, ok: (te[t], 0, f))
    return pl.pallas_call(
        _gate_up_kernel,
        grid_spec=pltpu.PrefetchScalarGridSpec(
            num_scalar_prefetch=3,
            grid=(MOE_F_SPLIT, nt),
            in_specs=[pl.BlockSpec((MOE_TILE, d), lambda f, t, te, tb, ok: (tb[t], 0)), w_spec, w_spec],
            out_specs=pl.BlockSpec((MOE_TILE, tf), lambda f, t, te, tb, ok: (t, f))),
        out_shape=jax.ShapeDtypeStruct((nt * MOE_TILE, ff), BF16),
        compiler_params=_params("arbitrary", "arbitrary"),
        name="moe_gate_up",
    )(tile_e, tile_blk, valid, xs, wg, wu)


def _down_kernel(te_ref, ok_ref, m_ref, wd_ref, o_ref):
    t = pl.program_id(0)

    @pl.when(ok_ref[t] > 0)
    def _():
        o_ref[...] = _mm(m_ref[...], wd_ref[0]).astype(o_ref.dtype)

    @pl.when(ok_ref[t] == 0)
    def _():
        o_ref[...] = jnp.zeros_like(o_ref)


def _down(tile_e, valid, mid, wd):
    ff, d = wd.shape[1:]
    nt = tile_e.shape[0]
    return pl.pallas_call(
        _down_kernel,
        grid_spec=pltpu.PrefetchScalarGridSpec(
            num_scalar_prefetch=2,
            grid=(nt,),
            in_specs=[pl.BlockSpec((MOE_TILE, ff), lambda t, te, ok: (t, 0)),
                      pl.BlockSpec((1, ff, d), lambda t, te, ok: (te[t], 0, 0))],
            out_specs=pl.BlockSpec((MOE_TILE, d), lambda t, te, ok: (t, 0))),
        out_shape=jax.ShapeDtypeStruct((nt * MOE_TILE, d), BF16),
        compiler_params=_params("arbitrary"),
        name="moe_down",
    )(tile_e, valid, mid, wd)


def _combine_kernel(final, cnt_ref, win_ref, shift_ref, x_ref, comb_ref, pos_ref, ys_ref, fg_ref, o_ref, win_buf,
                    sem):
    blk = pl.program_id(0)
    tb = x_ref.shape[0]
    n_sub = tb // MOE_WIN
    pieces = [(e, j) for e in range(N_EXPERTS) for j in range(n_sub)]

    def needed(e, j):
        return cnt_ref[blk * N_EXPERTS + e] > j * MOE_WIN

    def win_copy(e, j):
        row = pl.multiple_of(win_ref[(blk * N_EXPERTS + e) * n_sub + j], ROW_ALIGN)
        return pltpu.make_async_copy(ys_ref.at[pl.ds(row, MOE_WIN)], win_buf.at[e, j], sem.at[e, j])

    for e, j in pieces:
        @pl.when(needed(e, j))
        def _(e=e, j=j):
            win_copy(e, j).start()

    o_ref[...] = x_ref[...]
    col = lax.broadcasted_iota(jnp.int32, (tb, MOE_WIN), 1).astype(F32)
    for e, j in pieces:
        @pl.when(needed(e, j))
        def _(e=e, j=j):
            win_copy(e, j).wait()
            pos = pos_ref[...][:, e:e + 1]
            in_sub = (pos >= float(j * MOE_WIN)) & (pos < float((j + 1) * MOE_WIN))
            shift = shift_ref[(blk * N_EXPERTS + e) * n_sub + j].astype(F32)
            target = jnp.where(in_sub, pos - float(j * MOE_WIN) + shift, -1.0)
            onehot = (col == target).astype(BF16)
            o_ref[...] += comb_ref[...][:, e:e + 1] * _mm(onehot, win_buf[e, j])

    if final:
        o_ref[...] = _rms(o_ref[...], fg_ref[...])


def _combine(tables, x, comb, pos, ys, final_gain):
    t, d = x.shape
    final = final_gain is not None
    gain = final_gain if final else jnp.ones((d,), F32)
    tb = MOE_TB
    lanes = pl.BlockSpec((tb, LANES), lambda i, *_: (i, 0))
    return pl.pallas_call(
        functools.partial(_combine_kernel, final),
        grid_spec=pltpu.PrefetchScalarGridSpec(
            num_scalar_prefetch=len(tables),
            grid=(t // tb,),
            in_specs=[pl.BlockSpec((tb, d), lambda i, *_: (i, 0)), lanes, lanes,
                      pl.BlockSpec(memory_space=pl.ANY), pl.BlockSpec((1, d), lambda i, *_: (0, 0))],
            out_specs=pl.BlockSpec((tb, d), lambda i, *_: (i, 0)),
            scratch_shapes=[pltpu.VMEM((N_EXPERTS, tb // MOE_WIN, MOE_WIN, d), BF16),
                            pltpu.SemaphoreType.DMA((N_EXPERTS, tb // MOE_WIN))]),
        out_shape=jax.ShapeDtypeStruct((t, d), F32),
        compiler_params=_params("arbitrary"),
        name="moe_combine",
    )(*tables, x, comb, pos, ys, gain.reshape(1, d))


def _moe(x, gain, w_router, wg, wu, wd, final_gain):
    t, d = x.shape
    nb = t // MOE_TB
    n_tiles_max = (2 * t + nb * N_EXPERTS * (ROW_ALIGN - 1)) // MOE_TILE + N_EXPERTS
    hn, comb, pos, pos_t, cnt = _router(x, gain, w_router)
    write_tables, read_tables, (tile_e, tile_blk, valid) = _route_tables(cnt[:, 0, :N_EXPERTS], n_tiles_max)
    n_rows = (n_tiles_max + SPARE_TILES * N_EXPERTS) * MOE_TILE
    xs = _dispatch(write_tables, hn, pos_t, n_rows)
    mid = _gate_up(tile_e, tile_blk, valid, xs, wg, wu)
    ys = _down(tile_e, valid, mid, wd)
    return _combine(read_tables, x, comb, pos, ys, final_gain)


def _final_norm_kernel(x_ref, g_ref, o_ref):
    o_ref[...] = _rms(x_ref[...], g_ref[...])


def _final_norm(x, gain):
    t, d = x.shape
    tm = TM_NORM
    return pl.pallas_call(
        _final_norm_kernel,
        grid=(t // tm,),
        in_specs=[pl.BlockSpec((tm, d), lambda i: (i, 0)), _full((1, d))],
        out_specs=pl.BlockSpec((tm, d), lambda i: (i, 0)),
        out_shape=jax.ShapeDtypeStruct((t, d), F32),
        compiler_params=_params("parallel"),
        name="final_norm",
    )(x, gain.reshape(1, d))


def _hm_groups_spec(n, first):
    def spec(b, h):
        return pl.BlockSpec((n, b, h, CHUNK, HEAD), lambda c: (first // n, 0, 0, c, 0))
    assert first % n == 0
    return spec


def _tok_chunk_spec(b):
    return pl.BlockSpec((b, CHUNK, GROUP), lambda c: (0, c, 0))


def _hgrn_kernel(qfi_ref, g_ref, loglb_ref, log1mlb_ref, onemlb_ref, ng_ref, o_ref, st_ref, rows_ref):
    @pl.when(pl.program_id(0) == 0)
    def _():
        st_ref[...] = jnp.zeros_like(st_ref)

    shape = st_ref.shape[:1] + (CHUNK, HEAD)
    q, fl, v = (qfi_ref[j].reshape(shape) for j in range(3))
    g = g_ref[0].reshape(shape)
    log_sig = jnp.minimum(fl, 0.0) - jnp.log(1.0 + jnp.exp(-jnp.abs(fl)))
    la = loglb_ref[...]
    lc = log1mlb_ref[...] + log_sig
    log_f = jnp.maximum(la, lc) + jnp.log(1.0 + jnp.exp(-jnp.abs(la - lc)))
    k = onemlb_ref[...] * jax.nn.sigmoid(-fl)
    b = _cumsum_rows(log_f) * LOG2_E

    rows_ref[0] = b
    rows_ref[1] = k
    rows_ref[2] = v
    sub = HGRN_SUB
    a_rows = [jnp.zeros((shape[0], sub, CHUNK), F32)]
    for i in range(1, CHUNK // sub):
        bref = rows_ref[0, :, i * sub - 1:i * sub, :]
        qd = q[:, i * sub:(i + 1) * sub] * jnp.exp2(b[:, i * sub:(i + 1) * sub] - bref)
        kd = k[:, :i * sub] * jnp.exp2(bref - b[:, :i * sub])
        later = jnp.zeros((shape[0], CHUNK - i * sub, HEAD), F32)
        a_rows.append(_bmm(qd, jnp.concatenate([kd, later], axis=1), nt=True))
    o = _bmm(jnp.concatenate(a_rows, axis=1), v)

    t_loc = lax.broadcasted_iota(jnp.int32, (shape[0], sub, HEAD), 1)
    diag = []
    for i in range(CHUNK // sub):
        sl = slice(i * sub, (i + 1) * sub)
        bb, qq = b[:, sl], q[:, sl]
        acc = jnp.zeros((shape[0], sub, HEAD), F32)
        for s in range(sub):
            row = i * sub + s
            e = jnp.exp2(jnp.where(t_loc >= s, bb - rows_ref[0, :, row:row + 1, :], -BIG_EXPONENT))
            w = jnp.sum(e * qq * rows_ref[1, :, row:row + 1, :], axis=-1, keepdims=True)
            acc = acc + w * rows_ref[2, :, row:row + 1, :]
        diag.append(acc)
    o = o + jnp.concatenate(diag, axis=1)

    st = st_ref[...]
    o = o + _bmm(q * jnp.exp2(b), st, nt=True)
    b_last = b[:, CHUNK - 1:CHUNK]
    st_ref[...] = st * jnp.exp2(b_last) + _bmm(jnp.swapaxes(v, 1, 2), k * jnp.exp2(b_last - b))

    _store_tokens(o_ref, _rms(o, ng_ref[...]) * _silu(g))


def _gdn_kernel(qkv_ref, z_ref, gates_ref, cw_ref, alog_ref, dtb_ref, ng_ref, o_ref, xbuf_ref, st_ref):
    g3 = xbuf_ref.shape[0]
    g = g3 // 3

    @pl.when(pl.program_id(0) == 0)
    def _():
        st_ref[...] = jnp.zeros_like(st_ref)
        xbuf_ref[:, 0:HALO, :] = jnp.zeros((g3, HALO, HEAD), F32)

    xbuf_ref[:, HALO:HALO + CHUNK, :] = qkv_ref[...].reshape(g3, CHUNK, HEAD)
    conv = jnp.zeros((g3, CHUNK, HEAD), F32)
    for j in range(CONV_K):
        conv = conv + cw_ref[j] * xbuf_ref[:, HALO - (CONV_K - 1) + j:HALO - (CONV_K - 1) + j + CHUNK, :]
    xbuf_ref[:, 0:HALO, :] = xbuf_ref[:, CHUNK:CHUNK + HALO, :]
    qkv = _silu(conv)
    q, k, v = qkv[0:g], qkv[g:2 * g], qkv[2 * g:3 * g]
    q = q * lax.rsqrt(jnp.sum(q * q, axis=-1, keepdims=True) + EPS) * HEAD ** -0.5
    k = k * lax.rsqrt(jnp.sum(k * k, axis=-1, keepdims=True) + EPS)

    t_idx, s_idx = _tri_masks(CHUNK)
    causal = s_idx <= t_idx
    eye = s_idx == t_idx
    g_row = -jnp.exp(alog_ref[...]) * _softplus(gates_ref[0, 0] + dtb_ref[...])
    g_col = jnp.sum(jnp.where(eye, g_row, 0.0), axis=-1, keepdims=True)
    beta = jnp.sum(jnp.where(eye, jax.nn.sigmoid(gates_ref[1, 0]), 0.0), axis=-1, keepdims=True)
    gc_col = jnp.sum(jnp.where(causal, g_row, 0.0), axis=-1, keepdims=True)
    gc_row = jnp.sum(jnp.where(t_idx <= s_idx, g_col, 0.0), axis=1, keepdims=True)
    rel = jnp.where(causal, jnp.exp(jnp.minimum(gc_col - gc_row, 0.0)), 0.0)

    kb = k * beta
    with_k = _bmm(jnp.concatenate([kb, q], axis=1), k, nt=True)
    lower = jnp.where(s_idx < t_idx, with_k[:, :CHUNK] * rel, 0.0)
    uw = _solve_unit_lower(lower, jnp.concatenate([v * beta, kb * jnp.exp(gc_col)], axis=-1))
    u, w = uw[:, :, :HEAD], uw[:, :, HEAD:]
    scores = with_k[:, CHUNK:] * rel

    st = st_ref[...]
    from_state = _bmm(jnp.concatenate([w, q * jnp.exp(gc_col)], axis=1), st)
    v_new = u - from_state[:, :CHUNK]
    o = from_state[:, CHUNK:] + _bmm(scores, v_new)
    g_last = gc_col[:, CHUNK - 1:CHUNK]
    kd = k * jnp.exp(g_last - gc_col)
    st_ref[...] = st * jnp.exp(g_last) + _bmm(jnp.swapaxes(kd, 1, 2), v_new)

    _store_tokens(o_ref, _rms(o, ng_ref[...]) * _silu(z_ref[0].reshape(g, CHUNK, HEAD)))


def _lru_kernel(x_ref, gate_ref, cw_ref, cb_ref, wa_ref, ba_ref, wx_ref, bx_ref, lam_ref, o_ref,
                xbuf_ref, h_ref):
    tt = x_ref.shape[1]

    @pl.when(pl.program_id(1) == 0)
    def _():
        h_ref[...] = jnp.zeros_like(h_ref)
        xbuf_ref[0:HALO, :] = jnp.zeros((HALO, GROUP), F32)

    xbuf_ref[HALO:HALO + tt, :] = x_ref[0]
    xc = jnp.zeros((tt, GROUP), F32) + cb_ref[...]
    for j in range(CONV_K):
        xc = xc + cw_ref[j:j + 1, :] * xbuf_ref[HALO - (CONV_K - 1) + j:HALO - (CONV_K - 1) + j + tt, :]
    xbuf_ref[0:HALO, :] = xbuf_ref[tt:tt + HALO, :]

    r = jax.nn.sigmoid(_mm16(xc, wa_ref[...]) + ba_ref[...])
    i = jax.nn.sigmoid(_mm16(xc, wx_ref[...]) + bx_ref[...])
    log_a = -LRU_C * r * _softplus(-lam_ref[...])
    a = jnp.exp(log_a)
    u = jnp.sqrt(1.0 - a * a) * (i * xc)

    row = lax.broadcasted_iota(jnp.int32, (tt, GROUP), 0)
    shift = 1
    while shift < tt:
        keep = row >= shift
        a_prev = pltpu.roll(a, shift, 0)
        u_prev = pltpu.roll(u, shift, 0)
        u = jnp.where(keep, a * u_prev + u, u)
        a = jnp.where(keep, a * a_prev, a)
        shift *= 2
    hs = u + a * h_ref[...]
    h_ref[...] = hs[tt - 1:tt, :]

    gate = gate_ref[0]
    gelu = 0.5 * gate * (1.0 + jnp.tanh(math.sqrt(2.0 / math.pi) * (gate + 0.044715 * gate * gate * gate)))
    o_ref[0] = hs * gelu


def _lru_mixer(tok, conv_w, conv_b, wa, ba, wx, bx, lam):
    b, s, _ = tok.shape
    c = GROUP
    tt = TT_SCAN
    x_spec = pl.BlockSpec((1, tt, c), lambda i, t: (i, t, TOK_LRU // GROUP))
    gate_spec = pl.BlockSpec((1, tt, c), lambda i, t: (i, t, TOK_LRU // GROUP + 1))
    vec = _full((1, c))
    return pl.pallas_call(
        _lru_kernel,
        grid=(b, s // tt),
        in_specs=[x_spec, gate_spec, _full((CONV_K, c)), vec, _full((c, c)), vec, _full((c, c)), vec, vec],
        out_specs=pl.BlockSpec((1, tt, c), lambda i, t: (i, t, 0)),
        out_shape=jax.ShapeDtypeStruct((b, s, c), F32),
        scratch_shapes=[pltpu.VMEM((HALO + tt, c), F32), pltpu.VMEM((1, c), F32)],
        compiler_params=_params("parallel", "arbitrary"),
        name="rg_lru",
    )(tok, tok, conv_w, conv_b.reshape(1, c), wa, ba.reshape(1, c), wx, bx.reshape(1, c), lam.reshape(1, c))


def _rwkv_prep_kernel(has_vmix, keep_v, *refs):
    refs = list(refs)
    p_ref, mu_ref, w0_ref, w2_ref, a0_ref, a2_ref, g2_ref = refs[:7]
    del refs[:7]
    if has_vmix:
        vf_ref, v0_ref, v1_ref, v2_ref = refs[:4]
        del refs[:4]
    out_ref = refs.pop(0)
    vtok_ref = refs.pop(0) if keep_v else None
    xbuf_ref = refs.pop(0)
    tt = p_ref.shape[1]

    @pl.when(pl.program_id(1) == 0)
    def _():
        xbuf_ref[0:HALO, :] = jnp.zeros((HALO, xbuf_ref.shape[1]), F32)

    c = p_ref[0]
    xbuf_ref[HALO:HALO + tt, :] = c
    prev = xbuf_ref[HALO - 1:HALO - 1 + tt, :]
    xbuf_ref[0:HALO, :] = xbuf_ref[tt:tt + HALO, :]
    c = c + mu_ref[...] * (prev - c)

    r = c[:, 0:GROUP]
    k = c[:, GROUP:2 * GROUP]
    v = c[:, 2 * GROUP:3 * GROUP]
    wa_lo = c[:, 3 * GROUP:3 * GROUP + LANES]
    g_lo = c[:, 3 * GROUP + LANES:4 * GROUP]
    log_w = -math.exp(-0.5) * jax.nn.sigmoid(w0_ref[...] + _mm16(jnp.tanh(wa_lo), w2_ref[...]))
    a = jax.nn.sigmoid(a0_ref[...] + _mm16(wa_lo, a2_ref[...]))
    g = _mm16(jax.nn.sigmoid(g_lo), g2_ref[...])
    if has_vmix:
        mix = jax.nn.sigmoid(v0_ref[...] + _mm16(_mm16(v, v1_ref[...]), v2_ref[...]))
        v = v + (vf_ref[0] - v) * mix
    if keep_v:
        vtok_ref[0] = v
    for j, val in enumerate((r, log_w, k, v, a, g)):
        _store_heads(out_ref, val, (j, 0))


def _rwkv_prep(tok, mu, w0, w2, a0, a2, g2, v_first, v_mix):
    b, s, _ = tok.shape
    n = 4 * GROUP
    tt = TT_SCAN
    lora = w2.shape[0]
    w2p = jnp.zeros((LANES, GROUP), F32).at[:lora].set(w2)
    a2p = jnp.zeros((LANES, GROUP), F32).at[lora:].set(a2)
    tile = pl.BlockSpec((1, tt, GROUP), lambda i, t: (i, t, 0))
    vec = _full((1, GROUP))
    args = [tok, mu.reshape(1, n), w0.reshape(1, GROUP), w2p, a0.reshape(1, GROUP), a2p, g2]
    specs = [pl.BlockSpec((1, tt, n), lambda i, t: (i, t, TOK_RWKV // n)), _full((1, n)), vec,
             _full((LANES, GROUP)), vec, _full((LANES, GROUP)), _full(g2.shape)]
    keep_v = v_mix is None
    if v_mix is not None:
        v0, v1, v2 = v_mix
        rank = v1.shape[1]
        v1p = jnp.zeros((GROUP, LANES), F32).at[:, :rank].set(v1)
        v2p = jnp.zeros((LANES, GROUP), F32).at[:rank].set(v2)
        args += [v_first, v0.reshape(1, GROUP), v1p, v2p]
        specs += [tile, vec, _full((GROUP, LANES)), _full((LANES, GROUP))]
    out_specs = [pl.BlockSpec((6, 1, HEADS, tt, HEAD), lambda i, t: (0, i, 0, t, 0))]
    out_shape = [jax.ShapeDtypeStruct((6, b, HEADS, s, HEAD), F32)]
    if keep_v:
        out_specs.append(tile)
        out_shape.append(jax.ShapeDtypeStruct((b, s, GROUP), F32))
    outs = pl.pallas_call(
        functools.partial(_rwkv_prep_kernel, v_mix is not None, keep_v),
        grid=(b, s // tt),
        in_specs=specs,
        out_specs=out_specs,
        out_shape=out_shape,
        scratch_shapes=[pltpu.VMEM((HALO + tt, n), F32)],
        compiler_params=_params("parallel", "arbitrary"),
        name="rwkv7_prep",
    )(*args)
    return (outs[0], outs[1]) if keep_v else (outs[0], v_first)


def _rwkv_kernel(in_ref, kk_ref, ka_ref, rk_ref, lnw_ref, lnb_ref, o_ref, zt_ref):
    @pl.when(pl.program_id(0) == 0)
    def _():
        zt_ref[...] = jnp.zeros_like(zt_ref)

    shape = zt_ref.shape[:1] + (CHUNK, HEAD)
    r, lw, k, v, a, g = (in_ref[j].reshape(shape) for j in range(6))
    kk = k * kk_ref[...]
    kk = kk * lax.rsqrt(jnp.sum(kk * kk, axis=-1, keepdims=True) + EPS)
    k = k * (1.0 + (a - 1.0) * ka_ref[...])
    b = _cumsum_rows(lw)
    e_pos = jnp.exp(b)
    e_neg = jnp.exp(-b)
    r_t = r * e_pos
    al_t = -kk * jnp.exp(b - lw)
    be_t = kk * a * e_neg
    k_t = k * e_neg

    t_idx, s_idx = _tri_masks(CHUNK)
    strict = s_idx < t_idx
    ab_k = jnp.concatenate([be_t, k_t], axis=1)
    both = _bmm(jnp.concatenate([al_t, r_t], axis=1), ab_k, nt=True)
    l_ab = jnp.where(strict, both[:, :CHUNK, :CHUNK], 0.0)
    l_ak = jnp.where(strict, both[:, :CHUNK, CHUNK:], 0.0)
    t2 = lax.broadcasted_iota(jnp.int32, (CHUNK, 2 * CHUNK), 0)
    s2 = lax.broadcasted_iota(jnp.int32, (CHUNK, 2 * CHUNK), 1)
    incl2 = jnp.where(s2 >= CHUNK, s2 - CHUNK, s2) <= t2
    m_all = jnp.where(incl2, both[:, CHUNK:], 0.0)

    sol = _solve_unit_lower(-l_ab, jnp.concatenate([al_t, _bmm(l_ak, v)], axis=-1))
    w1, u0 = sol[:, :, :HEAD], sol[:, :, HEAD:]

    zt = zt_ref[...]
    from_state = _bmm(jnp.concatenate([w1, r_t], axis=1), zt, nt=True)
    u = from_state[:, :CHUNK] + u0
    uv = jnp.concatenate([u, v], axis=1)
    y = from_state[:, CHUNK:] + _bmm(m_all, uv)
    b_last = b[:, CHUNK - 1:CHUNK]
    e_last = jnp.exp(b_last - b)
    zt_ref[...] = zt * jnp.exp(b_last) + _bmm(jnp.swapaxes(uv, 1, 2),
                                              jnp.concatenate([kk * a * e_last, k * e_last], axis=1))

    mean = jnp.mean(y, axis=-1, keepdims=True)
    var = jnp.mean(jnp.square(y - mean), axis=-1, keepdims=True)
    yn = (y - mean) * lax.rsqrt(var + RWKV_GN_EPS) * lnw_ref[...] + lnb_ref[...]
    bonus = jnp.sum(r * k * rk_ref[...], axis=-1, keepdims=True) * v
    _store_tokens(o_ref, (yn + bonus) * g)


N_HGRN_IN, N_GDN_IN, N_RWKV_IN = 6, 7, 6


def _chunk_mixers_kernel(*refs):
    ins, rest = refs[:N_HGRN_IN + N_GDN_IN + N_RWKV_IN], refs[N_HGRN_IN + N_GDN_IN + N_RWKV_IN:]
    o_hgrn, o_gdn, o_rwkv, st_hgrn, rows_hgrn, xbuf_gdn, st_gdn, zt_rwkv = rest
    _hgrn_kernel(*ins[:N_HGRN_IN], o_hgrn, st_hgrn, rows_hgrn)
    _gdn_kernel(*ins[N_HGRN_IN:N_HGRN_IN + N_GDN_IN], o_gdn, xbuf_gdn, st_gdn)
    _rwkv_kernel(*ins[N_HGRN_IN + N_GDN_IN:], o_rwkv, zt_rwkv)


def _chunk_mixers(hm, d_hm, hgrn_args, gdn_args, rwkv_args):
    _, b, h, s, d = hm.shape
    g = b * h
    par = _full((g, 1, d))
    lb, hgrn_norm = hgrn_args
    gates, conv_w, a_log, dt_bias, gdn_norm = gdn_args
    hgrn_specs = [_hm_groups_spec(3, HM_HGRN_QFI)(b, h), _hm_groups_spec(1, HM_HGRN_G)(b, h)] + [par] * 4
    gdn_specs = [_hm_groups_spec(3, HM_GDN_QKV)(b, h), _hm_groups_spec(1, HM_GDN_Z)(b, h),
                 pl.BlockSpec((2, 1, g, 1, CHUNK), lambda c: (0, c, 0, 0, 0)),
                 _full((CONV_K, 3 * g, 1, d)), _full((g, 1, 1)), _full((g, 1, 1)), _full((1, 1, d))]
    rwkv_specs = [_hm_groups_spec(6, 0)(b, h)] + [par] * 5
    assert (len(hgrn_specs), len(gdn_specs), len(rwkv_specs)) == (N_HGRN_IN, N_GDN_IN, N_RWKV_IN)
    out = jax.ShapeDtypeStruct((b, s, h * d), F32)
    state = pltpu.VMEM((g, d, d), F32)
    return pl.pallas_call(
        _chunk_mixers_kernel,
        grid=(s // CHUNK,),
        in_specs=hgrn_specs + gdn_specs + rwkv_specs,
        out_specs=[_tok_chunk_spec(b)] * 3,
        out_shape=[out] * 3,
        scratch_shapes=[state, pltpu.VMEM((3, g, CHUNK, d), F32),
                        pltpu.VMEM((3 * g, HALO + CHUNK, d), F32), state, state],
        compiler_params=_params("arbitrary"),
        name="chunk_mixers",
    )(hm, hm, jnp.log(lb), jnp.log1p(-lb), 1.0 - lb, hgrn_norm,
      hm, hm, gates, conv_w, a_log, dt_bias, gdn_norm,
      d_hm, *rwkv_args)


def _par_hm(p, b, width=HEAD):
    return jnp.tile(p.reshape(-1, 1, width), (b, 1, 1))


def _gdn_gate_layout(gates):
    b, s, h2 = gates.shape
    g = gates.reshape(b, s // CHUNK, CHUNK, 2, h2 // 2).transpose(3, 1, 0, 4, 2)
    return g.reshape(2, s // CHUNK, b * (h2 // 2), 1, CHUNK)


def _gdn_conv_layout(w, b):
    k, c = w.shape
    h = c // (3 * HEAD)
    w = jnp.broadcast_to(w.reshape(k, 3, 1, h, 1, HEAD), (k, 3, b, h, 1, HEAD))
    return w.reshape(k, 3 * b * h, 1, HEAD)


def _block_diag(w):
    n, a, b = w.shape
    out = jnp.zeros((n * a, n * b), w.dtype)
    for i in range(n):
        out = out.at[i * a:(i + 1) * a, i * b:(i + 1) * b].set(w[i])
    return out


def _in_proj_weight(w):
    d = w.shape[0]
    g4 = 4 * GROUP
    gates = g4 + g4
    lru = gates + 2 * HEADS
    rwkv = lru + 2 * GROUP
    cols = [w[:, rwkv:rwkv + g4], w[:, lru:rwkv], w[:, gates:lru], jnp.zeros((d, LANES - 2 * HEADS), w.dtype),
            w[:, 0:3 * GROUP], w[:, g4:g4 + 3 * GROUP], w[:, 3 * GROUP:g4], w[:, g4 + 3 * GROUP:gates]]
    return jnp.concatenate(cols, axis=1).astype(BF16)


def kernel(x, mem, norm_mix, w_in, w_out, hgrn_lb_logits, hgrn_norm, gdn_conv_w, gdn_A_log, gdn_dt_bias, gdn_norm, lru_conv_w, lru_conv_b, lru_wa, lru_ba, lru_wx, lru_bx, lru_lambda, rwkv_mu, rwkv_w0, rwkv_w2, rwkv_a0, rwkv_a2, rwkv_g2, rwkv_k_k, rwkv_k_a, rwkv_r_k, rwkv_ln_w, rwkv_ln_b, rwkv_v0, rwkv_v1, rwkv_v2, mem_norm, norm_xattn, xattn_wq, xattn_wk, xattn_wv, xattn_wo, norm_ffn, ffn_w_gate, ffn_w_up, ffn_w_down, moe_router, moe_w_gate, moe_w_up, moe_w_down, norm_final):
    bsz, seq, d = x.shape
    depth = w_in.shape[0]
    tok = bsz * seq
    mlen = mem.shape[1]

    lb = jnp.cumsum(jax.nn.softmax(hgrn_lb_logits.astype(F32), axis=0), axis=0)
    lb = lb - lb[0]
    mem_f = mem.reshape(bsz * mlen, d)
    v_first = None
    for l in range(depth):
        p_tok, p_hm = _in_proj(x, norm_mix[l], _in_proj_weight(w_in[l]))

        y_c = _lru_mixer(p_tok, lru_conv_w[l], lru_conv_b[l], _block_diag(lru_wa[l]), lru_ba[l],
                         _block_diag(lru_wx[l]), lru_bx[l], lru_lambda[l])

        v_mix = None if l == 0 else (rwkv_v0[l - 1], rwkv_v1[l - 1], rwkv_v2[l - 1])
        d_hm, v_first = _rwkv_prep(p_tok, rwkv_mu[l], rwkv_w0[l], rwkv_w2[l], rwkv_a0[l], rwkv_a2[l],
                                   rwkv_g2[l], v_first, v_mix)

        gates = p_tok[..., TOK_GATES:TOK_GATES + 2 * HEADS]
        y_a, y_b, y_d = _chunk_mixers(
            p_hm, d_hm,
            (_par_hm(lb[l], bsz), _par_hm(hgrn_norm[l], bsz)),
            (_gdn_gate_layout(gates), _gdn_conv_layout(gdn_conv_w[l], bsz), _par_hm(gdn_A_log[l], bsz, 1),
             _par_hm(gdn_dt_bias[l], bsz, 1), gdn_norm[l].reshape(1, 1, HEAD)),
            tuple(_par_hm(p[l], bsz) for p in (rwkv_k_k, rwkv_k_a, rwkv_r_k, rwkv_ln_w, rwkv_ln_b)))

        w_kv = jnp.concatenate([xattn_wk[l], xattn_wv[l]], axis=1).astype(BF16)
        kv = _norm_matmul(mem_f, mem_norm, w_kv, mlen, BF16).reshape(bsz, mlen, 2 * d)
        x = _mix_attention((y_a, y_b, y_c, y_d), w_out[l].reshape(4, GROUP, d).astype(BF16), x, norm_xattn[l],
                           xattn_wq[l].astype(BF16), kv, xattn_wo[l].astype(BF16))

        xf = x.reshape(tok, d)
        j = l // 2
        if l % 2 == 0:
            ones = jnp.ones((1, tok, 1), F32)
            xf = _ffn(xf, norm_ffn[l], ffn_w_gate[j][None].astype(BF16), ffn_w_up[j][None].astype(BF16),
                      ffn_w_down[j][None].astype(BF16), ones, TF_FFN)
        else:
            xf = _moe(xf, norm_ffn[l], moe_router[j], moe_w_gate[j].astype(BF16), moe_w_up[j].astype(BF16),
                      moe_w_down[j].astype(BF16), norm_final if l == depth - 1 else None)
        x = xf.reshape(bsz, seq, d)
    if depth % 2 == 1:
        x = _final_norm(x.reshape(tok, d), norm_final).reshape(bsz, seq, d)
    return x
```

```python
import functools
import math

import jax
import jax.numpy as jnp
from jax import lax
from jax.experimental import pallas as pl
from jax.experimental.pallas import tpu as pltpu

F32 = jnp.float32
BF16 = jnp.bfloat16
HIGHEST = lax.Precision.HIGHEST

GROUP = 256
HEAD = 64
HEADS = GROUP // HEAD
CHUNK = 64
SUB = 16
HGRN_SUB = 8
LOG2_E = 1.4426950408889634
BIG_EXPONENT = 1e30
CONV_K = 4
HALO = 8
LRU_C = 8.0
RWKV_GN_EPS = 64e-5
XATTN_HEADS = 4
N_EXPERTS = 8
EPS = 1e-6
LANES = 128
VMEM_LIMIT_BYTES = 56 * 1024 * 1024

TOK_RWKV = 0
TOK_LRU = 4 * GROUP
TOK_GATES = 6 * GROUP
N_TOK = 6 * GROUP + LANES
PAIR = 2 * HEAD
HMP_GROUPS = 4
HM_GDN_QKV, HM_GDN_Z = 0, 3
HM_GROUPS = 4

TM_IN_PROJ = 512
TM_XATTN = 512
TM_FFN = 1024
TF_FFN = 1408
TM_NORM = 1024
TT_SCAN = 512
MOE_TB = 512
MOE_TILE = 512
MOE_F_SPLIT = 2
ROW_ALIGN = 16
SPARE_TILES = 2
MOE_SEG = 128
MOE_WIN = 256


def _params(*semantics):
    return pltpu.CompilerParams(dimension_semantics=semantics, vmem_limit_bytes=VMEM_LIMIT_BYTES)


def _full(shape):
    n = len(shape)
    return pl.BlockSpec(shape, lambda *_: (0,) * n)


def _bmm(a, b, nt=False, exact=False):
    dn = (((2,), (2 if nt else 1,)), ((0,), (0,)))
    if exact:
        return lax.dot_general(a, b, dn, precision=HIGHEST, preferred_element_type=F32)
    return lax.dot_general(a.astype(BF16), b.astype(BF16), dn, preferred_element_type=F32)


def _mm(a, b, precision=None):
    return jnp.dot(a, b, precision=precision, preferred_element_type=F32)


def _mm16(a, b):
    return jnp.dot(a.astype(BF16), b.astype(BF16), preferred_element_type=F32)


def _softplus(x):
    return jnp.maximum(x, 0.0) + jnp.log1p(jnp.exp(-jnp.abs(x)))


def _silu(x):
    return x * jax.nn.sigmoid(x)


def _rms(x, gain):
    return x * lax.rsqrt(jnp.mean(x * x, axis=-1, keepdims=True) + EPS) * gain


def _tri_masks(n):
    t = lax.broadcasted_iota(jnp.int32, (n, n), 0)
    s = lax.broadcasted_iota(jnp.int32, (n, n), 1)
    return t, s


def _cumsum_rows(x):
    g, c, _ = x.shape
    t, s = _tri_masks(c)
    tri = jnp.broadcast_to((s <= t).astype(BF16), (g, c, c))
    hi = x.astype(BF16)
    rest = x - hi.astype(F32)
    mid = rest.astype(BF16)
    lo = (rest - mid.astype(F32)).astype(BF16)
    return _bmm(tri, hi) + _bmm(tri, mid) + _bmm(tri, lo)


def _solve_unit_lower(low, rhs):
    c = low.shape[1]
    r = rhs.shape[2]
    assert c == 4 * SUB and SUB == 16
    t, s = _tri_masks(c)
    same_block = (t // SUB) == (s // SUB)
    d1 = jnp.where(same_block, low, 0.0)
    z = jnp.concatenate([rhs, jnp.where(same_block, 0.0, low)], axis=-1)
    d2 = _bmm(d1, d1)
    d4 = _bmm(d2, d2)
    d8 = _bmm(d4, d4)
    z = z + _bmm(d8, z)
    z = z + _bmm(d4, z)
    z = z + _bmm(d2, z)
    z = z - _bmm(d1, z)
    y = z[:, :, :r]
    n1 = z[:, :, r:]
    n2 = _bmm(n1, n1)
    y = y + _bmm(n2, y)
    return y - _bmm(n1, y)


def _store_heads(ref, val, lead=(), width=HEAD):
    for h in range(val.shape[1] // width):
        ref[lead + (h,)] = val[:, h * width:(h + 1) * width]


def _store_tokens(ref, o):
    nb = ref.shape[0]
    nh = o.shape[0] // nb
    for b in range(nb):
        ref[b] = jnp.concatenate([o[b * nh + h] for h in range(nh)], axis=-1)


def _in_proj_kernel(x_ref, g_ref, w_ref, tok_ref, hmp_ref, hm_ref):
    hn = _rms(x_ref[0], g_ref[...]).astype(BF16)
    acc = _mm(hn, w_ref[...])
    col = tok_ref.shape[2]
    tok_ref[0] = acc[:, :col]
    for g in range(hmp_ref.shape[0]):
        _store_heads(hmp_ref, acc[:, col:col + GROUP], (g, 0), PAIR)
        col += GROUP
    for g in range(hm_ref.shape[0]):
        _store_heads(hm_ref, acc[:, col:col + GROUP], (g, 0))
        col += GROUP


def _in_proj(x, gain, w):
    b, s, d = x.shape
    n = w.shape[1]
    tm = TM_IN_PROJ
    return pl.pallas_call(
        _in_proj_kernel,
        grid=(b, s // tm),
        in_specs=[pl.BlockSpec((1, tm, d), lambda i, j: (i, j, 0)), _full((1, d)), _full((d, n))],
        out_specs=[pl.BlockSpec((1, tm, N_TOK), lambda i, j: (i, j, 0)),
                   pl.BlockSpec((HMP_GROUPS, 1, HEADS // 2, tm, PAIR), lambda i, j: (0, i, 0, j, 0)),
                   pl.BlockSpec((HM_GROUPS, 1, HEADS, tm, HEAD), lambda i, j: (0, i, 0, j, 0))],
        out_shape=[jax.ShapeDtypeStruct((b, s, N_TOK), F32),
                   jax.ShapeDtypeStruct((HMP_GROUPS, b, HEADS // 2, s, PAIR), F32),
                   jax.ShapeDtypeStruct((HM_GROUPS, b, HEADS, s, HEAD), F32)],
        compiler_params=_params("parallel", "parallel"),
        name="in_proj",
    )(x, gain.reshape(1, d), w)


def _norm_mm_kernel(x_ref, g_ref, w_ref, o_ref):
    o_ref[...] = _mm(_rms(x_ref[...], g_ref[...]).astype(BF16), w_ref[...]).astype(o_ref.dtype)


def _norm_matmul(x, gain, w, tm, out_dtype):
    t, k = x.shape
    n = w.shape[1]
    return pl.pallas_call(
        _norm_mm_kernel,
        grid=(t // tm,),
        in_specs=[pl.BlockSpec((tm, k), lambda i: (i, 0)), _full((1, k)), _full((k, n))],
        out_specs=pl.BlockSpec((tm, n), lambda i: (i, 0)),
        out_shape=jax.ShapeDtypeStruct((t, n), out_dtype),
        compiler_params=_params("parallel"),
        name="norm_matmul",
    )(x, gain.reshape(1, k), w)


def _mix_attn_kernel(ya_ref, yb_ref, yc_ref, yd_ref, wout_ref, x_ref, g_ref, wq_ref, k_ref, v_ref, wo_ref, o_ref):
    x = x_ref[0]
    for m, y_ref in enumerate((ya_ref, yb_ref, yc_ref, yd_ref)):
        x = x + _mm(y_ref[0].astype(BF16), wout_ref[m])
    hn = _rms(x, g_ref[...]).astype(BF16)
    q = _mm(hn, wq_ref[...])
    d = q.shape[1] // XATTN_HEADS
    outs = []
    for h in range(XATTN_HEADS):
        qh = (q[:, h * d:(h + 1) * d] * d ** -0.5).astype(BF16)
        kh = k_ref[0, :, h * d:(h + 1) * d]
        vh = v_ref[0, :, h * d:(h + 1) * d]
        s = lax.dot_general(qh, kh, (((1,), (1,)), ((), ())), preferred_element_type=F32)
        s = s - jnp.max(s, axis=-1, keepdims=True)
        p = jnp.exp(s)
        p = p / jnp.sum(p, axis=-1, keepdims=True)
        outs.append(_mm(p.astype(BF16), vh))
    o = jnp.concatenate(outs, axis=-1).astype(BF16)
    o_ref[0] = x + _mm(o, wo_ref[...])


def _mix_attention(ys, w_out, x, gain, wq, kv, wo):
    b, s, d = x.shape
    m = kv.shape[1]
    tm = TM_XATTN
    y_spec = pl.BlockSpec((1, tm, GROUP), lambda i, j: (i, j, 0))
    x_spec = pl.BlockSpec((1, tm, d), lambda i, j: (i, j, 0))
    return pl.pallas_call(
        _mix_attn_kernel,
        grid=(b, s // tm),
        in_specs=[y_spec] * 4 + [_full(w_out.shape), x_spec, _full((1, d)), _full((d, d)),
                                 pl.BlockSpec((1, m, d), lambda i, j: (i, 0, 0)),
                                 pl.BlockSpec((1, m, d), lambda i, j: (i, 0, 1)),
                                 _full((d, d))],
        out_specs=x_spec,
        out_shape=jax.ShapeDtypeStruct((b, s, d), F32),
        compiler_params=_params("parallel", "parallel"),
        name="mix_attention",
    )(*ys, w_out, x, gain.reshape(1, d), wq, kv, kv, wo)


def _ffn_kernel(x_ref, g_ref, wg_ref, wu_ref, wd_ref, c_ref, o_ref, hn_ref, acc_ref):
    f = pl.program_id(1)

    @pl.when(f == 0)
    def _():
        hn_ref[...] = _rms(x_ref[...], g_ref[...]).astype(BF16)
        acc_ref[...] = x_ref[...]

    hn = hn_ref[...]
    gate = _mm(hn, wg_ref[0])
    up = _mm(hn, wu_ref[0])
    mid = _silu(gate) * up * c_ref[0]
    acc_ref[...] += _mm(mid.astype(BF16), wd_ref[0])

    @pl.when(f == pl.num_programs(1) - 1)
    def _():
        o_ref[...] = acc_ref[...]


def _ffn(x, gain, wg, wu, wd, combine, tf):
    t, d = x.shape
    e, _, ff = wg.shape
    nf = ff // tf
    tm = TM_FFN
    return pl.pallas_call(
        _ffn_kernel,
        grid=(t // tm, e * nf),
        in_specs=[pl.BlockSpec((tm, d), lambda i, f: (i, 0)),
                  _full((1, d)),
                  pl.BlockSpec((1, d, tf), lambda i, f: (f // nf, 0, f % nf)),
                  pl.BlockSpec((1, d, tf), lambda i, f: (f // nf, 0, f % nf)),
                  pl.BlockSpec((1, tf, d), lambda i, f: (f // nf, f % nf, 0)),
                  pl.BlockSpec((1, tm, 1), lambda i, f: (f // nf, i, 0))],
        out_specs=pl.BlockSpec((tm, d), lambda i, f: (i, 0)),
        out_shape=jax.ShapeDtypeStruct((t, d), F32),
        scratch_shapes=[pltpu.VMEM((tm, d), BF16), pltpu.VMEM((tm, d), F32)],
        compiler_params=_params("parallel", "arbitrary"),
        name="ffn",
    )(x, gain.reshape(1, d), wg, wu, wd, combine)


def _router_kernel(x_ref, g_ref, w_ref, hn_ref, comb_ref, pos_ref, post_ref, cnt_ref):
    hn = _rms(x_ref[...], g_ref[...])
    hn_ref[...] = hn.astype(BF16)
    logits = _mm(hn, w_ref[...], precision=HIGHEST)
    lane = lax.broadcasted_iota(jnp.int32, logits.shape, 1)
    neg = jnp.float32(-jnp.inf)
    logits = jnp.where(lane < N_EXPERTS, logits, neg)
    m1 = jnp.max(logits, axis=-1, keepdims=True)
    i1 = jnp.min(jnp.where(logits == m1, lane, LANES), axis=-1, keepdims=True)
    rest = jnp.where(lane == i1, neg, logits)
    m2 = jnp.max(rest, axis=-1, keepdims=True)
    i2 = jnp.min(jnp.where(rest == m2, lane, LANES), axis=-1, keepdims=True)
    e2 = jnp.exp(m2 - m1)
    comb = jnp.where(lane == i1, 1.0 / (1.0 + e2), 0.0) + jnp.where(lane == i2, e2 / (1.0 + e2), 0.0)
    comb_ref[...] = comb

    sel = (comb > 0.0).astype(F32)
    tb = sel.shape[0]
    row = lax.broadcasted_iota(jnp.int32, sel.shape, 0)
    run = sel
    shift = 1
    while shift < tb:
        run = run + jnp.where(row >= shift, pltpu.roll(run, shift, 0), 0.0)
        shift *= 2
    pos = jnp.where(sel > 0.0, run - 1.0, -1.0)
    pos_ref[...] = pos
    post_ref[0] = jnp.transpose(pos)[:N_EXPERTS, :]
    cnt_ref[0] = run[tb - 1:tb, :].astype(jnp.int32)


def _router(x, gain, w_router):
    t, d = x.shape
    tb = MOE_TB
    nb = t // tb
    w = jnp.zeros((d, LANES), F32).at[:, :N_EXPERTS].set(w_router)
    tile = pl.BlockSpec((tb, LANES), lambda i: (i, 0))
    return pl.pallas_call(
        _router_kernel,
        grid=(nb,),
        in_specs=[pl.BlockSpec((tb, d), lambda i: (i, 0)), _full((1, d)), _full((d, LANES))],
        out_specs=[pl.BlockSpec((tb, d), lambda i: (i, 0)), tile, tile,
                   pl.BlockSpec((1, N_EXPERTS, tb), lambda i: (i, 0, 0)),
                   pl.BlockSpec((1, 1, LANES), lambda i: (i, 0, 0))],
        out_shape=[jax.ShapeDtypeStruct((t, d), BF16), jax.ShapeDtypeStruct((t, LANES), F32),
                   jax.ShapeDtypeStruct((t, LANES), F32), jax.ShapeDtypeStruct((nb, N_EXPERTS, tb), F32),
                   jax.ShapeDtypeStruct((nb, 1, LANES), jnp.int32)],
        compiler_params=_params("parallel"),
        name="router",
    )(x, gain.reshape(1, d), w)


def _route_tables(cnt, n_tiles_max):
    nb, ne = cnt.shape
    cpad = (cnt + (ROW_ALIGN - 1)) // ROW_ALIGN * ROW_ALIGN
    off = jnp.cumsum(cpad, axis=0) - cpad
    total = jnp.sum(cpad, axis=0)
    ntile = (total + MOE_TILE - 1) // MOE_TILE
    first = jnp.cumsum(ntile) - ntile
    n_tiles = jnp.sum(ntile)
    region = (first + SPARE_TILES * jnp.arange(ne, dtype=jnp.int32)) * MOE_TILE
    dst = region[None, :] + off
    tail = region + total
    sub = jnp.arange(MOE_TB // MOE_WIN, dtype=jnp.int32)[None, None, :] * MOE_WIN
    want = off[:, :, None] + sub
    win_rel = jnp.clip(want, 0, jnp.maximum(ntile * MOE_TILE - MOE_WIN, 0)[None, :, None])
    win = first[None, :, None] * MOE_TILE + win_rel
    shift = want - win_rel
    t = jnp.minimum(jnp.arange(n_tiles_max, dtype=jnp.int32), n_tiles - 1)
    tile_e = jnp.minimum(jnp.sum(t[:, None] >= jnp.cumsum(ntile)[None, :], axis=1), ne - 1).astype(jnp.int32)
    tile_blk = t + SPARE_TILES * tile_e
    valid = (jnp.arange(n_tiles_max, dtype=jnp.int32) < n_tiles).astype(jnp.int32)
    n_blocks = n_tiles_max + SPARE_TILES * ne
    blocks = jnp.arange(n_blocks, dtype=jnp.int32)
    unused = 1 - jnp.max((tile_blk[:, None] == blocks[None, :]) * valid[:, None], axis=0)
    rank = jnp.cumsum(unused) - 1
    fill = jnp.sum(jnp.where((rank[None, :] == blocks[:, None]) & (unused[None, :] > 0), blocks[None, :], 0), axis=1)
    n_fill = jnp.sum(unused).astype(jnp.int32).reshape(1)
    flat = lambda a: a.reshape(-1).astype(jnp.int32)
    return ((flat(cpad), flat(dst), flat(tail), flat(fill), n_fill), (flat(cnt), flat(win), flat(shift)),
            (tile_e, tile_blk.astype(jnp.int32), valid))


def _dispatch_kernel(cpad_ref, dst_ref, tail_ref, fill_ref, nfill_ref, hn_ref, post_ref, xs_ref, seg_ref, sem):
    blk = pl.program_id(0)
    tb = hn_ref.shape[0]
    assert tb == MOE_TILE
    pieces = [(e, j) for e in range(N_EXPERTS) for j in range(tb // MOE_SEG)]

    def needed(e, j):
        return cpad_ref[blk * N_EXPERTS + e] > j * MOE_SEG

    def piece_copy(e, j):
        row = pl.multiple_of(dst_ref[blk * N_EXPERTS + e] + j * MOE_SEG, ROW_ALIGN)
        return pltpu.make_async_copy(seg_ref.at[e, pl.ds(j * MOE_SEG, MOE_SEG)], xs_ref.at[pl.ds(row, MOE_SEG)],
                                     sem.at[e, j])

    rank = lax.broadcasted_iota(jnp.int32, (MOE_SEG, tb), 0).astype(F32)
    for e, j in pieces:
        @pl.when(needed(e, j))
        def _(e=e, j=j):
            onehot = (rank + float(j * MOE_SEG) == post_ref[0, e:e + 1, :]).astype(BF16)
            seg_ref[e, j * MOE_SEG:(j + 1) * MOE_SEG, :] = _mm(onehot, hn_ref[...]).astype(BF16)
            piece_copy(e, j).start()
    for e, j in pieces:
        @pl.when(needed(e, j))
        def _(e=e, j=j):
            piece_copy(e, j).wait()

    @pl.when(blk == pl.num_programs(0) - 1)
    def _():
        seg_ref[0] = jnp.zeros(seg_ref.shape[1:], BF16)

        def tail_copy(e):
            row = pl.multiple_of(tail_ref[e], ROW_ALIGN)
            return pltpu.make_async_copy(seg_ref.at[0], xs_ref.at[pl.ds(row, tb)], sem.at[e, 0])

        for e in range(N_EXPERTS):
            tail_copy(e).start()
        for e in range(N_EXPERTS):
            tail_copy(e).wait()

        def fill_block(k, carry):
            row = pl.multiple_of(fill_ref[k] * MOE_TILE, MOE_TILE)
            copy = pltpu.make_async_copy(seg_ref.at[0], xs_ref.at[pl.ds(row, tb)], sem.at[0, 0])
            copy.start()
            copy.wait()
            return carry

        lax.fori_loop(0, nfill_ref[0], fill_block, 0)


def _dispatch(tables, hn, pos_t, n_rows):
    t, d = hn.shape
    tb = MOE_TB
    return pl.pallas_call(
        _dispatch_kernel,
        grid_spec=pltpu.PrefetchScalarGridSpec(
            num_scalar_prefetch=len(tables),
            grid=(t // tb,),
            in_specs=[pl.BlockSpec((tb, d), lambda i, *_: (i, 0)),
                      pl.BlockSpec((1, N_EXPERTS, tb), lambda i, *_: (i, 0, 0))],
            out_specs=pl.BlockSpec(memory_space=pl.ANY),
            scratch_shapes=[pltpu.VMEM((N_EXPERTS, tb, d), BF16),
                            pltpu.SemaphoreType.DMA((N_EXPERTS, tb // MOE_SEG))]),
        out_shape=jax.ShapeDtypeStruct((n_rows, d), BF16),
        compiler_params=_params("arbitrary"),
        name="moe_dispatch",
    )(*tables, hn, pos_t)


def _gate_up_kernel(te_ref, tb_ref, ok_ref, x_ref, wg_ref, wu_ref, o_ref):
    t = pl.program_id(1)

    @pl.when(ok_ref[t] > 0)
    def _():
        x = x_ref[...]
        o_ref[...] = (_silu(_mm(x, wg_ref[0])) * _mm(x, wu_ref[0])).astype(o_ref.dtype)

    @pl.when(ok_ref[t] == 0)
    def _():
        o_ref[...] = jnp.zeros_like(o_ref)


def _gate_up(tile_e, tile_blk, valid, xs, wg, wu):
    d = xs.shape[1]
    ff = wg.shape[2]
    nt = tile_e.shape[0]
    tf = ff // MOE_F_SPLIT
    w_spec = pl.BlockSpec((1, d, tf), lambda f, t, te, tb, ok: (te[t], 0, f))
    return pl.pallas_call(
        _gate_up_kernel,
        grid_spec=pltpu.PrefetchScalarGridSpec(
            num_scalar_prefetch=3,
            grid=(MOE_F_SPLIT, nt),
            in_specs=[pl.BlockSpec((MOE_TILE, d), lambda f, t, te, tb, ok: (tb[t], 0)), w_spec, w_spec],
            out_specs=pl.BlockSpec((MOE_TILE, tf), lambda f, t, te, tb, ok: (t, f))),
        out_shape=jax.ShapeDtypeStruct((nt * MOE_TILE, ff), BF16),
        compiler_params=_params("arbitrary", "arbitrary"),
        name="moe_gate_up",
    )(tile_e, tile_blk, valid, xs, wg, wu)


def _down_kernel(te_ref, ok_ref, m_ref, wd_ref, o_ref):
    t = pl.program_id(0)

    @pl.when(ok_ref[t] > 0)
    def _():
        o_ref[...] = _mm(m_ref[...], wd_ref[0]).astype(o_ref.dtype)

    @pl.when(ok_ref[t] == 0)
    def _():
        o_ref[...] = jnp.zeros_like(o_ref)


def _down(tile_e, valid, mid, wd):
    ff, d = wd.shape[1:]
    nt = tile_e.shape[0]
    return pl.pallas_call(
        _down_kernel,
        grid_spec=pltpu.PrefetchScalarGridSpec(
            num_scalar_prefetch=2,
            grid=(nt,),
            in_specs=[pl.BlockSpec((MOE_TILE, ff), lambda t, te, ok: (t, 0)),
                      pl.BlockSpec((1, ff, d), lambda t, te, ok: (te[t], 0, 0))],
            out_specs=pl.BlockSpec((MOE_TILE, d), lambda t, te, ok: (t, 0))),
        out_shape=jax.ShapeDtypeStruct((nt * MOE_TILE, d), BF16),
        compiler_params=_params("arbitrary"),
        name="moe_down",
    )(tile_e, valid, mid, wd)


def _combine_kernel(final, cnt_ref, win_ref, shift_ref, x_ref, comb_ref, pos_ref, ys_ref, fg_ref, o_ref, win_buf,
                    sem):
    blk = pl.program_id(0)
    tb = x_ref.shape[0]
    n_sub = tb // MOE_WIN
    pieces = [(e, j) for e in range(N_EXPERTS) for j in range(n_sub)]

    def needed(e, j):
        return cnt_ref[blk * N_EXPERTS + e] > j * MOE_WIN

    def win_copy(e, j):
        row = pl.multiple_of(win_ref[(blk * N_EXPERTS + e) * n_sub + j], ROW_ALIGN)
        return pltpu.make_async_copy(ys_ref.at[pl.ds(row, MOE_WIN)], win_buf.at[e, j], sem.at[e, j])

    for e, j in pieces:
        @pl.when(needed(e, j))
        def _(e=e, j=j):
            win_copy(e, j).start()

    o_ref[...] = x_ref[...]
    col = lax.broadcasted_iota(jnp.int32, (tb, MOE_WIN), 1).astype(F32)
    for e, j in pieces:
        @pl.when(needed(e, j))
        def _(e=e, j=j):
            win_copy(e, j).wait()
            pos = pos_ref[...][:, e:e + 1]
            in_sub = (pos >= float(j * MOE_WIN)) & (pos < float((j + 1) * MOE_WIN))
            shift = shift_ref[(blk * N_EXPERTS + e) * n_sub + j].astype(F32)
            target = jnp.where(in_sub, pos - float(j * MOE_WIN) + shift, -1.0)
            onehot = (col == target).astype(BF16)
            o_ref[...] += comb_ref[...][:, e:e + 1] * _mm(onehot, win_buf[e, j])

    if final:
        o_ref[...] = _rms(o_ref[...], fg_ref[...])


def _combine(tables, x, comb, pos, ys, final_gain):
    t, d = x.shape
    final = final_gain is not None
    gain = final_gain if final else jnp.ones((d,), F32)
    tb = MOE_TB
    lanes = pl.BlockSpec((tb, LANES), lambda i, *_: (i, 0))
    return pl.pallas_call(
        functools.partial(_combine_kernel, final),
        grid_spec=pltpu.PrefetchScalarGridSpec(
            num_scalar_prefetch=len(tables),
            grid=(t // tb,),
            in_specs=[pl.BlockSpec((tb, d), lambda i, *_: (i, 0)), lanes, lanes,
                      pl.BlockSpec(memory_space=pl.ANY), pl.BlockSpec((1, d), lambda i, *_: (0, 0))],
            out_specs=pl.BlockSpec((tb, d), lambda i, *_: (i, 0)),
            scratch_shapes=[pltpu.VMEM((N_EXPERTS, tb // MOE_WIN, MOE_WIN, d), BF16),
                            pltpu.SemaphoreType.DMA((N_EXPERTS, tb // MOE_WIN))]),
        out_shape=jax.ShapeDtypeStruct((t, d), F32),
        compiler_params=_params("arbitrary"),
        name="moe_combine",
    )(*tables, x, comb, pos, ys, gain.reshape(1, d))


def _moe(x, gain, w_router, wg, wu, wd, final_gain):
    t, d = x.shape
    nb = t // MOE_TB
    n_tiles_max = (2 * t + nb * N_EXPERTS * (ROW_ALIGN - 1)) // MOE_TILE + N_EXPERTS
    hn, comb, pos, pos_t, cnt = _router(x, gain, w_router)
    write_tables, read_tables, (tile_e, tile_blk, valid) = _route_tables(cnt[:, 0, :N_EXPERTS], n_tiles_max)
    n_rows = (n_tiles_max + SPARE_TILES * N_EXPERTS) * MOE_TILE
    xs = _dispatch(write_tables, hn, pos_t, n_rows)
    mid = _gate_up(tile_e, tile_blk, valid, xs, wg, wu)
    ys = _down(tile_e, valid, mid, wd)
    return _combine(read_tables, x, comb, pos, ys, final_gain)


def _final_norm_kernel(x_ref, g_ref, o_ref):
    o_ref[...] = _rms(x_ref[...], g_ref[...])


def _final_norm(x, gain):
    t, d = x.shape
    tm = TM_NORM
    return pl.pallas_call(
        _final_norm_kernel,
        grid=(t // tm,),
        in_specs=[pl.BlockSpec((tm, d), lambda i: (i, 0)), _full((1, d))],
        out_specs=pl.BlockSpec((tm, d), lambda i: (i, 0)),
        out_shape=jax.ShapeDtypeStruct((t, d), F32),
        compiler_params=_params("parallel"),
        name="final_norm",
    )(x, gain.reshape(1, d))


def _hm_groups_spec(n, first, width=HEAD):
    def spec(b, h):
        return pl.BlockSpec((n, b, h, CHUNK, width), lambda c: (first // n, 0, 0, c, 0))
    assert first % n == 0
    return spec


def _tok_chunk_spec(b):
    return pl.BlockSpec((b, CHUNK, GROUP), lambda c: (0, c, 0))


def _expand_pair(x):
    first = lax.broadcasted_iota(jnp.int32, x.shape, 2) < HEAD
    return jnp.concatenate([jnp.where(first, x, 0.0), jnp.where(first, 0.0, x)], axis=1)


def _same_head(n):
    t, s = _tri_masks(n)
    return (t >= n // 2) == (s >= n // 2)


def _hgrn_kernel(qfig_ref, loglb_ref, log1mlb_ref, onemlb_ref, ng_ref, o_ref, st_ref, rows_ref):
    @pl.when(pl.program_id(0) == 0)
    def _():
        st_ref[...] = jnp.zeros_like(st_ref)

    g2 = st_ref.shape[0]
    shape = (g2, CHUNK, PAIR)
    q, fl, v, g = (qfig_ref[j].reshape(shape) for j in range(4))
    log_sig = jnp.minimum(fl, 0.0) - jnp.log(1.0 + jnp.exp(-jnp.abs(fl)))
    la = loglb_ref[...]
    lc = log1mlb_ref[...] + log_sig
    log_f = jnp.maximum(la, lc) + jnp.log(1.0 + jnp.exp(-jnp.abs(la - lc)))
    k = onemlb_ref[...] * jax.nn.sigmoid(-fl)
    b = _cumsum_rows(log_f) * LOG2_E
    head_sum = jnp.broadcast_to(_same_head(PAIR).astype(BF16), (g2, PAIR, PAIR))

    rows_ref[0] = b
    rows_ref[1] = k
    rows_ref[2] = v
    sub = HGRN_SUB
    a_rows = [jnp.zeros((g2, sub, 2 * CHUNK), F32)]
    for i in range(1, CHUNK // sub):
        bref = rows_ref[0, :, i * sub - 1:i * sub, :]
        qd = q[:, i * sub:(i + 1) * sub] * jnp.exp2(b[:, i * sub:(i + 1) * sub] - bref)
        kd = k[:, :i * sub] * jnp.exp2(bref - b[:, :i * sub])
        later = jnp.zeros((g2, CHUNK - i * sub, PAIR), F32)
        a_rows.append(_bmm(qd, _expand_pair(jnp.concatenate([kd, later], axis=1)), nt=True))
    o = _bmm(jnp.concatenate(a_rows, axis=1), _expand_pair(v))

    t_loc = lax.broadcasted_iota(jnp.int32, (g2, sub, PAIR), 1)
    diag = []
    for i in range(CHUNK // sub):
        sl = slice(i * sub, (i + 1) * sub)
        bb, qq = b[:, sl], q[:, sl]
        terms = []
        for s in range(sub):
            row = i * sub + s
            e = jnp.exp2(jnp.where(t_loc >= s, bb - rows_ref[0, :, row:row + 1, :], -BIG_EXPONENT))
            terms.append(e * qq * rows_ref[1, :, row:row + 1, :])
        w = _bmm(jnp.concatenate(terms, axis=1), head_sum)
        acc = jnp.zeros((g2, sub, PAIR), F32)
        for s in range(sub):
            acc = acc + w[:, s * sub:(s + 1) * sub] * rows_ref[2, :, i * sub + s:i * sub + s + 1, :]
        diag.append(acc)
    o = o + jnp.concatenate(diag, axis=1)

    st = st_ref[...]
    o = o + _bmm(q * jnp.exp2(b), st, nt=True)
    b_last = b[:, CHUNK - 1:CHUNK]
    update = _bmm(jnp.swapaxes(v, 1, 2), k * jnp.exp2(b_last - b))
    st_ref[...] = st * jnp.exp2(b_last) + jnp.where(_same_head(PAIR), update, 0.0)

    mean_sq = _bmm(o * o, head_sum) * (1.0 / HEAD)
    _store_tokens(o_ref, o * lax.rsqrt(mean_sq + EPS) * ng_ref[...] * _silu(g))


def _gdn_kernel(qkv_ref, z_ref, gates_ref, cw_ref, alog_ref, dtb_ref, ng_ref, o_ref, xbuf_ref, st_ref):
    g3 = xbuf_ref.shape[0]
    g = g3 // 3

    @pl.when(pl.program_id(0) == 0)
    def _():
        st_ref[...] = jnp.zeros_like(st_ref)
        xbuf_ref[:, 0:HALO, :] = jnp.zeros((g3, HALO, HEAD), F32)

    xbuf_ref[:, HALO:HALO + CHUNK, :] = qkv_ref[...].reshape(g3, CHUNK, HEAD)
    conv = jnp.zeros((g3, CHUNK, HEAD), F32)
    for j in range(CONV_K):
        conv = conv + cw_ref[j] * xbuf_ref[:, HALO - (CONV_K - 1) + j:HALO - (CONV_K - 1) + j + CHUNK, :]
    xbuf_ref[:, 0:HALO, :] = xbuf_ref[:, CHUNK:CHUNK + HALO, :]
    qkv = _silu(conv)
    q, k, v = qkv[0:g], qkv[g:2 * g], qkv[2 * g:3 * g]
    q = q * lax.rsqrt(jnp.sum(q * q, axis=-1, keepdims=True) + EPS) * HEAD ** -0.5
    k = k * lax.rsqrt(jnp.sum(k * k, axis=-1, keepdims=True) + EPS)

    t_idx, s_idx = _tri_masks(CHUNK)
    causal = s_idx <= t_idx
    eye = s_idx == t_idx
    g_row = -jnp.exp(alog_ref[...]) * _softplus(gates_ref[0, 0] + dtb_ref[...])
    g_col = jnp.sum(jnp.where(eye, g_row, 0.0), axis=-1, keepdims=True)
    beta = jnp.sum(jnp.where(eye, jax.nn.sigmoid(gates_ref[1, 0]), 0.0), axis=-1, keepdims=True)
    gc_col = jnp.sum(jnp.where(causal, g_row, 0.0), axis=-1, keepdims=True)
    gc_row = jnp.sum(jnp.where(t_idx <= s_idx, g_col, 0.0), axis=1, keepdims=True)
    rel = jnp.where(causal, jnp.exp(jnp.minimum(gc_col - gc_row, 0.0)), 0.0)

    kb = k * beta
    with_k = _bmm(jnp.concatenate([kb, q], axis=1), k, nt=True)
    lower = jnp.where(s_idx < t_idx, with_k[:, :CHUNK] * rel, 0.0)
    uw = _solve_unit_lower(lower, jnp.concatenate([v * beta, kb * jnp.exp(gc_col)], axis=-1))
    u, w = uw[:, :, :HEAD], uw[:, :, HEAD:]
    scores = with_k[:, CHUNK:] * rel

    st = st_ref[...]
    from_state = _bmm(jnp.concatenate([w, q * jnp.exp(gc_col)], axis=1), st)
    v_new = u - from_state[:, :CHUNK]
    o = from_state[:, CHUNK:] + _bmm(scores, v_new)
    g_last = gc_col[:, CHUNK - 1:CHUNK]
    kd = k * jnp.exp(g_last - gc_col)
    st_ref[...] = st * jnp.exp(g_last) + _bmm(jnp.swapaxes(kd, 1, 2), v_new)

    _store_tokens(o_ref, _rms(o, ng_ref[...]) * _silu(z_ref[0].reshape(g, CHUNK, HEAD)))


def _lru_kernel(x_ref, gate_ref, cw_ref, cb_ref, wa_ref, ba_ref, wx_ref, bx_ref, lam_ref, o_ref,
                xbuf_ref, h_ref):
    tt = x_ref.shape[1]

    @pl.when(pl.program_id(1) == 0)
    def _():
        h_ref[...] = jnp.zeros_like(h_ref)
        xbuf_ref[0:HALO, :] = jnp.zeros((HALO, GROUP), F32)

    xbuf_ref[HALO:HALO + tt, :] = x_ref[0]
    xc = jnp.zeros((tt, GROUP), F32) + cb_ref[...]
    for j in range(CONV_K):
        xc = xc + cw_ref[j:j + 1, :] * xbuf_ref[HALO - (CONV_K - 1) + j:HALO - (CONV_K - 1) + j + tt, :]
    xbuf_ref[0:HALO, :] = xbuf_ref[tt:tt + HALO, :]

    r = jax.nn.sigmoid(_mm16(xc, wa_ref[...]) + ba_ref[...])
    i = jax.nn.sigmoid(_mm16(xc, wx_ref[...]) + bx_ref[...])
    log_a = -LRU_C * r * _softplus(-lam_ref[...])
    a = jnp.exp(log_a)
    u = jnp.sqrt(1.0 - a * a) * (i * xc)

    row = lax.broadcasted_iota(jnp.int32, (tt, GROUP), 0)
    shift = 1
    while shift < tt:
        keep = row >= shift
        a_prev = pltpu.roll(a, shift, 0)
        u_prev = pltpu.roll(u, shift, 0)
        u = jnp.where(keep, a * u_prev + u, u)
        a = jnp.where(keep, a * a_prev, a)
        shift *= 2
    hs = u + a * h_ref[...]
    h_ref[...] = hs[tt - 1:tt, :]

    gate = gate_ref[0]
    gelu = 0.5 * gate * (1.0 + jnp.tanh(math.sqrt(2.0 / math.pi) * (gate + 0.044715 * gate * gate * gate)))
    o_ref[0] = hs * gelu


def _lru_mixer(tok, conv_w, conv_b, wa, ba, wx, bx, lam):
    b, s, _ = tok.shape
    c = GROUP
    tt = TT_SCAN
    x_spec = pl.BlockSpec((1, tt, c), lambda i, t: (i, t, TOK_LRU // GROUP))
    gate_spec = pl.BlockSpec((1, tt, c), lambda i, t: (i, t, TOK_LRU // GROUP + 1))
    vec = _full((1, c))
    return pl.pallas_call(
        _lru_kernel,
        grid=(b, s // tt),
        in_specs=[x_spec, gate_spec, _full((CONV_K, c)), vec, _full((c, c)), vec, _full((c, c)), vec, vec],
        out_specs=pl.BlockSpec((1, tt, c), lambda i, t: (i, t, 0)),
        out_shape=jax.ShapeDtypeStruct((b, s, c), F32),
        scratch_shapes=[pltpu.VMEM((HALO + tt, c), F32), pltpu.VMEM((1, c), F32)],
        compiler_params=_params("parallel", "arbitrary"),
        name="rg_lru",
    )(tok, tok, conv_w, conv_b.reshape(1, c), wa, ba.reshape(1, c), wx, bx.reshape(1, c), lam.reshape(1, c))


def _rwkv_prep_kernel(has_vmix, keep_v, *refs):
    refs = list(refs)
    p_ref, mu_ref, w0_ref, w2_ref, a0_ref, a2_ref, g2_ref = refs[:7]
    del refs[:7]
    if has_vmix:
        vf_ref, v0_ref, v1_ref, v2_ref = refs[:4]
        del refs[:4]
    out_ref = refs.pop(0)
    vtok_ref = refs.pop(0) if keep_v else None
    xbuf_ref = refs.pop(0)
    tt = p_ref.shape[1]

    @pl.when(pl.program_id(1) == 0)
    def _():
        xbuf_ref[0:HALO, :] = jnp.zeros((HALO, xbuf_ref.shape[1]), F32)

    c = p_ref[0]
    xbuf_ref[HALO:HALO + tt, :] = c
    prev = xbuf_ref[HALO - 1:HALO - 1 + tt, :]
    xbuf_ref[0:HALO, :] = xbuf_ref[tt:tt + HALO, :]
    c = c + mu_ref[...] * (prev - c)

    r = c[:, 0:GROUP]
    k = c[:, GROUP:2 * GROUP]
    v = c[:, 2 * GROUP:3 * GROUP]
    wa_lo = c[:, 3 * GROUP:3 * GROUP + LANES]
    g_lo = c[:, 3 * GROUP + LANES:4 * GROUP]
    log_w = -math.exp(-0.5) * jax.nn.sigmoid(w0_ref[...] + _mm16(jnp.tanh(wa_lo), w2_ref[...]))
    a = jax.nn.sigmoid(a0_ref[...] + _mm16(wa_lo, a2_ref[...]))
    g = _mm16(jax.nn.sigmoid(g_lo), g2_ref[...])
    if has_vmix:
        mix = jax.nn.sigmoid(v0_ref[...] + _mm16(_mm16(v, v1_ref[...]), v2_ref[...]))
        v = v + (vf_ref[0] - v) * mix
    if keep_v:
        vtok_ref[0] = v
    for j, val in enumerate((r, log_w, k, v, a, g)):
        _store_heads(out_ref, val, (j, 0))


def _rwkv_prep(tok, mu, w0, w2, a0, a2, g2, v_first, v_mix):
    b, s, _ = tok.shape
    n = 4 * GROUP
    tt = TT_SCAN
    lora = w2.shape[0]
    w2p = jnp.zeros((LANES, GROUP), F32).at[:lora].set(w2)
    a2p = jnp.zeros((LANES, GROUP), F32).at[lora:].set(a2)
    tile = pl.BlockSpec((1, tt, GROUP), lambda i, t: (i, t, 0))
    vec = _full((1, GROUP))
    args = [tok, mu.reshape(1, n), w0.reshape(1, GROUP), w2p, a0.reshape(1, GROUP), a2p, g2]
    specs = [pl.BlockSpec((1, tt, n), lambda i, t: (i, t, TOK_RWKV // n)), _full((1, n)), vec,
             _full((LANES, GROUP)), vec, _full((LANES, GROUP)), _full(g2.shape)]
    keep_v = v_mix is None
    if v_mix is not None:
        v0, v1, v2 = v_mix
        rank = v1.shape[1]
        v1p = jnp.zeros((GROUP, LANES), F32).at[:, :rank].set(v1)
        v2p = jnp.zeros((LANES, GROUP), F32).at[:rank].set(v2)
        args += [v_first, v0.reshape(1, GROUP), v1p, v2p]
        specs += [tile, vec, _full((GROUP, LANES)), _full((LANES, GROUP))]
    out_specs = [pl.BlockSpec((6, 1, HEADS, tt, HEAD), lambda i, t: (0, i, 0, t, 0))]
    out_shape = [jax.ShapeDtypeStruct((6, b, HEADS, s, HEAD), F32)]
    if keep_v:
        out_specs.append(tile)
        out_shape.append(jax.ShapeDtypeStruct((b, s, GROUP), F32))
    outs = pl.pallas_call(
        functools.partial(_rwkv_prep_kernel, v_mix is not None, keep_v),
        grid=(b, s // tt),
        in_specs=specs,
        out_specs=out_specs,
        out_shape=out_shape,
        scratch_shapes=[pltpu.VMEM((HALO + tt, n), F32)],
        compiler_params=_params("parallel", "arbitrary"),
        name="rwkv7_prep",
    )(*args)
    return (outs[0], outs[1]) if keep_v else (outs[0], v_first)


def _rwkv_kernel(in_ref, kk_ref, ka_ref, rk_ref, lnw_ref, lnb_ref, o_ref, zt_ref):
    @pl.when(pl.program_id(0) == 0)
    def _():
        zt_ref[...] = jnp.zeros_like(zt_ref)

    shape = zt_ref.shape[:1] + (CHUNK, HEAD)
    r, lw, k, v, a, g = (in_ref[j].reshape(shape) for j in range(6))
    kk = k * kk_ref[...]
    kk = kk * lax.rsqrt(jnp.sum(kk * kk, axis=-1, keepdims=True) + EPS)
    k = k * (1.0 + (a - 1.0) * ka_ref[...])
    b = _cumsum_rows(lw)
    e_pos = jnp.exp(b)
    e_neg = jnp.exp(-b)
    r_t = r * e_pos
    al_t = -kk * jnp.exp(b - lw)
    be_t = kk * a * e_neg
    k_t = k * e_neg

    t_idx, s_idx = _tri_masks(CHUNK)
    strict = s_idx < t_idx
    ab_k = jnp.concatenate([be_t, k_t], axis=1)
    both = _bmm(jnp.concatenate([al_t, r_t], axis=1), ab_k, nt=True)
    l_ab = jnp.where(strict, both[:, :CHUNK, :CHUNK], 0.0)
    l_ak = jnp.where(strict, both[:, :CHUNK, CHUNK:], 0.0)
    t2 = lax.broadcasted_iota(jnp.int32, (CHUNK, 2 * CHUNK), 0)
    s2 = lax.broadcasted_iota(jnp.int32, (CHUNK, 2 * CHUNK), 1)
    incl2 = jnp.where(s2 >= CHUNK, s2 - CHUNK, s2) <= t2
    m_all = jnp.where(incl2, both[:, CHUNK:], 0.0)

    sol = _solve_unit_lower(-l_ab, jnp.concatenate([al_t, _bmm(l_ak, v)], axis=-1))
    w1, u0 = sol[:, :, :HEAD], sol[:, :, HEAD:]

    zt = zt_ref[...]
    from_state = _bmm(jnp.concatenate([w1, r_t], axis=1), zt, nt=True)
    u = from_state[:, :CHUNK] + u0
    uv = jnp.concatenate([u, v], axis=1)
    y = from_state[:, CHUNK:] + _bmm(m_all, uv)
    b_last = b[:, CHUNK - 1:CHUNK]
    e_last = jnp.exp(b_last - b)
    zt_ref[...] = zt * jnp.exp(b_last) + _bmm(jnp.swapaxes(uv, 1, 2),
                                              jnp.concatenate([kk * a * e_last, k * e_last], axis=1))

    mean = jnp.mean(y, axis=-1, keepdims=True)
    var = jnp.mean(jnp.square(y - mean), axis=-1, keepdims=True)
    yn = (y - mean) * lax.rsqrt(var + RWKV_GN_EPS) * lnw_ref[...] + lnb_ref[...]
    bonus = jnp.sum(r * k * rk_ref[...], axis=-1, keepdims=True) * v
    _store_tokens(o_ref, (yn + bonus) * g)


N_HGRN_IN, N_GDN_IN, N_RWKV_IN = 5, 7, 6


def _chunk_mixers_kernel(*refs):
    ins, rest = refs[:N_HGRN_IN + N_GDN_IN + N_RWKV_IN], refs[N_HGRN_IN + N_GDN_IN + N_RWKV_IN:]
    o_hgrn, o_gdn, o_rwkv, st_hgrn, rows_hgrn, xbuf_gdn, st_gdn, zt_rwkv = rest
    _hgrn_kernel(*ins[:N_HGRN_IN], o_hgrn, st_hgrn, rows_hgrn)
    _gdn_kernel(*ins[N_HGRN_IN:N_HGRN_IN + N_GDN_IN], o_gdn, xbuf_gdn, st_gdn)
    _rwkv_kernel(*ins[N_HGRN_IN + N_GDN_IN:], o_rwkv, zt_rwkv)


def _chunk_mixers(hmp, hm, d_hm, hgrn_args, gdn_args, rwkv_args):
    _, b, h, s, d = hm.shape
    g = b * h
    g2 = g // 2
    par = _full((g, 1, d))
    lb, hgrn_norm = hgrn_args
    gates, conv_w, a_log, dt_bias, gdn_norm = gdn_args
    hgrn_specs = [_hm_groups_spec(HMP_GROUPS, 0, PAIR)(b, h // 2)] + [_full((g2, 1, PAIR))] * 4
    gdn_specs = [_hm_groups_spec(3, HM_GDN_QKV)(b, h), _hm_groups_spec(1, HM_GDN_Z)(b, h),
                 pl.BlockSpec((2, 1, g, 1, CHUNK), lambda c: (0, c, 0, 0, 0)),
                 _full((CONV_K, 3 * g, 1, d)), _full((g, 1, 1)), _full((g, 1, 1)), _full((1, 1, d))]
    rwkv_specs = [_hm_groups_spec(6, 0)(b, h)] + [par] * 5
    assert (len(hgrn_specs), len(gdn_specs), len(rwkv_specs)) == (N_HGRN_IN, N_GDN_IN, N_RWKV_IN)
    out = jax.ShapeDtypeStruct((b, s, h * d), F32)
    state = pltpu.VMEM((g, d, d), F32)
    return pl.pallas_call(
        _chunk_mixers_kernel,
        grid=(s // CHUNK,),
        in_specs=hgrn_specs + gdn_specs + rwkv_specs,
        out_specs=[_tok_chunk_spec(b)] * 3,
        out_shape=[out] * 3,
        scratch_shapes=[pltpu.VMEM((g2, PAIR, PAIR), F32), pltpu.VMEM((3, g2, CHUNK, PAIR), F32),
                        pltpu.VMEM((3 * g, HALO + CHUNK, d), F32), state, state],
        compiler_params=_params("arbitrary"),
        name="chunk_mixers",
    )(hmp, jnp.log(lb), jnp.log1p(-lb), 1.0 - lb, hgrn_norm,
      hm, hm, gates, conv_w, a_log, dt_bias, gdn_norm,
      d_hm, *rwkv_args)


def _par_hm(p, b, width=HEAD):
    return jnp.tile(p.reshape(-1, 1, width), (b, 1, 1))


def _gdn_gate_layout(gates):
    b, s, h2 = gates.shape
    g = gates.reshape(b, s // CHUNK, CHUNK, 2, h2 // 2).transpose(3, 1, 0, 4, 2)
    return g.reshape(2, s // CHUNK, b * (h2 // 2), 1, CHUNK)


def _gdn_conv_layout(w, b):
    k, c = w.shape
    h = c // (3 * HEAD)
    w = jnp.broadcast_to(w.reshape(k, 3, 1, h, 1, HEAD), (k, 3, b, h, 1, HEAD))
    return w.reshape(k, 3 * b * h, 1, HEAD)


def _block_diag(w):
    n, a, b = w.shape
    out = jnp.zeros((n * a, n * b), w.dtype)
    for i in range(n):
        out = out.at[i * a:(i + 1) * a, i * b:(i + 1) * b].set(w[i])
    return out


def _in_proj_weight(w):
    d = w.shape[0]
    g4 = 4 * GROUP
    gates = g4 + g4
    lru = gates + 2 * HEADS
    rwkv = lru + 2 * GROUP
    cols = [w[:, rwkv:rwkv + g4], w[:, lru:rwkv], w[:, gates:lru], jnp.zeros((d, LANES - 2 * HEADS), w.dtype),
            w[:, 0:g4], w[:, g4:gates]]
    return jnp.concatenate(cols, axis=1).astype(BF16)


def kernel(x, mem, norm_mix, w_in, w_out, hgrn_lb_logits, hgrn_norm, gdn_conv_w, gdn_A_log, gdn_dt_bias, gdn_norm, lru_conv_w, lru_conv_b, lru_wa, lru_ba, lru_wx, lru_bx, lru_lambda, rwkv_mu, rwkv_w0, rwkv_w2, rwkv_a0, rwkv_a2, rwkv_g2, rwkv_k_k, rwkv_k_a, rwkv_r_k, rwkv_ln_w, rwkv_ln_b, rwkv_v0, rwkv_v1, rwkv_v2, mem_norm, norm_xattn, xattn_wq, xattn_wk, xattn_wv, xattn_wo, norm_ffn, ffn_w_gate, ffn_w_up, ffn_w_down, moe_router, moe_w_gate, moe_w_up, moe_w_down, norm_final):
    bsz, seq, d = x.shape
    depth = w_in.shape[0]
    tok = bsz * seq
    mlen = mem.shape[1]

    lb = jnp.cumsum(jax.nn.softmax(hgrn_lb_logits.astype(F32), axis=0), axis=0)
    lb = lb - lb[0]
    mem_f = mem.reshape(bsz * mlen, d)
    v_first = None
    for l in range(depth):
        p_tok, p_hmp, p_hm = _in_proj(x, norm_mix[l], _in_proj_weight(w_in[l]))

        y_c = _lru_mixer(p_tok, lru_conv_w[l], lru_conv_b[l], _block_diag(lru_wa[l]), lru_ba[l],
                         _block_diag(lru_wx[l]), lru_bx[l], lru_lambda[l])

        v_mix = None if l == 0 else (rwkv_v0[l - 1], rwkv_v1[l - 1], rwkv_v2[l - 1])
        d_hm, v_first = _rwkv_prep(p_tok, rwkv_mu[l], rwkv_w0[l], rwkv_w2[l], rwkv_a0[l], rwkv_a2[l],
                                   rwkv_g2[l], v_first, v_mix)

        gates = p_tok[..., TOK_GATES:TOK_GATES + 2 * HEADS]
        y_a, y_b, y_d = _chunk_mixers(
            p_hmp, p_hm, d_hm,
            (_par_hm(lb[l], bsz, PAIR), _par_hm(hgrn_norm[l], bsz, PAIR)),
            (_gdn_gate_layout(gates), _gdn_conv_layout(gdn_conv_w[l], bsz), _par_hm(gdn_A_log[l], bsz, 1),
             _par_hm(gdn_dt_bias[l], bsz, 1), gdn_norm[l].reshape(1, 1, HEAD)),
            tuple(_par_hm(p[l], bsz) for p in (rwkv_k_k, rwkv_k_a, rwkv_r_k, rwkv_ln_w, rwkv_ln_b)))

        w_kv = jnp.concatenate([xattn_wk[l], xattn_wv[l]], axis=1).astype(BF16)
        kv = _norm_matmul(mem_f, mem_norm, w_kv, mlen, BF16).reshape(bsz, mlen, 2 * d)
        x = _mix_attention((y_a, y_b, y_c, y_d), w_out[l].reshape(4, GROUP, d).astype(BF16), x, norm_xattn[l],
                           xattn_wq[l].astype(BF16), kv, xattn_wo[l].astype(BF16))

        xf = x.reshape(tok, d)
        j = l // 2
        if l % 2 == 0:
            ones = jnp.ones((1, tok, 1), F32)
            xf = _ffn(xf, norm_ffn[l], ffn_w_gate[j][None].astype(BF16), ffn_w_up[j][None].astype(BF16),
                      ffn_w_down[j][None].astype(BF16), ones, TF_FFN)
        else:
            xf = _moe(xf, norm_ffn[l], moe_router[j], moe_w_gate[j].astype(BF16), moe_w_up[j].astype(BF16),
                      moe_w_down[j].astype(BF16), norm_final if l == depth - 1 else None)
        x = xf.reshape(bsz, seq, d)
    if depth % 2 == 1:
        x = _final_norm(x.reshape(tok, d), norm_final).reshape(bsz, seq, d)
    return x
```

```python
import functools
import math

import jax
import jax.numpy as jnp
from jax import lax
from jax.experimental import pallas as pl
from jax.experimental.pallas import tpu as pltpu

F32 = jnp.float32
BF16 = jnp.bfloat16
HIGHEST = lax.Precision.HIGHEST

GROUP = 256
HEAD = 64
HEADS = GROUP // HEAD
CHUNK = 64
SUB = 16
HGRN_SUB = 8
LOG2_E = 1.4426950408889634
BIG_EXPONENT = 1e30
CONV_K = 4
HALO = 8
LRU_C = 8.0
RWKV_GN_EPS = 64e-5
XATTN_HEADS = 4
N_EXPERTS = 8
EPS = 1e-6
LANES = 128
VMEM_LIMIT_BYTES = 56 * 1024 * 1024

TOK_LRU = 0
TOK_GATES = 2 * GROUP
N_TOK = 2 * GROUP + LANES
PAIR = 2 * HEAD
HMP_GROUPS = 4
HM_GDN_QKV, HM_GDN_Z = 0, 3
HM_GROUPS = 4

TM_IN_PROJ = 512
TM_XATTN = 512
TM_FFN = 1024
TF_FFN = 1408
TM_NORM = 1024
TT_SCAN = 512
MOE_TB = 512
MOE_TILE = 512
MOE_F_SPLIT = 2
ROW_ALIGN = 16
SPARE_TILES = 2
MOE_SEG = 128
MOE_WIN = 256


def _params(*semantics):
    return pltpu.CompilerParams(dimension_semantics=semantics, vmem_limit_bytes=VMEM_LIMIT_BYTES)


def _full(shape):
    n = len(shape)
    return pl.BlockSpec(shape, lambda *_: (0,) * n)


def _bmm(a, b, nt=False, exact=False):
    dn = (((2,), (2 if nt else 1,)), ((0,), (0,)))
    if exact:
        return lax.dot_general(a, b, dn, precision=HIGHEST, preferred_element_type=F32)
    return lax.dot_general(a.astype(BF16), b.astype(BF16), dn, preferred_element_type=F32)


def _mm(a, b, precision=None):
    return jnp.dot(a, b, precision=precision, preferred_element_type=F32)


def _mm16(a, b):
    return jnp.dot(a.astype(BF16), b.astype(BF16), preferred_element_type=F32)


def _softplus(x):
    return jnp.maximum(x, 0.0) + jnp.log1p(jnp.exp(-jnp.abs(x)))


def _silu(x):
    return x * jax.nn.sigmoid(x)


def _rms(x, gain):
    return x * lax.rsqrt(jnp.mean(x * x, axis=-1, keepdims=True) + EPS) * gain


def _tri_masks(n):
    t = lax.broadcasted_iota(jnp.int32, (n, n), 0)
    s = lax.broadcasted_iota(jnp.int32, (n, n), 1)
    return t, s


def _cumsum_rows(x):
    g, c, _ = x.shape
    t, s = _tri_masks(c)
    tri = jnp.broadcast_to((s <= t).astype(BF16), (g, c, c))
    hi = x.astype(BF16)
    rest = x - hi.astype(F32)
    mid = rest.astype(BF16)
    lo = (rest - mid.astype(F32)).astype(BF16)
    return _bmm(tri, hi) + _bmm(tri, mid) + _bmm(tri, lo)


def _solve_unit_lower(low, rhs):
    c = low.shape[1]
    r = rhs.shape[2]
    assert c == 4 * SUB and SUB == 16
    t, s = _tri_masks(c)
    same_block = (t // SUB) == (s // SUB)
    d1 = jnp.where(same_block, low, 0.0)
    z = jnp.concatenate([rhs, jnp.where(same_block, 0.0, low)], axis=-1)
    d2 = _bmm(d1, d1)
    d4 = _bmm(d2, d2)
    d8 = _bmm(d4, d4)
    z = z + _bmm(d8, z)
    z = z + _bmm(d4, z)
    z = z + _bmm(d2, z)
    z = z - _bmm(d1, z)
    y = z[:, :, :r]
    n1 = z[:, :, r:]
    n2 = _bmm(n1, n1)
    y = y + _bmm(n2, y)
    return y - _bmm(n1, y)


def _store_heads(ref, val, lead=(), width=HEAD):
    for h in range(val.shape[1] // width):
        ref[lead + (h,)] = val[:, h * width:(h + 1) * width]


def _store_tokens(ref, o):
    nb = ref.shape[0]
    nh = o.shape[0] // nb
    for b in range(nb):
        ref[b] = jnp.concatenate([o[b * nh + h] for h in range(nh)], axis=-1)


def _in_proj_kernel(has_vmix, *refs):
    refs = list(refs)
    x_ref, g_ref, w_ref, mu_ref, w0_ref, w2_ref, a0_ref, a2_ref, g2_ref = refs[:9]
    del refs[:9]
    if has_vmix:
        vf_ref, v0_ref, v1_ref, v2_ref = refs[:4]
        del refs[:4]
    tok_ref, hmp_ref, hm_ref, dhm_ref = refs[:4]
    del refs[:4]
    vtok_ref = None if has_vmix else refs.pop(0)
    xbuf_ref = refs.pop(0)
    tm = x_ref.shape[1]

    hn = _rms(x_ref[0], g_ref[...]).astype(BF16)
    n_rwkv = 4 * GROUP
    c = _mm(hn, w_ref[:, :n_rwkv])
    acc = _mm(hn, w_ref[:, n_rwkv:])
    col = 0
    n_tok = tok_ref.shape[2]
    tok_ref[0] = acc[:, col:col + n_tok]
    col += n_tok
    for g in range(hmp_ref.shape[0]):
        _store_heads(hmp_ref, acc[:, col:col + GROUP], (g, 0), PAIR)
        col += GROUP
    for g in range(hm_ref.shape[0]):
        _store_heads(hm_ref, acc[:, col:col + GROUP], (g, 0))
        col += GROUP

    @pl.when(pl.program_id(1) == 0)
    def _():
        xbuf_ref[0:HALO, :] = jnp.zeros((HALO, xbuf_ref.shape[1]), F32)

    xbuf_ref[HALO:HALO + tm, :] = c
    prev = xbuf_ref[HALO - 1:HALO - 1 + tm, :]
    xbuf_ref[0:HALO, :] = xbuf_ref[tm:tm + HALO, :]
    c = c + mu_ref[...] * (prev - c)

    r = c[:, 0:GROUP]
    k = c[:, GROUP:2 * GROUP]
    v = c[:, 2 * GROUP:3 * GROUP]
    wa_lo = c[:, 3 * GROUP:3 * GROUP + LANES]
    g_lo = c[:, 3 * GROUP + LANES:4 * GROUP]
    log_w = -math.exp(-0.5) * jax.nn.sigmoid(w0_ref[...] + _mm16(jnp.tanh(wa_lo), w2_ref[...]))
    a = jax.nn.sigmoid(a0_ref[...] + _mm16(wa_lo, a2_ref[...]))
    gate = _mm16(jax.nn.sigmoid(g_lo), g2_ref[...])
    if has_vmix:
        mix = jax.nn.sigmoid(v0_ref[...] + _mm16(_mm16(v, v1_ref[...]), v2_ref[...]))
        v = v + (vf_ref[0] - v) * mix
    else:
        vtok_ref[0] = v
    for j, val in enumerate((r, log_w, k, v, a, gate)):
        _store_heads(dhm_ref, val, (j, 0))


def _in_proj(x, gain, w, mu, w0, w2, a0, a2, g2, v_first, v_mix):
    b, s, d = x.shape
    n = w.shape[1]
    tm = TM_IN_PROJ
    lora = w2.shape[0]
    w2p = jnp.zeros((LANES, GROUP), F32).at[:lora].set(w2)
    a2p = jnp.zeros((LANES, GROUP), F32).at[lora:].set(a2)
    tile = pl.BlockSpec((1, tm, GROUP), lambda i, j: (i, j, 0))
    vec = _full((1, GROUP))
    args = [x, gain.reshape(1, d), w, mu.reshape(1, 4 * GROUP), w0.reshape(1, GROUP), w2p, a0.reshape(1, GROUP),
            a2p, g2]
    specs = [pl.BlockSpec((1, tm, d), lambda i, j: (i, j, 0)), _full((1, d)), _full((d, n)), _full((1, 4 * GROUP)),
             vec, _full((LANES, GROUP)), vec, _full((LANES, GROUP)), _full(g2.shape)]
    out_specs = [pl.BlockSpec((1, tm, N_TOK), lambda i, j: (i, j, 0)),
                 pl.BlockSpec((HMP_GROUPS, 1, HEADS // 2, tm, PAIR), lambda i, j: (0, i, 0, j, 0)),
                 pl.BlockSpec((HM_GROUPS, 1, HEADS, tm, HEAD), lambda i, j: (0, i, 0, j, 0)),
                 pl.BlockSpec((6, 1, HEADS, tm, HEAD), lambda i, j: (0, i, 0, j, 0))]
    out_shape = [jax.ShapeDtypeStruct((b, s, N_TOK), F32),
                 jax.ShapeDtypeStruct((HMP_GROUPS, b, HEADS // 2, s, PAIR), F32),
                 jax.ShapeDtypeStruct((HM_GROUPS, b, HEADS, s, HEAD), F32),
                 jax.ShapeDtypeStruct((6, b, HEADS, s, HEAD), F32)]
    if v_mix is not None:
        v0, v1, v2 = v_mix
        rank = v1.shape[1]
        v1p = jnp.zeros((GROUP, LANES), F32).at[:, :rank].set(v1)
        v2p = jnp.zeros((LANES, GROUP), F32).at[:rank].set(v2)
        args += [v_first, v0.reshape(1, GROUP), v1p, v2p]
        specs += [tile, vec, _full((GROUP, LANES)), _full((LANES, GROUP))]
    else:
        out_specs.append(tile)
        out_shape.append(jax.ShapeDtypeStruct((b, s, GROUP), F32))
    outs = pl.pallas_call(
        functools.partial(_in_proj_kernel, v_mix is not None),
        grid=(b, s // tm),
        in_specs=specs,
        out_specs=out_specs,
        out_shape=out_shape,
        scratch_shapes=[pltpu.VMEM((HALO + tm, 4 * GROUP), F32)],
        compiler_params=_params("parallel", "arbitrary"),
        name="in_proj",
    )(*args)
    return tuple(outs[:4]) + ((v_first,) if v_mix is not None else (outs[4],))


def _norm_mm_kernel(x_ref, g_ref, w_ref, o_ref):
    o_ref[...] = _mm(_rms(x_ref[...], g_ref[...]).astype(BF16), w_ref[...]).astype(o_ref.dtype)


def _norm_matmul(x, gain, w, tm, out_dtype):
    t, k = x.shape
    n = w.shape[1]
    return pl.pallas_call(
        _norm_mm_kernel,
        grid=(t // tm,),
        in_specs=[pl.BlockSpec((tm, k), lambda i: (i, 0)), _full((1, k)), _full((k, n))],
        out_specs=pl.BlockSpec((tm, n), lambda i: (i, 0)),
        out_shape=jax.ShapeDtypeStruct((t, n), out_dtype),
        compiler_params=_params("parallel"),
        name="norm_matmul",
    )(x, gain.reshape(1, k), w)


def _mix_attn_kernel(ya_ref, yb_ref, yc_ref, yd_ref, wout_ref, x_ref, g_ref, wq_ref, k_ref, v_ref, wo_ref, o_ref):
    x = x_ref[0]
    for m, y_ref in enumerate((ya_ref, yb_ref, yc_ref, yd_ref)):
        x = x + _mm(y_ref[0].astype(BF16), wout_ref[m])
    hn = _rms(x, g_ref[...]).astype(BF16)
    q = _mm(hn, wq_ref[...])
    d = q.shape[1] // XATTN_HEADS
    outs = []
    for h in range(XATTN_HEADS):
        qh = (q[:, h * d:(h + 1) * d] * d ** -0.5).astype(BF16)
        kh = k_ref[0, :, h * d:(h + 1) * d]
        vh = v_ref[0, :, h * d:(h + 1) * d]
        s = lax.dot_general(qh, kh, (((1,), (1,)), ((), ())), preferred_element_type=F32)
        s = s - jnp.max(s, axis=-1, keepdims=True)
        p = jnp.exp(s)
        p = p / jnp.sum(p, axis=-1, keepdims=True)
        outs.append(_mm(p.astype(BF16), vh))
    o = jnp.concatenate(outs, axis=-1).astype(BF16)
    o_ref[0] = x + _mm(o, wo_ref[...])


def _mix_attention(ys, w_out, x, gain, wq, kv, wo):
    b, s, d = x.shape
    m = kv.shape[1]
    tm = TM_XATTN
    y_spec = pl.BlockSpec((1, tm, GROUP), lambda i, j: (i, j, 0))
    x_spec = pl.BlockSpec((1, tm, d), lambda i, j: (i, j, 0))
    return pl.pallas_call(
        _mix_attn_kernel,
        grid=(b, s // tm),
        in_specs=[y_spec] * 4 + [_full(w_out.shape), x_spec, _full((1, d)), _full((d, d)),
                                 pl.BlockSpec((1, m, d), lambda i, j: (i, 0, 0)),
                                 pl.BlockSpec((1, m, d), lambda i, j: (i, 0, 1)),
                                 _full((d, d))],
        out_specs=x_spec,
        out_shape=jax.ShapeDtypeStruct((b, s, d), F32),
        compiler_params=_params("parallel", "parallel"),
        name="mix_attention",
    )(*ys, w_out, x, gain.reshape(1, d), wq, kv, kv, wo)


def _ffn_kernel(x_ref, g_ref, wg_ref, wu_ref, wd_ref, c_ref, o_ref, hn_ref, acc_ref):
    f = pl.program_id(1)

    @pl.when(f == 0)
    def _():
        hn_ref[...] = _rms(x_ref[...], g_ref[...]).astype(BF16)
        acc_ref[...] = x_ref[...]

    hn = hn_ref[...]
    gate = _mm(hn, wg_ref[0])
    up = _mm(hn, wu_ref[0])
    mid = _silu(gate) * up * c_ref[0]
    acc_ref[...] += _mm(mid.astype(BF16), wd_ref[0])

    @pl.when(f == pl.num_programs(1) - 1)
    def _():
        o_ref[...] = acc_ref[...]


def _ffn(x, gain, wg, wu, wd, combine, tf):
    t, d = x.shape
    e, _, ff = wg.shape
    nf = ff // tf
    tm = TM_FFN
    return pl.pallas_call(
        _ffn_kernel,
        grid=(t // tm, e * nf),
        in_specs=[pl.BlockSpec((tm, d), lambda i, f: (i, 0)),
                  _full((1, d)),
                  pl.BlockSpec((1, d, tf), lambda i, f: (f // nf, 0, f % nf)),
                  pl.BlockSpec((1, d, tf), lambda i, f: (f // nf, 0, f % nf)),
                  pl.BlockSpec((1, tf, d), lambda i, f: (f // nf, f % nf, 0)),
                  pl.BlockSpec((1, tm, 1), lambda i, f: (f // nf, i, 0))],
        out_specs=pl.BlockSpec((tm, d), lambda i, f: (i, 0)),
        out_shape=jax.ShapeDtypeStruct((t, d), F32),
        scratch_shapes=[pltpu.VMEM((tm, d), BF16), pltpu.VMEM((tm, d), F32)],
        compiler_params=_params("parallel", "arbitrary"),
        name="ffn",
    )(x, gain.reshape(1, d), wg, wu, wd, combine)


def _router_kernel(x_ref, g_ref, w_ref, hn_ref, comb_ref, pos_ref, post_ref, cnt_ref):
    hn = _rms(x_ref[...], g_ref[...])
    hn_ref[...] = hn.astype(BF16)
    logits = _mm(hn, w_ref[...], precision=HIGHEST)
    lane = lax.broadcasted_iota(jnp.int32, logits.shape, 1)
    neg = jnp.float32(-jnp.inf)
    logits = jnp.where(lane < N_EXPERTS, logits, neg)
    m1 = jnp.max(logits, axis=-1, keepdims=True)
    i1 = jnp.min(jnp.where(logits == m1, lane, LANES), axis=-1, keepdims=True)
    rest = jnp.where(lane == i1, neg, logits)
    m2 = jnp.max(rest, axis=-1, keepdims=True)
    i2 = jnp.min(jnp.where(rest == m2, lane, LANES), axis=-1, keepdims=True)
    e2 = jnp.exp(m2 - m1)
    comb = jnp.where(lane == i1, 1.0 / (1.0 + e2), 0.0) + jnp.where(lane == i2, e2 / (1.0 + e2), 0.0)
    comb_ref[...] = comb

    sel = (comb > 0.0).astype(F32)
    tb = sel.shape[0]
    row = lax.broadcasted_iota(jnp.int32, sel.shape, 0)
    run = sel
    shift = 1
    while shift < tb:
        run = run + jnp.where(row >= shift, pltpu.roll(run, shift, 0), 0.0)
        shift *= 2
    pos = jnp.where(sel > 0.0, run - 1.0, -1.0)
    pos_ref[...] = pos
    post_ref[0] = jnp.transpose(pos)[:N_EXPERTS, :]
    cnt_ref[0] = run[tb - 1:tb, :].astype(jnp.int32)


def _router(x, gain, w_router):
    t, d = x.shape
    tb = MOE_TB
    nb = t // tb
    w = jnp.zeros((d, LANES), F32).at[:, :N_EXPERTS].set(w_router)
    tile = pl.BlockSpec((tb, LANES), lambda i: (i, 0))
    return pl.pallas_call(
        _router_kernel,
        grid=(nb,),
        in_specs=[pl.BlockSpec((tb, d), lambda i: (i, 0)), _full((1, d)), _full((d, LANES))],
        out_specs=[pl.BlockSpec((tb, d), lambda i: (i, 0)), tile, tile,
                   pl.BlockSpec((1, N_EXPERTS, tb), lambda i: (i, 0, 0)),
                   pl.BlockSpec((1, 1, LANES), lambda i: (i, 0, 0))],
        out_shape=[jax.ShapeDtypeStruct((t, d), BF16), jax.ShapeDtypeStruct((t, LANES), F32),
                   jax.ShapeDtypeStruct((t, LANES), F32), jax.ShapeDtypeStruct((nb, N_EXPERTS, tb), F32),
                   jax.ShapeDtypeStruct((nb, 1, LANES), jnp.int32)],
        compiler_params=_params("parallel"),
        name="router",
    )(x, gain.reshape(1, d), w)


def _route_tables(cnt, n_tiles_max):
    nb, ne = cnt.shape
    cpad = (cnt + (ROW_ALIGN - 1)) // ROW_ALIGN * ROW_ALIGN
    off = jnp.cumsum(cpad, axis=0) - cpad
    total = jnp.sum(cpad, axis=0)
    ntile = (total + MOE_TILE - 1) // MOE_TILE
    first = jnp.cumsum(ntile) - ntile
    n_tiles = jnp.sum(ntile)
    region = (first + SPARE_TILES * jnp.arange(ne, dtype=jnp.int32)) * MOE_TILE
    dst = region[None, :] + off
    tail = region + total
    sub = jnp.arange(MOE_TB // MOE_WIN, dtype=jnp.int32)[None, None, :] * MOE_WIN
    want = off[:, :, None] + sub
    win_rel = jnp.clip(want, 0, jnp.maximum(ntile * MOE_TILE - MOE_WIN, 0)[None, :, None])
    win = first[None, :, None] * MOE_TILE + win_rel
    shift = want - win_rel
    t = jnp.minimum(jnp.arange(n_tiles_max, dtype=jnp.int32), n_tiles - 1)
    tile_e = jnp.minimum(jnp.sum(t[:, None] >= jnp.cumsum(ntile)[None, :], axis=1), ne - 1).astype(jnp.int32)
    tile_blk = t + SPARE_TILES * tile_e
    valid = (jnp.arange(n_tiles_max, dtype=jnp.int32) < n_tiles).astype(jnp.int32)
    n_blocks = n_tiles_max + SPARE_TILES * ne
    blocks = jnp.arange(n_blocks, dtype=jnp.int32)
    unused = 1 - jnp.max((tile_blk[:, None] == blocks[None, :]) * valid[:, None], axis=0)
    rank = jnp.cumsum(unused) - 1
    fill = jnp.sum(jnp.where((rank[None, :] == blocks[:, None]) & (unused[None, :] > 0), blocks[None, :], 0), axis=1)
    n_fill = jnp.sum(unused).astype(jnp.int32).reshape(1)
    flat = lambda a: a.reshape(-1).astype(jnp.int32)
    return ((flat(cpad), flat(dst), flat(tail), flat(fill), n_fill), (flat(cnt), flat(win), flat(shift)),
            (tile_e, tile_blk.astype(jnp.int32), valid))


def _dispatch_kernel(cpad_ref, dst_ref, tail_ref, fill_ref, nfill_ref, hn_ref, post_ref, xs_ref, seg_ref, sem):
    blk = pl.program_id(0)
    tb = hn_ref.shape[0]
    assert tb == MOE_TILE
    pieces = [(e, j) for e in range(N_EXPERTS) for j in range(tb // MOE_SEG)]

    def needed(e, j):
        return cpad_ref[blk * N_EXPERTS + e] > j * MOE_SEG

    def piece_copy(e, j):
        row = pl.multiple_of(dst_ref[blk * N_EXPERTS + e] + j * MOE_SEG, ROW_ALIGN)
        return pltpu.make_async_copy(seg_ref.at[e, pl.ds(j * MOE_SEG, MOE_SEG)], xs_ref.at[pl.ds(row, MOE_SEG)],
                                     sem.at[e, j])

    rank = lax.broadcasted_iota(jnp.int32, (MOE_SEG, tb), 0).astype(F32)
    for e, j in pieces:
        @pl.when(needed(e, j))
        def _(e=e, j=j):
            onehot = (rank + float(j * MOE_SEG) == post_ref[0, e:e + 1, :]).astype(BF16)
            seg_ref[e, j * MOE_SEG:(j + 1) * MOE_SEG, :] = _mm(onehot, hn_ref[...]).astype(BF16)
            piece_copy(e, j).start()
    for e, j in pieces:
        @pl.when(needed(e, j))
        def _(e=e, j=j):
            piece_copy(e, j).wait()

    @pl.when(blk == pl.num_programs(0) - 1)
    def _():
        seg_ref[0] = jnp.zeros(seg_ref.shape[1:], BF16)

        def tail_copy(e):
            row = pl.multiple_of(tail_ref[e], ROW_ALIGN)
            return pltpu.make_async_copy(seg_ref.at[0], xs_ref.at[pl.ds(row, tb)], sem.at[e, 0])

        for e in range(N_EXPERTS):
            tail_copy(e).start()
        for e in range(N_EXPERTS):
            tail_copy(e).wait()

        def fill_block(k, carry):
            row = pl.multiple_of(fill_ref[k] * MOE_TILE, MOE_TILE)
            copy = pltpu.make_async_copy(seg_ref.at[0], xs_ref.at[pl.ds(row, tb)], sem.at[0, 0])
            copy.start()
            copy.wait()
            return carry

        lax.fori_loop(0, nfill_ref[0], fill_block, 0)


def _dispatch(tables, hn, pos_t, n_rows):
    t, d = hn.shape
    tb = MOE_TB
    return pl.pallas_call(
        _dispatch_kernel,
        grid_spec=pltpu.PrefetchScalarGridSpec(
            num_scalar_prefetch=len(tables),
            grid=(t // tb,),
            in_specs=[pl.BlockSpec((tb, d), lambda i, *_: (i, 0)),
                      pl.BlockSpec((1, N_EXPERTS, tb), lambda i, *_: (i, 0, 0))],
            out_specs=pl.BlockSpec(memory_space=pl.ANY),
            scratch_shapes=[pltpu.VMEM((N_EXPERTS, tb, d), BF16),
                            pltpu.SemaphoreType.DMA((N_EXPERTS, tb // MOE_SEG))]),
        out_shape=jax.ShapeDtypeStruct((n_rows, d), BF16),
        compiler_params=_params("arbitrary"),
        name="moe_dispatch",
    )(*tables, hn, pos_t)


def _gate_up_kernel(te_ref, tb_ref, ok_ref, x_ref, wg_ref, wu_ref, o_ref):
    t = pl.program_id(1)

    @pl.when(ok_ref[t] > 0)
    def _():
        x = x_ref[...]
        o_ref[...] = (_silu(_mm(x, wg_ref[0])) * _mm(x, wu_ref[0])).astype(o_ref.dtype)

    @pl.when(ok_ref[t] == 0)
    def _():
        o_ref[...] = jnp.zeros_like(o_ref)


def _gate_up(tile_e, tile_blk, valid, xs, wg, wu):
    d = xs.shape[1]
    ff = wg.shape[2]
    nt = tile_e.shape[0]
    tf = ff // MOE_F_SPLIT
    w_spec = pl.BlockSpec((1, d, tf), lambda f, t, te, tb, ok: (te[t], 0, f))
    return pl.pallas_call(
        _gate_up_kernel,
        grid_spec=pltpu.PrefetchScalarGridSpec(
            num_scalar_prefetch=3,
            grid=(MOE_F_SPLIT, nt),
            in_specs=[pl.BlockSpec((MOE_TILE, d), lambda f, t, te, tb, ok: (tb[t], 0)), w_spec, w_spec],
            out_specs=pl.BlockSpec((MOE_TILE, tf), lambda f, t, te, tb, ok: (t, f))),
        out_shape=jax.ShapeDtypeStruct((nt * MOE_TILE, ff), BF16),
        compiler_params=_params("arbitrary", "arbitrary"),
        name="moe_gate_up",
    )(tile_e, tile_blk, valid, xs, wg, wu)


def _down_kernel(te_ref, ok_ref, m_ref, wd_ref, o_ref):
    t = pl.program_id(0)

    @pl.when(ok_ref[t] > 0)
    def _():
        o_ref[...] = _mm(m_ref[...], wd_ref[0]).astype(o_ref.dtype)

    @pl.when(ok_ref[t] == 0)
    def _():
        o_ref[...] = jnp.zeros_like(o_ref)


def _down(tile_e, valid, mid, wd):
    ff, d = wd.shape[1:]
    nt = tile_e.shape[0]
    return pl.pallas_call(
        _down_kernel,
        grid_spec=pltpu.PrefetchScalarGridSpec(
            num_scalar_prefetch=2,
            grid=(nt,),
            in_specs=[pl.BlockSpec((MOE_TILE, ff), lambda t, te, ok: (t, 0)),
                      pl.BlockSpec((1, ff, d), lambda t, te, ok: (te[t], 0, 0))],
            out_specs=pl.BlockSpec((MOE_TILE, d), lambda t, te, ok: (t, 0))),
        out_shape=jax.ShapeDtypeStruct((nt * MOE_TILE, d), BF16),
        compiler_params=_params("arbitrary"),
        name="moe_down",
    )(tile_e, valid, mid, wd)


def _combine_kernel(final, cnt_ref, win_ref, shift_ref, x_ref, comb_ref, pos_ref, ys_ref, fg_ref, o_ref, win_buf,
                    sem):
    blk = pl.program_id(0)
    tb = x_ref.shape[0]
    n_sub = tb // MOE_WIN
    pieces = [(e, j) for e in range(N_EXPERTS) for j in range(n_sub)]

    def needed(e, j):
        return cnt_ref[blk * N_EXPERTS + e] > j * MOE_WIN

    def win_copy(e, j):
        row = pl.multiple_of(win_ref[(blk * N_EXPERTS + e) * n_sub + j], ROW_ALIGN)
        return pltpu.make_async_copy(ys_ref.at[pl.ds(row, MOE_WIN)], win_buf.at[e, j], sem.at[e, j])

    for e, j in pieces:
        @pl.when(needed(e, j))
        def _(e=e, j=j):
            win_copy(e, j).start()

    o_ref[...] = x_ref[...]
    col = lax.broadcasted_iota(jnp.int32, (tb, MOE_WIN), 1).astype(F32)
    for e, j in pieces:
        @pl.when(needed(e, j))
        def _(e=e, j=j):
            win_copy(e, j).wait()
            pos = pos_ref[...][:, e:e + 1]
            in_sub = (pos >= float(j * MOE_WIN)) & (pos < float((j + 1) * MOE_WIN))
            shift = shift_ref[(blk * N_EXPERTS + e) * n_sub + j].astype(F32)
            target = jnp.where(in_sub, pos - float(j * MOE_WIN) + shift, -1.0)
            onehot = (col == target).astype(BF16)
            o_ref[...] += comb_ref[...][:, e:e + 1] * _mm(onehot, win_buf[e, j])

    if final:
        o_ref[...] = _rms(o_ref[...], fg_ref[...])


def _combine(tables, x, comb, pos, ys, final_gain):
    t, d = x.shape
    final = final_gain is not None
    gain = final_gain if final else jnp.ones((d,), F32)
    tb = MOE_TB
    lanes = pl.BlockSpec((tb, LANES), lambda i, *_: (i, 0))
    return pl.pallas_call(
        functools.partial(_combine_kernel, final),
        grid_spec=pltpu.PrefetchScalarGridSpec(
            num_scalar_prefetch=len(tables),
            grid=(t // tb,),
            in_specs=[pl.BlockSpec((tb, d), lambda i, *_: (i, 0)), lanes, lanes,
                      pl.BlockSpec(memory_space=pl.ANY), pl.BlockSpec((1, d), lambda i, *_: (0, 0))],
            out_specs=pl.BlockSpec((tb, d), lambda i, *_: (i, 0)),
            scratch_shapes=[pltpu.VMEM((N_EXPERTS, tb // MOE_WIN, MOE_WIN, d), BF16),
                            pltpu.SemaphoreType.DMA((N_EXPERTS, tb // MOE_WIN))]),
        out_shape=jax.ShapeDtypeStruct((t, d), F32),
        compiler_params=_params("arbitrary"),
        name="moe_combine",
    )(*tables, x, comb, pos, ys, gain.reshape(1, d))


def _moe(x, gain, w_router, wg, wu, wd, final_gain):
    t, d = x.shape
    nb = t // MOE_TB
    n_tiles_max = (2 * t + nb * N_EXPERTS * (ROW_ALIGN - 1)) // MOE_TILE + N_EXPERTS
    hn, comb, pos, pos_t, cnt = _router(x, gain, w_router)
    write_tables, read_tables, (tile_e, tile_blk, valid) = _route_tables(cnt[:, 0, :N_EXPERTS], n_tiles_max)
    n_rows = (n_tiles_max + SPARE_TILES * N_EXPERTS) * MOE_TILE
    xs = _dispatch(write_tables, hn, pos_t, n_rows)
    mid = _gate_up(tile_e, tile_blk, valid, xs, wg, wu)
    ys = _down(tile_e, valid, mid, wd)
    return _combine(read_tables, x, comb, pos, ys, final_gain)


def _final_norm_kernel(x_ref, g_ref, o_ref):
    o_ref[...] = _rms(x_ref[...], g_ref[...])


def _final_norm(x, gain):
    t, d = x.shape
    tm = TM_NORM
    return pl.pallas_call(
        _final_norm_kernel,
        grid=(t // tm,),
        in_specs=[pl.BlockSpec((tm, d), lambda i: (i, 0)), _full((1, d))],
        out_specs=pl.BlockSpec((tm, d), lambda i: (i, 0)),
        out_shape=jax.ShapeDtypeStruct((t, d), F32),
        compiler_params=_params("parallel"),
        name="final_norm",
    )(x, gain.reshape(1, d))


def _hm_groups_spec(n, first, width=HEAD):
    def spec(b, h):
        return pl.BlockSpec((n, b, h, CHUNK, width), lambda c: (first // n, 0, 0, c, 0))
    assert first % n == 0
    return spec


def _tok_chunk_spec(b):
    return pl.BlockSpec((b, CHUNK, GROUP), lambda c: (0, c, 0))


def _expand_pair(x):
    first = lax.broadcasted_iota(jnp.int32, x.shape, 2) < HEAD
    return jnp.concatenate([jnp.where(first, x, 0.0), jnp.where(first, 0.0, x)], axis=1)


def _same_head(n):
    t, s = _tri_masks(n)
    return (t >= n // 2) == (s >= n // 2)


def _hgrn_kernel(qfig_ref, loglb_ref, log1mlb_ref, onemlb_ref, ng_ref, o_ref, st_ref, rows_ref):
    @pl.when(pl.program_id(0) == 0)
    def _():
        st_ref[...] = jnp.zeros_like(st_ref)

    g2 = st_ref.shape[0]
    shape = (g2, CHUNK, PAIR)
    q, fl, v, g = (qfig_ref[j].reshape(shape) for j in range(4))
    log_sig = jnp.minimum(fl, 0.0) - jnp.log(1.0 + jnp.exp(-jnp.abs(fl)))
    la = loglb_ref[...]
    lc = log1mlb_ref[...] + log_sig
    log_f = jnp.maximum(la, lc) + jnp.log(1.0 + jnp.exp(-jnp.abs(la - lc)))
    k = onemlb_ref[...] * jax.nn.sigmoid(-fl)
    b = _cumsum_rows(log_f) * LOG2_E
    head_sum = jnp.broadcast_to(_same_head(PAIR).astype(BF16), (g2, PAIR, PAIR))

    rows_ref[0] = b
    rows_ref[1] = k
    rows_ref[2] = v
    sub = HGRN_SUB
    a_rows = [jnp.zeros((g2, sub, 2 * CHUNK), F32)]
    for i in range(1, CHUNK // sub):
        bref = rows_ref[0, :, i * sub - 1:i * sub, :]
        qd = q[:, i * sub:(i + 1) * sub] * jnp.exp2(b[:, i * sub:(i + 1) * sub] - bref)
        kd = k[:, :i * sub] * jnp.exp2(bref - b[:, :i * sub])
        later = jnp.zeros((g2, CHUNK - i * sub, PAIR), F32)
        a_rows.append(_bmm(qd, _expand_pair(jnp.concatenate([kd, later], axis=1)), nt=True))
    o = _bmm(jnp.concatenate(a_rows, axis=1), _expand_pair(v))

    t_loc = lax.broadcasted_iota(jnp.int32, (g2, sub, PAIR), 1)
    diag = []
    for i in range(CHUNK // sub):
        sl = slice(i * sub, (i + 1) * sub)
        bb, qq = b[:, sl], q[:, sl]
        terms = []
        for s in range(sub):
            row = i * sub + s
            e = jnp.exp2(jnp.where(t_loc >= s, bb - rows_ref[0, :, row:row + 1, :], -BIG_EXPONENT))
            terms.append(e * qq * rows_ref[1, :, row:row + 1, :])
        w = _bmm(jnp.concatenate(terms, axis=1), head_sum)
        acc = jnp.zeros((g2, sub, PAIR), F32)
        for s in range(sub):
            acc = acc + w[:, s * sub:(s + 1) * sub] * rows_ref[2, :, i * sub + s:i * sub + s + 1, :]
        diag.append(acc)
    o = o + jnp.concatenate(diag, axis=1)

    st = st_ref[...]
    o = o + _bmm(q * jnp.exp2(b), st, nt=True)
    b_last = b[:, CHUNK - 1:CHUNK]
    update = _bmm(jnp.swapaxes(v, 1, 2), k * jnp.exp2(b_last - b))
    st_ref[...] = st * jnp.exp2(b_last) + jnp.where(_same_head(PAIR), update, 0.0)

    mean_sq = _bmm(o * o, head_sum) * (1.0 / HEAD)
    _store_tokens(o_ref, o * lax.rsqrt(mean_sq + EPS) * ng_ref[...] * _silu(g))


def _gdn_kernel(qkv_ref, z_ref, gates_ref, cw_ref, alog_ref, dtb_ref, ng_ref, o_ref, xbuf_ref, st_ref):
    g3 = xbuf_ref.shape[0]
    g = g3 // 3

    @pl.when(pl.program_id(0) == 0)
    def _():
        st_ref[...] = jnp.zeros_like(st_ref)
        xbuf_ref[:, 0:HALO, :] = jnp.zeros((g3, HALO, HEAD), F32)

    xbuf_ref[:, HALO:HALO + CHUNK, :] = qkv_ref[...].reshape(g3, CHUNK, HEAD)
    conv = jnp.zeros((g3, CHUNK, HEAD), F32)
    for j in range(CONV_K):
        conv = conv + cw_ref[j] * xbuf_ref[:, HALO - (CONV_K - 1) + j:HALO - (CONV_K - 1) + j + CHUNK, :]
    xbuf_ref[:, 0:HALO, :] = xbuf_ref[:, CHUNK:CHUNK + HALO, :]
    qkv = _silu(conv)
    q, k, v = qkv[0:g], qkv[g:2 * g], qkv[2 * g:3 * g]
    q = q * lax.rsqrt(jnp.sum(q * q, axis=-1, keepdims=True) + EPS) * HEAD ** -0.5
    k = k * lax.rsqrt(jnp.sum(k * k, axis=-1, keepdims=True) + EPS)

    t_idx, s_idx = _tri_masks(CHUNK)
    causal = s_idx <= t_idx
    eye = s_idx == t_idx
    g_row = -jnp.exp(alog_ref[...]) * _softplus(gates_ref[0, 0] + dtb_ref[...])
    g_col = jnp.sum(jnp.where(eye, g_row, 0.0), axis=-1, keepdims=True)
    beta = jnp.sum(jnp.where(eye, jax.nn.sigmoid(gates_ref[1, 0]), 0.0), axis=-1, keepdims=True)
    gc_col = jnp.sum(jnp.where(causal, g_row, 0.0), axis=-1, keepdims=True)
    gc_row = jnp.sum(jnp.where(t_idx <= s_idx, g_col, 0.0), axis=1, keepdims=True)
    rel = jnp.where(causal, jnp.exp(jnp.minimum(gc_col - gc_row, 0.0)), 0.0)

    kb = k * beta
    with_k = _bmm(jnp.concatenate([kb, q], axis=1), k, nt=True)
    lower = jnp.where(s_idx < t_idx, with_k[:, :CHUNK] * rel, 0.0)
    uw = _solve_unit_lower(lower, jnp.concatenate([v * beta, kb * jnp.exp(gc_col)], axis=-1))
    u, w = uw[:, :, :HEAD], uw[:, :, HEAD:]
    scores = with_k[:, CHUNK:] * rel

    st = st_ref[...]
    from_state = _bmm(jnp.concatenate([w, q * jnp.exp(gc_col)], axis=1), st)
    v_new = u - from_state[:, :CHUNK]
    o = from_state[:, CHUNK:] + _bmm(scores, v_new)
    g_last = gc_col[:, CHUNK - 1:CHUNK]
    kd = k * jnp.exp(g_last - gc_col)
    st_ref[...] = st * jnp.exp(g_last) + _bmm(jnp.swapaxes(kd, 1, 2), v_new)

    _store_tokens(o_ref, _rms(o, ng_ref[...]) * _silu(z_ref[0].reshape(g, CHUNK, HEAD)))


def _lru_kernel(x_ref, gate_ref, cw_ref, cb_ref, wa_ref, ba_ref, wx_ref, bx_ref, lam_ref, o_ref,
                xbuf_ref, h_ref):
    tt = x_ref.shape[1]

    @pl.when(pl.program_id(1) == 0)
    def _():
        h_ref[...] = jnp.zeros_like(h_ref)
        xbuf_ref[0:HALO, :] = jnp.zeros((HALO, GROUP), F32)

    xbuf_ref[HALO:HALO + tt, :] = x_ref[0]
    xc = jnp.zeros((tt, GROUP), F32) + cb_ref[...]
    for j in range(CONV_K):
        xc = xc + cw_ref[j:j + 1, :] * xbuf_ref[HALO - (CONV_K - 1) + j:HALO - (CONV_K - 1) + j + tt, :]
    xbuf_ref[0:HALO, :] = xbuf_ref[tt:tt + HALO, :]

    r = jax.nn.sigmoid(_mm16(xc, wa_ref[...]) + ba_ref[...])
    i = jax.nn.sigmoid(_mm16(xc, wx_ref[...]) + bx_ref[...])
    log_a = -LRU_C * r * _softplus(-lam_ref[...])
    a = jnp.exp(log_a)
    u = jnp.sqrt(1.0 - a * a) * (i * xc)

    row = lax.broadcasted_iota(jnp.int32, (tt, GROUP), 0)
    shift = 1
    while shift < tt:
        keep = row >= shift
        a_prev = pltpu.roll(a, shift, 0)
        u_prev = pltpu.roll(u, shift, 0)
        u = jnp.where(keep, a * u_prev + u, u)
        a = jnp.where(keep, a * a_prev, a)
        shift *= 2
    hs = u + a * h_ref[...]
    h_ref[...] = hs[tt - 1:tt, :]

    gate = gate_ref[0]
    gelu = 0.5 * gate * (1.0 + jnp.tanh(math.sqrt(2.0 / math.pi) * (gate + 0.044715 * gate * gate * gate)))
    o_ref[0] = hs * gelu


def _lru_mixer(tok, conv_w, conv_b, wa, ba, wx, bx, lam):
    b, s, _ = tok.shape
    c = GROUP
    tt = TT_SCAN
    x_spec = pl.BlockSpec((1, tt, c), lambda i, t: (i, t, TOK_LRU // GROUP))
    gate_spec = pl.BlockSpec((1, tt, c), lambda i, t: (i, t, TOK_LRU // GROUP + 1))
    vec = _full((1, c))
    return pl.pallas_call(
        _lru_kernel,
        grid=(b, s // tt),
        in_specs=[x_spec, gate_spec, _full((CONV_K, c)), vec, _full((c, c)), vec, _full((c, c)), vec, vec],
        out_specs=pl.BlockSpec((1, tt, c), lambda i, t: (i, t, 0)),
        out_shape=jax.ShapeDtypeStruct((b, s, c), F32),
        scratch_shapes=[pltpu.VMEM((HALO + tt, c), F32), pltpu.VMEM((1, c), F32)],
        compiler_params=_params("parallel", "arbitrary"),
        name="rg_lru",
    )(tok, tok, conv_w, conv_b.reshape(1, c), wa, ba.reshape(1, c), wx, bx.reshape(1, c), lam.reshape(1, c))


def _rwkv_kernel(in_ref, kk_ref, ka_ref, rk_ref, lnw_ref, lnb_ref, o_ref, zt_ref):
    @pl.when(pl.program_id(0) == 0)
    def _():
        zt_ref[...] = jnp.zeros_like(zt_ref)

    shape = zt_ref.shape[:1] + (CHUNK, HEAD)
    r, lw, k, v, a, g = (in_ref[j].reshape(shape) for j in range(6))
    kk = k * kk_ref[...]
    kk = kk * lax.rsqrt(jnp.sum(kk * kk, axis=-1, keepdims=True) + EPS)
    k = k * (1.0 + (a - 1.0) * ka_ref[...])
    b = _cumsum_rows(lw)
    e_pos = jnp.exp(b)
    e_neg = jnp.exp(-b)
    r_t = r * e_pos
    al_t = -kk * jnp.exp(b - lw)
    be_t = kk * a * e_neg
    k_t = k * e_neg

    t_idx, s_idx = _tri_masks(CHUNK)
    strict = s_idx < t_idx
    ab_k = jnp.concatenate([be_t, k_t], axis=1)
    both = _bmm(jnp.concatenate([al_t, r_t], axis=1), ab_k, nt=True)
    l_ab = jnp.where(strict, both[:, :CHUNK, :CHUNK], 0.0)
    l_ak = jnp.where(strict, both[:, :CHUNK, CHUNK:], 0.0)
    t2 = lax.broadcasted_iota(jnp.int32, (CHUNK, 2 * CHUNK), 0)
    s2 = lax.broadcasted_iota(jnp.int32, (CHUNK, 2 * CHUNK), 1)
    incl2 = jnp.where(s2 >= CHUNK, s2 - CHUNK, s2) <= t2
    m_all = jnp.where(incl2, both[:, CHUNK:], 0.0)

    sol = _solve_unit_lower(-l_ab, jnp.concatenate([al_t, _bmm(l_ak, v)], axis=-1))
    w1, u0 = sol[:, :, :HEAD], sol[:, :, HEAD:]

    zt = zt_ref[...]
    from_state = _bmm(jnp.concatenate([w1, r_t], axis=1), zt, nt=True)
    u = from_state[:, :CHUNK] + u0
    uv = jnp.concatenate([u, v], axis=1)
    y = from_state[:, CHUNK:] + _bmm(m_all, uv)
    b_last = b[:, CHUNK - 1:CHUNK]
    e_last = jnp.exp(b_last - b)
    zt_ref[...] = zt * jnp.exp(b_last) + _bmm(jnp.swapaxes(uv, 1, 2),
                                              jnp.concatenate([kk * a * e_last, k * e_last], axis=1))

    mean = jnp.mean(y, axis=-1, keepdims=True)
    var = jnp.mean(jnp.square(y - mean), axis=-1, keepdims=True)
    yn = (y - mean) * lax.rsqrt(var + RWKV_GN_EPS) * lnw_ref[...] + lnb_ref[...]
    bonus = jnp.sum(r * k * rk_ref[...], axis=-1, keepdims=True) * v
    _store_tokens(o_ref, (yn + bonus) * g)


N_HGRN_IN, N_GDN_IN, N_RWKV_IN = 5, 7, 6


def _chunk_mixers_kernel(*refs):
    ins, rest = refs[:N_HGRN_IN + N_GDN_IN + N_RWKV_IN], refs[N_HGRN_IN + N_GDN_IN + N_RWKV_IN:]
    o_hgrn, o_gdn, o_rwkv, st_hgrn, rows_hgrn, xbuf_gdn, st_gdn, zt_rwkv = rest
    _hgrn_kernel(*ins[:N_HGRN_IN], o_hgrn, st_hgrn, rows_hgrn)
    _gdn_kernel(*ins[N_HGRN_IN:N_HGRN_IN + N_GDN_IN], o_gdn, xbuf_gdn, st_gdn)
    _rwkv_kernel(*ins[N_HGRN_IN + N_GDN_IN:], o_rwkv, zt_rwkv)


def _chunk_mixers(hmp, hm, d_hm, hgrn_args, gdn_args, rwkv_args):
    _, b, h, s, d = hm.shape
    g = b * h
    g2 = g // 2
    par = _full((g, 1, d))
    lb, hgrn_norm = hgrn_args
    gates, conv_w, a_log, dt_bias, gdn_norm = gdn_args
    hgrn_specs = [_hm_groups_spec(HMP_GROUPS, 0, PAIR)(b, h // 2)] + [_full((g2, 1, PAIR))] * 4
    gdn_specs = [_hm_groups_spec(3, HM_GDN_QKV)(b, h), _hm_groups_spec(1, HM_GDN_Z)(b, h),
                 pl.BlockSpec((2, 1, g, 1, CHUNK), lambda c: (0, c, 0, 0, 0)),
                 _full((CONV_K, 3 * g, 1, d)), _full((g, 1, 1)), _full((g, 1, 1)), _full((1, 1, d))]
    rwkv_specs = [_hm_groups_spec(6, 0)(b, h)] + [par] * 5
    assert (len(hgrn_specs), len(gdn_specs), len(rwkv_specs)) == (N_HGRN_IN, N_GDN_IN, N_RWKV_IN)
    out = jax.ShapeDtypeStruct((b, s, h * d), F32)
    state = pltpu.VMEM((g, d, d), F32)
    return pl.pallas_call(
        _chunk_mixers_kernel,
        grid=(s // CHUNK,),
        in_specs=hgrn_specs + gdn_specs + rwkv_specs,
        out_specs=[_tok_chunk_spec(b)] * 3,
        out_shape=[out] * 3,
        scratch_shapes=[pltpu.VMEM((g2, PAIR, PAIR), F32), pltpu.VMEM((3, g2, CHUNK, PAIR), F32),
                        pltpu.VMEM((3 * g, HALO + CHUNK, d), F32), state, state],
        compiler_params=_params("arbitrary"),
        name="chunk_mixers",
    )(hmp, jnp.log(lb), jnp.log1p(-lb), 1.0 - lb, hgrn_norm,
      hm, hm, gates, conv_w, a_log, dt_bias, gdn_norm,
      d_hm, *rwkv_args)


def _par_hm(p, b, width=HEAD):
    return jnp.tile(p.reshape(-1, 1, width), (b, 1, 1))


def _gdn_gate_layout(gates):
    b, s, h2 = gates.shape
    g = gates.reshape(b, s // CHUNK, CHUNK, 2, h2 // 2).transpose(3, 1, 0, 4, 2)
    return g.reshape(2, s // CHUNK, b * (h2 // 2), 1, CHUNK)


def _gdn_conv_layout(w, b):
    k, c = w.shape
    h = c // (3 * HEAD)
    w = jnp.broadcast_to(w.reshape(k, 3, 1, h, 1, HEAD), (k, 3, b, h, 1, HEAD))
    return w.reshape(k, 3 * b * h, 1, HEAD)


def _block_diag(w):
    n, a, b = w.shape
    out = jnp.zeros((n * a, n * b), w.dtype)
    for i in range(n):
        out = out.at[i * a:(i + 1) * a, i * b:(i + 1) * b].set(w[i])
    return out


def _in_proj_weight(w):
    d = w.shape[0]
    g4 = 4 * GROUP
    gates = g4 + g4
    lru = gates + 2 * HEADS
    rwkv = lru + 2 * GROUP
    cols = [w[:, rwkv:rwkv + g4], w[:, lru:rwkv], w[:, gates:lru], jnp.zeros((d, LANES - 2 * HEADS), w.dtype),
            w[:, 0:g4], w[:, g4:gates]]
    return jnp.concatenate(cols, axis=1).astype(BF16)


def kernel(x, mem, norm_mix, w_in, w_out, hgrn_lb_logits, hgrn_norm, gdn_conv_w, gdn_A_log, gdn_dt_bias, gdn_norm, lru_conv_w, lru_conv_b, lru_wa, lru_ba, lru_wx, lru_bx, lru_lambda, rwkv_mu, rwkv_w0, rwkv_w2, rwkv_a0, rwkv_a2, rwkv_g2, rwkv_k_k, rwkv_k_a, rwkv_r_k, rwkv_ln_w, rwkv_ln_b, rwkv_v0, rwkv_v1, rwkv_v2, mem_norm, norm_xattn, xattn_wq, xattn_wk, xattn_wv, xattn_wo, norm_ffn, ffn_w_gate, ffn_w_up, ffn_w_down, moe_router, moe_w_gate, moe_w_up, moe_w_down, norm_final):
    bsz, seq, d = x.shape
    depth = w_in.shape[0]
    tok = bsz * seq
    mlen = mem.shape[1]

    lb = jnp.cumsum(jax.nn.softmax(hgrn_lb_logits.astype(F32), axis=0), axis=0)
    lb = lb - lb[0]
    mem_f = mem.reshape(bsz * mlen, d)
    v_first = None
    for l in range(depth):
        v_mix = None if l == 0 else (rwkv_v0[l - 1], rwkv_v1[l - 1], rwkv_v2[l - 1])
        p_tok, p_hmp, p_hm, d_hm, v_first = _in_proj(
            x, norm_mix[l], _in_proj_weight(w_in[l]), rwkv_mu[l], rwkv_w0[l], rwkv_w2[l], rwkv_a0[l], rwkv_a2[l],
            rwkv_g2[l], v_first, v_mix)

        y_c = _lru_mixer(p_tok, lru_conv_w[l], lru_conv_b[l], _block_diag(lru_wa[l]), lru_ba[l],
                         _block_diag(lru_wx[l]), lru_bx[l], lru_lambda[l])

        gates = p_tok[..., TOK_GATES:TOK_GATES + 2 * HEADS]
        y_a, y_b, y_d = _chunk_mixers(
            p_hmp, p_hm, d_hm,
            (_par_hm(lb[l], bsz, PAIR), _par_hm(hgrn_norm[l], bsz, PAIR)),
            (_gdn_gate_layout(gates), _gdn_conv_layout(gdn_conv_w[l], bsz), _par_hm(gdn_A_log[l], bsz, 1),
             _par_hm(gdn_dt_bias[l], bsz, 1), gdn_norm[l].reshape(1, 1, HEAD)),
            tuple(_par_hm(p[l], bsz) for p in (rwkv_k_k, rwkv_k_a, rwkv_r_k, rwkv_ln_w, rwkv_ln_b)))

        w_kv = jnp.concatenate([xattn_wk[l], xattn_wv[l]], axis=1).astype(BF16)
        kv = _norm_matmul(mem_f, mem_norm, w_kv, mlen, BF16).reshape(bsz, mlen, 2 * d)
        x = _mix_attention((y_a, y_b, y_c, y_d), w_out[l].reshape(4, GROUP, d).astype(BF16), x, norm_xattn[l],
                           xattn_wq[l].astype(BF16), kv, xattn_wo[l].astype(BF16))

        xf = x.reshape(tok, d)
        j = l // 2
        if l % 2 == 0:
            ones = jnp.ones((1, tok, 1), F32)
            xf = _ffn(xf, norm_ffn[l], ffn_w_gate[j][None].astype(BF16), ffn_w_up[j][None].astype(BF16),
                      ffn_w_down[j][None].astype(BF16), ones, TF_FFN)
        else:
            xf = _moe(xf, norm_ffn[l], moe_router[j], moe_w_gate[j].astype(BF16), moe_w_up[j].astype(BF16),
                      moe_w_down[j].astype(BF16), norm_final if l == depth - 1 else None)
        x = xf.reshape(bsz, seq, d)
    if depth % 2 == 1:
        x = _final_norm(x.reshape(tok, d), norm_final).reshape(bsz, seq, d)
    return x
```

```python
import functools
import math

import jax
import jax.numpy as jnp
from jax import lax
from jax.experimental import pallas as pl
from jax.experimental.pallas import tpu as pltpu

F32 = jnp.float32
BF16 = jnp.bfloat16
HIGHEST = lax.Precision.HIGHEST

GROUP = 256
HEAD = 64
HEADS = GROUP // HEAD
CHUNK = 64
SUB = 16
HGRN_SUB = 8
LOG2_E = 1.4426950408889634
BIG_EXPONENT = 1e30
CONV_K = 4
HALO = 8
LRU_C = 8.0
RWKV_GN_EPS = 64e-5
XATTN_HEADS = 4
N_EXPERTS = 8
EPS = 1e-6
LANES = 128
VMEM_LIMIT_BYTES = 56 * 1024 * 1024

TOK_LRU = 0
TOK_GATES = 2 * GROUP
N_TOK = 2 * GROUP + LANES
PAIR = 2 * HEAD
HMP_GROUPS = 4
HM_GDN_QKV, HM_GDN_Z = 0, 3
HM_GROUPS = 4

TM_IN_PROJ = 512
TM_XATTN = 512
TM_FFN = 1024
TF_FFN = 1408
TM_NORM = 1024
TT_SCAN = 512
MOE_TB = 512
MOE_TILE = 512
MOE_F_SPLIT = 2
ROW_ALIGN = 16
SPARE_TILES = 2
MOE_SEG = 128
MOE_WIN = 256


def _params(*semantics):
    return pltpu.CompilerParams(dimension_semantics=semantics, vmem_limit_bytes=VMEM_LIMIT_BYTES)


def _full(shape):
    n = len(shape)
    return pl.BlockSpec(shape, lambda *_: (0,) * n)


def _bmm(a, b, nt=False, exact=False):
    dn = (((2,), (2 if nt else 1,)), ((0,), (0,)))
    if exact:
        return lax.dot_general(a, b, dn, precision=HIGHEST, preferred_element_type=F32)
    return lax.dot_general(a.astype(BF16), b.astype(BF16), dn, preferred_element_type=F32)


def _mm(a, b, precision=None):
    return jnp.dot(a, b, precision=precision, preferred_element_type=F32)


def _mm16(a, b):
    return jnp.dot(a.astype(BF16), b.astype(BF16), preferred_element_type=F32)


def _softplus(x):
    return jnp.maximum(x, 0.0) + jnp.log1p(jnp.exp(-jnp.abs(x)))


def _silu(x):
    return x * jax.nn.sigmoid(x)


def _rms(x, gain):
    return x * lax.rsqrt(jnp.mean(x * x, axis=-1, keepdims=True) + EPS) * gain


def _tri_masks(n):
    t = lax.broadcasted_iota(jnp.int32, (n, n), 0)
    s = lax.broadcasted_iota(jnp.int32, (n, n), 1)
    return t, s


def _cumsum_rows(x):
    g, c, _ = x.shape
    t, s = _tri_masks(c)
    tri = jnp.broadcast_to((s <= t).astype(BF16), (g, c, c))
    hi = x.astype(BF16)
    rest = x - hi.astype(F32)
    mid = rest.astype(BF16)
    lo = (rest - mid.astype(F32)).astype(BF16)
    return _bmm(tri, hi) + _bmm(tri, mid) + _bmm(tri, lo)


def _solve_unit_lower(low, rhs):
    c = low.shape[1]
    r = rhs.shape[2]
    assert c == 4 * SUB and SUB == 16
    t, s = _tri_masks(c)
    same_block = (t // SUB) == (s // SUB)
    d1 = jnp.where(same_block, low, 0.0)
    z = jnp.concatenate([rhs, jnp.where(same_block, 0.0, low)], axis=-1)
    d2 = _bmm(d1, d1)
    d4 = _bmm(d2, d2)
    d8 = _bmm(d4, d4)
    z = z + _bmm(d8, z)
    z = z + _bmm(d4, z)
    z = z + _bmm(d2, z)
    z = z - _bmm(d1, z)
    y = z[:, :, :r]
    n1 = z[:, :, r:]
    n2 = _bmm(n1, n1)
    y = y + _bmm(n2, y)
    return y - _bmm(n1, y)


def _store_heads(ref, val, lead=(), width=HEAD):
    for h in range(val.shape[1] // width):
        ref[lead + (h,)] = val[:, h * width:(h + 1) * width]


def _store_tokens(ref, o):
    nb = ref.shape[0]
    nh = o.shape[0] // nb
    for b in range(nb):
        ref[b] = jnp.concatenate([o[b * nh + h] for h in range(nh)], axis=-1)


def _in_proj_kernel(has_vmix, *refs):
    refs = list(refs)
    x_ref, g_ref, w_ref, mu_ref, w0_ref, w2_ref, a0_ref, a2_ref, g2_ref = refs[:9]
    del refs[:9]
    if has_vmix:
        vf_ref, v0_ref, v1_ref, v2_ref = refs[:4]
        del refs[:4]
    tok_ref, hmp_ref, hm_ref, dhm_ref = refs[:4]
    del refs[:4]
    vtok_ref = None if has_vmix else refs.pop(0)
    xbuf_ref = refs.pop(0)
    tm = x_ref.shape[1]

    hn = _rms(x_ref[0], g_ref[...]).astype(BF16)
    n_rwkv = 4 * GROUP
    c = _mm(hn, w_ref[:, :n_rwkv])
    acc = _mm(hn, w_ref[:, n_rwkv:])
    col = 0
    n_tok = tok_ref.shape[2]
    tok_ref[0] = acc[:, col:col + n_tok]
    col += n_tok
    for g in range(hmp_ref.shape[0]):
        _store_heads(hmp_ref, acc[:, col:col + GROUP], (g, 0), PAIR)
        col += GROUP
    for g in range(hm_ref.shape[0]):
        _store_heads(hm_ref, acc[:, col:col + GROUP], (g, 0))
        col += GROUP

    @pl.when(pl.program_id(1) == 0)
    def _():
        xbuf_ref[0:HALO, :] = jnp.zeros((HALO, xbuf_ref.shape[1]), F32)

    xbuf_ref[HALO:HALO + tm, :] = c
    prev = xbuf_ref[HALO - 1:HALO - 1 + tm, :]
    xbuf_ref[0:HALO, :] = xbuf_ref[tm:tm + HALO, :]
    c = c + mu_ref[...] * (prev - c)

    r = c[:, 0:GROUP]
    k = c[:, GROUP:2 * GROUP]
    v = c[:, 2 * GROUP:3 * GROUP]
    wa_lo = c[:, 3 * GROUP:3 * GROUP + LANES]
    g_lo = c[:, 3 * GROUP + LANES:4 * GROUP]
    log_w = -math.exp(-0.5) * jax.nn.sigmoid(w0_ref[...] + _mm16(jnp.tanh(wa_lo), w2_ref[...]))
    a = jax.nn.sigmoid(a0_ref[...] + _mm16(wa_lo, a2_ref[...]))
    gate = _mm16(jax.nn.sigmoid(g_lo), g2_ref[...])
    if has_vmix:
        mix = jax.nn.sigmoid(v0_ref[...] + _mm16(_mm16(v, v1_ref[...]), v2_ref[...]))
        v = v + (vf_ref[0] - v) * mix
    else:
        vtok_ref[0] = v
    for j, val in enumerate((r, log_w, k, v, a, gate)):
        _store_heads(dhm_ref, val, (j, 0))


def _in_proj(x, gain, w, mu, w0, w2, a0, a2, g2, v_first, v_mix):
    b, s, d = x.shape
    n = w.shape[1]
    tm = TM_IN_PROJ
    lora = w2.shape[0]
    w2p = jnp.zeros((LANES, GROUP), F32).at[:lora].set(w2)
    a2p = jnp.zeros((LANES, GROUP), F32).at[lora:].set(a2)
    tile = pl.BlockSpec((1, tm, GROUP), lambda i, j: (i, j, 0))
    vec = _full((1, GROUP))
    args = [x, gain.reshape(1, d), w, mu.reshape(1, 4 * GROUP), w0.reshape(1, GROUP), w2p, a0.reshape(1, GROUP),
            a2p, g2]
    specs = [pl.BlockSpec((1, tm, d), lambda i, j: (i, j, 0)), _full((1, d)), _full((d, n)), _full((1, 4 * GROUP)),
             vec, _full((LANES, GROUP)), vec, _full((LANES, GROUP)), _full(g2.shape)]
    out_specs = [pl.BlockSpec((1, tm, N_TOK), lambda i, j: (i, j, 0)),
                 pl.BlockSpec((HMP_GROUPS, 1, HEADS // 2, tm, PAIR), lambda i, j: (0, i, 0, j, 0)),
                 pl.BlockSpec((HM_GROUPS, 1, HEADS, tm, HEAD), lambda i, j: (0, i, 0, j, 0)),
                 pl.BlockSpec((6, 1, HEADS, tm, HEAD), lambda i, j: (0, i, 0, j, 0))]
    out_shape = [jax.ShapeDtypeStruct((b, s, N_TOK), F32),
                 jax.ShapeDtypeStruct((HMP_GROUPS, b, HEADS // 2, s, PAIR), F32),
                 jax.ShapeDtypeStruct((HM_GROUPS, b, HEADS, s, HEAD), F32),
                 jax.ShapeDtypeStruct((6, b, HEADS, s, HEAD), F32)]
    if v_mix is not None:
        v0, v1, v2 = v_mix
        rank = v1.shape[1]
        v1p = jnp.zeros((GROUP, LANES), F32).at[:, :rank].set(v1)
        v2p = jnp.zeros((LANES, GROUP), F32).at[:rank].set(v2)
        args += [v_first, v0.reshape(1, GROUP), v1p, v2p]
        specs += [tile, vec, _full((GROUP, LANES)), _full((LANES, GROUP))]
    else:
        out_specs.append(tile)
        out_shape.append(jax.ShapeDtypeStruct((b, s, GROUP), F32))
    outs = pl.pallas_call(
        functools.partial(_in_proj_kernel, v_mix is not None),
        grid=(b, s // tm),
        in_specs=specs,
        out_specs=out_specs,
        out_shape=out_shape,
        scratch_shapes=[pltpu.VMEM((HALO + tm, 4 * GROUP), F32)],
        compiler_params=_params("parallel", "arbitrary"),
        name="in_proj",
    )(*args)
    return tuple(outs[:4]) + ((v_first,) if v_mix is not None else (outs[4],))


def _norm_mm_kernel(x_ref, g_ref, w_ref, o_ref):
    o_ref[...] = _mm(_rms(x_ref[...], g_ref[...]).astype(BF16), w_ref[...]).astype(o_ref.dtype)


def _norm_matmul(x, gain, w, tm, out_dtype):
    t, k = x.shape
    n = w.shape[1]
    return pl.pallas_call(
        _norm_mm_kernel,
        grid=(t // tm,),
        in_specs=[pl.BlockSpec((tm, k), lambda i: (i, 0)), _full((1, k)), _full((k, n))],
        out_specs=pl.BlockSpec((tm, n), lambda i: (i, 0)),
        out_shape=jax.ShapeDtypeStruct((t, n), out_dtype),
        compiler_params=_params("parallel"),
        name="norm_matmul",
    )(x, gain.reshape(1, k), w)


def _mix_attn_kernel(ya_ref, yb_ref, yc_ref, yd_ref, wout_ref, x_ref, g_ref, wq_ref, k_ref, v_ref, wo_ref, o_ref):
    x = x_ref[0]
    for m, y_ref in enumerate((ya_ref, yb_ref, yc_ref, yd_ref)):
        x = x + _mm(y_ref[0].astype(BF16), wout_ref[m])
    hn = _rms(x, g_ref[...]).astype(BF16)
    q = _mm(hn, wq_ref[...])
    d = q.shape[1] // XATTN_HEADS
    outs = []
    for h in range(XATTN_HEADS):
        qh = (q[:, h * d:(h + 1) * d] * d ** -0.5).astype(BF16)
        kh = k_ref[0, :, h * d:(h + 1) * d]
        vh = v_ref[0, :, h * d:(h + 1) * d]
        s = lax.dot_general(qh, kh, (((1,), (1,)), ((), ())), preferred_element_type=F32)
        s = s - jnp.max(s, axis=-1, keepdims=True)
        p = jnp.exp(s)
        p = p / jnp.sum(p, axis=-1, keepdims=True)
        outs.append(_mm(p.astype(BF16), vh))
    o = jnp.concatenate(outs, axis=-1).astype(BF16)
    o_ref[0] = x + _mm(o, wo_ref[...])


def _mix_attention(ys, w_out, x, gain, wq, kv, wo):
    b, s, d = x.shape
    m = kv.shape[1]
    tm = TM_XATTN
    y_spec = pl.BlockSpec((1, tm, GROUP), lambda i, j: (i, j, 0))
    x_spec = pl.BlockSpec((1, tm, d), lambda i, j: (i, j, 0))
    return pl.pallas_call(
        _mix_attn_kernel,
        grid=(b, s // tm),
        in_specs=[y_spec] * 4 + [_full(w_out.shape), x_spec, _full((1, d)), _full((d, d)),
                                 pl.BlockSpec((1, m, d), lambda i, j: (i, 0, 0)),
                                 pl.BlockSpec((1, m, d), lambda i, j: (i, 0, 1)),
                                 _full((d, d))],
        out_specs=x_spec,
        out_shape=jax.ShapeDtypeStruct((b, s, d), F32),
        compiler_params=_params("parallel", "parallel"),
        name="mix_attention",
    )(*ys, w_out, x, gain.reshape(1, d), wq, kv, kv, wo)


def _ffn_kernel(x_ref, g_ref, wg_ref, wu_ref, wd_ref, c_ref, o_ref, hn_ref, acc_ref):
    f = pl.program_id(1)

    @pl.when(f == 0)
    def _():
        hn_ref[...] = _rms(x_ref[...], g_ref[...]).astype(BF16)
        acc_ref[...] = x_ref[...]

    hn = hn_ref[...]
    gate = _mm(hn, wg_ref[0])
    up = _mm(hn, wu_ref[0])
    mid = _silu(gate) * up * c_ref[0]
    acc_ref[...] += _mm(mid.astype(BF16), wd_ref[0])

    @pl.when(f == pl.num_programs(1) - 1)
    def _():
        o_ref[...] = acc_ref[...]


def _ffn(x, gain, wg, wu, wd, combine, tf):
    t, d = x.shape
    e, _, ff = wg.shape
    nf = ff // tf
    tm = TM_FFN
    return pl.pallas_call(
        _ffn_kernel,
        grid=(t // tm, e * nf),
        in_specs=[pl.BlockSpec((tm, d), lambda i, f: (i, 0)),
                  _full((1, d)),
                  pl.BlockSpec((1, d, tf), lambda i, f: (f // nf, 0, f % nf)),
                  pl.BlockSpec((1, d, tf), lambda i, f: (f // nf, 0, f % nf)),
                  pl.BlockSpec((1, tf, d), lambda i, f: (f // nf, f % nf, 0)),
                  pl.BlockSpec((1, tm, 1), lambda i, f: (f // nf, i, 0))],
        out_specs=pl.BlockSpec((tm, d), lambda i, f: (i, 0)),
        out_shape=jax.ShapeDtypeStruct((t, d), F32),
        scratch_shapes=[pltpu.VMEM((tm, d), BF16), pltpu.VMEM((tm, d), F32)],
        compiler_params=_params("parallel", "arbitrary"),
        name="ffn",
    )(x, gain.reshape(1, d), wg, wu, wd, combine)


def _router_kernel(x_ref, g_ref, w_ref, hn_ref, pos_ref, post_ref, cnt_ref):
    hn = _rms(x_ref[...], g_ref[...])
    d = hn.shape[1]
    hn_ref[:, :d] = hn.astype(BF16)
    logits = _mm(hn, w_ref[...], precision=HIGHEST)
    lane = lax.broadcasted_iota(jnp.int32, logits.shape, 1)
    neg = jnp.float32(-jnp.inf)
    logits = jnp.where(lane < N_EXPERTS, logits, neg)
    m1 = jnp.max(logits, axis=-1, keepdims=True)
    i1 = jnp.min(jnp.where(logits == m1, lane, LANES), axis=-1, keepdims=True)
    rest = jnp.where(lane == i1, neg, logits)
    m2 = jnp.max(rest, axis=-1, keepdims=True)
    i2 = jnp.min(jnp.where(rest == m2, lane, LANES), axis=-1, keepdims=True)
    e2 = jnp.exp(m2 - m1)
    comb = jnp.where(lane == i1, 1.0 / (1.0 + e2), 0.0) + jnp.where(lane == i2, e2 / (1.0 + e2), 0.0)
    hi = comb.astype(BF16).astype(F32)
    rest = comb - hi
    mid = rest.astype(BF16).astype(F32)
    lo = rest - mid
    hn_ref[:, d:] = (hi + pltpu.roll(mid, N_EXPERTS, 1) + pltpu.roll(lo, 2 * N_EXPERTS, 1)).astype(BF16)

    sel = (comb > 0.0).astype(F32)
    tb = sel.shape[0]
    row = lax.broadcasted_iota(jnp.int32, sel.shape, 0)
    run = sel
    shift = 1
    while shift < tb:
        run = run + jnp.where(row >= shift, pltpu.roll(run, shift, 0), 0.0)
        shift *= 2
    pos = jnp.where(sel > 0.0, run - 1.0, -1.0)
    pos_ref[...] = pos
    post_ref[0] = jnp.transpose(pos)[:N_EXPERTS, :]
    cnt_ref[0] = run[tb - 1:tb, :].astype(jnp.int32)


def _router(x, gain, w_router):
    t, d = x.shape
    tb = MOE_TB
    nb = t // tb
    w = jnp.zeros((d, LANES), F32).at[:, :N_EXPERTS].set(w_router)
    return pl.pallas_call(
        _router_kernel,
        grid=(nb,),
        in_specs=[pl.BlockSpec((tb, d), lambda i: (i, 0)), _full((1, d)), _full((d, LANES))],
        out_specs=[pl.BlockSpec((tb, d + LANES), lambda i: (i, 0)), pl.BlockSpec((tb, LANES), lambda i: (i, 0)),
                   pl.BlockSpec((1, N_EXPERTS, tb), lambda i: (i, 0, 0)),
                   pl.BlockSpec((1, 1, LANES), lambda i: (i, 0, 0))],
        out_shape=[jax.ShapeDtypeStruct((t, d + LANES), BF16), jax.ShapeDtypeStruct((t, LANES), F32),
                   jax.ShapeDtypeStruct((nb, N_EXPERTS, tb), F32),
                   jax.ShapeDtypeStruct((nb, 1, LANES), jnp.int32)],
        compiler_params=_params("parallel"),
        name="router",
    )(x, gain.reshape(1, d), w)


def _route_tables(cnt, n_tiles_max):
    nb, ne = cnt.shape
    cpad = (cnt + (ROW_ALIGN - 1)) // ROW_ALIGN * ROW_ALIGN
    off = jnp.cumsum(cpad, axis=0) - cpad
    total = jnp.sum(cpad, axis=0)
    ntile = (total + MOE_TILE - 1) // MOE_TILE
    first = jnp.cumsum(ntile) - ntile
    n_tiles = jnp.sum(ntile)
    region = (first + SPARE_TILES * jnp.arange(ne, dtype=jnp.int32)) * MOE_TILE
    dst = region[None, :] + off
    tail = region + total
    sub = jnp.arange(MOE_TB // MOE_WIN, dtype=jnp.int32)[None, None, :] * MOE_WIN
    want = off[:, :, None] + sub
    win_rel = jnp.clip(want, 0, jnp.maximum(ntile * MOE_TILE - MOE_WIN, 0)[None, :, None])
    win = jnp.minimum(first[None, :, None] * MOE_TILE + win_rel, n_tiles_max * MOE_TILE - MOE_WIN)
    shift = want - win_rel
    t = jnp.minimum(jnp.arange(n_tiles_max, dtype=jnp.int32), n_tiles - 1)
    tile_e = jnp.minimum(jnp.sum(t[:, None] >= jnp.cumsum(ntile)[None, :], axis=1), ne - 1).astype(jnp.int32)
    tile_blk = t + SPARE_TILES * tile_e
    valid = (jnp.arange(n_tiles_max, dtype=jnp.int32) < n_tiles).astype(jnp.int32)
    n_blocks = n_tiles_max + SPARE_TILES * ne
    blocks = jnp.arange(n_blocks, dtype=jnp.int32)
    unused = 1 - jnp.max((tile_blk[:, None] == blocks[None, :]) * valid[:, None], axis=0)
    rank = jnp.cumsum(unused) - 1
    fill = jnp.sum(jnp.where((rank[None, :] == blocks[:, None]) & (unused[None, :] > 0), blocks[None, :], 0), axis=1)
    n_fill = jnp.sum(unused).astype(jnp.int32).reshape(1)
    flat = lambda a: a.reshape(-1).astype(jnp.int32)
    return ((flat(cpad), flat(dst), flat(tail), flat(fill), n_fill), (flat(cnt), flat(win), flat(shift)),
            (tile_e, tile_blk.astype(jnp.int32), valid))


def _dispatch_kernel(cpad_ref, dst_ref, tail_ref, fill_ref, nfill_ref, hn_ref, post_ref, xs_ref, seg_ref, sem):
    blk = pl.program_id(0)
    tb = hn_ref.shape[0]
    assert tb == MOE_TILE
    pieces = [(e, j) for e in range(N_EXPERTS) for j in range(tb // MOE_SEG)]

    def needed(e, j):
        return cpad_ref[blk * N_EXPERTS + e] > j * MOE_SEG

    def piece_copy(e, j):
        row = pl.multiple_of(dst_ref[blk * N_EXPERTS + e] + j * MOE_SEG, ROW_ALIGN)
        return pltpu.make_async_copy(seg_ref.at[e, pl.ds(j * MOE_SEG, MOE_SEG)], xs_ref.at[pl.ds(row, MOE_SEG)],
                                     sem.at[e, j])

    rank = lax.broadcasted_iota(jnp.int32, (MOE_SEG, tb), 0).astype(F32)
    for e, j in pieces:
        @pl.when(needed(e, j))
        def _(e=e, j=j):
            onehot = (rank + float(j * MOE_SEG) == post_ref[0, e:e + 1, :]).astype(BF16)
            seg_ref[e, j * MOE_SEG:(j + 1) * MOE_SEG, :] = _mm(onehot, hn_ref[...]).astype(BF16)
            piece_copy(e, j).start()
    for e, j in pieces:
        @pl.when(needed(e, j))
        def _(e=e, j=j):
            piece_copy(e, j).wait()

    @pl.when(blk == pl.num_programs(0) - 1)
    def _():
        seg_ref[0] = jnp.zeros(seg_ref.shape[1:], BF16)

        def tail_copy(e):
            row = pl.multiple_of(tail_ref[e], ROW_ALIGN)
            return pltpu.make_async_copy(seg_ref.at[0], xs_ref.at[pl.ds(row, tb)], sem.at[e, 0])

        for e in range(N_EXPERTS):
            tail_copy(e).start()
        for e in range(N_EXPERTS):
            tail_copy(e).wait()

        def fill_block(k, carry):
            row = pl.multiple_of(fill_ref[k] * MOE_TILE, MOE_TILE)
            copy = pltpu.make_async_copy(seg_ref.at[0], xs_ref.at[pl.ds(row, tb)], sem.at[0, 0])
            copy.start()
            copy.wait()
            return carry

        lax.fori_loop(0, nfill_ref[0], fill_block, 0)


def _dispatch(tables, hn, pos_t, n_rows):
    t, d = hn.shape
    tb = MOE_TB
    return pl.pallas_call(
        _dispatch_kernel,
        grid_spec=pltpu.PrefetchScalarGridSpec(
            num_scalar_prefetch=len(tables),
            grid=(t // tb,),
            in_specs=[pl.BlockSpec((tb, d), lambda i, *_: (i, 0)),
                      pl.BlockSpec((1, N_EXPERTS, tb), lambda i, *_: (i, 0, 0))],
            out_specs=pl.BlockSpec(memory_space=pl.ANY),
            scratch_shapes=[pltpu.VMEM((N_EXPERTS, tb, d), BF16),
                            pltpu.SemaphoreType.DMA((N_EXPERTS, tb // MOE_SEG))]),
        out_shape=jax.ShapeDtypeStruct((n_rows, d), BF16),
        compiler_params=_params("arbitrary"),
        name="moe_dispatch",
    )(*tables, hn, pos_t)


def _gate_up_kernel(te_ref, tb_ref, ok_ref, x_ref, wg_ref, wu_ref, o_ref):
    t = pl.program_id(1)

    @pl.when(ok_ref[t] > 0)
    def _():
        x = x_ref[...]
        o_ref[...] = (_silu(_mm(x, wg_ref[0])) * _mm(x, wu_ref[0])).astype(o_ref.dtype)

    @pl.when(ok_ref[t] == 0)
    def _():
        o_ref[...] = jnp.zeros_like(o_ref)


def _gate_up(tile_e, tile_blk, valid, xs, wg, wu):
    d = wg.shape[1]
    ff = wg.shape[2]
    nt = tile_e.shape[0]
    tf = ff // MOE_F_SPLIT
    w_spec = pl.BlockSpec((1, d, tf), lambda f, t, te, tb, ok: (te[t], 0, f))
    return pl.pallas_call(
        _gate_up_kernel,
        grid_spec=pltpu.PrefetchScalarGridSpec(
            num_scalar_prefetch=3,
            grid=(MOE_F_SPLIT, nt),
            in_specs=[pl.BlockSpec((MOE_TILE, d), lambda f, t, te, tb, ok: (tb[t], 0)), w_spec, w_spec],
            out_specs=pl.BlockSpec((MOE_TILE, tf), lambda f, t, te, tb, ok: (t, f))),
        out_shape=jax.ShapeDtypeStruct((nt * MOE_TILE, ff), BF16),
        compiler_params=_params("arbitrary", "arbitrary"),
        name="moe_gate_up",
    )(tile_e, tile_blk, valid, xs, wg, wu)


def _down_kernel(te_ref, tb_ref, ok_ref, m_ref, gate_ref, wd_ref, o_ref, wbf_ref):
    t = pl.program_id(0)
    expert = te_ref[t]

    @pl.when((t == 0) | (expert != te_ref[jnp.maximum(t - 1, 0)]))
    def _():
        wbf_ref[...] = wd_ref[0].astype(BF16)

    @pl.when(ok_ref[t] > 0)
    def _():
        lane = lax.broadcasted_iota(jnp.int32, gate_ref.shape, 1)
        mine = ((lane & (N_EXPERTS - 1)) == expert) & (lane < 3 * N_EXPERTS)
        gate = jnp.sum(jnp.where(mine, gate_ref[...].astype(F32), 0.0), axis=-1, keepdims=True)
        o_ref[...] = (gate * _mm(m_ref[...], wbf_ref[...])).astype(o_ref.dtype)

    @pl.when(ok_ref[t] == 0)
    def _():
        o_ref[...] = jnp.zeros_like(o_ref)


def _down(tile_e, tile_blk, valid, mid, xs, wd):
    ff, d = wd.shape[1:]
    nt = tile_e.shape[0]
    assert N_EXPERTS & (N_EXPERTS - 1) == 0
    return pl.pallas_call(
        _down_kernel,
        grid_spec=pltpu.PrefetchScalarGridSpec(
            num_scalar_prefetch=3,
            grid=(nt,),
            in_specs=[pl.BlockSpec((MOE_TILE, ff), lambda t, te, tb, ok: (t, 0)),
                      pl.BlockSpec((MOE_TILE, LANES), lambda t, te, tb, ok: (tb[t], d // LANES)),
                      pl.BlockSpec((1, ff, d), lambda t, te, tb, ok: (te[t], 0, 0))],
            out_specs=pl.BlockSpec((MOE_TILE, d), lambda t, te, tb, ok: (t, 0)),
            scratch_shapes=[pltpu.VMEM((ff, d), BF16)]),
        out_shape=jax.ShapeDtypeStruct((nt * MOE_TILE, d), BF16),
        compiler_params=_params("arbitrary"),
        name="moe_down",
    )(tile_e, tile_blk, valid, mid, xs, wd)


def _combine_kernel(final, cnt_ref, win_ref, shift_ref, x_ref, pos_ref, ys_ref, fg_ref, o_ref, win_buf, sem):
    blk = pl.program_id(0)
    tb = x_ref.shape[0]
    n_sub = tb // MOE_WIN
    extra = [(e, j) for e in range(N_EXPERTS) for j in range(1, n_sub)]

    def needed(e, j):
        return cnt_ref[blk * N_EXPERTS + e] > j * MOE_WIN

    def win_copy(e, j):
        row = pl.multiple_of(win_ref[(blk * N_EXPERTS + e) * n_sub + j], ROW_ALIGN)
        return pltpu.make_async_copy(ys_ref.at[pl.ds(row, MOE_WIN)], win_buf.at[j, e], sem.at[j, e])

    def onehot(e, j):
        pos = pos_ref[...][:, e:e + 1]
        in_sub = (pos >= float(j * MOE_WIN)) & (pos < float((j + 1) * MOE_WIN))
        shift = shift_ref[(blk * N_EXPERTS + e) * n_sub + j].astype(F32)
        target = jnp.where(in_sub, pos - float(j * MOE_WIN) + shift, -1.0)
        col = lax.broadcasted_iota(jnp.int32, (tb, MOE_WIN), 1).astype(F32)
        return (col == target).astype(BF16)

    for e in range(N_EXPERTS):
        win_copy(e, 0).start()
    for e, j in extra:
        @pl.when(needed(e, j))
        def _(e=e, j=j):
            win_copy(e, j).start()

    first = jnp.concatenate([onehot(e, 0) for e in range(N_EXPERTS)], axis=1)
    for e in range(N_EXPERTS):
        win_copy(e, 0).wait()
    o_ref[...] = x_ref[...] + _mm(first, win_buf[0].reshape(N_EXPERTS * MOE_WIN, win_buf.shape[3]))
    for e, j in extra:
        @pl.when(needed(e, j))
        def _(e=e, j=j):
            win_copy(e, j).wait()
            o_ref[...] += _mm(onehot(e, j), win_buf[j, e])

    if final:
        o_ref[...] = _rms(o_ref[...], fg_ref[...])


def _combine(tables, x, pos, ys, final_gain):
    t, d = x.shape
    final = final_gain is not None
    gain = final_gain if final else jnp.ones((d,), F32)
    tb = MOE_TB
    n_sub = tb // MOE_WIN
    return pl.pallas_call(
        functools.partial(_combine_kernel, final),
        grid_spec=pltpu.PrefetchScalarGridSpec(
            num_scalar_prefetch=len(tables),
            grid=(t // tb,),
            in_specs=[pl.BlockSpec((tb, d), lambda i, *_: (i, 0)), pl.BlockSpec((tb, LANES), lambda i, *_: (i, 0)),
                      pl.BlockSpec(memory_space=pl.ANY), pl.BlockSpec((1, d), lambda i, *_: (0, 0))],
            out_specs=pl.BlockSpec((tb, d), lambda i, *_: (i, 0)),
            scratch_shapes=[pltpu.VMEM((n_sub, N_EXPERTS, MOE_WIN, d), BF16),
                            pltpu.SemaphoreType.DMA((n_sub, N_EXPERTS))]),
        out_shape=jax.ShapeDtypeStruct((t, d), F32),
        compiler_params=_params("arbitrary"),
        name="moe_combine",
    )(*tables, x, pos, ys, gain.reshape(1, d))


def _moe(x, gain, w_router, wg, wu, wd, final_gain):
    t, d = x.shape
    nb = t // MOE_TB
    n_tiles_max = (2 * t + nb * N_EXPERTS * (ROW_ALIGN - 1)) // MOE_TILE + N_EXPERTS
    hn, pos, pos_t, cnt = _router(x, gain, w_router)
    write_tables, read_tables, (tile_e, tile_blk, valid) = _route_tables(cnt[:, 0, :N_EXPERTS], n_tiles_max)
    n_rows = (n_tiles_max + SPARE_TILES * N_EXPERTS) * MOE_TILE
    xs = _dispatch(write_tables, hn, pos_t, n_rows)
    mid = _gate_up(tile_e, tile_blk, valid, xs, wg, wu)
    ys = _down(tile_e, tile_blk, valid, mid, xs, wd)
    return _combine(read_tables, x, pos, ys, final_gain)


def _final_norm_kernel(x_ref, g_ref, o_ref):
    o_ref[...] = _rms(x_ref[...], g_ref[...])


def _final_norm(x, gain):
    t, d = x.shape
    tm = TM_NORM
    return pl.pallas_call(
        _final_norm_kernel,
        grid=(t // tm,),
        in_specs=[pl.BlockSpec((tm, d), lambda i: (i, 0)), _full((1, d))],
        out_specs=pl.BlockSpec((tm, d), lambda i: (i, 0)),
        out_shape=jax.ShapeDtypeStruct((t, d), F32),
        compiler_params=_params("parallel"),
        name="final_norm",
    )(x, gain.reshape(1, d))


def _hm_groups_spec(n, first, width=HEAD):
    def spec(b, h):
        return pl.BlockSpec((n, b, h, CHUNK, width), lambda c: (first // n, 0, 0, c, 0))
    assert first % n == 0
    return spec


def _tok_chunk_spec(b):
    return pl.BlockSpec((b, CHUNK, GROUP), lambda c: (0, c, 0))


def _expand_pair(x):
    first = lax.broadcasted_iota(jnp.int32, x.shape, 2) < HEAD
    return jnp.concatenate([jnp.where(first, x, 0.0), jnp.where(first, 0.0, x)], axis=1)


def _same_head(n):
    t, s = _tri_masks(n)
    return (t >= n // 2) == (s >= n // 2)


def _hgrn_kernel(qfig_ref, loglb_ref, log1mlb_ref, onemlb_ref, ng_ref, o_ref, st_ref, rows_ref):
    @pl.when(pl.program_id(0) == 0)
    def _():
        st_ref[...] = jnp.zeros_like(st_ref)

    g2 = st_ref.shape[0]
    shape = (g2, CHUNK, PAIR)
    q, fl, v, g = (qfig_ref[j].reshape(shape) for j in range(4))
    log_sig = jnp.minimum(fl, 0.0) - jnp.log(1.0 + jnp.exp(-jnp.abs(fl)))
    la = loglb_ref[...]
    lc = log1mlb_ref[...] + log_sig
    log_f = jnp.maximum(la, lc) + jnp.log(1.0 + jnp.exp(-jnp.abs(la - lc)))
    k = onemlb_ref[...] * jax.nn.sigmoid(-fl)
    b = _cumsum_rows(log_f) * LOG2_E
    head_sum = jnp.broadcast_to(_same_head(PAIR).astype(BF16), (g2, PAIR, PAIR))

    rows_ref[0] = b
    rows_ref[1] = k
    rows_ref[2] = v
    sub = HGRN_SUB
    a_rows = [jnp.zeros((g2, sub, 2 * CHUNK), F32)]
    for i in range(1, CHUNK // sub):
        bref = rows_ref[0, :, i * sub - 1:i * sub, :]
        qd = q[:, i * sub:(i + 1) * sub] * jnp.exp2(b[:, i * sub:(i + 1) * sub] - bref)
        kd = k[:, :i * sub] * jnp.exp2(bref - b[:, :i * sub])
        later = jnp.zeros((g2, CHUNK - i * sub, PAIR), F32)
        a_rows.append(_bmm(qd, _expand_pair(jnp.concatenate([kd, later], axis=1)), nt=True))
    o = _bmm(jnp.concatenate(a_rows, axis=1), _expand_pair(v))

    t_loc = lax.broadcasted_iota(jnp.int32, (g2, sub, PAIR), 1)
    diag = []
    for i in range(CHUNK // sub):
        sl = slice(i * sub, (i + 1) * sub)
        bb, qq = b[:, sl], q[:, sl]
        terms = []
        for s in range(sub):
            row = i * sub + s
            e = jnp.exp2(jnp.where(t_loc >= s, bb - rows_ref[0, :, row:row + 1, :], -BIG_EXPONENT))
            terms.append(e * qq * rows_ref[1, :, row:row + 1, :])
        w = _bmm(jnp.concatenate(terms, axis=1), head_sum)
        acc = jnp.zeros((g2, sub, PAIR), F32)
        for s in range(sub):
            acc = acc + w[:, s * sub:(s + 1) * sub] * rows_ref[2, :, i * sub + s:i * sub + s + 1, :]
        diag.append(acc)
    o = o + jnp.concatenate(diag, axis=1)

    st = st_ref[...]
    o = o + _bmm(q * jnp.exp2(b), st, nt=True)
    b_last = b[:, CHUNK - 1:CHUNK]
    update = _bmm(jnp.swapaxes(v, 1, 2), k * jnp.exp2(b_last - b))
    st_ref[...] = st * jnp.exp2(b_last) + jnp.where(_same_head(PAIR), update, 0.0)

    mean_sq = _bmm(o * o, head_sum) * (1.0 / HEAD)
    _store_tokens(o_ref, o * lax.rsqrt(mean_sq + EPS) * ng_ref[...] * _silu(g))


def _gdn_kernel(qkv_ref, z_ref, gates_ref, cw_ref, alog_ref, dtb_ref, ng_ref, o_ref, xbuf_ref, st_ref):
    g3 = xbuf_ref.shape[0]
    g = g3 // 3

    @pl.when(pl.program_id(0) == 0)
    def _():
        st_ref[...] = jnp.zeros_like(st_ref)
        xbuf_ref[:, 0:HALO, :] = jnp.zeros((g3, HALO, HEAD), F32)

    xbuf_ref[:, HALO:HALO + CHUNK, :] = qkv_ref[...].reshape(g3, CHUNK, HEAD)
    conv = jnp.zeros((g3, CHUNK, HEAD), F32)
    for j in range(CONV_K):
        conv = conv + cw_ref[j] * xbuf_ref[:, HALO - (CONV_K - 1) + j:HALO - (CONV_K - 1) + j + CHUNK, :]
    xbuf_ref[:, 0:HALO, :] = xbuf_ref[:, CHUNK:CHUNK + HALO, :]
    qkv = _silu(conv)
    q, k, v = qkv[0:g], qkv[g:2 * g], qkv[2 * g:3 * g]
    q = q * lax.rsqrt(jnp.sum(q * q, axis=-1, keepdims=True) + EPS) * HEAD ** -0.5
    k = k * lax.rsqrt(jnp.sum(k * k, axis=-1, keepdims=True) + EPS)

    t_idx, s_idx = _tri_masks(CHUNK)
    causal = s_idx <= t_idx
    eye = s_idx == t_idx
    g_row = -jnp.exp(alog_ref[...]) * _softplus(gates_ref[0, 0] + dtb_ref[...])
    g_col = jnp.sum(jnp.where(eye, g_row, 0.0), axis=-1, keepdims=True)
    beta = jnp.sum(jnp.where(eye, jax.nn.sigmoid(gates_ref[1, 0]), 0.0), axis=-1, keepdims=True)
    gc_col = jnp.sum(jnp.where(causal, g_row, 0.0), axis=-1, keepdims=True)
    gc_row = jnp.sum(jnp.where(t_idx <= s_idx, g_col, 0.0), axis=1, keepdims=True)
    rel = jnp.where(causal, jnp.exp(jnp.minimum(gc_col - gc_row, 0.0)), 0.0)

    kb = k * beta
    with_k = _bmm(jnp.concatenate([kb, q], axis=1), k, nt=True)
    lower = jnp.where(s_idx < t_idx, with_k[:, :CHUNK] * rel, 0.0)
    uw = _solve_unit_lower(lower, jnp.concatenate([v * beta, kb * jnp.exp(gc_col)], axis=-1))
    u, w = uw[:, :, :HEAD], uw[:, :, HEAD:]
    scores = with_k[:, CHUNK:] * rel

    st = st_ref[...]
    from_state = _bmm(jnp.concatenate([w, q * jnp.exp(gc_col)], axis=1), st)
    v_new = u - from_state[:, :CHUNK]
    o = from_state[:, CHUNK:] + _bmm(scores, v_new)
    g_last = gc_col[:, CHUNK - 1:CHUNK]
    kd = k * jnp.exp(g_last - gc_col)
    st_ref[...] = st * jnp.exp(g_last) + _bmm(jnp.swapaxes(kd, 1, 2), v_new)

    _store_tokens(o_ref, _rms(o, ng_ref[...]) * _silu(z_ref[0].reshape(g, CHUNK, HEAD)))


def _lru_kernel(x_ref, gate_ref, cw_ref, cb_ref, wa_ref, ba_ref, wx_ref, bx_ref, lam_ref, o_ref,
                xbuf_ref, h_ref):
    tt = x_ref.shape[1]

    @pl.when(pl.program_id(1) == 0)
    def _():
        h_ref[...] = jnp.zeros_like(h_ref)
        xbuf_ref[0:HALO, :] = jnp.zeros((HALO, GROUP), F32)

    xbuf_ref[HALO:HALO + tt, :] = x_ref[0]
    xc = jnp.zeros((tt, GROUP), F32) + cb_ref[...]
    for j in range(CONV_K):
        xc = xc + cw_ref[j:j + 1, :] * xbuf_ref[HALO - (CONV_K - 1) + j:HALO - (CONV_K - 1) + j + tt, :]
    xbuf_ref[0:HALO, :] = xbuf_ref[tt:tt + HALO, :]

    r = jax.nn.sigmoid(_mm16(xc, wa_ref[...]) + ba_ref[...])
    i = jax.nn.sigmoid(_mm16(xc, wx_ref[...]) + bx_ref[...])
    log_a = -LRU_C * r * _softplus(-lam_ref[...])
    a = jnp.exp(log_a)
    u = jnp.sqrt(1.0 - a * a) * (i * xc)

    row = lax.broadcasted_iota(jnp.int32, (tt, GROUP), 0)
    shift = 1
    while shift < tt:
        keep = row >= shift
        a_prev = pltpu.roll(a, shift, 0)
        u_prev = pltpu.roll(u, shift, 0)
        u = jnp.where(keep, a * u_prev + u, u)
        a = jnp.where(keep, a * a_prev, a)
        shift *= 2
    hs = u + a * h_ref[...]
    h_ref[...] = hs[tt - 1:tt, :]

    gate = gate_ref[0]
    gelu = 0.5 * gate * (1.0 + jnp.tanh(math.sqrt(2.0 / math.pi) * (gate + 0.044715 * gate * gate * gate)))
    o_ref[0] = hs * gelu


def _lru_mixer(tok, conv_w, conv_b, wa, ba, wx, bx, lam):
    b, s, _ = tok.shape
    c = GROUP
    tt = TT_SCAN
    x_spec = pl.BlockSpec((1, tt, c), lambda i, t: (i, t, TOK_LRU // GROUP))
    gate_spec = pl.BlockSpec((1, tt, c), lambda i, t: (i, t, TOK_LRU // GROUP + 1))
    vec = _full((1, c))
    return pl.pallas_call(
        _lru_kernel,
        grid=(b, s // tt),
        in_specs=[x_spec, gate_spec, _full((CONV_K, c)), vec, _full((c, c)), vec, _full((c, c)), vec, vec],
        out_specs=pl.BlockSpec((1, tt, c), lambda i, t: (i, t, 0)),
        out_shape=jax.ShapeDtypeStruct((b, s, c), F32),
        scratch_shapes=[pltpu.VMEM((HALO + tt, c), F32), pltpu.VMEM((1, c), F32)],
        compiler_params=_params("parallel", "arbitrary"),
        name="rg_lru",
    )(tok, tok, conv_w, conv_b.reshape(1, c), wa, ba.reshape(1, c), wx, bx.reshape(1, c), lam.reshape(1, c))


def _rwkv_kernel(in_ref, kk_ref, ka_ref, rk_ref, lnw_ref, lnb_ref, o_ref, zt_ref):
    @pl.when(pl.program_id(0) == 0)
    def _():
        zt_ref[...] = jnp.zeros_like(zt_ref)

    shape = zt_ref.shape[:1] + (CHUNK, HEAD)
    r, lw, k, v, a, g = (in_ref[j].reshape(shape) for j in range(6))
    kk = k * kk_ref[...]
    kk = kk * lax.rsqrt(jnp.sum(kk * kk, axis=-1, keepdims=True) + EPS)
    k = k * (1.0 + (a - 1.0) * ka_ref[...])
    b = _cumsum_rows(lw)
    e_pos = jnp.exp(b)
    e_neg = jnp.exp(-b)
    r_t = r * e_pos
    al_t = -kk * jnp.exp(b - lw)
    be_t = kk * a * e_neg
    k_t = k * e_neg

    t_idx, s_idx = _tri_masks(CHUNK)
    strict = s_idx < t_idx
    ab_k = jnp.concatenate([be_t, k_t], axis=1)
    both = _bmm(jnp.concatenate([al_t, r_t], axis=1), ab_k, nt=True)
    l_ab = jnp.where(strict, both[:, :CHUNK, :CHUNK], 0.0)
    l_ak = jnp.where(strict, both[:, :CHUNK, CHUNK:], 0.0)
    t2 = lax.broadcasted_iota(jnp.int32, (CHUNK, 2 * CHUNK), 0)
    s2 = lax.broadcasted_iota(jnp.int32, (CHUNK, 2 * CHUNK), 1)
    incl2 = jnp.where(s2 >= CHUNK, s2 - CHUNK, s2) <= t2
    m_all = jnp.where(incl2, both[:, CHUNK:], 0.0)

    sol = _solve_unit_lower(-l_ab, jnp.concatenate([al_t, _bmm(l_ak, v)], axis=-1))
    w1, u0 = sol[:, :, :HEAD], sol[:, :, HEAD:]

    zt = zt_ref[...]
    from_state = _bmm(jnp.concatenate([w1, r_t], axis=1), zt, nt=True)
    u = from_state[:, :CHUNK] + u0
    uv = jnp.concatenate([u, v], axis=1)
    y = from_state[:, CHUNK:] + _bmm(m_all, uv)
    b_last = b[:, CHUNK - 1:CHUNK]
    e_last = jnp.exp(b_last - b)
    zt_ref[...] = zt * jnp.exp(b_last) + _bmm(jnp.swapaxes(uv, 1, 2),
                                              jnp.concatenate([kk * a * e_last, k * e_last], axis=1))

    mean = jnp.mean(y, axis=-1, keepdims=True)
    var = jnp.mean(jnp.square(y - mean), axis=-1, keepdims=True)
    yn = (y - mean) * lax.rsqrt(var + RWKV_GN_EPS) * lnw_ref[...] + lnb_ref[...]
    bonus = jnp.sum(r * k * rk_ref[...], axis=-1, keepdims=True) * v
    _store_tokens(o_ref, (yn + bonus) * g)


N_HGRN_IN, N_GDN_IN, N_RWKV_IN = 5, 7, 6


def _chunk_mixers_kernel(*refs):
    ins, rest = refs[:N_HGRN_IN + N_GDN_IN + N_RWKV_IN], refs[N_HGRN_IN + N_GDN_IN + N_RWKV_IN:]
    o_hgrn, o_gdn, o_rwkv, st_hgrn, rows_hgrn, xbuf_gdn, st_gdn, zt_rwkv = rest
    _hgrn_kernel(*ins[:N_HGRN_IN], o_hgrn, st_hgrn, rows_hgrn)
    _gdn_kernel(*ins[N_HGRN_IN:N_HGRN_IN + N_GDN_IN], o_gdn, xbuf_gdn, st_gdn)
    _rwkv_kernel(*ins[N_HGRN_IN + N_GDN_IN:], o_rwkv, zt_rwkv)


def _chunk_mixers(hmp, hm, d_hm, hgrn_args, gdn_args, rwkv_args):
    _, b, h, s, d = hm.shape
    g = b * h
    g2 = g // 2
    par = _full((g, 1, d))
    lb, hgrn_norm = hgrn_args
    gates, conv_w, a_log, dt_bias, gdn_norm = gdn_args
    hgrn_specs = [_hm_groups_spec(HMP_GROUPS, 0, PAIR)(b, h // 2)] + [_full((g2, 1, PAIR))] * 4
    gdn_specs = [_hm_groups_spec(3, HM_GDN_QKV)(b, h), _hm_groups_spec(1, HM_GDN_Z)(b, h),
                 pl.BlockSpec((2, 1, g, 1, CHUNK), lambda c: (0, c, 0, 0, 0)),
                 _full((CONV_K, 3 * g, 1, d)), _full((g, 1, 1)), _full((g, 1, 1)), _full((1, 1, d))]
    rwkv_specs = [_hm_groups_spec(6, 0)(b, h)] + [par] * 5
    assert (len(hgrn_specs), len(gdn_specs), len(rwkv_specs)) == (N_HGRN_IN, N_GDN_IN, N_RWKV_IN)
    out = jax.ShapeDtypeStruct((b, s, h * d), F32)
    state = pltpu.VMEM((g, d, d), F32)
    return pl.pallas_call(
        _chunk_mixers_kernel,
        grid=(s // CHUNK,),
        in_specs=hgrn_specs + gdn_specs + rwkv_specs,
        out_specs=[_tok_chunk_spec(b)] * 3,
        out_shape=[out] * 3,
        scratch_shapes=[pltpu.VMEM((g2, PAIR, PAIR), F32), pltpu.VMEM((3, g2, CHUNK, PAIR), F32),
                        pltpu.VMEM((3 * g, HALO + CHUNK, d), F32), state, state],
        compiler_params=_params("arbitrary"),
        name="chunk_mixers",
    )(hmp, jnp.log(lb), jnp.log1p(-lb), 1.0 - lb, hgrn_norm,
      hm, hm, gates, conv_w, a_log, dt_bias, gdn_norm,
      d_hm, *rwkv_args)


def _par_hm(p, b, width=HEAD):
    return jnp.tile(p.reshape(-1, 1, width), (b, 1, 1))


def _gdn_gate_layout(gates):
    b, s, h2 = gates.shape
    g = gates.reshape(b, s // CHUNK, CHUNK, 2, h2 // 2).transpose(3, 1, 0, 4, 2)
    return g.reshape(2, s // CHUNK, b * (h2 // 2), 1, CHUNK)


def _gdn_conv_layout(w, b):
    k, c = w.shape
    h = c // (3 * HEAD)
    w = jnp.broadcast_to(w.reshape(k, 3, 1, h, 1, HEAD), (k, 3, b, h, 1, HEAD))
    return w.reshape(k, 3 * b * h, 1, HEAD)


def _block_diag(w):
    n, a, b = w.shape
    out = jnp.zeros((n * a, n * b), w.dtype)
    for i in range(n):
        out = out.at[i * a:(i + 1) * a, i * b:(i + 1) * b].set(w[i])
    return out


def _in_proj_weight(w):
    d = w.shape[0]
    g4 = 4 * GROUP
    gates = g4 + g4
    lru = gates + 2 * HEADS
    rwkv = lru + 2 * GROUP
    cols = [w[:, rwkv:rwkv + g4], w[:, lru:rwkv], w[:, gates:lru], jnp.zeros((d, LANES - 2 * HEADS), w.dtype),
            w[:, 0:g4], w[:, g4:gates]]
    return jnp.concatenate(cols, axis=1).astype(BF16)


def kernel(x, mem, norm_mix, w_in, w_out, hgrn_lb_logits, hgrn_norm, gdn_conv_w, gdn_A_log, gdn_dt_bias, gdn_norm, lru_conv_w, lru_conv_b, lru_wa, lru_ba, lru_wx, lru_bx, lru_lambda, rwkv_mu, rwkv_w0, rwkv_w2, rwkv_a0, rwkv_a2, rwkv_g2, rwkv_k_k, rwkv_k_a, rwkv_r_k, rwkv_ln_w, rwkv_ln_b, rwkv_v0, rwkv_v1, rwkv_v2, mem_norm, norm_xattn, xattn_wq, xattn_wk, xattn_wv, xattn_wo, norm_ffn, ffn_w_gate, ffn_w_up, ffn_w_down, moe_router, moe_w_gate, moe_w_up, moe_w_down, norm_final):
    bsz, seq, d = x.shape
    depth = w_in.shape[0]
    tok = bsz * seq
    mlen = mem.shape[1]

    lb = jnp.cumsum(jax.nn.softmax(hgrn_lb_logits.astype(F32), axis=0), axis=0)
    lb = lb - lb[0]
    mem_f = mem.reshape(bsz * mlen, d)
    v_first = None
    for l in range(depth):
        v_mix = None if l == 0 else (rwkv_v0[l - 1], rwkv_v1[l - 1], rwkv_v2[l - 1])
        p_tok, p_hmp, p_hm, d_hm, v_first = _in_proj(
            x, norm_mix[l], _in_proj_weight(w_in[l]), rwkv_mu[l], rwkv_w0[l], rwkv_w2[l], rwkv_a0[l], rwkv_a2[l],
            rwkv_g2[l], v_first, v_mix)

        y_c = _lru_mixer(p_tok, lru_conv_w[l], lru_conv_b[l], _block_diag(lru_wa[l]), lru_ba[l],
                         _block_diag(lru_wx[l]), lru_bx[l], lru_lambda[l])

        gates = p_tok[..., TOK_GATES:TOK_GATES + 2 * HEADS]
        y_a, y_b, y_d = _chunk_mixers(
            p_hmp, p_hm, d_hm,
            (_par_hm(lb[l], bsz, PAIR), _par_hm(hgrn_norm[l], bsz, PAIR)),
            (_gdn_gate_layout(gates), _gdn_conv_layout(gdn_conv_w[l], bsz), _par_hm(gdn_A_log[l], bsz, 1),
             _par_hm(gdn_dt_bias[l], bsz, 1), gdn_norm[l].reshape(1, 1, HEAD)),
            tuple(_par_hm(p[l], bsz) for p in (rwkv_k_k, rwkv_k_a, rwkv_r_k, rwkv_ln_w, rwkv_ln_b)))

        w_kv = jnp.concatenate([xattn_wk[l], xattn_wv[l]], axis=1).astype(BF16)
        kv = _norm_matmul(mem_f, mem_norm, w_kv, mlen, BF16).reshape(bsz, mlen, 2 * d)
        x = _mix_attention((y_a, y_b, y_c, y_d), w_out[l].reshape(4, GROUP, d).astype(BF16), x, norm_xattn[l],
                           xattn_wq[l].astype(BF16), kv, xattn_wo[l].astype(BF16))

        xf = x.reshape(tok, d)
        j = l // 2
        if l % 2 == 0:
            ones = jnp.ones((1, tok, 1), F32)
            xf = _ffn(xf, norm_ffn[l], ffn_w_gate[j][None].astype(BF16), ffn_w_up[j][None].astype(BF16),
                      ffn_w_down[j][None].astype(BF16), ones, TF_FFN)
        else:
            xf = _moe(xf, norm_ffn[l], moe_router[j], moe_w_gate[j].astype(BF16), moe_w_up[j].astype(BF16),
                      moe_w_down[j], norm_final if l == depth - 1 else None)
        x = xf.reshape(bsz, seq, d)
    if depth % 2 == 1:
        x = _final_norm(x.reshape(tok, d), norm_final).reshape(bsz, seq, d)
    return x
```

```python
import functools
import math

import jax
import jax.numpy as jnp
from jax import lax
from jax.experimental import pallas as pl
from jax.experimental.pallas import tpu as pltpu

F32 = jnp.float32
BF16 = jnp.bfloat16
HIGHEST = lax.Precision.HIGHEST

GROUP = 256
HEAD = 64
HEADS = GROUP // HEAD
CHUNK = 64
SUB = 16
HGRN_SUB = 8
LOG2_E = 1.4426950408889634
BIG_EXPONENT = 1e30
CONV_K = 4
HALO = 8
LRU_C = 8.0
RWKV_GN_EPS = 64e-5
XATTN_HEADS = 4
N_EXPERTS = 8
EPS = 1e-6
LANES = 128
VMEM_LIMIT_BYTES = 56 * 1024 * 1024

TOK_LRU = 0
TOK_GATES = 2 * GROUP
N_TOK = 2 * GROUP + LANES
PAIR = 2 * HEAD
HMP_GROUPS = 4
HM_GDN_QKV, HM_GDN_Z = 0, 3
HM_GROUPS = 4

TM_IN_PROJ = 512
TM_XATTN = 512
TM_FFN = 1024
TF_FFN = 1408
TM_NORM = 1024
TT_SCAN = 512
MOE_TB = 512
MOE_TILE = 512
MOE_F_SPLIT = 2
ROW_ALIGN = 16
SPARE_TILES = 2
MOE_SEG = 128
MOE_WIN = 256


def _params(*semantics):
    return pltpu.CompilerParams(dimension_semantics=semantics, vmem_limit_bytes=VMEM_LIMIT_BYTES)


def _full(shape):
    n = len(shape)
    return pl.BlockSpec(shape, lambda *_: (0,) * n)


def _bmm(a, b, nt=False, exact=False):
    dn = (((2,), (2 if nt else 1,)), ((0,), (0,)))
    if exact:
        return lax.dot_general(a, b, dn, precision=HIGHEST, preferred_element_type=F32)
    return lax.dot_general(a.astype(BF16), b.astype(BF16), dn, preferred_element_type=F32)


def _mm(a, b, precision=None):
    return jnp.dot(a, b, precision=precision, preferred_element_type=F32)


def _mm16(a, b):
    return jnp.dot(a.astype(BF16), b.astype(BF16), preferred_element_type=F32)


def _softplus(x):
    return jnp.maximum(x, 0.0) + jnp.log1p(jnp.exp(-jnp.abs(x)))


def _silu(x):
    return x * jax.nn.sigmoid(x)


def _rms(x, gain):
    return x * lax.rsqrt(jnp.mean(x * x, axis=-1, keepdims=True) + EPS) * gain


def _tri_masks(n):
    t = lax.broadcasted_iota(jnp.int32, (n, n), 0)
    s = lax.broadcasted_iota(jnp.int32, (n, n), 1)
    return t, s


def _cumsum_rows(x):
    g, c, _ = x.shape
    t, s = _tri_masks(c)
    tri = jnp.broadcast_to((s <= t).astype(BF16), (g, c, c))
    hi = x.astype(BF16)
    rest = x - hi.astype(F32)
    mid = rest.astype(BF16)
    lo = (rest - mid.astype(F32)).astype(BF16)
    return _bmm(tri, hi) + _bmm(tri, mid) + _bmm(tri, lo)


def _solve_unit_lower(low, rhs):
    c = low.shape[1]
    r = rhs.shape[2]
    assert c == 4 * SUB and SUB == 16
    t, s = _tri_masks(c)
    same_block = (t // SUB) == (s // SUB)
    d1 = jnp.where(same_block, low, 0.0)
    z = jnp.concatenate([rhs, jnp.where(same_block, 0.0, low)], axis=-1)
    d2 = _bmm(d1, d1)
    d4 = _bmm(d2, d2)
    d8 = _bmm(d4, d4)
    z = z + _bmm(d8, z)
    z = z + _bmm(d4, z)
    z = z + _bmm(d2, z)
    z = z - _bmm(d1, z)
    y = z[:, :, :r]
    n1 = z[:, :, r:]
    n2 = _bmm(n1, n1)
    y = y + _bmm(n2, y)
    return y - _bmm(n1, y)


def _store_heads(ref, val, lead=(), width=HEAD):
    for h in range(val.shape[1] // width):
        ref[lead + (h,)] = val[:, h * width:(h + 1) * width]


def _store_tokens(ref, o):
    nb = ref.shape[0]
    nh = o.shape[0] // nb
    for b in range(nb):
        ref[b] = jnp.concatenate([o[b * nh + h] for h in range(nh)], axis=-1)


def _in_proj_kernel(has_vmix, *refs):
    refs = list(refs)
    x_ref, g_ref, w_ref, mu_ref, w0_ref, w2_ref, a0_ref, a2_ref, g2_ref = refs[:9]
    del refs[:9]
    if has_vmix:
        vf_ref, v0_ref, v1_ref, v2_ref = refs[:4]
        del refs[:4]
    tok_ref, hmp_ref, hm_ref, dhm_ref = refs[:4]
    del refs[:4]
    vtok_ref = None if has_vmix else refs.pop(0)
    xbuf_ref = refs.pop(0)
    tm = x_ref.shape[1]

    hn = _rms(x_ref[0], g_ref[...]).astype(BF16)
    n_rwkv = 4 * GROUP
    c = _mm(hn, w_ref[:, :n_rwkv])
    acc = _mm(hn, w_ref[:, n_rwkv:])
    col = 0
    n_tok = tok_ref.shape[2]
    tok_ref[0] = acc[:, col:col + n_tok]
    col += n_tok
    for g in range(hmp_ref.shape[0]):
        _store_heads(hmp_ref, acc[:, col:col + GROUP], (g, 0), PAIR)
        col += GROUP
    for g in range(hm_ref.shape[0]):
        _store_heads(hm_ref, acc[:, col:col + GROUP], (g, 0))
        col += GROUP

    @pl.when(pl.program_id(1) == 0)
    def _():
        xbuf_ref[0:HALO, :] = jnp.zeros((HALO, xbuf_ref.shape[1]), F32)

    xbuf_ref[HALO:HALO + tm, :] = c
    prev = xbuf_ref[HALO - 1:HALO - 1 + tm, :]
    xbuf_ref[0:HALO, :] = xbuf_ref[tm:tm + HALO, :]
    c = c + mu_ref[...] * (prev - c)

    r = c[:, 0:GROUP]
    k = c[:, GROUP:2 * GROUP]
    v = c[:, 2 * GROUP:3 * GROUP]
    wa_lo = c[:, 3 * GROUP:3 * GROUP + LANES]
    g_lo = c[:, 3 * GROUP + LANES:4 * GROUP]
    log_w = -math.exp(-0.5) * jax.nn.sigmoid(w0_ref[...] + _mm16(jnp.tanh(wa_lo), w2_ref[...]))
    a = jax.nn.sigmoid(a0_ref[...] + _mm16(wa_lo, a2_ref[...]))
    gate = _mm16(jax.nn.sigmoid(g_lo), g2_ref[...])
    if has_vmix:
        mix = jax.nn.sigmoid(v0_ref[...] + _mm16(_mm16(v, v1_ref[...]), v2_ref[...]))
        v = v + (vf_ref[0] - v) * mix
    else:
        vtok_ref[0] = v
    for j, val in enumerate((r, log_w, k, v, a, gate)):
        _store_heads(dhm_ref, val, (j, 0))


def _in_proj(x, gain, w, mu, w0, w2, a0, a2, g2, v_first, v_mix):
    b, s, d = x.shape
    n = w.shape[1]
    tm = TM_IN_PROJ
    lora = w2.shape[0]
    w2p = jnp.zeros((LANES, GROUP), F32).at[:lora].set(w2)
    a2p = jnp.zeros((LANES, GROUP), F32).at[lora:].set(a2)
    tile = pl.BlockSpec((1, tm, GROUP), lambda i, j: (i, j, 0))
    vec = _full((1, GROUP))
    args = [x, gain.reshape(1, d), w, mu.reshape(1, 4 * GROUP), w0.reshape(1, GROUP), w2p, a0.reshape(1, GROUP),
            a2p, g2]
    specs = [pl.BlockSpec((1, tm, d), lambda i, j: (i, j, 0)), _full((1, d)), _full((d, n)), _full((1, 4 * GROUP)),
             vec, _full((LANES, GROUP)), vec, _full((LANES, GROUP)), _full(g2.shape)]
    out_specs = [pl.BlockSpec((1, tm, N_TOK), lambda i, j: (i, j, 0)),
                 pl.BlockSpec((HMP_GROUPS, 1, HEADS // 2, tm, PAIR), lambda i, j: (0, i, 0, j, 0)),
                 pl.BlockSpec((HM_GROUPS, 1, HEADS, tm, HEAD), lambda i, j: (0, i, 0, j, 0)),
                 pl.BlockSpec((6, 1, HEADS, tm, HEAD), lambda i, j: (0, i, 0, j, 0))]
    out_shape = [jax.ShapeDtypeStruct((b, s, N_TOK), F32),
                 jax.ShapeDtypeStruct((HMP_GROUPS, b, HEADS // 2, s, PAIR), F32),
                 jax.ShapeDtypeStruct((HM_GROUPS, b, HEADS, s, HEAD), F32),
                 jax.ShapeDtypeStruct((6, b, HEADS, s, HEAD), F32)]
    if v_mix is not None:
        v0, v1, v2 = v_mix
        rank = v1.shape[1]
        v1p = jnp.zeros((GROUP, LANES), F32).at[:, :rank].set(v1)
        v2p = jnp.zeros((LANES, GROUP), F32).at[:rank].set(v2)
        args += [v_first, v0.reshape(1, GROUP), v1p, v2p]
        specs += [tile, vec, _full((GROUP, LANES)), _full((LANES, GROUP))]
    else:
        out_specs.append(tile)
        out_shape.append(jax.ShapeDtypeStruct((b, s, GROUP), F32))
    outs = pl.pallas_call(
        functools.partial(_in_proj_kernel, v_mix is not None),
        grid=(b, s // tm),
        in_specs=specs,
        out_specs=out_specs,
        out_shape=out_shape,
        scratch_shapes=[pltpu.VMEM((HALO + tm, 4 * GROUP), F32)],
        compiler_params=_params("parallel", "arbitrary"),
        name="in_proj",
    )(*args)
    return tuple(outs[:4]) + ((v_first,) if v_mix is not None else (outs[4],))


def _norm_mm_kernel(x_ref, g_ref, w_ref, o_ref):
    o_ref[...] = _mm(_rms(x_ref[...], g_ref[...]).astype(BF16), w_ref[...]).astype(o_ref.dtype)


def _norm_matmul(x, gain, w, tm, out_dtype):
    t, k = x.shape
    n = w.shape[1]
    return pl.pallas_call(
        _norm_mm_kernel,
        grid=(t // tm,),
        in_specs=[pl.BlockSpec((tm, k), lambda i: (i, 0)), _full((1, k)), _full((k, n))],
        out_specs=pl.BlockSpec((tm, n), lambda i: (i, 0)),
        out_shape=jax.ShapeDtypeStruct((t, n), out_dtype),
        compiler_params=_params("parallel"),
        name="norm_matmul",
    )(x, gain.reshape(1, k), w)


def _mix_attn_kernel(ya_ref, yb_ref, yc_ref, yd_ref, wout_ref, x_ref, g_ref, wq_ref, k_ref, v_ref, wo_ref, o_ref):
    x = x_ref[0]
    for m, y_ref in enumerate((ya_ref, yb_ref, yc_ref, yd_ref)):
        x = x + _mm(y_ref[0].astype(BF16), wout_ref[m])
    hn = _rms(x, g_ref[...]).astype(BF16)
    q = _mm(hn, wq_ref[...])
    d = q.shape[1] // XATTN_HEADS
    outs = []
    for h in range(XATTN_HEADS):
        qh = (q[:, h * d:(h + 1) * d] * d ** -0.5).astype(BF16)
        kh = k_ref[0, :, h * d:(h + 1) * d]
        vh = v_ref[0, :, h * d:(h + 1) * d]
        s = lax.dot_general(qh, kh, (((1,), (1,)), ((), ())), preferred_element_type=F32)
        s = s - jnp.max(s, axis=-1, keepdims=True)
        p = jnp.exp(s)
        p = p / jnp.sum(p, axis=-1, keepdims=True)
        outs.append(_mm(p.astype(BF16), vh))
    o = jnp.concatenate(outs, axis=-1).astype(BF16)
    o_ref[0] = x + _mm(o, wo_ref[...])


def _mix_attention(ys, w_out, x, gain, wq, kv, wo):
    b, s, d = x.shape
    m = kv.shape[1]
    tm = TM_XATTN
    y_spec = pl.BlockSpec((1, tm, GROUP), lambda i, j: (i, j, 0))
    x_spec = pl.BlockSpec((1, tm, d), lambda i, j: (i, j, 0))
    return pl.pallas_call(
        _mix_attn_kernel,
        grid=(b, s // tm),
        in_specs=[y_spec] * 4 + [_full(w_out.shape), x_spec, _full((1, d)), _full((d, d)),
                                 pl.BlockSpec((1, m, d), lambda i, j: (i, 0, 0)),
                                 pl.BlockSpec((1, m, d), lambda i, j: (i, 0, 1)),
                                 _full((d, d))],
        out_specs=x_spec,
        out_shape=jax.ShapeDtypeStruct((b, s, d), F32),
        compiler_params=_params("parallel", "parallel"),
        name="mix_attention",
    )(*ys, w_out, x, gain.reshape(1, d), wq, kv, kv, wo)


def _ffn_kernel(x_ref, g_ref, wg_ref, wu_ref, wd_ref, c_ref, o_ref, hn_ref, acc_ref):
    f = pl.program_id(1)

    @pl.when(f == 0)
    def _():
        hn_ref[...] = _rms(x_ref[...], g_ref[...]).astype(BF16)
        acc_ref[...] = x_ref[...]

    hn = hn_ref[...]
    gate = _mm(hn, wg_ref[0])
    up = _mm(hn, wu_ref[0])
    mid = _silu(gate) * up * c_ref[0]
    acc_ref[...] += _mm(mid.astype(BF16), wd_ref[0])

    @pl.when(f == pl.num_programs(1) - 1)
    def _():
        o_ref[...] = acc_ref[...]


def _ffn(x, gain, wg, wu, wd, combine, tf):
    t, d = x.shape
    e, _, ff = wg.shape
    nf = ff // tf
    tm = TM_FFN
    return pl.pallas_call(
        _ffn_kernel,
        grid=(t // tm, e * nf),
        in_specs=[pl.BlockSpec((tm, d), lambda i, f: (i, 0)),
                  _full((1, d)),
                  pl.BlockSpec((1, d, tf), lambda i, f: (f // nf, 0, f % nf)),
                  pl.BlockSpec((1, d, tf), lambda i, f: (f // nf, 0, f % nf)),
                  pl.BlockSpec((1, tf, d), lambda i, f: (f // nf, f % nf, 0)),
                  pl.BlockSpec((1, tm, 1), lambda i, f: (f // nf, i, 0))],
        out_specs=pl.BlockSpec((tm, d), lambda i, f: (i, 0)),
        out_shape=jax.ShapeDtypeStruct((t, d), F32),
        scratch_shapes=[pltpu.VMEM((tm, d), BF16), pltpu.VMEM((tm, d), F32)],
        compiler_params=_params("parallel", "arbitrary"),
        name="ffn",
    )(x, gain.reshape(1, d), wg, wu, wd, combine)


def _router_kernel(x_ref, g_ref, w_ref, hn_ref, pos_ref, post_ref, cnt_ref):
    hn = _rms(x_ref[...], g_ref[...])
    d = hn.shape[1]
    hn_ref[:, :d] = hn.astype(BF16)
    logits = _mm(hn, w_ref[...], precision=HIGHEST)
    lane = lax.broadcasted_iota(jnp.int32, logits.shape, 1)
    neg = jnp.float32(-jnp.inf)
    logits = jnp.where(lane < N_EXPERTS, logits, neg)
    m1 = jnp.max(logits, axis=-1, keepdims=True)
    i1 = jnp.min(jnp.where(logits == m1, lane, LANES), axis=-1, keepdims=True)
    rest = jnp.where(lane == i1, neg, logits)
    m2 = jnp.max(rest, axis=-1, keepdims=True)
    i2 = jnp.min(jnp.where(rest == m2, lane, LANES), axis=-1, keepdims=True)
    e2 = jnp.exp(m2 - m1)
    comb = jnp.where(lane == i1, 1.0 / (1.0 + e2), 0.0) + jnp.where(lane == i2, e2 / (1.0 + e2), 0.0)
    hi = comb.astype(BF16).astype(F32)
    rest = comb - hi
    mid = rest.astype(BF16).astype(F32)
    lo = rest - mid
    hn_ref[:, d:] = (hi + pltpu.roll(mid, N_EXPERTS, 1) + pltpu.roll(lo, 2 * N_EXPERTS, 1)).astype(BF16)

    sel = (comb > 0.0).astype(F32)
    tb = sel.shape[0]
    row = lax.broadcasted_iota(jnp.int32, sel.shape, 0)
    run = sel
    shift = 1
    while shift < tb:
        run = run + jnp.where(row >= shift, pltpu.roll(run, shift, 0), 0.0)
        shift *= 2
    pos = jnp.where(sel > 0.0, run - 1.0, -1.0)
    pos_ref[...] = pos
    post_ref[0] = jnp.transpose(pos)[:N_EXPERTS, :]
    cnt_ref[0] = run[tb - 1:tb, :].astype(jnp.int32)


def _router(x, gain, w_router):
    t, d = x.shape
    tb = MOE_TB
    nb = t // tb
    w = jnp.zeros((d, LANES), F32).at[:, :N_EXPERTS].set(w_router)
    return pl.pallas_call(
        _router_kernel,
        grid=(nb,),
        in_specs=[pl.BlockSpec((tb, d), lambda i: (i, 0)), _full((1, d)), _full((d, LANES))],
        out_specs=[pl.BlockSpec((tb, d + LANES), lambda i: (i, 0)), pl.BlockSpec((tb, LANES), lambda i: (i, 0)),
                   pl.BlockSpec((1, N_EXPERTS, tb), lambda i: (i, 0, 0)),
                   pl.BlockSpec((1, 1, LANES), lambda i: (i, 0, 0))],
        out_shape=[jax.ShapeDtypeStruct((t, d + LANES), BF16), jax.ShapeDtypeStruct((t, LANES), F32),
                   jax.ShapeDtypeStruct((nb, N_EXPERTS, tb), F32),
                   jax.ShapeDtypeStruct((nb, 1, LANES), jnp.int32)],
        compiler_params=_params("parallel"),
        name="router",
    )(x, gain.reshape(1, d), w)


def _route_tables(cnt, n_tiles_max):
    nb, ne = cnt.shape
    cpad = (cnt + (ROW_ALIGN - 1)) // ROW_ALIGN * ROW_ALIGN
    off = jnp.cumsum(cpad, axis=0) - cpad
    total = jnp.sum(cpad, axis=0)
    ntile = (total + MOE_TILE - 1) // MOE_TILE
    first = jnp.cumsum(ntile) - ntile
    n_tiles = jnp.sum(ntile)
    region = (first + SPARE_TILES * jnp.arange(ne, dtype=jnp.int32)) * MOE_TILE
    dst = region[None, :] + off
    tail = region + total
    sub = jnp.arange(MOE_TB // MOE_WIN, dtype=jnp.int32)[None, None, :] * MOE_WIN
    want = off[:, :, None] + sub
    win_rel = jnp.clip(want, 0, jnp.maximum(ntile * MOE_TILE - MOE_WIN, 0)[None, :, None])
    win = jnp.minimum(first[None, :, None] * MOE_TILE + win_rel, n_tiles_max * MOE_TILE - MOE_WIN)
    shift = want - win_rel
    t = jnp.minimum(jnp.arange(n_tiles_max, dtype=jnp.int32), n_tiles - 1)
    tile_e = jnp.minimum(jnp.sum(t[:, None] >= jnp.cumsum(ntile)[None, :], axis=1), ne - 1).astype(jnp.int32)
    tile_blk = t + SPARE_TILES * tile_e
    valid = (jnp.arange(n_tiles_max, dtype=jnp.int32) < n_tiles).astype(jnp.int32)
    n_blocks = n_tiles_max + SPARE_TILES * ne
    blocks = jnp.arange(n_blocks, dtype=jnp.int32)
    unused = 1 - jnp.max((tile_blk[:, None] == blocks[None, :]) * valid[:, None], axis=0)
    rank = jnp.cumsum(unused) - 1
    fill = jnp.sum(jnp.where((rank[None, :] == blocks[:, None]) & (unused[None, :] > 0), blocks[None, :], 0), axis=1)
    n_fill = jnp.sum(unused).astype(jnp.int32).reshape(1)
    flat = lambda a: a.reshape(-1).astype(jnp.int32)
    return ((flat(cpad), flat(dst), flat(tail), flat(fill), n_fill), (flat(cnt), flat(win), flat(shift)),
            (tile_e, tile_blk.astype(jnp.int32), valid))


def _dispatch_kernel(cpad_ref, dst_ref, tail_ref, fill_ref, nfill_ref, hn_ref, post_ref, xs_ref, seg_ref, sem):
    blk = pl.program_id(0)
    last = pl.num_programs(0) - 1
    tb = hn_ref.shape[0]
    assert tb == MOE_TILE
    n_piece = tb // MOE_SEG
    slot = blk % 2

    def needed(b, e, j):
        return cpad_ref[b * N_EXPERTS + e] > j * MOE_SEG

    def piece_copy(b, sl, e, j):
        row = pl.multiple_of(dst_ref[b * N_EXPERTS + e] + j * MOE_SEG, ROW_ALIGN)
        return pltpu.make_async_copy(seg_ref.at[sl, e, pl.ds(j * MOE_SEG, MOE_SEG)],
                                     xs_ref.at[pl.ds(row, MOE_SEG)], sem.at[e, j])

    rank = lax.broadcasted_iota(jnp.int32, (MOE_SEG, tb), 0).astype(F32)
    for e in range(N_EXPERTS):
        for j in range(n_piece):
            @pl.when(needed(blk, e, j))
            def _(e=e, j=j):
                onehot = (rank + float(j * MOE_SEG) == post_ref[0, e:e + 1, :]).astype(BF16)
                seg_ref[slot, e, j * MOE_SEG:(j + 1) * MOE_SEG, :] = _mm(onehot, hn_ref[...]).astype(BF16)

        @pl.when(blk > 0)
        def _(e=e):
            for j in range(n_piece):
                @pl.when(needed(blk - 1, e, j))
                def _(j=j):
                    piece_copy(blk - 1, 1 - slot, e, j).wait()

        for j in range(n_piece):
            @pl.when(needed(blk, e, j))
            def _(e=e, j=j):
                piece_copy(blk, slot, e, j).start()

    @pl.when(blk == last)
    def _():
        for e in range(N_EXPERTS):
            for j in range(n_piece):
                @pl.when(needed(blk, e, j))
                def _(e=e, j=j):
                    piece_copy(blk, slot, e, j).wait()

        zeros_ref = seg_ref.at[0, 0]
        zeros_ref[...] = jnp.zeros(zeros_ref.shape, BF16)

        def tail_copy(e):
            row = pl.multiple_of(tail_ref[e], ROW_ALIGN)
            return pltpu.make_async_copy(zeros_ref, xs_ref.at[pl.ds(row, tb)], sem.at[e, 0])

        for e in range(N_EXPERTS):
            tail_copy(e).start()
        for e in range(N_EXPERTS):
            tail_copy(e).wait()

        def fill_block(k, carry):
            row = pl.multiple_of(fill_ref[k] * MOE_TILE, MOE_TILE)
            copy = pltpu.make_async_copy(zeros_ref, xs_ref.at[pl.ds(row, tb)], sem.at[0, 0])
            copy.start()
            copy.wait()
            return carry

        lax.fori_loop(0, nfill_ref[0], fill_block, 0)


def _dispatch(tables, hn, pos_t, n_rows):
    t, d = hn.shape
    tb = MOE_TB
    return pl.pallas_call(
        _dispatch_kernel,
        grid_spec=pltpu.PrefetchScalarGridSpec(
            num_scalar_prefetch=len(tables),
            grid=(t // tb,),
            in_specs=[pl.BlockSpec((tb, d), lambda i, *_: (i, 0)),
                      pl.BlockSpec((1, N_EXPERTS, tb), lambda i, *_: (i, 0, 0))],
            out_specs=pl.BlockSpec(memory_space=pl.ANY),
            scratch_shapes=[pltpu.VMEM((2, N_EXPERTS, tb, d), BF16),
                            pltpu.SemaphoreType.DMA((N_EXPERTS, tb // MOE_SEG))]),
        out_shape=jax.ShapeDtypeStruct((n_rows, d), BF16),
        compiler_params=_params("arbitrary"),
        name="moe_dispatch",
    )(*tables, hn, pos_t)


def _gate_up_kernel(te_ref, tb_ref, ok_ref, x_ref, wg_ref, wu_ref, o_ref):
    t = pl.program_id(1)

    @pl.when(ok_ref[t] > 0)
    def _():
        x = x_ref[...]
        o_ref[...] = (_silu(_mm(x, wg_ref[0])) * _mm(x, wu_ref[0])).astype(o_ref.dtype)

    @pl.when(ok_ref[t] == 0)
    def _():
        o_ref[...] = jnp.zeros_like(o_ref)


def _gate_up(tile_e, tile_blk, valid, xs, wg, wu):
    d = wg.shape[1]
    ff = wg.shape[2]
    nt = tile_e.shape[0]
    tf = ff // MOE_F_SPLIT
    w_spec = pl.BlockSpec((1, d, tf), lambda f, t, te, tb, ok: (te[t], 0, f))
    return pl.pallas_call(
        _gate_up_kernel,
        grid_spec=pltpu.PrefetchScalarGridSpec(
            num_scalar_prefetch=3,
            grid=(MOE_F_SPLIT, nt),
            in_specs=[pl.BlockSpec((MOE_TILE, d), lambda f, t, te, tb, ok: (tb[t], 0)), w_spec, w_spec],
            out_specs=pl.BlockSpec((MOE_TILE, tf), lambda f, t, te, tb, ok: (t, f))),
        out_shape=jax.ShapeDtypeStruct((nt * MOE_TILE, ff), BF16),
        compiler_params=_params("arbitrary", "arbitrary"),
        name="moe_gate_up",
    )(tile_e, tile_blk, valid, xs, wg, wu)


def _down_kernel(te_ref, tb_ref, ok_ref, m_ref, gate_ref, wd_ref, o_ref, wbf_ref):
    t = pl.program_id(0)
    expert = te_ref[t]

    @pl.when((t == 0) | (expert != te_ref[jnp.maximum(t - 1, 0)]))
    def _():
        wbf_ref[...] = wd_ref[0].astype(BF16)

    @pl.when(ok_ref[t] > 0)
    def _():
        lane = lax.broadcasted_iota(jnp.int32, gate_ref.shape, 1)
        mine = ((lane & (N_EXPERTS - 1)) == expert) & (lane < 3 * N_EXPERTS)
        gate = jnp.sum(jnp.where(mine, gate_ref[...].astype(F32), 0.0), axis=-1, keepdims=True)
        o_ref[...] = (gate * _mm(m_ref[...], wbf_ref[...])).astype(o_ref.dtype)

    @pl.when(ok_ref[t] == 0)
    def _():
        o_ref[...] = jnp.zeros_like(o_ref)


def _down(tile_e, tile_blk, valid, mid, xs, wd):
    ff, d = wd.shape[1:]
    nt = tile_e.shape[0]
    assert N_EXPERTS & (N_EXPERTS - 1) == 0
    return pl.pallas_call(
        _down_kernel,
        grid_spec=pltpu.PrefetchScalarGridSpec(
            num_scalar_prefetch=3,
            grid=(nt,),
            in_specs=[pl.BlockSpec((MOE_TILE, ff), lambda t, te, tb, ok: (t, 0)),
                      pl.BlockSpec((MOE_TILE, LANES), lambda t, te, tb, ok: (tb[t], d // LANES)),
                      pl.BlockSpec((1, ff, d), lambda t, te, tb, ok: (te[t], 0, 0))],
            out_specs=pl.BlockSpec((MOE_TILE, d), lambda t, te, tb, ok: (t, 0)),
            scratch_shapes=[pltpu.VMEM((ff, d), BF16)]),
        out_shape=jax.ShapeDtypeStruct((nt * MOE_TILE, d), BF16),
        compiler_params=_params("arbitrary"),
        name="moe_down",
    )(tile_e, tile_blk, valid, mid, xs, wd)


def _combine_kernel(final, cnt_ref, win_ref, shift_ref, x_ref, pos_ref, ys_ref, fg_ref, o_ref, win_buf, sem):
    blk = pl.program_id(0)
    tb = x_ref.shape[0]
    n_sub = tb // MOE_WIN
    extra = [(e, j) for e in range(N_EXPERTS) for j in range(1, n_sub)]
    slot = blk % 2

    def needed(b, e, j):
        return cnt_ref[b * N_EXPERTS + e] > j * MOE_WIN

    def win_copy(b, sl, e, j):
        row = pl.multiple_of(win_ref[(b * N_EXPERTS + e) * n_sub + j], ROW_ALIGN)
        return pltpu.make_async_copy(ys_ref.at[pl.ds(row, MOE_WIN)], win_buf.at[sl, j, e], sem.at[sl, j, e])

    def start_block(b, sl):
        for e in range(N_EXPERTS):
            win_copy(b, sl, e, 0).start()
        for e, j in extra:
            @pl.when(needed(b, e, j))
            def _(e=e, j=j):
                win_copy(b, sl, e, j).start()

    @pl.when(blk == 0)
    def _():
        start_block(blk, slot)

    @pl.when(blk + 1 < pl.num_programs(0))
    def _():
        start_block(blk + 1, 1 - slot)

    def onehot(e, j):
        pos = pos_ref[...][:, e:e + 1]
        in_sub = (pos >= float(j * MOE_WIN)) & (pos < float((j + 1) * MOE_WIN))
        shift = shift_ref[(blk * N_EXPERTS + e) * n_sub + j].astype(F32)
        target = jnp.where(in_sub, pos - float(j * MOE_WIN) + shift, -1.0)
        col = lax.broadcasted_iota(jnp.int32, (tb, MOE_WIN), 1).astype(F32)
        return (col == target).astype(BF16)

    first = jnp.concatenate([onehot(e, 0) for e in range(N_EXPERTS)], axis=1)
    for e in range(N_EXPERTS):
        win_copy(blk, slot, e, 0).wait()
    o_ref[...] = x_ref[...] + _mm(first, win_buf[slot, 0].reshape(N_EXPERTS * MOE_WIN, win_buf.shape[4]))
    for e, j in extra:
        @pl.when(needed(blk, e, j))
        def _(e=e, j=j):
            win_copy(blk, slot, e, j).wait()
            o_ref[...] += _mm(onehot(e, j), win_buf[slot, j, e])

    if final:
        o_ref[...] = _rms(o_ref[...], fg_ref[...])


def _combine(tables, x, pos, ys, final_gain):
    t, d = x.shape
    final = final_gain is not None
    gain = final_gain if final else jnp.ones((d,), F32)
    tb = MOE_TB
    n_sub = tb // MOE_WIN
    return pl.pallas_call(
        functools.partial(_combine_kernel, final),
        grid_spec=pltpu.PrefetchScalarGridSpec(
            num_scalar_prefetch=len(tables),
            grid=(t // tb,),
            in_specs=[pl.BlockSpec((tb, d), lambda i, *_: (i, 0)), pl.BlockSpec((tb, LANES), lambda i, *_: (i, 0)),
                      pl.BlockSpec(memory_space=pl.ANY), pl.BlockSpec((1, d), lambda i, *_: (0, 0))],
            out_specs=pl.BlockSpec((tb, d), lambda i, *_: (i, 0)),
            scratch_shapes=[pltpu.VMEM((2, n_sub, N_EXPERTS, MOE_WIN, d), BF16),
                            pltpu.SemaphoreType.DMA((2, n_sub, N_EXPERTS))]),
        out_shape=jax.ShapeDtypeStruct((t, d), F32),
        compiler_params=_params("arbitrary"),
        name="moe_combine",
    )(*tables, x, pos, ys, gain.reshape(1, d))


def _moe(x, gain, w_router, wg, wu, wd, final_gain):
    t, d = x.shape
    nb = t // MOE_TB
    n_tiles_max = (2 * t + nb * N_EXPERTS * (ROW_ALIGN - 1)) // MOE_TILE + N_EXPERTS
    hn, pos, pos_t, cnt = _router(x, gain, w_router)
    write_tables, read_tables, (tile_e, tile_blk, valid) = _route_tables(cnt[:, 0, :N_EXPERTS], n_tiles_max)
    n_rows = (n_tiles_max + SPARE_TILES * N_EXPERTS) * MOE_TILE
    xs = _dispatch(write_tables, hn, pos_t, n_rows)
    mid = _gate_up(tile_e, tile_blk, valid, xs, wg, wu)
    ys = _down(tile_e, tile_blk, valid, mid, xs, wd)
    return _combine(read_tables, x, pos, ys, final_gain)


def _final_norm_kernel(x_ref, g_ref, o_ref):
    o_ref[...] = _rms(x_ref[...], g_ref[...])


def _final_norm(x, gain):
    t, d = x.shape
    tm = TM_NORM
    return pl.pallas_call(
        _final_norm_kernel,
        grid=(t // tm,),
        in_specs=[pl.BlockSpec((tm, d), lambda i: (i, 0)), _full((1, d))],
        out_specs=pl.BlockSpec((tm, d), lambda i: (i, 0)),
        out_shape=jax.ShapeDtypeStruct((t, d), F32),
        compiler_params=_params("parallel"),
        name="final_norm",
    )(x, gain.reshape(1, d))


def _hm_groups_spec(n, first, width=HEAD):
    def spec(b, h):
        return pl.BlockSpec((n, b, h, CHUNK, width), lambda c: (first // n, 0, 0, c, 0))
    assert first % n == 0
    return spec


def _tok_chunk_spec(b):
    return pl.BlockSpec((b, CHUNK, GROUP), lambda c: (0, c, 0))


def _expand_pair(x):
    first = lax.broadcasted_iota(jnp.int32, x.shape, 2) < HEAD
    return jnp.concatenate([jnp.where(first, x, 0.0), jnp.where(first, 0.0, x)], axis=1)


def _same_head(n):
    t, s = _tri_masks(n)
    return (t >= n // 2) == (s >= n // 2)


def _hgrn_kernel(qfig_ref, loglb_ref, log1mlb_ref, onemlb_ref, ng_ref, o_ref, st_ref, rows_ref):
    @pl.when(pl.program_id(0) == 0)
    def _():
        st_ref[...] = jnp.zeros_like(st_ref)

    g2 = st_ref.shape[0]
    shape = (g2, CHUNK, PAIR)
    q, fl, v, g = (qfig_ref[j].reshape(shape) for j in range(4))
    log_sig = jnp.minimum(fl, 0.0) - jnp.log(1.0 + jnp.exp(-jnp.abs(fl)))
    la = loglb_ref[...]
    lc = log1mlb_ref[...] + log_sig
    log_f = jnp.maximum(la, lc) + jnp.log(1.0 + jnp.exp(-jnp.abs(la - lc)))
    k = onemlb_ref[...] * jax.nn.sigmoid(-fl)
    b = _cumsum_rows(log_f) * LOG2_E
    head_sum = jnp.broadcast_to(_same_head(PAIR).astype(BF16), (g2, PAIR, PAIR))

    rows_ref[0] = b
    rows_ref[1] = k
    rows_ref[2] = v
    sub = HGRN_SUB
    a_rows = [jnp.zeros((g2, sub, 2 * CHUNK), F32)]
    for i in range(1, CHUNK // sub):
        bref = rows_ref[0, :, i * sub - 1:i * sub, :]
        qd = q[:, i * sub:(i + 1) * sub] * jnp.exp2(b[:, i * sub:(i + 1) * sub] - bref)
        kd = k[:, :i * sub] * jnp.exp2(bref - b[:, :i * sub])
        later = jnp.zeros((g2, CHUNK - i * sub, PAIR), F32)
        a_rows.append(_bmm(qd, _expand_pair(jnp.concatenate([kd, later], axis=1)), nt=True))
    o = _bmm(jnp.concatenate(a_rows, axis=1), _expand_pair(v))

    t_loc = lax.broadcasted_iota(jnp.int32, (g2, sub, PAIR), 1)
    diag = []
    for i in range(CHUNK // sub):
        sl = slice(i * sub, (i + 1) * sub)
        bb, qq = b[:, sl], q[:, sl]
        terms = []
        for s in range(sub):
            row = i * sub + s
            e = jnp.exp2(jnp.where(t_loc >= s, bb - rows_ref[0, :, row:row + 1, :], -BIG_EXPONENT))
            terms.append(e * qq * rows_ref[1, :, row:row + 1, :])
        w = _bmm(jnp.concatenate(terms, axis=1), head_sum)
        acc = jnp.zeros((g2, sub, PAIR), F32)
        for s in range(sub):
            acc = acc + w[:, s * sub:(s + 1) * sub] * rows_ref[2, :, i * sub + s:i * sub + s + 1, :]
        diag.append(acc)
    o = o + jnp.concatenate(diag, axis=1)

    st = st_ref[...]
    o = o + _bmm(q * jnp.exp2(b), st, nt=True)
    b_last = b[:, CHUNK - 1:CHUNK]
    update = _bmm(jnp.swapaxes(v, 1, 2), k * jnp.exp2(b_last - b))
    st_ref[...] = st * jnp.exp2(b_last) + jnp.where(_same_head(PAIR), update, 0.0)

    mean_sq = _bmm(o * o, head_sum) * (1.0 / HEAD)
    _store_tokens(o_ref, o * lax.rsqrt(mean_sq + EPS) * ng_ref[...] * _silu(g))


def _gdn_kernel(qkv_ref, z_ref, gates_ref, cw_ref, alog_ref, dtb_ref, ng_ref, o_ref, xbuf_ref, st_ref):
    g3 = xbuf_ref.shape[0]
    g = g3 // 3

    @pl.when(pl.program_id(0) == 0)
    def _():
        st_ref[...] = jnp.zeros_like(st_ref)
        xbuf_ref[:, 0:HALO, :] = jnp.zeros((g3, HALO, HEAD), F32)

    xbuf_ref[:, HALO:HALO + CHUNK, :] = qkv_ref[...].reshape(g3, CHUNK, HEAD)
    conv = jnp.zeros((g3, CHUNK, HEAD), F32)
    for j in range(CONV_K):
        conv = conv + cw_ref[j] * xbuf_ref[:, HALO - (CONV_K - 1) + j:HALO - (CONV_K - 1) + j + CHUNK, :]
    xbuf_ref[:, 0:HALO, :] = xbuf_ref[:, CHUNK:CHUNK + HALO, :]
    qkv = _silu(conv)
    q, k, v = qkv[0:g], qkv[g:2 * g], qkv[2 * g:3 * g]
    q = q * lax.rsqrt(jnp.sum(q * q, axis=-1, keepdims=True) + EPS) * HEAD ** -0.5
    k = k * lax.rsqrt(jnp.sum(k * k, axis=-1, keepdims=True) + EPS)

    t_idx, s_idx = _tri_masks(CHUNK)
    causal = s_idx <= t_idx
    eye = s_idx == t_idx
    g_row = -jnp.exp(alog_ref[...]) * _softplus(gates_ref[0, 0] + dtb_ref[...])
    g_col = jnp.sum(jnp.where(eye, g_row, 0.0), axis=-1, keepdims=True)
    beta = jnp.sum(jnp.where(eye, jax.nn.sigmoid(gates_ref[1, 0]), 0.0), axis=-1, keepdims=True)
    gc_col = jnp.sum(jnp.where(causal, g_row, 0.0), axis=-1, keepdims=True)
    gc_row = jnp.sum(jnp.where(t_idx <= s_idx, g_col, 0.0), axis=1, keepdims=True)
    rel = jnp.where(causal, jnp.exp(jnp.minimum(gc_col - gc_row, 0.0)), 0.0)

    kb = k * beta
    with_k = _bmm(jnp.concatenate([kb, q], axis=1), k, nt=True)
    lower = jnp.where(s_idx < t_idx, with_k[:, :CHUNK] * rel, 0.0)
    uw = _solve_unit_lower(lower, jnp.concatenate([v * beta, kb * jnp.exp(gc_col)], axis=-1))
    u, w = uw[:, :, :HEAD], uw[:, :, HEAD:]
    scores = with_k[:, CHUNK:] * rel

    st = st_ref[...]
    from_state = _bmm(jnp.concatenate([w, q * jnp.exp(gc_col)], axis=1), st)
    v_new = u - from_state[:, :CHUNK]
    o = from_state[:, CHUNK:] + _bmm(scores, v_new)
    g_last = gc_col[:, CHUNK - 1:CHUNK]
    kd = k * jnp.exp(g_last - gc_col)
    st_ref[...] = st * jnp.exp(g_last) + _bmm(jnp.swapaxes(kd, 1, 2), v_new)

    _store_tokens(o_ref, _rms(o, ng_ref[...]) * _silu(z_ref[0].reshape(g, CHUNK, HEAD)))


def _lru_kernel(x_ref, gate_ref, cw_ref, cb_ref, wa_ref, ba_ref, wx_ref, bx_ref, lam_ref, o_ref,
                xbuf_ref, h_ref):
    tt = x_ref.shape[1]

    @pl.when(pl.program_id(1) == 0)
    def _():
        h_ref[...] = jnp.zeros_like(h_ref)
        xbuf_ref[0:HALO, :] = jnp.zeros((HALO, GROUP), F32)

    xbuf_ref[HALO:HALO + tt, :] = x_ref[0]
    xc = jnp.zeros((tt, GROUP), F32) + cb_ref[...]
    for j in range(CONV_K):
        xc = xc + cw_ref[j:j + 1, :] * xbuf_ref[HALO - (CONV_K - 1) + j:HALO - (CONV_K - 1) + j + tt, :]
    xbuf_ref[0:HALO, :] = xbuf_ref[tt:tt + HALO, :]

    r = jax.nn.sigmoid(_mm16(xc, wa_ref[...]) + ba_ref[...])
    i = jax.nn.sigmoid(_mm16(xc, wx_ref[...]) + bx_ref[...])
    log_a = -LRU_C * r * _softplus(-lam_ref[...])
    a = jnp.exp(log_a)
    u = jnp.sqrt(1.0 - a * a) * (i * xc)

    row = lax.broadcasted_iota(jnp.int32, (tt, GROUP), 0)
    shift = 1
    while shift < tt:
        keep = row >= shift
        a_prev = pltpu.roll(a, shift, 0)
        u_prev = pltpu.roll(u, shift, 0)
        u = jnp.where(keep, a * u_prev + u, u)
        a = jnp.where(keep, a * a_prev, a)
        shift *= 2
    hs = u + a * h_ref[...]
    h_ref[...] = hs[tt - 1:tt, :]

    gate = gate_ref[0]
    gelu = 0.5 * gate * (1.0 + jnp.tanh(math.sqrt(2.0 / math.pi) * (gate + 0.044715 * gate * gate * gate)))
    o_ref[0] = hs * gelu


def _lru_mixer(tok, conv_w, conv_b, wa, ba, wx, bx, lam):
    b, s, _ = tok.shape
    c = GROUP
    tt = TT_SCAN
    x_spec = pl.BlockSpec((1, tt, c), lambda i, t: (i, t, TOK_LRU // GROUP))
    gate_spec = pl.BlockSpec((1, tt, c), lambda i, t: (i, t, TOK_LRU // GROUP + 1))
    vec = _full((1, c))
    return pl.pallas_call(
        _lru_kernel,
        grid=(b, s // tt),
        in_specs=[x_spec, gate_spec, _full((CONV_K, c)), vec, _full((c, c)), vec, _full((c, c)), vec, vec],
        out_specs=pl.BlockSpec((1, tt, c), lambda i, t: (i, t, 0)),
        out_shape=jax.ShapeDtypeStruct((b, s, c), F32),
        scratch_shapes=[pltpu.VMEM((HALO + tt, c), F32), pltpu.VMEM((1, c), F32)],
        compiler_params=_params("parallel", "arbitrary"),
        name="rg_lru",
    )(tok, tok, conv_w, conv_b.reshape(1, c), wa, ba.reshape(1, c), wx, bx.reshape(1, c), lam.reshape(1, c))


def _rwkv_kernel(in_ref, kk_ref, ka_ref, rk_ref, lnw_ref, lnb_ref, o_ref, zt_ref):
    @pl.when(pl.program_id(0) == 0)
    def _():
        zt_ref[...] = jnp.zeros_like(zt_ref)

    shape = zt_ref.shape[:1] + (CHUNK, HEAD)
    r, lw, k, v, a, g = (in_ref[j].reshape(shape) for j in range(6))
    kk = k * kk_ref[...]
    kk = kk * lax.rsqrt(jnp.sum(kk * kk, axis=-1, keepdims=True) + EPS)
    k = k * (1.0 + (a - 1.0) * ka_ref[...])
    b = _cumsum_rows(lw)
    e_pos = jnp.exp(b)
    e_neg = jnp.exp(-b)
    r_t = r * e_pos
    al_t = -kk * jnp.exp(b - lw)
    be_t = kk * a * e_neg
    k_t = k * e_neg

    t_idx, s_idx = _tri_masks(CHUNK)
    strict = s_idx < t_idx
    ab_k = jnp.concatenate([be_t, k_t], axis=1)
    both = _bmm(jnp.concatenate([al_t, r_t], axis=1), ab_k, nt=True)
    l_ab = jnp.where(strict, both[:, :CHUNK, :CHUNK], 0.0)
    l_ak = jnp.where(strict, both[:, :CHUNK, CHUNK:], 0.0)
    t2 = lax.broadcasted_iota(jnp.int32, (CHUNK, 2 * CHUNK), 0)
    s2 = lax.broadcasted_iota(jnp.int32, (CHUNK, 2 * CHUNK), 1)
    incl2 = jnp.where(s2 >= CHUNK, s2 - CHUNK, s2) <= t2
    m_all = jnp.where(incl2, both[:, CHUNK:], 0.0)

    sol = _solve_unit_lower(-l_ab, jnp.concatenate([al_t, _bmm(l_ak, v)], axis=-1))
    w1, u0 = sol[:, :, :HEAD], sol[:, :, HEAD:]

    zt = zt_ref[...]
    from_state = _bmm(jnp.concatenate([w1, r_t], axis=1), zt, nt=True)
    u = from_state[:, :CHUNK] + u0
    uv = jnp.concatenate([u, v], axis=1)
    y = from_state[:, CHUNK:] + _bmm(m_all, uv)
    b_last = b[:, CHUNK - 1:CHUNK]
    e_last = jnp.exp(b_last - b)
    zt_ref[...] = zt * jnp.exp(b_last) + _bmm(jnp.swapaxes(uv, 1, 2),
                                              jnp.concatenate([kk * a * e_last, k * e_last], axis=1))

    mean = jnp.mean(y, axis=-1, keepdims=True)
    var = jnp.mean(jnp.square(y - mean), axis=-1, keepdims=True)
    yn = (y - mean) * lax.rsqrt(var + RWKV_GN_EPS) * lnw_ref[...] + lnb_ref[...]
    bonus = jnp.sum(r * k * rk_ref[...], axis=-1, keepdims=True) * v
    _store_tokens(o_ref, (yn + bonus) * g)


N_HGRN_IN, N_GDN_IN, N_RWKV_IN = 5, 7, 6


def _chunk_mixers_kernel(*refs):
    ins, rest = refs[:N_HGRN_IN + N_GDN_IN + N_RWKV_IN], refs[N_HGRN_IN + N_GDN_IN + N_RWKV_IN:]
    o_hgrn, o_gdn, o_rwkv, st_hgrn, rows_hgrn, xbuf_gdn, st_gdn, zt_rwkv = rest
    _hgrn_kernel(*ins[:N_HGRN_IN], o_hgrn, st_hgrn, rows_hgrn)
    _gdn_kernel(*ins[N_HGRN_IN:N_HGRN_IN + N_GDN_IN], o_gdn, xbuf_gdn, st_gdn)
    _rwkv_kernel(*ins[N_HGRN_IN + N_GDN_IN:], o_rwkv, zt_rwkv)


def _chunk_mixers(hmp, hm, d_hm, hgrn_args, gdn_args, rwkv_args):
    _, b, h, s, d = hm.shape
    g = b * h
    g2 = g // 2
    par = _full((g, 1, d))
    lb, hgrn_norm = hgrn_args
    gates, conv_w, a_log, dt_bias, gdn_norm = gdn_args
    hgrn_specs = [_hm_groups_spec(HMP_GROUPS, 0, PAIR)(b, h // 2)] + [_full((g2, 1, PAIR))] * 4
    gdn_specs = [_hm_groups_spec(3, HM_GDN_QKV)(b, h), _hm_groups_spec(1, HM_GDN_Z)(b, h),
                 pl.BlockSpec((2, 1, g, 1, CHUNK), lambda c: (0, c, 0, 0, 0)),
                 _full((CONV_K, 3 * g, 1, d)), _full((g, 1, 1)), _full((g, 1, 1)), _full((1, 1, d))]
    rwkv_specs = [_hm_groups_spec(6, 0)(b, h)] + [par] * 5
    assert (len(hgrn_specs), len(gdn_specs), len(rwkv_specs)) == (N_HGRN_IN, N_GDN_IN, N_RWKV_IN)
    out = jax.ShapeDtypeStruct((b, s, h * d), F32)
    state = pltpu.VMEM((g, d, d), F32)
    return pl.pallas_call(
        _chunk_mixers_kernel,
        grid=(s // CHUNK,),
        in_specs=hgrn_specs + gdn_specs + rwkv_specs,
        out_specs=[_tok_chunk_spec(b)] * 3,
        out_shape=[out] * 3,
        scratch_shapes=[pltpu.VMEM((g2, PAIR, PAIR), F32), pltpu.VMEM((3, g2, CHUNK, PAIR), F32),
                        pltpu.VMEM((3 * g, HALO + CHUNK, d), F32), state, state],
        compiler_params=_params("arbitrary"),
        name="chunk_mixers",
    )(hmp, jnp.log(lb), jnp.log1p(-lb), 1.0 - lb, hgrn_norm,
      hm, hm, gates, conv_w, a_log, dt_bias, gdn_norm,
      d_hm, *rwkv_args)


def _par_hm(p, b, width=HEAD):
    return jnp.tile(p.reshape(-1, 1, width), (b, 1, 1))


def _gdn_gate_layout(gates):
    b, s, h2 = gates.shape
    g = gates.reshape(b, s // CHUNK, CHUNK, 2, h2 // 2).transpose(3, 1, 0, 4, 2)
    return g.reshape(2, s // CHUNK, b * (h2 // 2), 1, CHUNK)


def _gdn_conv_layout(w, b):
    k, c = w.shape
    h = c // (3 * HEAD)
    w = jnp.broadcast_to(w.reshape(k, 3, 1, h, 1, HEAD), (k, 3, b, h, 1, HEAD))
    return w.reshape(k, 3 * b * h, 1, HEAD)


def _block_diag(w):
    n, a, b = w.shape
    out = jnp.zeros((n * a, n * b), w.dtype)
    for i in range(n):
        out = out.at[i * a:(i + 1) * a, i * b:(i + 1) * b].set(w[i])
    return out


def _in_proj_weight(w):
    d = w.shape[0]
    g4 = 4 * GROUP
    gates = g4 + g4
    lru = gates + 2 * HEADS
    rwkv = lru + 2 * GROUP
    cols = [w[:, rwkv:rwkv + g4], w[:, lru:rwkv], w[:, gates:lru], jnp.zeros((d, LANES - 2 * HEADS), w.dtype),
            w[:, 0:g4], w[:, g4:gates]]
    return jnp.concatenate(cols, axis=1).astype(BF16)


def kernel(x, mem, norm_mix, w_in, w_out, hgrn_lb_logits, hgrn_norm, gdn_conv_w, gdn_A_log, gdn_dt_bias, gdn_norm, lru_conv_w, lru_conv_b, lru_wa, lru_ba, lru_wx, lru_bx, lru_lambda, rwkv_mu, rwkv_w0, rwkv_w2, rwkv_a0, rwkv_a2, rwkv_g2, rwkv_k_k, rwkv_k_a, rwkv_r_k, rwkv_ln_w, rwkv_ln_b, rwkv_v0, rwkv_v1, rwkv_v2, mem_norm, norm_xattn, xattn_wq, xattn_wk, xattn_wv, xattn_wo, norm_ffn, ffn_w_gate, ffn_w_up, ffn_w_down, moe_router, moe_w_gate, moe_w_up, moe_w_down, norm_final):
    bsz, seq, d = x.shape
    depth = w_in.shape[0]
    tok = bsz * seq
    mlen = mem.shape[1]

    lb = jnp.cumsum(jax.nn.softmax(hgrn_lb_logits.astype(F32), axis=0), axis=0)
    lb = lb - lb[0]
    mem_f = mem.reshape(bsz * mlen, d)
    v_first = None
    for l in range(depth):
        v_mix = None if l == 0 else (rwkv_v0[l - 1], rwkv_v1[l - 1], rwkv_v2[l - 1])
        p_tok, p_hmp, p_hm, d_hm, v_first = _in_proj(
            x, norm_mix[l], _in_proj_weight(w_in[l]), rwkv_mu[l], rwkv_w0[l], rwkv_w2[l], rwkv_a0[l], rwkv_a2[l],
            rwkv_g2[l], v_first, v_mix)

        y_c = _lru_mixer(p_tok, lru_conv_w[l], lru_conv_b[l], _block_diag(lru_wa[l]), lru_ba[l],
                         _block_diag(lru_wx[l]), lru_bx[l], lru_lambda[l])

        gates = p_tok[..., TOK_GATES:TOK_GATES + 2 * HEADS]
        y_a, y_b, y_d = _chunk_mixers(
            p_hmp, p_hm, d_hm,
            (_par_hm(lb[l], bsz, PAIR), _par_hm(hgrn_norm[l], bsz, PAIR)),
            (_gdn_gate_layout(gates), _gdn_conv_layout(gdn_conv_w[l], bsz), _par_hm(gdn_A_log[l], bsz, 1),
             _par_hm(gdn_dt_bias[l], bsz, 1), gdn_norm[l].reshape(1, 1, HEAD)),
            tuple(_par_hm(p[l], bsz) for p in (rwkv_k_k, rwkv_k_a, rwkv_r_k, rwkv_ln_w, rwkv_ln_b)))

        w_kv = jnp.concatenate([xattn_wk[l], xattn_wv[l]], axis=1).astype(BF16)
        kv = _norm_matmul(mem_f, mem_norm, w_kv, mlen, BF16).reshape(bsz, mlen, 2 * d)
        x = _mix_attention((y_a, y_b, y_c, y_d), w_out[l].reshape(4, GROUP, d).astype(BF16), x, norm_xattn[l],
                           xattn_wq[l].astype(BF16), kv, xattn_wo[l].astype(BF16))

        xf = x.reshape(tok, d)
        j = l // 2
        if l % 2 == 0:
            ones = jnp.ones((1, tok, 1), F32)
            xf = _ffn(xf, norm_ffn[l], ffn_w_gate[j][None].astype(BF16), ffn_w_up[j][None].astype(BF16),
                      ffn_w_down[j][None].astype(BF16), ones, TF_FFN)
        else:
            xf = _moe(xf, norm_ffn[l], moe_router[j], moe_w_gate[j].astype(BF16), moe_w_up[j].astype(BF16),
                      moe_w_down[j], norm_final if l == depth - 1 else None)
        x = xf.reshape(bsz, seq, d)
    if depth % 2 == 1:
        x = _final_norm(x.reshape(tok, d), norm_final).reshape(bsz, seq, d)
    return x
```

```python
import functools
import math

import jax
import jax.numpy as jnp
from jax import lax
from jax.experimental import pallas as pl
from jax.experimental.pallas import tpu as pltpu

F32 = jnp.float32
BF16 = jnp.bfloat16
HIGHEST = lax.Precision.HIGHEST

GROUP = 256
HEAD = 64
HEADS = GROUP // HEAD
CHUNK = 64
SUB = 16
HGRN_SUB = 8
LOG2_E = 1.4426950408889634
BIG_EXPONENT = 1e30
CONV_K = 4
HALO = 8
LRU_C = 8.0
RWKV_GN_EPS = 64e-5
XATTN_HEADS = 4
N_EXPERTS = 8
EPS = 1e-6
LANES = 128
VMEM_LIMIT_BYTES = 56 * 1024 * 1024

TOK_LRU = 0
TOK_GATES = 2 * GROUP
N_TOK = 2 * GROUP + LANES
PAIR = 2 * HEAD
HMP_GROUPS = 4
HM_GDN_QKV, HM_GDN_Z = 0, 3
HM_GROUPS = 4

TM_IN_PROJ = 512
TM_XATTN = 1024
TM_FFN = 1024
TF_FFN = 1408
TM_NORM = 1024
TT_SCAN = 512
MOE_TB = 512
MOE_TILE = 512
MOE_F_SPLIT = 2
ROW_ALIGN = 16
SPARE_TILES = 2
MOE_SEG = 128
MOE_WIN = 256


def _params(*semantics):
    return pltpu.CompilerParams(dimension_semantics=semantics, vmem_limit_bytes=VMEM_LIMIT_BYTES)


def _full(shape):
    n = len(shape)
    return pl.BlockSpec(shape, lambda *_: (0,) * n)


def _bmm(a, b, nt=False, exact=False):
    dn = (((2,), (2 if nt else 1,)), ((0,), (0,)))
    if exact:
        return lax.dot_general(a, b, dn, precision=HIGHEST, preferred_element_type=F32)
    return lax.dot_general(a.astype(BF16), b.astype(BF16), dn, preferred_element_type=F32)


def _mm(a, b, precision=None):
    return jnp.dot(a, b, precision=precision, preferred_element_type=F32)


def _mm16(a, b):
    return jnp.dot(a.astype(BF16), b.astype(BF16), preferred_element_type=F32)


def _softplus(x):
    return jnp.maximum(x, 0.0) + jnp.log1p(jnp.exp(-jnp.abs(x)))


def _silu(x):
    return x * jax.nn.sigmoid(x)


def _rms(x, gain):
    return x * lax.rsqrt(jnp.mean(x * x, axis=-1, keepdims=True) + EPS) * gain


def _tri_masks(n):
    t = lax.broadcasted_iota(jnp.int32, (n, n), 0)
    s = lax.broadcasted_iota(jnp.int32, (n, n), 1)
    return t, s


def _cumsum_rows(x):
    g, c, _ = x.shape
    t, s = _tri_masks(c)
    tri = jnp.broadcast_to((s <= t).astype(BF16), (g, c, c))
    hi = x.astype(BF16)
    rest = x - hi.astype(F32)
    mid = rest.astype(BF16)
    lo = (rest - mid.astype(F32)).astype(BF16)
    return _bmm(tri, hi) + _bmm(tri, mid) + _bmm(tri, lo)


def _solve_unit_lower(low, rhs):
    c = low.shape[1]
    r = rhs.shape[2]
    assert c == 4 * SUB and SUB == 16
    t, s = _tri_masks(c)
    same_block = (t // SUB) == (s // SUB)
    d1 = jnp.where(same_block, low, 0.0)
    z = jnp.concatenate([rhs, jnp.where(same_block, 0.0, low)], axis=-1)
    d2 = _bmm(d1, d1)
    d4 = _bmm(d2, d2)
    d8 = _bmm(d4, d4)
    z = z + _bmm(d8, z)
    z = z + _bmm(d4, z)
    z = z + _bmm(d2, z)
    z = z - _bmm(d1, z)
    y = z[:, :, :r]
    n1 = z[:, :, r:]
    n2 = _bmm(n1, n1)
    y = y + _bmm(n2, y)
    return y - _bmm(n1, y)


def _store_heads(ref, val, lead=(), width=HEAD):
    for h in range(val.shape[1] // width):
        ref[lead + (h,)] = val[:, h * width:(h + 1) * width]


def _store_tokens(ref, o):
    nb = ref.shape[0]
    nh = o.shape[0] // nb
    for b in range(nb):
        ref[b] = jnp.concatenate([o[b * nh + h] for h in range(nh)], axis=-1)


def _in_proj_kernel(has_vmix, *refs):
    refs = list(refs)
    x_ref, g_ref, w_ref, mu_ref, w0_ref, w2_ref, a0_ref, a2_ref, g2_ref = refs[:9]
    del refs[:9]
    if has_vmix:
        vf_ref, v0_ref, v1_ref, v2_ref = refs[:4]
        del refs[:4]
    tok_ref, hmp_ref, hm_ref, dhm_ref = refs[:4]
    del refs[:4]
    vtok_ref = None if has_vmix else refs.pop(0)
    xbuf_ref = refs.pop(0)
    tm = x_ref.shape[1]

    hn = _rms(x_ref[0], g_ref[...]).astype(BF16)
    n_rwkv = 4 * GROUP
    c = _mm(hn, w_ref[:, :n_rwkv])
    acc = _mm(hn, w_ref[:, n_rwkv:])
    col = 0
    n_tok = tok_ref.shape[2]
    tok_ref[0] = acc[:, col:col + n_tok]
    col += n_tok
    for g in range(hmp_ref.shape[0]):
        _store_heads(hmp_ref, acc[:, col:col + GROUP], (g, 0), PAIR)
        col += GROUP
    for g in range(hm_ref.shape[0]):
        _store_heads(hm_ref, acc[:, col:col + GROUP], (g, 0))
        col += GROUP

    @pl.when(pl.program_id(1) == 0)
    def _():
        xbuf_ref[0:HALO, :] = jnp.zeros((HALO, xbuf_ref.shape[1]), F32)

    xbuf_ref[HALO:HALO + tm, :] = c
    prev = xbuf_ref[HALO - 1:HALO - 1 + tm, :]
    xbuf_ref[0:HALO, :] = xbuf_ref[tm:tm + HALO, :]
    c = c + mu_ref[...] * (prev - c)

    r = c[:, 0:GROUP]
    k = c[:, GROUP:2 * GROUP]
    v = c[:, 2 * GROUP:3 * GROUP]
    wa_lo = c[:, 3 * GROUP:3 * GROUP + LANES]
    g_lo = c[:, 3 * GROUP + LANES:4 * GROUP]
    log_w = -math.exp(-0.5) * jax.nn.sigmoid(w0_ref[...] + _mm16(jnp.tanh(wa_lo), w2_ref[...]))
    a = jax.nn.sigmoid(a0_ref[...] + _mm16(wa_lo, a2_ref[...]))
    gate = _mm16(jax.nn.sigmoid(g_lo), g2_ref[...])
    if has_vmix:
        mix = jax.nn.sigmoid(v0_ref[...] + _mm16(_mm16(v, v1_ref[...]), v2_ref[...]))
        v = v + (vf_ref[0] - v) * mix
    else:
        vtok_ref[0] = v
    for j, val in enumerate((r, log_w, k, v, a, gate)):
        _store_heads(dhm_ref, val, (j, 0))


def _in_proj(x, gain, w, mu, w0, w2, a0, a2, g2, v_first, v_mix):
    b, s, d = x.shape
    n = w.shape[1]
    tm = TM_IN_PROJ
    lora = w2.shape[0]
    w2p = jnp.zeros((LANES, GROUP), F32).at[:lora].set(w2)
    a2p = jnp.zeros((LANES, GROUP), F32).at[lora:].set(a2)
    tile = pl.BlockSpec((1, tm, GROUP), lambda i, j: (i, j, 0))
    vec = _full((1, GROUP))
    args = [x, gain.reshape(1, d), w, mu.reshape(1, 4 * GROUP), w0.reshape(1, GROUP), w2p, a0.reshape(1, GROUP),
            a2p, g2]
    specs = [pl.BlockSpec((1, tm, d), lambda i, j: (i, j, 0)), _full((1, d)), _full((d, n)), _full((1, 4 * GROUP)),
             vec, _full((LANES, GROUP)), vec, _full((LANES, GROUP)), _full(g2.shape)]
    out_specs = [pl.BlockSpec((1, tm, N_TOK), lambda i, j: (i, j, 0)),
                 pl.BlockSpec((HMP_GROUPS, 1, HEADS // 2, tm, PAIR), lambda i, j: (0, i, 0, j, 0)),
                 pl.BlockSpec((HM_GROUPS, 1, HEADS, tm, HEAD), lambda i, j: (0, i, 0, j, 0)),
                 pl.BlockSpec((6, 1, HEADS, tm, HEAD), lambda i, j: (0, i, 0, j, 0))]
    out_shape = [jax.ShapeDtypeStruct((b, s, N_TOK), F32),
                 jax.ShapeDtypeStruct((HMP_GROUPS, b, HEADS // 2, s, PAIR), F32),
                 jax.ShapeDtypeStruct((HM_GROUPS, b, HEADS, s, HEAD), F32),
                 jax.ShapeDtypeStruct((6, b, HEADS, s, HEAD), F32)]
    if v_mix is not None:
        v0, v1, v2 = v_mix
        rank = v1.shape[1]
        v1p = jnp.zeros((GROUP, LANES), F32).at[:, :rank].set(v1)
        v2p = jnp.zeros((LANES, GROUP), F32).at[:rank].set(v2)
        args += [v_first, v0.reshape(1, GROUP), v1p, v2p]
        specs += [tile, vec, _full((GROUP, LANES)), _full((LANES, GROUP))]
    else:
        out_specs.append(tile)
        out_shape.append(jax.ShapeDtypeStruct((b, s, GROUP), F32))
    outs = pl.pallas_call(
        functools.partial(_in_proj_kernel, v_mix is not None),
        grid=(b, s // tm),
        in_specs=specs,
        out_specs=out_specs,
        out_shape=out_shape,
        scratch_shapes=[pltpu.VMEM((HALO + tm, 4 * GROUP), F32)],
        compiler_params=_params("parallel", "arbitrary"),
        name="in_proj",
    )(*args)
    return tuple(outs[:4]) + ((v_first,) if v_mix is not None else (outs[4],))


def _norm_mm_kernel(x_ref, g_ref, w_ref, o_ref):
    o_ref[...] = _mm(_rms(x_ref[...], g_ref[...]).astype(BF16), w_ref[...]).astype(o_ref.dtype)


def _norm_matmul(x, gain, w, tm, out_dtype):
    t, k = x.shape
    n = w.shape[1]
    return pl.pallas_call(
        _norm_mm_kernel,
        grid=(t // tm,),
        in_specs=[pl.BlockSpec((tm, k), lambda i: (i, 0)), _full((1, k)), _full((k, n))],
        out_specs=pl.BlockSpec((tm, n), lambda i: (i, 0)),
        out_shape=jax.ShapeDtypeStruct((t, n), out_dtype),
        compiler_params=_params("parallel"),
        name="norm_matmul",
    )(x, gain.reshape(1, k), w)


def _mix_attn_kernel(ya_ref, yb_ref, yc_ref, yd_ref, wout_ref, x_ref, g_ref, wq_ref, k_ref, v_ref, wo_ref, o_ref):
    x = x_ref[0]
    for m, y_ref in enumerate((ya_ref, yb_ref, yc_ref, yd_ref)):
        x = x + _mm(y_ref[0].astype(BF16), wout_ref[m])
    hn = _rms(x, g_ref[...]).astype(BF16)
    q = _mm(hn, wq_ref[...])
    d = q.shape[1] // XATTN_HEADS
    outs = []
    for h in range(XATTN_HEADS):
        qh = (q[:, h * d:(h + 1) * d] * d ** -0.5).astype(BF16)
        kh = k_ref[0, :, h * d:(h + 1) * d]
        vh = v_ref[0, :, h * d:(h + 1) * d]
        s = lax.dot_general(qh, kh, (((1,), (1,)), ((), ())), preferred_element_type=F32)
        s = s - jnp.max(s, axis=-1, keepdims=True)
        p = jnp.exp(s)
        p = p / jnp.sum(p, axis=-1, keepdims=True)
        outs.append(_mm(p.astype(BF16), vh))
    o = jnp.concatenate(outs, axis=-1).astype(BF16)
    o_ref[0] = x + _mm(o, wo_ref[...])


def _mix_attention(ys, w_out, x, gain, wq, kv, wo):
    b, s, d = x.shape
    m = kv.shape[1]
    tm = TM_XATTN
    y_spec = pl.BlockSpec((1, tm, GROUP), lambda i, j: (i, j, 0))
    x_spec = pl.BlockSpec((1, tm, d), lambda i, j: (i, j, 0))
    return pl.pallas_call(
        _mix_attn_kernel,
        grid=(b, s // tm),
        in_specs=[y_spec] * 4 + [_full(w_out.shape), x_spec, _full((1, d)), _full((d, d)),
                                 pl.BlockSpec((1, m, d), lambda i, j: (i, 0, 0)),
                                 pl.BlockSpec((1, m, d), lambda i, j: (i, 0, 1)),
                                 _full((d, d))],
        out_specs=x_spec,
        out_shape=jax.ShapeDtypeStruct((b, s, d), F32),
        compiler_params=_params("parallel", "parallel"),
        name="mix_attention",
    )(*ys, w_out, x, gain.reshape(1, d), wq, kv, kv, wo)


def _ffn_kernel(x_ref, g_ref, wg_ref, wu_ref, wd_ref, c_ref, o_ref, hn_ref, acc_ref):
    f = pl.program_id(1)

    @pl.when(f == 0)
    def _():
        hn_ref[...] = _rms(x_ref[...], g_ref[...]).astype(BF16)
        acc_ref[...] = x_ref[...]

    hn = hn_ref[...]
    gate = _mm(hn, wg_ref[0])
    up = _mm(hn, wu_ref[0])
    mid = _silu(gate) * up * c_ref[0]
    acc_ref[...] += _mm(mid.astype(BF16), wd_ref[0])

    @pl.when(f == pl.num_programs(1) - 1)
    def _():
        o_ref[...] = acc_ref[...]


def _ffn(x, gain, wg, wu, wd, combine, tf):
    t, d = x.shape
    e, _, ff = wg.shape
    nf = ff // tf
    tm = TM_FFN
    return pl.pallas_call(
        _ffn_kernel,
        grid=(t // tm, e * nf),
        in_specs=[pl.BlockSpec((tm, d), lambda i, f: (i, 0)),
                  _full((1, d)),
                  pl.BlockSpec((1, d, tf), lambda i, f: (f // nf, 0, f % nf)),
                  pl.BlockSpec((1, d, tf), lambda i, f: (f // nf, 0, f % nf)),
                  pl.BlockSpec((1, tf, d), lambda i, f: (f // nf, f % nf, 0)),
                  pl.BlockSpec((1, tm, 1), lambda i, f: (f // nf, i, 0))],
        out_specs=pl.BlockSpec((tm, d), lambda i, f: (i, 0)),
        out_shape=jax.ShapeDtypeStruct((t, d), F32),
        scratch_shapes=[pltpu.VMEM((tm, d), BF16), pltpu.VMEM((tm, d), F32)],
        compiler_params=_params("parallel", "arbitrary"),
        name="ffn",
    )(x, gain.reshape(1, d), wg, wu, wd, combine)


def _router_kernel(x_ref, g_ref, w_ref, hn_ref, pos_ref, post_ref, cnt_ref):
    hn = _rms(x_ref[...], g_ref[...])
    d = hn.shape[1]
    hn_ref[:, :d] = hn.astype(BF16)
    logits = _mm(hn, w_ref[...], precision=HIGHEST)
    lane = lax.broadcasted_iota(jnp.int32, logits.shape, 1)
    neg = jnp.float32(-jnp.inf)
    logits = jnp.where(lane < N_EXPERTS, logits, neg)
    m1 = jnp.max(logits, axis=-1, keepdims=True)
    i1 = jnp.min(jnp.where(logits == m1, lane, LANES), axis=-1, keepdims=True)
    rest = jnp.where(lane == i1, neg, logits)
    m2 = jnp.max(rest, axis=-1, keepdims=True)
    i2 = jnp.min(jnp.where(rest == m2, lane, LANES), axis=-1, keepdims=True)
    e2 = jnp.exp(m2 - m1)
    comb = jnp.where(lane == i1, 1.0 / (1.0 + e2), 0.0) + jnp.where(lane == i2, e2 / (1.0 + e2), 0.0)
    hi = comb.astype(BF16).astype(F32)
    rest = comb - hi
    mid = rest.astype(BF16).astype(F32)
    lo = rest - mid
    hn_ref[:, d:] = (hi + pltpu.roll(mid, N_EXPERTS, 1) + pltpu.roll(lo, 2 * N_EXPERTS, 1)).astype(BF16)

    sel = (comb > 0.0).astype(F32)
    tb = sel.shape[0]
    row = lax.broadcasted_iota(jnp.int32, sel.shape, 0)
    run = sel
    shift = 1
    while shift < tb:
        run = run + jnp.where(row >= shift, pltpu.roll(run, shift, 0), 0.0)
        shift *= 2
    pos = jnp.where(sel > 0.0, run - 1.0, -1.0)
    pos_ref[...] = pos
    post_ref[0] = jnp.transpose(pos)[:N_EXPERTS, :]
    cnt_ref[0] = run[tb - 1:tb, :].astype(jnp.int32)


def _router(x, gain, w_router):
    t, d = x.shape
    tb = MOE_TB
    nb = t // tb
    w = jnp.zeros((d, LANES), F32).at[:, :N_EXPERTS].set(w_router)
    return pl.pallas_call(
        _router_kernel,
        grid=(nb,),
        in_specs=[pl.BlockSpec((tb, d), lambda i: (i, 0)), _full((1, d)), _full((d, LANES))],
        out_specs=[pl.BlockSpec((tb, d + LANES), lambda i: (i, 0)), pl.BlockSpec((tb, LANES), lambda i: (i, 0)),
                   pl.BlockSpec((1, N_EXPERTS, tb), lambda i: (i, 0, 0)),
                   pl.BlockSpec((1, 1, LANES), lambda i: (i, 0, 0))],
        out_shape=[jax.ShapeDtypeStruct((t, d + LANES), BF16), jax.ShapeDtypeStruct((t, LANES), F32),
                   jax.ShapeDtypeStruct((nb, N_EXPERTS, tb), F32),
                   jax.ShapeDtypeStruct((nb, 1, LANES), jnp.int32)],
        compiler_params=_params("parallel"),
        name="router",
    )(x, gain.reshape(1, d), w)


def _route_tables(cnt, n_tiles_max):
    nb, ne = cnt.shape
    cpad = (cnt + (ROW_ALIGN - 1)) // ROW_ALIGN * ROW_ALIGN
    off = jnp.cumsum(cpad, axis=0) - cpad
    total = jnp.sum(cpad, axis=0)
    ntile = (total + MOE_TILE - 1) // MOE_TILE
    first = jnp.cumsum(ntile) - ntile
    n_tiles = jnp.sum(ntile)
    region = (first + SPARE_TILES * jnp.arange(ne, dtype=jnp.int32)) * MOE_TILE
    dst = region[None, :] + off
    tail = region + total
    sub = jnp.arange(MOE_TB // MOE_WIN, dtype=jnp.int32)[None, None, :] * MOE_WIN
    want = off[:, :, None] + sub
    win_rel = jnp.clip(want, 0, jnp.maximum(ntile * MOE_TILE - MOE_WIN, 0)[None, :, None])
    win = jnp.minimum(first[None, :, None] * MOE_TILE + win_rel, n_tiles_max * MOE_TILE - MOE_WIN)
    shift = want - win_rel
    t = jnp.minimum(jnp.arange(n_tiles_max, dtype=jnp.int32), n_tiles - 1)
    tile_e = jnp.minimum(jnp.sum(t[:, None] >= jnp.cumsum(ntile)[None, :], axis=1), ne - 1).astype(jnp.int32)
    tile_blk = t + SPARE_TILES * tile_e
    valid = (jnp.arange(n_tiles_max, dtype=jnp.int32) < n_tiles).astype(jnp.int32)
    n_blocks = n_tiles_max + SPARE_TILES * ne
    blocks = jnp.arange(n_blocks, dtype=jnp.int32)
    unused = 1 - jnp.max((tile_blk[:, None] == blocks[None, :]) * valid[:, None], axis=0)
    rank = jnp.cumsum(unused) - 1
    fill = jnp.sum(jnp.where((rank[None, :] == blocks[:, None]) & (unused[None, :] > 0), blocks[None, :], 0), axis=1)
    n_fill = jnp.sum(unused).astype(jnp.int32).reshape(1)
    flat = lambda a: a.reshape(-1).astype(jnp.int32)
    return ((flat(cpad), flat(dst), flat(tail), flat(fill), n_fill), (flat(cnt), flat(win), flat(shift)),
            (tile_e, tile_blk.astype(jnp.int32), valid))


def _dispatch_kernel(cpad_ref, dst_ref, tail_ref, fill_ref, nfill_ref, hn_ref, post_ref, xs_ref, seg_ref, sem):
    blk = pl.program_id(0)
    last = pl.num_programs(0) - 1
    tb = hn_ref.shape[0]
    assert tb == MOE_TILE
    n_piece = tb // MOE_SEG
    slot = blk % 2

    def needed(b, e, j):
        return cpad_ref[b * N_EXPERTS + e] > j * MOE_SEG

    def piece_copy(b, sl, e, j):
        row = pl.multiple_of(dst_ref[b * N_EXPERTS + e] + j * MOE_SEG, ROW_ALIGN)
        return pltpu.make_async_copy(seg_ref.at[sl, e, pl.ds(j * MOE_SEG, MOE_SEG)],
                                     xs_ref.at[pl.ds(row, MOE_SEG)], sem.at[e, j])

    rank = lax.broadcasted_iota(jnp.int32, (MOE_SEG, tb), 0).astype(F32)
    for e in range(N_EXPERTS):
        for j in range(n_piece):
            @pl.when(needed(blk, e, j))
            def _(e=e, j=j):
                onehot = (rank + float(j * MOE_SEG) == post_ref[0, e:e + 1, :]).astype(BF16)
                seg_ref[slot, e, j * MOE_SEG:(j + 1) * MOE_SEG, :] = _mm(onehot, hn_ref[...]).astype(BF16)

        @pl.when(blk > 0)
        def _(e=e):
            for j in range(n_piece):
                @pl.when(needed(blk - 1, e, j))
                def _(j=j):
                    piece_copy(blk - 1, 1 - slot, e, j).wait()

        for j in range(n_piece):
            @pl.when(needed(blk, e, j))
            def _(e=e, j=j):
                piece_copy(blk, slot, e, j).start()

    @pl.when(blk == last)
    def _():
        for e in range(N_EXPERTS):
            for j in range(n_piece):
                @pl.when(needed(blk, e, j))
                def _(e=e, j=j):
                    piece_copy(blk, slot, e, j).wait()

        zeros_ref = seg_ref.at[0, 0]
        zeros_ref[...] = jnp.zeros(zeros_ref.shape, BF16)

        def tail_copy(e):
            row = pl.multiple_of(tail_ref[e], ROW_ALIGN)
            return pltpu.make_async_copy(zeros_ref, xs_ref.at[pl.ds(row, tb)], sem.at[e, 0])

        for e in range(N_EXPERTS):
            tail_copy(e).start()
        for e in range(N_EXPERTS):
            tail_copy(e).wait()

        def fill_copy(k):
            row = pl.multiple_of(fill_ref[k] * MOE_TILE, MOE_TILE)
            return pltpu.make_async_copy(zeros_ref, xs_ref.at[pl.ds(row, tb)], sem.at[0, 0])

        def start_fill(k, carry):
            fill_copy(k).start()
            return carry

        def wait_fill(k, carry):
            fill_copy(k).wait()
            return carry

        lax.fori_loop(0, nfill_ref[0], start_fill, 0)
        lax.fori_loop(0, nfill_ref[0], wait_fill, 0)


def _dispatch(tables, hn, pos_t, n_rows):
    t, d = hn.shape
    tb = MOE_TB
    return pl.pallas_call(
        _dispatch_kernel,
        grid_spec=pltpu.PrefetchScalarGridSpec(
            num_scalar_prefetch=len(tables),
            grid=(t // tb,),
            in_specs=[pl.BlockSpec((tb, d), lambda i, *_: (i, 0)),
                      pl.BlockSpec((1, N_EXPERTS, tb), lambda i, *_: (i, 0, 0))],
            out_specs=pl.BlockSpec(memory_space=pl.ANY),
            scratch_shapes=[pltpu.VMEM((2, N_EXPERTS, tb, d), BF16),
                            pltpu.SemaphoreType.DMA((N_EXPERTS, tb // MOE_SEG))]),
        out_shape=jax.ShapeDtypeStruct((n_rows, d), BF16),
        compiler_params=_params("arbitrary"),
        name="moe_dispatch",
    )(*tables, hn, pos_t)


def _gate_up_kernel(te_ref, tb_ref, ok_ref, x_ref, wg_ref, wu_ref, o_ref):
    t = pl.program_id(1)

    @pl.when(ok_ref[t] > 0)
    def _():
        x = x_ref[...]
        o_ref[...] = (_silu(_mm(x, wg_ref[0])) * _mm(x, wu_ref[0])).astype(o_ref.dtype)

    @pl.when(ok_ref[t] == 0)
    def _():
        o_ref[...] = jnp.zeros_like(o_ref)


def _gate_up(tile_e, tile_blk, valid, xs, wg, wu):
    d = wg.shape[1]
    ff = wg.shape[2]
    nt = tile_e.shape[0]
    tf = ff // MOE_F_SPLIT
    w_spec = pl.BlockSpec((1, d, tf), lambda f, t, te, tb, ok: (te[t], 0, f))
    return pl.pallas_call(
        _gate_up_kernel,
        grid_spec=pltpu.PrefetchScalarGridSpec(
            num_scalar_prefetch=3,
            grid=(MOE_F_SPLIT, nt),
            in_specs=[pl.BlockSpec((MOE_TILE, d), lambda f, t, te, tb, ok: (tb[t], 0)), w_spec, w_spec],
            out_specs=pl.BlockSpec((MOE_TILE, tf), lambda f, t, te, tb, ok: (t, f))),
        out_shape=jax.ShapeDtypeStruct((nt * MOE_TILE, ff), BF16),
        compiler_params=_params("arbitrary", "arbitrary"),
        name="moe_gate_up",
    )(tile_e, tile_blk, valid, xs, wg, wu)


def _down_kernel(te_ref, tb_ref, ok_ref, m_ref, gate_ref, wd_ref, o_ref, wbf_ref):
    t = pl.program_id(0)
    expert = te_ref[t]

    @pl.when((t == 0) | (expert != te_ref[jnp.maximum(t - 1, 0)]))
    def _():
        wbf_ref[...] = wd_ref[0].astype(BF16)

    @pl.when(ok_ref[t] > 0)
    def _():
        lane = lax.broadcasted_iota(jnp.int32, gate_ref.shape, 1)
        mine = ((lane & (N_EXPERTS - 1)) == expert) & (lane < 3 * N_EXPERTS)
        gate = jnp.sum(jnp.where(mine, gate_ref[...].astype(F32), 0.0), axis=-1, keepdims=True)
        o_ref[...] = (gate * _mm(m_ref[...], wbf_ref[...])).astype(o_ref.dtype)

    @pl.when(ok_ref[t] == 0)
    def _():
        o_ref[...] = jnp.zeros_like(o_ref)


def _down(tile_e, tile_blk, valid, mid, xs, wd):
    ff, d = wd.shape[1:]
    nt = tile_e.shape[0]
    assert N_EXPERTS & (N_EXPERTS - 1) == 0
    return pl.pallas_call(
        _down_kernel,
        grid_spec=pltpu.PrefetchScalarGridSpec(
            num_scalar_prefetch=3,
            grid=(nt,),
            in_specs=[pl.BlockSpec((MOE_TILE, ff), lambda t, te, tb, ok: (t, 0)),
                      pl.BlockSpec((MOE_TILE, LANES), lambda t, te, tb, ok: (tb[t], d // LANES)),
                      pl.BlockSpec((1, ff, d), lambda t, te, tb, ok: (te[t], 0, 0))],
            out_specs=pl.BlockSpec((MOE_TILE, d), lambda t, te, tb, ok: (t, 0)),
            scratch_shapes=[pltpu.VMEM((ff, d), BF16)]),
        out_shape=jax.ShapeDtypeStruct((nt * MOE_TILE, d), BF16),
        compiler_params=_params("arbitrary"),
        name="moe_down",
    )(tile_e, tile_blk, valid, mid, xs, wd)


def _combine_kernel(final, cnt_ref, win_ref, shift_ref, x_ref, pos_ref, ys_ref, fg_ref, o_ref, win_buf, sem):
    blk = pl.program_id(0)
    tb = x_ref.shape[0]
    n_sub = tb // MOE_WIN
    extra = [(e, j) for e in range(N_EXPERTS) for j in range(1, n_sub)]
    slot = blk % 2

    def needed(b, e, j):
        return cnt_ref[b * N_EXPERTS + e] > j * MOE_WIN

    def win_copy(b, sl, e, j):
        row = pl.multiple_of(win_ref[(b * N_EXPERTS + e) * n_sub + j], ROW_ALIGN)
        return pltpu.make_async_copy(ys_ref.at[pl.ds(row, MOE_WIN)], win_buf.at[sl, j, e], sem.at[sl, j, e])

    def start_block(b, sl):
        for e in range(N_EXPERTS):
            win_copy(b, sl, e, 0).start()
        for e, j in extra:
            @pl.when(needed(b, e, j))
            def _(e=e, j=j):
                win_copy(b, sl, e, j).start()

    @pl.when(blk == 0)
    def _():
        start_block(blk, slot)

    @pl.when(blk + 1 < pl.num_programs(0))
    def _():
        start_block(blk + 1, 1 - slot)

    def onehot(e, j):
        pos = pos_ref[...][:, e:e + 1]
        in_sub = (pos >= float(j * MOE_WIN)) & (pos < float((j + 1) * MOE_WIN))
        shift = shift_ref[(blk * N_EXPERTS + e) * n_sub + j].astype(F32)
        target = jnp.where(in_sub, pos - float(j * MOE_WIN) + shift, -1.0)
        col = lax.broadcasted_iota(jnp.int32, (tb, MOE_WIN), 1).astype(F32)
        return (col == target).astype(BF16)

    first = jnp.concatenate([onehot(e, 0) for e in range(N_EXPERTS)], axis=1)
    for e in range(N_EXPERTS):
        win_copy(blk, slot, e, 0).wait()
    o_ref[...] = x_ref[...] + _mm(first, win_buf[slot, 0].reshape(N_EXPERTS * MOE_WIN, win_buf.shape[4]))
    for e, j in extra:
        @pl.when(needed(blk, e, j))
        def _(e=e, j=j):
            win_copy(blk, slot, e, j).wait()
            o_ref[...] += _mm(onehot(e, j), win_buf[slot, j, e])

    if final:
        o_ref[...] = _rms(o_ref[...], fg_ref[...])


def _combine(tables, x, pos, ys, final_gain):
    t, d = x.shape
    final = final_gain is not None
    gain = final_gain if final else jnp.ones((d,), F32)
    tb = MOE_TB
    n_sub = tb // MOE_WIN
    return pl.pallas_call(
        functools.partial(_combine_kernel, final),
        grid_spec=pltpu.PrefetchScalarGridSpec(
            num_scalar_prefetch=len(tables),
            grid=(t // tb,),
            in_specs=[pl.BlockSpec((tb, d), lambda i, *_: (i, 0)), pl.BlockSpec((tb, LANES), lambda i, *_: (i, 0)),
                      pl.BlockSpec(memory_space=pl.ANY), pl.BlockSpec((1, d), lambda i, *_: (0, 0))],
            out_specs=pl.BlockSpec((tb, d), lambda i, *_: (i, 0)),
            scratch_shapes=[pltpu.VMEM((2, n_sub, N_EXPERTS, MOE_WIN, d), BF16),
                            pltpu.SemaphoreType.DMA((2, n_sub, N_EXPERTS))]),
        out_shape=jax.ShapeDtypeStruct((t, d), F32),
        compiler_params=_params("arbitrary"),
        name="moe_combine",
    )(*tables, x, pos, ys, gain.reshape(1, d))


def _moe(x, gain, w_router, wg, wu, wd, final_gain):
    t, d = x.shape
    nb = t // MOE_TB
    n_tiles_max = (2 * t + nb * N_EXPERTS * (ROW_ALIGN - 1)) // MOE_TILE + N_EXPERTS
    hn, pos, pos_t, cnt = _router(x, gain, w_router)
    write_tables, read_tables, (tile_e, tile_blk, valid) = _route_tables(cnt[:, 0, :N_EXPERTS], n_tiles_max)
    n_rows = (n_tiles_max + SPARE_TILES * N_EXPERTS) * MOE_TILE
    xs = _dispatch(write_tables, hn, pos_t, n_rows)
    mid = _gate_up(tile_e, tile_blk, valid, xs, wg, wu)
    ys = _down(tile_e, tile_blk, valid, mid, xs, wd)
    return _combine(read_tables, x, pos, ys, final_gain)


def _final_norm_kernel(x_ref, g_ref, o_ref):
    o_ref[...] = _rms(x_ref[...], g_ref[...])


def _final_norm(x, gain):
    t, d = x.shape
    tm = TM_NORM
    return pl.pallas_call(
        _final_norm_kernel,
        grid=(t // tm,),
        in_specs=[pl.BlockSpec((tm, d), lambda i: (i, 0)), _full((1, d))],
        out_specs=pl.BlockSpec((tm, d), lambda i: (i, 0)),
        out_shape=jax.ShapeDtypeStruct((t, d), F32),
        compiler_params=_params("parallel"),
        name="final_norm",
    )(x, gain.reshape(1, d))


def _hm_groups_spec(n, first, width=HEAD):
    def spec(b, h):
        return pl.BlockSpec((n, b, h, CHUNK, width), lambda c: (first // n, 0, 0, c, 0))
    assert first % n == 0
    return spec


def _tok_chunk_spec(b):
    return pl.BlockSpec((b, CHUNK, GROUP), lambda c: (0, c, 0))


def _expand_pair(x):
    first = lax.broadcasted_iota(jnp.int32, x.shape, 2) < HEAD
    return jnp.concatenate([jnp.where(first, x, 0.0), jnp.where(first, 0.0, x)], axis=1)


def _same_head(n):
    t, s = _tri_masks(n)
    return (t >= n // 2) == (s >= n // 2)


def _hgrn_kernel(qfig_ref, loglb_ref, log1mlb_ref, onemlb_ref, ng_ref, o_ref, st_ref, rows_ref):
    @pl.when(pl.program_id(0) == 0)
    def _():
        st_ref[...] = jnp.zeros_like(st_ref)

    g2 = st_ref.shape[0]
    shape = (g2, CHUNK, PAIR)
    q, fl, v, g = (qfig_ref[j].reshape(shape) for j in range(4))
    log_sig = jnp.minimum(fl, 0.0) - jnp.log(1.0 + jnp.exp(-jnp.abs(fl)))
    la = loglb_ref[...]
    lc = log1mlb_ref[...] + log_sig
    log_f = jnp.maximum(la, lc) + jnp.log(1.0 + jnp.exp(-jnp.abs(la - lc)))
    k = onemlb_ref[...] * jax.nn.sigmoid(-fl)
    b = _cumsum_rows(log_f) * LOG2_E
    head_sum = jnp.broadcast_to(_same_head(PAIR).astype(BF16), (g2, PAIR, PAIR))

    rows_ref[0] = b
    rows_ref[1] = k
    rows_ref[2] = v
    sub = HGRN_SUB
    a_rows = [jnp.zeros((g2, sub, 2 * CHUNK), F32)]
    for i in range(1, CHUNK // sub):
        bref = rows_ref[0, :, i * sub - 1:i * sub, :]
        qd = q[:, i * sub:(i + 1) * sub] * jnp.exp2(b[:, i * sub:(i + 1) * sub] - bref)
        kd = k[:, :i * sub] * jnp.exp2(bref - b[:, :i * sub])
        later = jnp.zeros((g2, CHUNK - i * sub, PAIR), F32)
        a_rows.append(_bmm(qd, _expand_pair(jnp.concatenate([kd, later], axis=1)), nt=True))
    o = _bmm(jnp.concatenate(a_rows, axis=1), _expand_pair(v))

    t_loc = lax.broadcasted_iota(jnp.int32, (g2, sub, PAIR), 1)
    diag = []
    for i in range(CHUNK // sub):
        sl = slice(i * sub, (i + 1) * sub)
        bb, qq = b[:, sl], q[:, sl]
        terms = []
        for s in range(sub):
            row = i * sub + s
            e = jnp.exp2(jnp.where(t_loc >= s, bb - rows_ref[0, :, row:row + 1, :], -BIG_EXPONENT))
            terms.append(e * qq * rows_ref[1, :, row:row + 1, :])
        w = _bmm(jnp.concatenate(terms, axis=1), head_sum)
        acc = jnp.zeros((g2, sub, PAIR), F32)
        for s in range(sub):
            acc = acc + w[:, s * sub:(s + 1) * sub] * rows_ref[2, :, i * sub + s:i * sub + s + 1, :]
        diag.append(acc)
    o = o + jnp.concatenate(diag, axis=1)

    st = st_ref[...]
    o = o + _bmm(q * jnp.exp2(b), st, nt=True)
    b_last = b[:, CHUNK - 1:CHUNK]
    update = _bmm(jnp.swapaxes(v, 1, 2), k * jnp.exp2(b_last - b))
    st_ref[...] = st * jnp.exp2(b_last) + jnp.where(_same_head(PAIR), update, 0.0)

    mean_sq = _bmm(o * o, head_sum) * (1.0 / HEAD)
    _store_tokens(o_ref, o * lax.rsqrt(mean_sq + EPS) * ng_ref[...] * _silu(g))


def _gdn_kernel(qkv_ref, z_ref, gates_ref, cw_ref, alog_ref, dtb_ref, ng_ref, o_ref, xbuf_ref, st_ref):
    g3 = xbuf_ref.shape[0]
    g = g3 // 3

    @pl.when(pl.program_id(0) == 0)
    def _():
        st_ref[...] = jnp.zeros_like(st_ref)
        xbuf_ref[:, 0:HALO, :] = jnp.zeros((g3, HALO, HEAD), F32)

    xbuf_ref[:, HALO:HALO + CHUNK, :] = qkv_ref[...].reshape(g3, CHUNK, HEAD)
    conv = jnp.zeros((g3, CHUNK, HEAD), F32)
    for j in range(CONV_K):
        conv = conv + cw_ref[j] * xbuf_ref[:, HALO - (CONV_K - 1) + j:HALO - (CONV_K - 1) + j + CHUNK, :]
    xbuf_ref[:, 0:HALO, :] = xbuf_ref[:, CHUNK:CHUNK + HALO, :]
    qkv = _silu(conv)
    q, k, v = qkv[0:g], qkv[g:2 * g], qkv[2 * g:3 * g]
    q = q * lax.rsqrt(jnp.sum(q * q, axis=-1, keepdims=True) + EPS) * HEAD ** -0.5
    k = k * lax.rsqrt(jnp.sum(k * k, axis=-1, keepdims=True) + EPS)

    t_idx, s_idx = _tri_masks(CHUNK)
    causal = s_idx <= t_idx
    eye = s_idx == t_idx
    g_row = -jnp.exp(alog_ref[...]) * _softplus(gates_ref[0, 0] + dtb_ref[...])
    g_col = jnp.sum(jnp.where(eye, g_row, 0.0), axis=-1, keepdims=True)
    beta = jnp.sum(jnp.where(eye, jax.nn.sigmoid(gates_ref[1, 0]), 0.0), axis=-1, keepdims=True)
    gc_col = jnp.sum(jnp.where(causal, g_row, 0.0), axis=-1, keepdims=True)
    gc_row = jnp.sum(jnp.where(t_idx <= s_idx, g_col, 0.0), axis=1, keepdims=True)
    rel = jnp.where(causal, jnp.exp(jnp.minimum(gc_col - gc_row, 0.0)), 0.0)

    kb = k * beta
    with_k = _bmm(jnp.concatenate([kb, q], axis=1), k, nt=True)
    lower = jnp.where(s_idx < t_idx, with_k[:, :CHUNK] * rel, 0.0)
    uw = _solve_unit_lower(lower, jnp.concatenate([v * beta, kb * jnp.exp(gc_col)], axis=-1))
    u, w = uw[:, :, :HEAD], uw[:, :, HEAD:]
    scores = with_k[:, CHUNK:] * rel

    st = st_ref[...]
    from_state = _bmm(jnp.concatenate([w, q * jnp.exp(gc_col)], axis=1), st)
    v_new = u - from_state[:, :CHUNK]
    o = from_state[:, CHUNK:] + _bmm(scores, v_new)
    g_last = gc_col[:, CHUNK - 1:CHUNK]
    kd = k * jnp.exp(g_last - gc_col)
    st_ref[...] = st * jnp.exp(g_last) + _bmm(jnp.swapaxes(kd, 1, 2), v_new)

    _store_tokens(o_ref, _rms(o, ng_ref[...]) * _silu(z_ref[0].reshape(g, CHUNK, HEAD)))


def _lru_kernel(x_ref, gate_ref, cw_ref, cb_ref, wa_ref, ba_ref, wx_ref, bx_ref, lam_ref, o_ref,
                xbuf_ref, h_ref):
    tt = x_ref.shape[1]

    @pl.when(pl.program_id(1) == 0)
    def _():
        h_ref[...] = jnp.zeros_like(h_ref)
        xbuf_ref[0:HALO, :] = jnp.zeros((HALO, GROUP), F32)

    xbuf_ref[HALO:HALO + tt, :] = x_ref[0]
    xc = jnp.zeros((tt, GROUP), F32) + cb_ref[...]
    for j in range(CONV_K):
        xc = xc + cw_ref[j:j + 1, :] * xbuf_ref[HALO - (CONV_K - 1) + j:HALO - (CONV_K - 1) + j + tt, :]
    xbuf_ref[0:HALO, :] = xbuf_ref[tt:tt + HALO, :]

    r = jax.nn.sigmoid(_mm16(xc, wa_ref[...]) + ba_ref[...])
    i = jax.nn.sigmoid(_mm16(xc, wx_ref[...]) + bx_ref[...])
    log_a = -LRU_C * r * _softplus(-lam_ref[...])
    a = jnp.exp(log_a)
    u = jnp.sqrt(1.0 - a * a) * (i * xc)

    row = lax.broadcasted_iota(jnp.int32, (tt, GROUP), 0)
    shift = 1
    while shift < tt:
        keep = row >= shift
        a_prev = pltpu.roll(a, shift, 0)
        u_prev = pltpu.roll(u, shift, 0)
        u = jnp.where(keep, a * u_prev + u, u)
        a = jnp.where(keep, a * a_prev, a)
        shift *= 2
    hs = u + a * h_ref[...]
    h_ref[...] = hs[tt - 1:tt, :]

    gate = gate_ref[0]
    gelu = 0.5 * gate * (1.0 + jnp.tanh(math.sqrt(2.0 / math.pi) * (gate + 0.044715 * gate * gate * gate)))
    o_ref[0] = hs * gelu


def _lru_mixer(tok, conv_w, conv_b, wa, ba, wx, bx, lam):
    b, s, _ = tok.shape
    c = GROUP
    tt = TT_SCAN
    x_spec = pl.BlockSpec((1, tt, c), lambda i, t: (i, t, TOK_LRU // GROUP))
    gate_spec = pl.BlockSpec((1, tt, c), lambda i, t: (i, t, TOK_LRU // GROUP + 1))
    vec = _full((1, c))
    return pl.pallas_call(
        _lru_kernel,
        grid=(b, s // tt),
        in_specs=[x_spec, gate_spec, _full((CONV_K, c)), vec, _full((c, c)), vec, _full((c, c)), vec, vec],
        out_specs=pl.BlockSpec((1, tt, c), lambda i, t: (i, t, 0)),
        out_shape=jax.ShapeDtypeStruct((b, s, c), F32),
        scratch_shapes=[pltpu.VMEM((HALO + tt, c), F32), pltpu.VMEM((1, c), F32)],
        compiler_params=_params("parallel", "arbitrary"),
        name="rg_lru",
    )(tok, tok, conv_w, conv_b.reshape(1, c), wa, ba.reshape(1, c), wx, bx.reshape(1, c), lam.reshape(1, c))


def _rwkv_kernel(in_ref, kk_ref, ka_ref, rk_ref, lnw_ref, lnb_ref, o_ref, zt_ref):
    @pl.when(pl.program_id(0) == 0)
    def _():
        zt_ref[...] = jnp.zeros_like(zt_ref)

    shape = zt_ref.shape[:1] + (CHUNK, HEAD)
    r, lw, k, v, a, g = (in_ref[j].reshape(shape) for j in range(6))
    kk = k * kk_ref[...]
    kk = kk * lax.rsqrt(jnp.sum(kk * kk, axis=-1, keepdims=True) + EPS)
    k = k * (1.0 + (a - 1.0) * ka_ref[...])
    b = _cumsum_rows(lw)
    e_pos = jnp.exp(b)
    e_neg = jnp.exp(-b)
    r_t = r * e_pos
    al_t = -kk * jnp.exp(b - lw)
    be_t = kk * a * e_neg
    k_t = k * e_neg

    t_idx, s_idx = _tri_masks(CHUNK)
    strict = s_idx < t_idx
    ab_k = jnp.concatenate([be_t, k_t], axis=1)
    both = _bmm(jnp.concatenate([al_t, r_t], axis=1), ab_k, nt=True)
    l_ab = jnp.where(strict, both[:, :CHUNK, :CHUNK], 0.0)
    l_ak = jnp.where(strict, both[:, :CHUNK, CHUNK:], 0.0)
    t2 = lax.broadcasted_iota(jnp.int32, (CHUNK, 2 * CHUNK), 0)
    s2 = lax.broadcasted_iota(jnp.int32, (CHUNK, 2 * CHUNK), 1)
    incl2 = jnp.where(s2 >= CHUNK, s2 - CHUNK, s2) <= t2
    m_all = jnp.where(incl2, both[:, CHUNK:], 0.0)

    sol = _solve_unit_lower(-l_ab, jnp.concatenate([al_t, _bmm(l_ak, v)], axis=-1))
    w1, u0 = sol[:, :, :HEAD], sol[:, :, HEAD:]

    zt = zt_ref[...]
    from_state = _bmm(jnp.concatenate([w1, r_t], axis=1), zt, nt=True)
    u = from_state[:, :CHUNK] + u0
    uv = jnp.concatenate([u, v], axis=1)
    y = from_state[:, CHUNK:] + _bmm(m_all, uv)
    b_last = b[:, CHUNK - 1:CHUNK]
    e_last = jnp.exp(b_last - b)
    zt_ref[...] = zt * jnp.exp(b_last) + _bmm(jnp.swapaxes(uv, 1, 2),
                                              jnp.concatenate([kk * a * e_last, k * e_last], axis=1))

    mean = jnp.mean(y, axis=-1, keepdims=True)
    var = jnp.mean(jnp.square(y - mean), axis=-1, keepdims=True)
    yn = (y - mean) * lax.rsqrt(var + RWKV_GN_EPS) * lnw_ref[...] + lnb_ref[...]
    bonus = jnp.sum(r * k * rk_ref[...], axis=-1, keepdims=True) * v
    _store_tokens(o_ref, (yn + bonus) * g)


N_HGRN_IN, N_GDN_IN, N_RWKV_IN = 5, 7, 6


def _chunk_mixers_kernel(*refs):
    ins, rest = refs[:N_HGRN_IN + N_GDN_IN + N_RWKV_IN], refs[N_HGRN_IN + N_GDN_IN + N_RWKV_IN:]
    o_hgrn, o_gdn, o_rwkv, st_hgrn, rows_hgrn, xbuf_gdn, st_gdn, zt_rwkv = rest
    _hgrn_kernel(*ins[:N_HGRN_IN], o_hgrn, st_hgrn, rows_hgrn)
    _gdn_kernel(*ins[N_HGRN_IN:N_HGRN_IN + N_GDN_IN], o_gdn, xbuf_gdn, st_gdn)
    _rwkv_kernel(*ins[N_HGRN_IN + N_GDN_IN:], o_rwkv, zt_rwkv)


def _chunk_mixers(hmp, hm, d_hm, hgrn_args, gdn_args, rwkv_args):
    _, b, h, s, d = hm.shape
    g = b * h
    g2 = g // 2
    par = _full((g, 1, d))
    lb, hgrn_norm = hgrn_args
    gates, conv_w, a_log, dt_bias, gdn_norm = gdn_args
    hgrn_specs = [_hm_groups_spec(HMP_GROUPS, 0, PAIR)(b, h // 2)] + [_full((g2, 1, PAIR))] * 4
    gdn_specs = [_hm_groups_spec(3, HM_GDN_QKV)(b, h), _hm_groups_spec(1, HM_GDN_Z)(b, h),
                 pl.BlockSpec((2, 1, g, 1, CHUNK), lambda c: (0, c, 0, 0, 0)),
                 _full((CONV_K, 3 * g, 1, d)), _full((g, 1, 1)), _full((g, 1, 1)), _full((1, 1, d))]
    rwkv_specs = [_hm_groups_spec(6, 0)(b, h)] + [par] * 5
    assert (len(hgrn_specs), len(gdn_specs), len(rwkv_specs)) == (N_HGRN_IN, N_GDN_IN, N_RWKV_IN)
    out = jax.ShapeDtypeStruct((b, s, h * d), F32)
    state = pltpu.VMEM((g, d, d), F32)
    return pl.pallas_call(
        _chunk_mixers_kernel,
        grid=(s // CHUNK,),
        in_specs=hgrn_specs + gdn_specs + rwkv_specs,
        out_specs=[_tok_chunk_spec(b)] * 3,
        out_shape=[out] * 3,
        scratch_shapes=[pltpu.VMEM((g2, PAIR, PAIR), F32), pltpu.VMEM((3, g2, CHUNK, PAIR), F32),
                        pltpu.VMEM((3 * g, HALO + CHUNK, d), F32), state, state],
        compiler_params=_params("arbitrary"),
        name="chunk_mixers",
    )(hmp, jnp.log(lb), jnp.log1p(-lb), 1.0 - lb, hgrn_norm,
      hm, hm, gates, conv_w, a_log, dt_bias, gdn_norm,
      d_hm, *rwkv_args)


def _par_hm(p, b, width=HEAD):
    return jnp.tile(p.reshape(-1, 1, width), (b, 1, 1))


def _gdn_gate_layout(gates):
    b, s, h2 = gates.shape
    g = gates.reshape(b, s // CHUNK, CHUNK, 2, h2 // 2).transpose(3, 1, 0, 4, 2)
    return g.reshape(2, s // CHUNK, b * (h2 // 2), 1, CHUNK)


def _gdn_conv_layout(w, b):
    k, c = w.shape
    h = c // (3 * HEAD)
    w = jnp.broadcast_to(w.reshape(k, 3, 1, h, 1, HEAD), (k, 3, b, h, 1, HEAD))
    return w.reshape(k, 3 * b * h, 1, HEAD)


def _block_diag(w):
    n, a, b = w.shape
    out = jnp.zeros((n * a, n * b), w.dtype)
    for i in range(n):
        out = out.at[i * a:(i + 1) * a, i * b:(i + 1) * b].set(w[i])
    return out


def _in_proj_weight(w):
    d = w.shape[0]
    g4 = 4 * GROUP
    gates = g4 + g4
    lru = gates + 2 * HEADS
    rwkv = lru + 2 * GROUP
    cols = [w[:, rwkv:rwkv + g4], w[:, lru:rwkv], w[:, gates:lru], jnp.zeros((d, LANES - 2 * HEADS), w.dtype),
            w[:, 0:g4], w[:, g4:gates]]
    return jnp.concatenate(cols, axis=1).astype(BF16)


def kernel(x, mem, norm_mix, w_in, w_out, hgrn_lb_logits, hgrn_norm, gdn_conv_w, gdn_A_log, gdn_dt_bias, gdn_norm, lru_conv_w, lru_conv_b, lru_wa, lru_ba, lru_wx, lru_bx, lru_lambda, rwkv_mu, rwkv_w0, rwkv_w2, rwkv_a0, rwkv_a2, rwkv_g2, rwkv_k_k, rwkv_k_a, rwkv_r_k, rwkv_ln_w, rwkv_ln_b, rwkv_v0, rwkv_v1, rwkv_v2, mem_norm, norm_xattn, xattn_wq, xattn_wk, xattn_wv, xattn_wo, norm_ffn, ffn_w_gate, ffn_w_up, ffn_w_down, moe_router, moe_w_gate, moe_w_up, moe_w_down, norm_final):
    bsz, seq, d = x.shape
    depth = w_in.shape[0]
    tok = bsz * seq
    mlen = mem.shape[1]

    lb = jnp.cumsum(jax.nn.softmax(hgrn_lb_logits.astype(F32), axis=0), axis=0)
    lb = lb - lb[0]
    mem_f = mem.reshape(bsz * mlen, d)
    v_first = None
    for l in range(depth):
        v_mix = None if l == 0 else (rwkv_v0[l - 1], rwkv_v1[l - 1], rwkv_v2[l - 1])
        p_tok, p_hmp, p_hm, d_hm, v_first = _in_proj(
            x, norm_mix[l], _in_proj_weight(w_in[l]), rwkv_mu[l], rwkv_w0[l], rwkv_w2[l], rwkv_a0[l], rwkv_a2[l],
            rwkv_g2[l], v_first, v_mix)

        y_c = _lru_mixer(p_tok, lru_conv_w[l], lru_conv_b[l], _block_diag(lru_wa[l]), lru_ba[l],
                         _block_diag(lru_wx[l]), lru_bx[l], lru_lambda[l])

        gates = p_tok[..., TOK_GATES:TOK_GATES + 2 * HEADS]
        y_a, y_b, y_d = _chunk_mixers(
            p_hmp, p_hm, d_hm,
            (_par_hm(lb[l], bsz, PAIR), _par_hm(hgrn_norm[l], bsz, PAIR)),
            (_gdn_gate_layout(gates), _gdn_conv_layout(gdn_conv_w[l], bsz), _par_hm(gdn_A_log[l], bsz, 1),
             _par_hm(gdn_dt_bias[l], bsz, 1), gdn_norm[l].reshape(1, 1, HEAD)),
            tuple(_par_hm(p[l], bsz) for p in (rwkv_k_k, rwkv_k_a, rwkv_r_k, rwkv_ln_w, rwkv_ln_b)))

        w_kv = jnp.concatenate([xattn_wk[l], xattn_wv[l]], axis=1).astype(BF16)
        kv = _norm_matmul(mem_f, mem_norm, w_kv, mlen, BF16).reshape(bsz, mlen, 2 * d)
        x = _mix_attention((y_a, y_b, y_c, y_d), w_out[l].reshape(4, GROUP, d).astype(BF16), x, norm_xattn[l],
                           xattn_wq[l].astype(BF16), kv, xattn_wo[l].astype(BF16))

        xf = x.reshape(tok, d)
        j = l // 2
        if l % 2 == 0:
            ones = jnp.ones((1, tok, 1), F32)
            xf = _ffn(xf, norm_ffn[l], ffn_w_gate[j][None].astype(BF16), ffn_w_up[j][None].astype(BF16),
                      ffn_w_down[j][None].astype(BF16), ones, TF_FFN)
        else:
            xf = _moe(xf, norm_ffn[l], moe_router[j], moe_w_gate[j].astype(BF16), moe_w_up[j].astype(BF16),
                      moe_w_down[j], norm_final if l == depth - 1 else None)
        x = xf.reshape(bsz, seq, d)
    if depth % 2 == 1:
        x = _final_norm(x.reshape(tok, d), norm_final).reshape(bsz, seq, d)
    return x
```

```python
import functools
import math

import jax
import jax.numpy as jnp
from jax import lax
from jax.experimental import pallas as pl
from jax.experimental.pallas import tpu as pltpu

F32 = jnp.float32
BF16 = jnp.bfloat16
HIGHEST = lax.Precision.HIGHEST

GROUP = 256
HEAD = 64
HEADS = GROUP // HEAD
CHUNK = 64
SUB = 16
HGRN_SUB = 8
LOG2_E = 1.4426950408889634
BIG_EXPONENT = 1e30
CONV_K = 4
HALO = 8
LRU_C = 8.0
RWKV_GN_EPS = 64e-5
XATTN_HEADS = 4
N_EXPERTS = 8
EPS = 1e-6
LANES = 128
VMEM_LIMIT_BYTES = 56 * 1024 * 1024

TOK_LRU = 0
TOK_GATES = 2 * GROUP
N_TOK = 2 * GROUP + LANES
PAIR = 2 * HEAD
HMP_GROUPS = 4
HM_GDN_QKV, HM_GDN_Z = 0, 3
HM_GROUPS = 4

TM_IN_PROJ = 512
TM_XATTN = 1024
TM_FFN = 1024
TF_FFN = 1408
TM_NORM = 1024
TT_SCAN = 512
MOE_TB = 512
MOE_TILE = 512
MOE_F_SPLIT = 2
ROW_ALIGN = 16
SPARE_TILES = 2
MOE_SEG = 128
MOE_WIN = 256


def _params(*semantics):
    return pltpu.CompilerParams(dimension_semantics=semantics, vmem_limit_bytes=VMEM_LIMIT_BYTES)


def _full(shape):
    n = len(shape)
    return pl.BlockSpec(shape, lambda *_: (0,) * n)


def _bmm(a, b, nt=False, exact=False):
    dn = (((2,), (2 if nt else 1,)), ((0,), (0,)))
    if exact:
        return lax.dot_general(a, b, dn, precision=HIGHEST, preferred_element_type=F32)
    return lax.dot_general(a.astype(BF16), b.astype(BF16), dn, preferred_element_type=F32)


def _mm(a, b, precision=None):
    return jnp.dot(a, b, precision=precision, preferred_element_type=F32)


def _mm16(a, b):
    return jnp.dot(a.astype(BF16), b.astype(BF16), preferred_element_type=F32)


def _softplus(x):
    return jnp.maximum(x, 0.0) + jnp.log1p(jnp.exp(-jnp.abs(x)))


def _silu(x):
    return x * jax.nn.sigmoid(x)


def _rms(x, gain):
    return x * lax.rsqrt(jnp.mean(x * x, axis=-1, keepdims=True) + EPS) * gain


def _tri_masks(n):
    t = lax.broadcasted_iota(jnp.int32, (n, n), 0)
    s = lax.broadcasted_iota(jnp.int32, (n, n), 1)
    return t, s


def _cumsum_rows(x):
    g, c, _ = x.shape
    t, s = _tri_masks(c)
    tri = jnp.broadcast_to((s <= t).astype(BF16), (g, c, c))
    hi = x.astype(BF16)
    rest = x - hi.astype(F32)
    mid = rest.astype(BF16)
    lo = (rest - mid.astype(F32)).astype(BF16)
    return _bmm(tri, hi) + _bmm(tri, mid) + _bmm(tri, lo)


def _solve_unit_lower(low, rhs):
    c = low.shape[1]
    r = rhs.shape[2]
    assert c == 4 * SUB and SUB == 16
    t, s = _tri_masks(c)
    same_block = (t // SUB) == (s // SUB)
    d1 = jnp.where(same_block, low, 0.0)
    z = jnp.concatenate([rhs, jnp.where(same_block, 0.0, low)], axis=-1)
    d2 = _bmm(d1, d1)
    d4 = _bmm(d2, d2)
    d8 = _bmm(d4, d4)
    z = z + _bmm(d8, z)
    z = z + _bmm(d4, z)
    z = z + _bmm(d2, z)
    z = z - _bmm(d1, z)
    y = z[:, :, :r]
    n1 = z[:, :, r:]
    n2 = _bmm(n1, n1)
    y = y + _bmm(n2, y)
    return y - _bmm(n1, y)


def _store_heads(ref, val, lead=(), width=HEAD):
    for h in range(val.shape[1] // width):
        ref[lead + (h,)] = val[:, h * width:(h + 1) * width]


def _store_tokens(ref, o):
    nb = ref.shape[0]
    nh = o.shape[0] // nb
    for b in range(nb):
        ref[b] = jnp.concatenate([o[b * nh + h] for h in range(nh)], axis=-1)


def _in_proj_kernel(has_vmix, *refs):
    refs = list(refs)
    x_ref, g_ref, w_ref, mu_ref, w0_ref, w2_ref, a0_ref, a2_ref, g2_ref = refs[:9]
    del refs[:9]
    if has_vmix:
        vf_ref, v0_ref, v1_ref, v2_ref = refs[:4]
        del refs[:4]
    tok_ref, hmp_ref, hm_ref, dhm_ref = refs[:4]
    del refs[:4]
    vtok_ref = None if has_vmix else refs.pop(0)
    xbuf_ref = refs.pop(0)
    tm = x_ref.shape[1]

    hn = _rms(x_ref[0], g_ref[...]).astype(BF16)
    n_rwkv = 4 * GROUP
    c = _mm(hn, w_ref[:, :n_rwkv])
    acc = _mm(hn, w_ref[:, n_rwkv:])
    col = 0
    n_tok = tok_ref.shape[2]
    tok_ref[0] = acc[:, col:col + n_tok]
    col += n_tok
    for g in range(hmp_ref.shape[0]):
        _store_heads(hmp_ref, acc[:, col:col + GROUP], (g, 0), PAIR)
        col += GROUP
    for g in range(hm_ref.shape[0]):
        _store_heads(hm_ref, acc[:, col:col + GROUP], (g, 0))
        col += GROUP

    @pl.when(pl.program_id(1) == 0)
    def _():
        xbuf_ref[0:HALO, :] = jnp.zeros((HALO, xbuf_ref.shape[1]), F32)

    xbuf_ref[HALO:HALO + tm, :] = c
    prev = xbuf_ref[HALO - 1:HALO - 1 + tm, :]
    xbuf_ref[0:HALO, :] = xbuf_ref[tm:tm + HALO, :]
    c = c + mu_ref[...] * (prev - c)

    r = c[:, 0:GROUP]
    k = c[:, GROUP:2 * GROUP]
    v = c[:, 2 * GROUP:3 * GROUP]
    wa_lo = c[:, 3 * GROUP:3 * GROUP + LANES]
    g_lo = c[:, 3 * GROUP + LANES:4 * GROUP]
    log_w = -math.exp(-0.5) * jax.nn.sigmoid(w0_ref[...] + _mm16(jnp.tanh(wa_lo), w2_ref[...]))
    a = jax.nn.sigmoid(a0_ref[...] + _mm16(wa_lo, a2_ref[...]))
    gate = _mm16(jax.nn.sigmoid(g_lo), g2_ref[...])
    if has_vmix:
        mix = jax.nn.sigmoid(v0_ref[...] + _mm16(_mm16(v, v1_ref[...]), v2_ref[...]))
        v = v + (vf_ref[0] - v) * mix
    else:
        vtok_ref[0] = v
    for j, val in enumerate((r, log_w, k, v, a, gate)):
        _store_heads(dhm_ref, val, (j, 0))


def _in_proj(x, gain, w, mu, w0, w2, a0, a2, g2, v_first, v_mix):
    b, s, d = x.shape
    n = w.shape[1]
    tm = TM_IN_PROJ
    lora = w2.shape[0]
    w2p = jnp.zeros((LANES, GROUP), F32).at[:lora].set(w2)
    a2p = jnp.zeros((LANES, GROUP), F32).at[lora:].set(a2)
    tile = pl.BlockSpec((1, tm, GROUP), lambda i, j: (i, j, 0))
    vec = _full((1, GROUP))
    args = [x, gain.reshape(1, d), w, mu.reshape(1, 4 * GROUP), w0.reshape(1, GROUP), w2p, a0.reshape(1, GROUP),
            a2p, g2]
    specs = [pl.BlockSpec((1, tm, d), lambda i, j: (i, j, 0)), _full((1, d)), _full((d, n)), _full((1, 4 * GROUP)),
             vec, _full((LANES, GROUP)), vec, _full((LANES, GROUP)), _full(g2.shape)]
    out_specs = [pl.BlockSpec((1, tm, N_TOK), lambda i, j: (i, j, 0)),
                 pl.BlockSpec((HMP_GROUPS, 1, HEADS // 2, tm, PAIR), lambda i, j: (0, i, 0, j, 0)),
                 pl.BlockSpec((HM_GROUPS, 1, HEADS, tm, HEAD), lambda i, j: (0, i, 0, j, 0)),
                 pl.BlockSpec((6, 1, HEADS, tm, HEAD), lambda i, j: (0, i, 0, j, 0))]
    out_shape = [jax.ShapeDtypeStruct((b, s, N_TOK), F32),
                 jax.ShapeDtypeStruct((HMP_GROUPS, b, HEADS // 2, s, PAIR), F32),
                 jax.ShapeDtypeStruct((HM_GROUPS, b, HEADS, s, HEAD), F32),
                 jax.ShapeDtypeStruct((6, b, HEADS, s, HEAD), F32)]
    if v_mix is not None:
        v0, v1, v2 = v_mix
        rank = v1.shape[1]
        v1p = jnp.zeros((GROUP, LANES), F32).at[:, :rank].set(v1)
        v2p = jnp.zeros((LANES, GROUP), F32).at[:rank].set(v2)
        args += [v_first, v0.reshape(1, GROUP), v1p, v2p]
        specs += [tile, vec, _full((GROUP, LANES)), _full((LANES, GROUP))]
    else:
        out_specs.append(tile)
        out_shape.append(jax.ShapeDtypeStruct((b, s, GROUP), F32))
    outs = pl.pallas_call(
        functools.partial(_in_proj_kernel, v_mix is not None),
        grid=(b, s // tm),
        in_specs=specs,
        out_specs=out_specs,
        out_shape=out_shape,
        scratch_shapes=[pltpu.VMEM((HALO + tm, 4 * GROUP), F32)],
        compiler_params=_params("parallel", "arbitrary"),
        name="in_proj",
    )(*args)
    return tuple(outs[:4]) + ((v_first,) if v_mix is not None else (outs[4],))


def _norm_mm_kernel(x_ref, g_ref, w_ref, o_ref):
    o_ref[...] = _mm(_rms(x_ref[...], g_ref[...]).astype(BF16), w_ref[...]).astype(o_ref.dtype)


def _norm_matmul(x, gain, w, tm, out_dtype):
    t, k = x.shape
    n = w.shape[1]
    return pl.pallas_call(
        _norm_mm_kernel,
        grid=(t // tm,),
        in_specs=[pl.BlockSpec((tm, k), lambda i: (i, 0)), _full((1, k)), _full((k, n))],
        out_specs=pl.BlockSpec((tm, n), lambda i: (i, 0)),
        out_shape=jax.ShapeDtypeStruct((t, n), out_dtype),
        compiler_params=_params("parallel"),
        name="norm_matmul",
    )(x, gain.reshape(1, k), w)


def _mix_attn_kernel(ya_ref, yb_ref, yc_ref, yd_ref, wout_ref, x_ref, g_ref, wq_ref, k_ref, v_ref, wo_ref, o_ref):
    x = x_ref[0]
    for m, y_ref in enumerate((ya_ref, yb_ref, yc_ref, yd_ref)):
        x = x + _mm(y_ref[0].astype(BF16), wout_ref[m])
    hn = _rms(x, g_ref[...]).astype(BF16)
    q = _mm(hn, wq_ref[...])
    d = q.shape[1] // XATTN_HEADS
    outs = []
    for h in range(XATTN_HEADS):
        qh = (q[:, h * d:(h + 1) * d] * d ** -0.5).astype(BF16)
        kh = k_ref[0, :, h * d:(h + 1) * d]
        vh = v_ref[0, :, h * d:(h + 1) * d]
        s = lax.dot_general(qh, kh, (((1,), (1,)), ((), ())), preferred_element_type=F32)
        s = s - jnp.max(s, axis=-1, keepdims=True)
        p = jnp.exp(s)
        p = p / jnp.sum(p, axis=-1, keepdims=True)
        outs.append(_mm(p.astype(BF16), vh))
    o = jnp.concatenate(outs, axis=-1).astype(BF16)
    o_ref[0] = x + _mm(o, wo_ref[...])


def _mix_attention(ys, w_out, x, gain, wq, kv, wo):
    b, s, d = x.shape
    m = kv.shape[1]
    tm = TM_XATTN
    y_spec = pl.BlockSpec((1, tm, GROUP), lambda i, j: (i, j, 0))
    x_spec = pl.BlockSpec((1, tm, d), lambda i, j: (i, j, 0))
    return pl.pallas_call(
        _mix_attn_kernel,
        grid=(b, s // tm),
        in_specs=[y_spec] * 4 + [_full(w_out.shape), x_spec, _full((1, d)), _full((d, d)),
                                 pl.BlockSpec((1, m, d), lambda i, j: (i, 0, 0)),
                                 pl.BlockSpec((1, m, d), lambda i, j: (i, 0, 1)),
                                 _full((d, d))],
        out_specs=x_spec,
        out_shape=jax.ShapeDtypeStruct((b, s, d), F32),
        compiler_params=_params("parallel", "parallel"),
        name="mix_attention",
    )(*ys, w_out, x, gain.reshape(1, d), wq, kv, kv, wo)


def _ffn_kernel(x_ref, g_ref, wg_ref, wu_ref, wd_ref, c_ref, o_ref, hn_ref, acc_ref):
    f = pl.program_id(1)

    @pl.when(f == 0)
    def _():
        hn_ref[...] = _rms(x_ref[...], g_ref[...]).astype(BF16)
        acc_ref[...] = x_ref[...]

    hn = hn_ref[...]
    gate = _mm(hn, wg_ref[0])
    up = _mm(hn, wu_ref[0])
    mid = _silu(gate) * up * c_ref[0]
    acc_ref[...] += _mm(mid.astype(BF16), wd_ref[0])

    @pl.when(f == pl.num_programs(1) - 1)
    def _():
        o_ref[...] = acc_ref[...]


def _ffn(x, gain, wg, wu, wd, combine, tf):
    t, d = x.shape
    e, _, ff = wg.shape
    nf = ff // tf
    tm = TM_FFN
    return pl.pallas_call(
        _ffn_kernel,
        grid=(t // tm, e * nf),
        in_specs=[pl.BlockSpec((tm, d), lambda i, f: (i, 0)),
                  _full((1, d)),
                  pl.BlockSpec((1, d, tf), lambda i, f: (f // nf, 0, f % nf)),
                  pl.BlockSpec((1, d, tf), lambda i, f: (f // nf, 0, f % nf)),
                  pl.BlockSpec((1, tf, d), lambda i, f: (f // nf, f % nf, 0)),
                  pl.BlockSpec((1, tm, 1), lambda i, f: (f // nf, i, 0))],
        out_specs=pl.BlockSpec((tm, d), lambda i, f: (i, 0)),
        out_shape=jax.ShapeDtypeStruct((t, d), F32),
        scratch_shapes=[pltpu.VMEM((tm, d), BF16), pltpu.VMEM((tm, d), F32)],
        compiler_params=_params("parallel", "arbitrary"),
        name="ffn",
    )(x, gain.reshape(1, d), wg, wu, wd, combine)


def _router_kernel(x_ref, g_ref, w_ref, hn_ref, pos_ref, post_ref, cnt_ref):
    hn = _rms(x_ref[...], g_ref[...])
    d = hn.shape[1]
    hn_ref[:, :d] = hn.astype(BF16)
    logits = _mm(hn, w_ref[...], precision=HIGHEST)
    lane = lax.broadcasted_iota(jnp.int32, logits.shape, 1)
    neg = jnp.float32(-jnp.inf)
    logits = jnp.where(lane < N_EXPERTS, logits, neg)
    m1 = jnp.max(logits, axis=-1, keepdims=True)
    i1 = jnp.min(jnp.where(logits == m1, lane, LANES), axis=-1, keepdims=True)
    rest = jnp.where(lane == i1, neg, logits)
    m2 = jnp.max(rest, axis=-1, keepdims=True)
    i2 = jnp.min(jnp.where(rest == m2, lane, LANES), axis=-1, keepdims=True)
    e2 = jnp.exp(m2 - m1)
    comb = jnp.where(lane == i1, 1.0 / (1.0 + e2), 0.0) + jnp.where(lane == i2, e2 / (1.0 + e2), 0.0)
    hi = comb.astype(BF16).astype(F32)
    rest = comb - hi
    mid = rest.astype(BF16).astype(F32)
    lo = rest - mid
    hn_ref[:, d:] = (hi + pltpu.roll(mid, N_EXPERTS, 1) + pltpu.roll(lo, 2 * N_EXPERTS, 1)).astype(BF16)

    sel = (comb > 0.0).astype(F32)
    tb = sel.shape[0]
    row = lax.broadcasted_iota(jnp.int32, sel.shape, 0)
    run = sel
    shift = 1
    while shift < tb:
        run = run + jnp.where(row >= shift, pltpu.roll(run, shift, 0), 0.0)
        shift *= 2
    pos = jnp.where(sel > 0.0, run - 1.0, -1.0)
    pos_ref[...] = pos
    post_ref[0] = jnp.transpose(pos)[:N_EXPERTS, :]
    cnt_ref[0] = run[tb - 1:tb, :].astype(jnp.int32)


def _router(x, gain, w_router):
    t, d = x.shape
    tb = MOE_TB
    nb = t // tb
    w = jnp.zeros((d, LANES), F32).at[:, :N_EXPERTS].set(w_router)
    return pl.pallas_call(
        _router_kernel,
        grid=(nb,),
        in_specs=[pl.BlockSpec((tb, d), lambda i: (i, 0)), _full((1, d)), _full((d, LANES))],
        out_specs=[pl.BlockSpec((tb, d + LANES), lambda i: (i, 0)), pl.BlockSpec((tb, LANES), lambda i: (i, 0)),
                   pl.BlockSpec((1, N_EXPERTS, tb), lambda i: (i, 0, 0)),
                   pl.BlockSpec((1, 1, LANES), lambda i: (i, 0, 0))],
        out_shape=[jax.ShapeDtypeStruct((t, d + LANES), BF16), jax.ShapeDtypeStruct((t, LANES), F32),
                   jax.ShapeDtypeStruct((nb, N_EXPERTS, tb), F32),
                   jax.ShapeDtypeStruct((nb, 1, LANES), jnp.int32)],
        compiler_params=_params("parallel"),
        name="router",
    )(x, gain.reshape(1, d), w)


def _route_tables(cnt, n_tiles_max):
    nb, ne = cnt.shape
    cpad = (cnt + (ROW_ALIGN - 1)) // ROW_ALIGN * ROW_ALIGN
    off = jnp.cumsum(cpad, axis=0) - cpad
    total = jnp.sum(cpad, axis=0)
    ntile = (total + MOE_TILE - 1) // MOE_TILE
    first = jnp.cumsum(ntile) - ntile
    n_tiles = jnp.sum(ntile)
    region = (first + SPARE_TILES * jnp.arange(ne, dtype=jnp.int32)) * MOE_TILE
    dst = region[None, :] + off
    tail = region + total
    sub = jnp.arange(MOE_TB // MOE_WIN, dtype=jnp.int32)[None, None, :] * MOE_WIN
    want = off[:, :, None] + sub
    win_rel = jnp.clip(want, 0, jnp.maximum(ntile * MOE_TILE - MOE_WIN, 0)[None, :, None])
    win = jnp.minimum(first[None, :, None] * MOE_TILE + win_rel, n_tiles_max * MOE_TILE - MOE_WIN)
    shift = want - win_rel
    t = jnp.minimum(jnp.arange(n_tiles_max, dtype=jnp.int32), n_tiles - 1)
    tile_e = jnp.minimum(jnp.sum(t[:, None] >= jnp.cumsum(ntile)[None, :], axis=1), ne - 1).astype(jnp.int32)
    tile_blk = t + SPARE_TILES * tile_e
    valid = (jnp.arange(n_tiles_max, dtype=jnp.int32) < n_tiles).astype(jnp.int32)
    n_blocks = n_tiles_max + SPARE_TILES * ne
    blocks = jnp.arange(n_blocks, dtype=jnp.int32)
    unused = 1 - jnp.max((tile_blk[:, None] == blocks[None, :]) * valid[:, None], axis=0)
    rank = jnp.cumsum(unused) - 1
    fill = jnp.sum(jnp.where((rank[None, :] == blocks[:, None]) & (unused[None, :] > 0), blocks[None, :], 0), axis=1)
    n_fill = jnp.sum(unused).astype(jnp.int32).reshape(1)
    flat = lambda a: a.reshape(-1).astype(jnp.int32)
    return ((flat(cpad), flat(dst), flat(tail), flat(fill), n_fill), (flat(cnt), flat(win), flat(shift)),
            (tile_e, tile_blk.astype(jnp.int32), valid))


def _dispatch_kernel(cpad_ref, dst_ref, tail_ref, fill_ref, nfill_ref, hn_ref, post_ref, xs_ref, seg_ref, sem):
    blk = pl.program_id(0)
    last = pl.num_programs(0) - 1
    tb = hn_ref.shape[0]
    assert tb == MOE_TILE
    n_piece = tb // MOE_SEG
    slot = blk % 2

    def needed(b, e, j):
        return cpad_ref[b * N_EXPERTS + e] > j * MOE_SEG

    def piece_copy(b, sl, e, j):
        row = pl.multiple_of(dst_ref[b * N_EXPERTS + e] + j * MOE_SEG, ROW_ALIGN)
        return pltpu.make_async_copy(seg_ref.at[sl, e, pl.ds(j * MOE_SEG, MOE_SEG)],
                                     xs_ref.at[pl.ds(row, MOE_SEG)], sem.at[e, j])

    rank = lax.broadcasted_iota(jnp.int32, (MOE_SEG, tb), 0).astype(F32)
    for e in range(N_EXPERTS):
        for j in range(n_piece):
            @pl.when(needed(blk, e, j))
            def _(e=e, j=j):
                onehot = (rank + float(j * MOE_SEG) == post_ref[0, e:e + 1, :]).astype(BF16)
                seg_ref[slot, e, j * MOE_SEG:(j + 1) * MOE_SEG, :] = _mm(onehot, hn_ref[...]).astype(BF16)

        @pl.when(blk > 0)
        def _(e=e):
            for j in range(n_piece):
                @pl.when(needed(blk - 1, e, j))
                def _(j=j):
                    piece_copy(blk - 1, 1 - slot, e, j).wait()

        for j in range(n_piece):
            @pl.when(needed(blk, e, j))
            def _(e=e, j=j):
                piece_copy(blk, slot, e, j).start()

    @pl.when(blk == last)
    def _():
        for e in range(N_EXPERTS):
            for j in range(n_piece):
                @pl.when(needed(blk, e, j))
                def _(e=e, j=j):
                    piece_copy(blk, slot, e, j).wait()

        zeros_ref = seg_ref.at[0, 0]
        zeros_ref[...] = jnp.zeros(zeros_ref.shape, BF16)

        def tail_copy(e):
            row = pl.multiple_of(tail_ref[e], ROW_ALIGN)
            return pltpu.make_async_copy(zeros_ref, xs_ref.at[pl.ds(row, tb)], sem.at[e, 0])

        for e in range(N_EXPERTS):
            tail_copy(e).start()
        for e in range(N_EXPERTS):
            tail_copy(e).wait()

        def fill_copy(k):
            row = pl.multiple_of(fill_ref[k] * MOE_TILE, MOE_TILE)
            return pltpu.make_async_copy(zeros_ref, xs_ref.at[pl.ds(row, tb)], sem.at[0, 0])

        def start_fill(k, carry):
            fill_copy(k).start()
            return carry

        def wait_fill(k, carry):
            fill_copy(k).wait()
            return carry

        lax.fori_loop(0, nfill_ref[0], start_fill, 0)
        lax.fori_loop(0, nfill_ref[0], wait_fill, 0)


def _dispatch(tables, hn, pos_t, n_rows):
    t, d = hn.shape
    tb = MOE_TB
    return pl.pallas_call(
        _dispatch_kernel,
        grid_spec=pltpu.PrefetchScalarGridSpec(
            num_scalar_prefetch=len(tables),
            grid=(t // tb,),
            in_specs=[pl.BlockSpec((tb, d), lambda i, *_: (i, 0)),
                      pl.BlockSpec((1, N_EXPERTS, tb), lambda i, *_: (i, 0, 0))],
            out_specs=pl.BlockSpec(memory_space=pl.ANY),
            scratch_shapes=[pltpu.VMEM((2, N_EXPERTS, tb, d), BF16),
                            pltpu.SemaphoreType.DMA((N_EXPERTS, tb // MOE_SEG))]),
        out_shape=jax.ShapeDtypeStruct((n_rows, d), BF16),
        compiler_params=_params("arbitrary"),
        name="moe_dispatch",
    )(*tables, hn, pos_t)


def _gate_up_kernel(te_ref, tb_ref, ok_ref, x_ref, wg_ref, wu_ref, o_ref):
    t = pl.program_id(1)

    @pl.when(ok_ref[t] > 0)
    def _():
        x = x_ref[...]
        o_ref[...] = (_silu(_mm(x, wg_ref[0])) * _mm(x, wu_ref[0])).astype(o_ref.dtype)

    @pl.when(ok_ref[t] == 0)
    def _():
        o_ref[...] = jnp.zeros_like(o_ref)


def _gate_up(tile_e, tile_blk, valid, xs, wg, wu):
    d = wg.shape[1]
    ff = wg.shape[2]
    nt = tile_e.shape[0]
    tf = ff // MOE_F_SPLIT
    w_spec = pl.BlockSpec((1, d, tf), lambda f, t, te, tb, ok: (te[t], 0, f))
    return pl.pallas_call(
        _gate_up_kernel,
        grid_spec=pltpu.PrefetchScalarGridSpec(
            num_scalar_prefetch=3,
            grid=(MOE_F_SPLIT, nt),
            in_specs=[pl.BlockSpec((MOE_TILE, d), lambda f, t, te, tb, ok: (tb[t], 0)), w_spec, w_spec],
            out_specs=pl.BlockSpec((MOE_TILE, tf), lambda f, t, te, tb, ok: (t, f))),
        out_shape=jax.ShapeDtypeStruct((nt * MOE_TILE, ff), BF16),
        compiler_params=_params("arbitrary", "arbitrary"),
        name="moe_gate_up",
    )(tile_e, tile_blk, valid, xs, wg, wu)


def _down_kernel(te_ref, tb_ref, ok_ref, m_ref, gate_ref, wd_ref, o_ref, wbf_ref):
    t = pl.program_id(0)
    expert = te_ref[t]

    @pl.when((t == 0) | (expert != te_ref[jnp.maximum(t - 1, 0)]))
    def _():
        wbf_ref[...] = wd_ref[0].astype(BF16)

    @pl.when(ok_ref[t] > 0)
    def _():
        lane = lax.broadcasted_iota(jnp.int32, gate_ref.shape, 1)
        mine = ((lane & (N_EXPERTS - 1)) == expert) & (lane < 3 * N_EXPERTS)
        gate = jnp.sum(jnp.where(mine, gate_ref[...].astype(F32), 0.0), axis=-1, keepdims=True)
        o_ref[...] = (gate * _mm(m_ref[...], wbf_ref[...])).astype(o_ref.dtype)

    @pl.when(ok_ref[t] == 0)
    def _():
        o_ref[...] = jnp.zeros_like(o_ref)


def _down(tile_e, tile_blk, valid, mid, xs, wd):
    ff, d = wd.shape[1:]
    nt = tile_e.shape[0]
    assert N_EXPERTS & (N_EXPERTS - 1) == 0
    return pl.pallas_call(
        _down_kernel,
        grid_spec=pltpu.PrefetchScalarGridSpec(
            num_scalar_prefetch=3,
            grid=(nt,),
            in_specs=[pl.BlockSpec((MOE_TILE, ff), lambda t, te, tb, ok: (t, 0)),
                      pl.BlockSpec((MOE_TILE, LANES), lambda t, te, tb, ok: (tb[t], d // LANES)),
                      pl.BlockSpec((1, ff, d), lambda t, te, tb, ok: (te[t], 0, 0))],
            out_specs=pl.BlockSpec((MOE_TILE, d), lambda t, te, tb, ok: (t, 0)),
            scratch_shapes=[pltpu.VMEM((ff, d), BF16)]),
        out_shape=jax.ShapeDtypeStruct((nt * MOE_TILE, d), BF16),
        compiler_params=_params("arbitrary"),
        name="moe_down",
    )(tile_e, tile_blk, valid, mid, xs, wd)


def _combine_kernel(final, cnt_ref, win_ref, shift_ref, x_ref, pos_ref, ys_ref, fg_ref, o_ref, win_buf, sem):
    blk = pl.program_id(0)
    tb = x_ref.shape[0]
    n_sub = tb // MOE_WIN
    extra = [(e, j) for e in range(N_EXPERTS) for j in range(1, n_sub)]
    slot = blk % 2

    def needed(b, e, j):
        return cnt_ref[b * N_EXPERTS + e] > j * MOE_WIN

    def win_copy(b, sl, e, j):
        row = pl.multiple_of(win_ref[(b * N_EXPERTS + e) * n_sub + j], ROW_ALIGN)
        return pltpu.make_async_copy(ys_ref.at[pl.ds(row, MOE_WIN)], win_buf.at[sl, j, e], sem.at[sl, j, e])

    def start_block(b, sl):
        for e in range(N_EXPERTS):
            win_copy(b, sl, e, 0).start()
        for e, j in extra:
            @pl.when(needed(b, e, j))
            def _(e=e, j=j):
                win_copy(b, sl, e, j).start()

    @pl.when(blk == 0)
    def _():
        start_block(blk, slot)

    @pl.when(blk + 1 < pl.num_programs(0))
    def _():
        start_block(blk + 1, 1 - slot)

    def onehot(e, j):
        pos = pos_ref[...][:, e:e + 1]
        in_sub = (pos >= float(j * MOE_WIN)) & (pos < float((j + 1) * MOE_WIN))
        shift = shift_ref[(blk * N_EXPERTS + e) * n_sub + j].astype(F32)
        target = jnp.where(in_sub, pos - float(j * MOE_WIN) + shift, -1.0)
        col = lax.broadcasted_iota(jnp.int32, (tb, MOE_WIN), 1).astype(F32)
        return (col == target).astype(BF16)

    first = jnp.concatenate([onehot(e, 0) for e in range(N_EXPERTS)], axis=1)
    for e in range(N_EXPERTS):
        win_copy(blk, slot, e, 0).wait()
    o_ref[...] = x_ref[...] + _mm(first, win_buf[slot, 0].reshape(N_EXPERTS * MOE_WIN, win_buf.shape[4]))
    for e, j in extra:
        @pl.when(needed(blk, e, j))
        def _(e=e, j=j):
            win_copy(blk, slot, e, j).wait()
            o_ref[...] += _mm(onehot(e, j), win_buf[slot, j, e])

    if final:
        o_ref[...] = _rms(o_ref[...], fg_ref[...])


def _combine(tables, x, pos, ys, final_gain):
    t, d = x.shape
    final = final_gain is not None
    gain = final_gain if final else jnp.ones((d,), F32)
    tb = MOE_TB
    n_sub = tb // MOE_WIN
    return pl.pallas_call(
        functools.partial(_combine_kernel, final),
        grid_spec=pltpu.PrefetchScalarGridSpec(
            num_scalar_prefetch=len(tables),
            grid=(t // tb,),
            in_specs=[pl.BlockSpec((tb, d), lambda i, *_: (i, 0)), pl.BlockSpec((tb, LANES), lambda i, *_: (i, 0)),
                      pl.BlockSpec(memory_space=pl.ANY), pl.BlockSpec((1, d), lambda i, *_: (0, 0))],
            out_specs=pl.BlockSpec((tb, d), lambda i, *_: (i, 0)),
            scratch_shapes=[pltpu.VMEM((2, n_sub, N_EXPERTS, MOE_WIN, d), BF16),
                            pltpu.SemaphoreType.DMA((2, n_sub, N_EXPERTS))]),
        out_shape=jax.ShapeDtypeStruct((t, d), F32),
        compiler_params=_params("arbitrary"),
        name="moe_combine",
    )(*tables, x, pos, ys, gain.reshape(1, d))


def _moe(x, gain, w_router, wg, wu, wd, final_gain):
    t, d = x.shape
    nb = t // MOE_TB
    n_tiles_max = (2 * t + nb * N_EXPERTS * (ROW_ALIGN - 1)) // MOE_TILE + N_EXPERTS
    hn, pos, pos_t, cnt = _router(x, gain, w_router)
    write_tables, read_tables, (tile_e, tile_blk, valid) = _route_tables(cnt[:, 0, :N_EXPERTS], n_tiles_max)
    n_rows = (n_tiles_max + SPARE_TILES * N_EXPERTS) * MOE_TILE
    xs = _dispatch(write_tables, hn, pos_t, n_rows)
    mid = _gate_up(tile_e, tile_blk, valid, xs, wg, wu)
    ys = _down(tile_e, tile_blk, valid, mid, xs, wd)
    return _combine(read_tables, x, pos, ys, final_gain)


def _final_norm_kernel(x_ref, g_ref, o_ref):
    o_ref[...] = _rms(x_ref[...], g_ref[...])


def _final_norm(x, gain):
    t, d = x.shape
    tm = TM_NORM
    return pl.pallas_call(
        _final_norm_kernel,
        grid=(t // tm,),
        in_specs=[pl.BlockSpec((tm, d), lambda i: (i, 0)), _full((1, d))],
        out_specs=pl.BlockSpec((tm, d), lambda i: (i, 0)),
        out_shape=jax.ShapeDtypeStruct((t, d), F32),
        compiler_params=_params("parallel"),
        name="final_norm",
    )(x, gain.reshape(1, d))


def _hm_groups_spec(n, first, width=HEAD):
    def spec(b, h):
        return pl.BlockSpec((n, b, h, CHUNK, width), lambda c: (first // n, 0, 0, c, 0))
    assert first % n == 0
    return spec


def _tok_chunk_spec(b):
    return pl.BlockSpec((b, CHUNK, GROUP), lambda c: (0, c, 0))


def _expand_pair(x):
    first = lax.broadcasted_iota(jnp.int32, x.shape, 2) < HEAD
    return jnp.concatenate([jnp.where(first, x, 0.0), jnp.where(first, 0.0, x)], axis=1)


def _same_head(n):
    t, s = _tri_masks(n)
    return (t >= n // 2) == (s >= n // 2)


def _hgrn_kernel(qfig_ref, loglb_ref, log1mlb_ref, onemlb_ref, ng_ref, o_ref, st_ref, rows_ref):
    @pl.when(pl.program_id(0) == 0)
    def _():
        st_ref[...] = jnp.zeros_like(st_ref)

    g2 = st_ref.shape[0]
    shape = (g2, CHUNK, PAIR)
    q, fl, v, g = (qfig_ref[j].reshape(shape) for j in range(4))
    log_sig = jnp.minimum(fl, 0.0) - jnp.log(1.0 + jnp.exp(-jnp.abs(fl)))
    la = loglb_ref[...]
    lc = log1mlb_ref[...] + log_sig
    log_f = jnp.maximum(la, lc) + jnp.log(1.0 + jnp.exp(-jnp.abs(la - lc)))
    k = onemlb_ref[...] * jax.nn.sigmoid(-fl)
    b = _cumsum_rows(log_f) * LOG2_E
    head_sum = jnp.broadcast_to(_same_head(PAIR).astype(BF16), (g2, PAIR, PAIR))

    rows_ref[0] = b
    rows_ref[1] = k
    rows_ref[2] = v
    sub = HGRN_SUB
    a_rows = [jnp.zeros((g2, sub, 2 * CHUNK), F32)]
    for i in range(1, CHUNK // sub):
        bref = rows_ref[0, :, i * sub - 1:i * sub, :]
        qd = q[:, i * sub:(i + 1) * sub] * jnp.exp2(b[:, i * sub:(i + 1) * sub] - bref)
        kd = k[:, :i * sub] * jnp.exp2(bref - b[:, :i * sub])
        later = jnp.zeros((g2, CHUNK - i * sub, PAIR), F32)
        a_rows.append(_bmm(qd, _expand_pair(jnp.concatenate([kd, later], axis=1)), nt=True))
    o = _bmm(jnp.concatenate(a_rows, axis=1), _expand_pair(v))

    t_loc = lax.broadcasted_iota(jnp.int32, (g2, sub, PAIR), 1)
    diag = []
    for i in range(CHUNK // sub):
        sl = slice(i * sub, (i + 1) * sub)
        bb, qq = b[:, sl], q[:, sl]
        terms = []
        for s in range(sub):
            row = i * sub + s
            e = jnp.exp2(jnp.where(t_loc >= s, bb - rows_ref[0, :, row:row + 1, :], -BIG_EXPONENT))
            terms.append(e * qq * rows_ref[1, :, row:row + 1, :])
        w = _bmm(jnp.concatenate(terms, axis=1), head_sum)
        acc = jnp.zeros((g2, sub, PAIR), F32)
        for s in range(sub):
            acc = acc + w[:, s * sub:(s + 1) * sub] * rows_ref[2, :, i * sub + s:i * sub + s + 1, :]
        diag.append(acc)
    o = o + jnp.concatenate(diag, axis=1)

    st = st_ref[...]
    o = o + _bmm(q * jnp.exp2(b), st, nt=True)
    b_last = b[:, CHUNK - 1:CHUNK]
    update = _bmm(jnp.swapaxes(v, 1, 2), k * jnp.exp2(b_last - b))
    st_ref[...] = st * jnp.exp2(b_last) + jnp.where(_same_head(PAIR), update, 0.0)

    mean_sq = _bmm(o * o, head_sum) * (1.0 / HEAD)
    _store_tokens(o_ref, o * lax.rsqrt(mean_sq + EPS) * ng_ref[...] * _silu(g))


def _gdn_kernel(qkv_ref, z_ref, gates_ref, cw_ref, alog_ref, dtb_ref, ng_ref, o_ref, xbuf_ref, st_ref):
    g3 = xbuf_ref.shape[0]
    g = g3 // 3

    @pl.when(pl.program_id(0) == 0)
    def _():
        st_ref[...] = jnp.zeros_like(st_ref)
        xbuf_ref[:, 0:HALO, :] = jnp.zeros((g3, HALO, HEAD), F32)

    xbuf_ref[:, HALO:HALO + CHUNK, :] = qkv_ref[...].reshape(g3, CHUNK, HEAD)
    conv = jnp.zeros((g3, CHUNK, HEAD), F32)
    for j in range(CONV_K):
        conv = conv + cw_ref[j] * xbuf_ref[:, HALO - (CONV_K - 1) + j:HALO - (CONV_K - 1) + j + CHUNK, :]
    xbuf_ref[:, 0:HALO, :] = xbuf_ref[:, CHUNK:CHUNK + HALO, :]
    qkv = _silu(conv)
    q, k, v = qkv[0:g], qkv[g:2 * g], qkv[2 * g:3 * g]
    q = q * lax.rsqrt(jnp.sum(q * q, axis=-1, keepdims=True) + EPS) * HEAD ** -0.5
    k = k * lax.rsqrt(jnp.sum(k * k, axis=-1, keepdims=True) + EPS)

    t_idx, s_idx = _tri_masks(CHUNK)
    causal = s_idx <= t_idx
    eye = s_idx == t_idx
    g_row = -jnp.exp(alog_ref[...]) * _softplus(gates_ref[0, 0] + dtb_ref[...])
    g_col = jnp.sum(jnp.where(eye, g_row, 0.0), axis=-1, keepdims=True)
    beta = jnp.sum(jnp.where(eye, jax.nn.sigmoid(gates_ref[1, 0]), 0.0), axis=-1, keepdims=True)
    gc_col = jnp.sum(jnp.where(causal, g_row, 0.0), axis=-1, keepdims=True)
    gc_row = jnp.sum(jnp.where(t_idx <= s_idx, g_col, 0.0), axis=1, keepdims=True)
    rel = jnp.where(causal, jnp.exp(jnp.minimum(gc_col - gc_row, 0.0)), 0.0)

    kb = k * beta
    with_k = _bmm(jnp.concatenate([kb, q], axis=1), k, nt=True)
    lower = jnp.where(s_idx < t_idx, with_k[:, :CHUNK] * rel, 0.0)
    uw = _solve_unit_lower(lower, jnp.concatenate([v * beta, kb * jnp.exp(gc_col)], axis=-1))
    u, w = uw[:, :, :HEAD], uw[:, :, HEAD:]
    scores = with_k[:, CHUNK:] * rel

    st = st_ref[...]
    from_state = _bmm(jnp.concatenate([w, q * jnp.exp(gc_col)], axis=1), st)
    v_new = u - from_state[:, :CHUNK]
    o = from_state[:, CHUNK:] + _bmm(scores, v_new)
    g_last = gc_col[:, CHUNK - 1:CHUNK]
    kd = k * jnp.exp(g_last - gc_col)
    st_ref[...] = st * jnp.exp(g_last) + _bmm(jnp.swapaxes(kd, 1, 2), v_new)

    _store_tokens(o_ref, _rms(o, ng_ref[...]) * _silu(z_ref[0].reshape(g, CHUNK, HEAD)))


def _lru_kernel(x_ref, gate_ref, cw_ref, cb_ref, wa_ref, ba_ref, wx_ref, bx_ref, lam_ref, o_ref,
                xbuf_ref, h_ref):
    tt = x_ref.shape[1]

    @pl.when(pl.program_id(1) == 0)
    def _():
        h_ref[...] = jnp.zeros_like(h_ref)
        xbuf_ref[0:HALO, :] = jnp.zeros((HALO, GROUP), F32)

    xbuf_ref[HALO:HALO + tt, :] = x_ref[0]
    xc = jnp.zeros((tt, GROUP), F32) + cb_ref[...]
    for j in range(CONV_K):
        xc = xc + cw_ref[j:j + 1, :] * xbuf_ref[HALO - (CONV_K - 1) + j:HALO - (CONV_K - 1) + j + tt, :]
    xbuf_ref[0:HALO, :] = xbuf_ref[tt:tt + HALO, :]

    r = jax.nn.sigmoid(_mm16(xc, wa_ref[...]) + ba_ref[...])
    i = jax.nn.sigmoid(_mm16(xc, wx_ref[...]) + bx_ref[...])
    log_a = -LRU_C * r * _softplus(-lam_ref[...])
    a = jnp.exp(log_a)
    u = jnp.sqrt(1.0 - a * a) * (i * xc)

    row = lax.broadcasted_iota(jnp.int32, (tt, GROUP), 0)
    shift = 1
    while shift < tt:
        keep = row >= shift
        a_prev = pltpu.roll(a, shift, 0)
        u_prev = pltpu.roll(u, shift, 0)
        u = jnp.where(keep, a * u_prev + u, u)
        a = jnp.where(keep, a * a_prev, a)
        shift *= 2
    hs = u + a * h_ref[...]
    h_ref[...] = hs[tt - 1:tt, :]

    gate = gate_ref[0]
    gelu = 0.5 * gate * (1.0 + jnp.tanh(math.sqrt(2.0 / math.pi) * (gate + 0.044715 * gate * gate * gate)))
    o_ref[0] = hs * gelu


def _lru_mixer(tok, conv_w, conv_b, wa, ba, wx, bx, lam):
    b, s, _ = tok.shape
    c = GROUP
    tt = TT_SCAN
    x_spec = pl.BlockSpec((1, tt, c), lambda i, t: (i, t, TOK_LRU // GROUP))
    gate_spec = pl.BlockSpec((1, tt, c), lambda i, t: (i, t, TOK_LRU // GROUP + 1))
    vec = _full((1, c))
    return pl.pallas_call(
        _lru_kernel,
        grid=(b, s // tt),
        in_specs=[x_spec, gate_spec, _full((CONV_K, c)), vec, _full((c, c)), vec, _full((c, c)), vec, vec],
        out_specs=pl.BlockSpec((1, tt, c), lambda i, t: (i, t, 0)),
        out_shape=jax.ShapeDtypeStruct((b, s, c), F32),
        scratch_shapes=[pltpu.VMEM((HALO + tt, c), F32), pltpu.VMEM((1, c), F32)],
        compiler_params=_params("parallel", "arbitrary"),
        name="rg_lru",
    )(tok, tok, conv_w, conv_b.reshape(1, c), wa, ba.reshape(1, c), wx, bx.reshape(1, c), lam.reshape(1, c))


def _rwkv_kernel(in_ref, kk_ref, ka_ref, rk_ref, lnw_ref, lnb_ref, o_ref, zt_ref):
    @pl.when(pl.program_id(0) == 0)
    def _():
        zt_ref[...] = jnp.zeros_like(zt_ref)

    shape = zt_ref.shape[:1] + (CHUNK, HEAD)
    r, lw, k, v, a, g = (in_ref[j].reshape(shape) for j in range(6))
    kk = k * kk_ref[...]
    kk = kk * lax.rsqrt(jnp.sum(kk * kk, axis=-1, keepdims=True) + EPS)
    k = k * (1.0 + (a - 1.0) * ka_ref[...])
    b = _cumsum_rows(lw)
    e_pos = jnp.exp(b)
    e_neg = jnp.exp(-b)
    r_t = r * e_pos
    al_t = -kk * jnp.exp(b - lw)
    be_t = kk * a * e_neg
    k_t = k * e_neg

    t_idx, s_idx = _tri_masks(CHUNK)
    strict = s_idx < t_idx
    ab_k = jnp.concatenate([be_t, k_t], axis=1)
    both = _bmm(jnp.concatenate([al_t, r_t], axis=1), ab_k, nt=True)
    l_ab = jnp.where(strict, both[:, :CHUNK, :CHUNK], 0.0)
    l_ak = jnp.where(strict, both[:, :CHUNK, CHUNK:], 0.0)
    t2 = lax.broadcasted_iota(jnp.int32, (CHUNK, 2 * CHUNK), 0)
    s2 = lax.broadcasted_iota(jnp.int32, (CHUNK, 2 * CHUNK), 1)
    incl2 = jnp.where(s2 >= CHUNK, s2 - CHUNK, s2) <= t2
    m_all = jnp.where(incl2, both[:, CHUNK:], 0.0)

    sol = _solve_unit_lower(-l_ab, jnp.concatenate([al_t, _bmm(l_ak, v)], axis=-1))
    w1, u0 = sol[:, :, :HEAD], sol[:, :, HEAD:]

    zt = zt_ref[...]
    from_state = _bmm(jnp.concatenate([w1, r_t], axis=1), zt, nt=True)
    u = from_state[:, :CHUNK] + u0
    uv = jnp.concatenate([u, v], axis=1)
    y = from_state[:, CHUNK:] + _bmm(m_all, uv)
    b_last = b[:, CHUNK - 1:CHUNK]
    e_last = jnp.exp(b_last - b)
    zt_ref[...] = zt * jnp.exp(b_last) + _bmm(jnp.swapaxes(uv, 1, 2),
                                              jnp.concatenate([kk * a * e_last, k * e_last], axis=1))

    mean = jnp.mean(y, axis=-1, keepdims=True)
    var = jnp.mean(jnp.square(y - mean), axis=-1, keepdims=True)
    yn = (y - mean) * lax.rsqrt(var + RWKV_GN_EPS) * lnw_ref[...] + lnb_ref[...]
    bonus = jnp.sum(r * k * rk_ref[...], axis=-1, keepdims=True) * v
    _store_tokens(o_ref, (yn + bonus) * g)


N_HGRN_IN, N_GDN_IN, N_RWKV_IN = 5, 7, 6


def _chunk_mixers_kernel(*refs):
    ins, rest = refs[:N_HGRN_IN + N_GDN_IN + N_RWKV_IN], refs[N_HGRN_IN + N_GDN_IN + N_RWKV_IN:]
    o_hgrn, o_gdn, o_rwkv, st_hgrn, rows_hgrn, xbuf_gdn, st_gdn, zt_rwkv = rest
    _rwkv_kernel(*ins[N_HGRN_IN + N_GDN_IN:], o_rwkv, zt_rwkv)
    _hgrn_kernel(*ins[:N_HGRN_IN], o_hgrn, st_hgrn, rows_hgrn)
    _gdn_kernel(*ins[N_HGRN_IN:N_HGRN_IN + N_GDN_IN], o_gdn, xbuf_gdn, st_gdn)


def _chunk_mixers(hmp, hm, d_hm, hgrn_args, gdn_args, rwkv_args):
    _, b, h, s, d = hm.shape
    g = b * h
    g2 = g // 2
    par = _full((g, 1, d))
    lb, hgrn_norm = hgrn_args
    gates, conv_w, a_log, dt_bias, gdn_norm = gdn_args
    hgrn_specs = [_hm_groups_spec(HMP_GROUPS, 0, PAIR)(b, h // 2)] + [_full((g2, 1, PAIR))] * 4
    gdn_specs = [_hm_groups_spec(3, HM_GDN_QKV)(b, h), _hm_groups_spec(1, HM_GDN_Z)(b, h),
                 pl.BlockSpec((2, 1, g, 1, CHUNK), lambda c: (0, c, 0, 0, 0)),
                 _full((CONV_K, 3 * g, 1, d)), _full((g, 1, 1)), _full((g, 1, 1)), _full((1, 1, d))]
    rwkv_specs = [_hm_groups_spec(6, 0)(b, h)] + [par] * 5
    assert (len(hgrn_specs), len(gdn_specs), len(rwkv_specs)) == (N_HGRN_IN, N_GDN_IN, N_RWKV_IN)
    out = jax.ShapeDtypeStruct((b, s, h * d), F32)
    state = pltpu.VMEM((g, d, d), F32)
    return pl.pallas_call(
        _chunk_mixers_kernel,
        grid=(s // CHUNK,),
        in_specs=hgrn_specs + gdn_specs + rwkv_specs,
        out_specs=[_tok_chunk_spec(b)] * 3,
        out_shape=[out] * 3,
        scratch_shapes=[pltpu.VMEM((g2, PAIR, PAIR), F32), pltpu.VMEM((3, g2, CHUNK, PAIR), F32),
                        pltpu.VMEM((3 * g, HALO + CHUNK, d), F32), state, state],
        compiler_params=_params("arbitrary"),
        name="chunk_mixers",
    )(hmp, jnp.log(lb), jnp.log1p(-lb), 1.0 - lb, hgrn_norm,
      hm, hm, gates, conv_w, a_log, dt_bias, gdn_norm,
      d_hm, *rwkv_args)


def _par_hm(p, b, width=HEAD):
    return jnp.tile(p.reshape(-1, 1, width), (b, 1, 1))


def _gdn_gate_layout(gates):
    b, s, h2 = gates.shape
    g = gates.reshape(b, s // CHUNK, CHUNK, 2, h2 // 2).transpose(3, 1, 0, 4, 2)
    return g.reshape(2, s // CHUNK, b * (h2 // 2), 1, CHUNK)


def _gdn_conv_layout(w, b):
    k, c = w.shape
    h = c // (3 * HEAD)
    w = jnp.broadcast_to(w.reshape(k, 3, 1, h, 1, HEAD), (k, 3, b, h, 1, HEAD))
    return w.reshape(k, 3 * b * h, 1, HEAD)


def _block_diag(w):
    n, a, b = w.shape
    out = jnp.zeros((n * a, n * b), w.dtype)
    for i in range(n):
        out = out.at[i * a:(i + 1) * a, i * b:(i + 1) * b].set(w[i])
    return out


def _in_proj_weight(w):
    d = w.shape[0]
    g4 = 4 * GROUP
    gates = g4 + g4
    lru = gates + 2 * HEADS
    rwkv = lru + 2 * GROUP
    cols = [w[:, rwkv:rwkv + g4], w[:, lru:rwkv], w[:, gates:lru], jnp.zeros((d, LANES - 2 * HEADS), w.dtype),
            w[:, 0:g4], w[:, g4:gates]]
    return jnp.concatenate(cols, axis=1).astype(BF16)


def kernel(x, mem, norm_mix, w_in, w_out, hgrn_lb_logits, hgrn_norm, gdn_conv_w, gdn_A_log, gdn_dt_bias, gdn_norm, lru_conv_w, lru_conv_b, lru_wa, lru_ba, lru_wx, lru_bx, lru_lambda, rwkv_mu, rwkv_w0, rwkv_w2, rwkv_a0, rwkv_a2, rwkv_g2, rwkv_k_k, rwkv_k_a, rwkv_r_k, rwkv_ln_w, rwkv_ln_b, rwkv_v0, rwkv_v1, rwkv_v2, mem_norm, norm_xattn, xattn_wq, xattn_wk, xattn_wv, xattn_wo, norm_ffn, ffn_w_gate, ffn_w_up, ffn_w_down, moe_router, moe_w_gate, moe_w_up, moe_w_down, norm_final):
    bsz, seq, d = x.shape
    depth = w_in.shape[0]
    tok = bsz * seq
    mlen = mem.shape[1]

    lb = jnp.cumsum(jax.nn.softmax(hgrn_lb_logits.astype(F32), axis=0), axis=0)
    lb = lb - lb[0]
    mem_f = mem.reshape(bsz * mlen, d)
    v_first = None
    for l in range(depth):
        v_mix = None if l == 0 else (rwkv_v0[l - 1], rwkv_v1[l - 1], rwkv_v2[l - 1])
        p_tok, p_hmp, p_hm, d_hm, v_first = _in_proj(
            x, norm_mix[l], _in_proj_weight(w_in[l]), rwkv_mu[l], rwkv_w0[l], rwkv_w2[l], rwkv_a0[l], rwkv_a2[l],
            rwkv_g2[l], v_first, v_mix)

        y_c = _lru_mixer(p_tok, lru_conv_w[l], lru_conv_b[l], _block_diag(lru_wa[l]), lru_ba[l],
                         _block_diag(lru_wx[l]), lru_bx[l], lru_lambda[l])

        gates = p_tok[..., TOK_GATES:TOK_GATES + 2 * HEADS]
        y_a, y_b, y_d = _chunk_mixers(
            p_hmp, p_hm, d_hm,
            (_par_hm(lb[l], bsz, PAIR), _par_hm(hgrn_norm[l], bsz, PAIR)),
            (_gdn_gate_layout(gates), _gdn_conv_layout(gdn_conv_w[l], bsz), _par_hm(gdn_A_log[l], bsz, 1),
             _par_hm(gdn_dt_bias[l], bsz, 1), gdn_norm[l].reshape(1, 1, HEAD)),
            tuple(_par_hm(p[l], bsz) for p in (rwkv_k_k, rwkv_k_a, rwkv_r_k, rwkv_ln_w, rwkv_ln_b)))

        w_kv = jnp.concatenate([xattn_wk[l], xattn_wv[l]], axis=1).astype(BF16)
        kv = _norm_matmul(mem_f, mem_norm, w_kv, mlen, BF16).reshape(bsz, mlen, 2 * d)
        x = _mix_attention((y_a, y_b, y_c, y_d), w_out[l].reshape(4, GROUP, d).astype(BF16), x, norm_xattn[l],
                           xattn_wq[l].astype(BF16), kv, xattn_wo[l].astype(BF16))

        xf = x.reshape(tok, d)
        j = l // 2
        if l % 2 == 0:
            ones = jnp.ones((1, tok, 1), F32)
            xf = _ffn(xf, norm_ffn[l], ffn_w_gate[j][None].astype(BF16), ffn_w_up[j][None].astype(BF16),
                      ffn_w_down[j][None].astype(BF16), ones, TF_FFN)
        else:
            xf = _moe(xf, norm_ffn[l], moe_router[j], moe_w_gate[j].astype(BF16), moe_w_up[j].astype(BF16),
                      moe_w_down[j], norm_final if l == depth - 1 else None)
        x = xf.reshape(bsz, seq, d)
    if depth % 2 == 1:
        x = _final_norm(x.reshape(tok, d), norm_final).reshape(bsz, seq, d)
    return x
```
